```python
import math
import jax
import jax.numpy as jnp
from jax import lax
import numpy as np


D_MODEL = 1024
BATCH = 16
SEQ = 2048
DEPTH = 1

MIX_WIDTH = D_MODEL
HEAD_DIM = 64
ATT_WIDTH = MIX_WIDTH // 2
ATT_HEADS = ATT_WIDTH // HEAD_DIM
IDX_HEADS = 4
IDX_DIM = HEAD_DIM
TOPK_MAX = 256
Q_BLOCK = 128
ROPE_THETA = 10000.0
INDEXER_SCALE = (IDX_HEADS ** -0.5) * (IDX_DIM ** -0.5)
SSD_WIDTH = MIX_WIDTH - ATT_WIDTH
SSD_HEAD_DIM = 64
SSD_HEADS = SSD_WIDTH // SSD_HEAD_DIM
SSD_GROUPS = 2
D_STATE = 64
CONV_WIDTH = 4
CONV_CH = SSD_WIDTH + 2 * SSD_GROUPS * D_STATE
CHUNK = 128
N_GROUPS_MOE = 4
EXPERTS_PER_GROUP = 4
N_EXPERTS = N_GROUPS_MOE * EXPERTS_PER_GROUP
TOP_K_IN_GROUP = 2
EXPERT_FF = 256
ALPHA = (2 * DEPTH) ** 0.25
BETA = (8 * DEPTH) ** -0.25
LN_EPS = 1e-5
IN_SPLIT_SIZES = (ATT_WIDTH, HEAD_DIM, HEAD_DIM, IDX_HEADS * IDX_DIM, IDX_DIM, IDX_HEADS, SSD_WIDTH, CONV_CH, SSD_HEADS)
IN_WIDTH = sum(IN_SPLIT_SIZES)
IN_SPLIT_POINTS = tuple(int(v) for v in np.cumsum(IN_SPLIT_SIZES)[:-1])

kernel_name = 'hybrid_dsa_ssd_hmoe_deepnorm'


def layer_norm(x, g, b):
    xf = x.astype(jnp.float32)
    mu = jnp.mean(xf, -1, keepdims=True)
    var = jnp.mean(jnp.square(xf - mu), -1, keepdims=True)
    return ((xf - mu) * lax.rsqrt(var + LN_EPS)).astype(x.dtype) * g + b


def rope_tables(seq_len):
    inv = ROPE_THETA ** (-jnp.arange(0, HEAD_DIM, 2, dtype=jnp.float32) / HEAD_DIM)
    ang = jnp.arange(seq_len, dtype=jnp.float32)[:, None] * inv[None, :]
    return jnp.cos(ang), jnp.sin(ang)


def apply_rope(x, cos, sin):
    half = x.shape[-1] // 2
    x1, x2 = x[..., :half], x[..., half:]
    cos = cos.astype(x.dtype)
    sin = sin.astype(x.dtype)
    return jnp.concatenate([x1 * cos - x2 * sin, x1 * sin + x2 * cos], -1)


def dsa_attention(q, k, v, iq, ik, iw, topk):
    bsz, s, h, dh = q.shape
    n_blocks = s // Q_BLOCK
    key_pos = jnp.arange(s)

    def block(i):
        t0 = i * Q_BLOCK
        qb = lax.dynamic_slice_in_dim(q, t0, Q_BLOCK, 1)
        iqb = lax.dynamic_slice_in_dim(iq, t0, Q_BLOCK, 1)
        iwb = lax.dynamic_slice_in_dim(iw, t0, Q_BLOCK, 1)
        qpos = t0 + jnp.arange(Q_BLOCK)
        causal = key_pos[None, :] <= qpos[:, None]
        sc = jax.nn.relu(jnp.einsum('bthd,bsd->bths', iqb, ik))
        sc = jnp.einsum('bths,bth->bts', sc, iwb).astype(jnp.float32)
        sc = jnp.where(causal[None], sc, -jnp.inf)
        _, sel = lax.top_k(sc, topk)
        valid = sel <= qpos[None, :, None]
        k_sel = jax.vmap(lambda a, idx: a[idx])(k, sel)
        v_sel = jax.vmap(lambda a, idx: a[idx])(v, sel)
        logits = jnp.einsum('bthd,btkd->bthk', qb, k_sel).astype(jnp.float32) * (dh ** -0.5)
        logits = jnp.where(valid[:, :, None, :], logits, -jnp.inf)
        p = jax.nn.softmax(logits, -1).astype(v.dtype)
        return jnp.einsum('bthk,btkd->bthd', p, v_sel)

    out = lax.map(block, jnp.arange(n_blocks))
    return jnp.moveaxis(out, 0, 1).reshape(bsz, s, h * dh)


def causal_depthwise_conv(u, w, b):
    out = lax.conv_general_dilated(
        u, w[:, None, :].astype(u.dtype), window_strides=(1,),
        padding=[(CONV_WIDTH - 1, 0)], dimension_numbers=('NWC', 'WIO', 'NWC'),
        feature_group_count=u.shape[-1])
    return out + b


def segsum(a):
    t = a.shape[-1]
    ar = jnp.broadcast_to(a[..., :, None], a.shape + (t,))
    strict = jnp.tril(jnp.ones((t, t), dtype=bool), -1)
    cs = jnp.cumsum(jnp.where(strict, ar, 0.0), axis=-2)
    return jnp.where(jnp.tril(jnp.ones((t, t), dtype=bool)), cs, -jnp.inf)


def ssd_chunked(xdt, adt, bm, cm):
    b, s, h, p = xdt.shape
    n = bm.shape[-1]
    c = s // CHUNK
    X = xdt.reshape(b, c, CHUNK, h, p)
    Bc = bm.reshape(b, c, CHUNK, h, n)
    Cc = cm.reshape(b, c, CHUNK, h, n)
    A = adt.reshape(b, c, CHUNK, h).transpose(0, 3, 1, 2)
    a_cum = jnp.cumsum(A, -1)
    Lmat = jnp.exp(segsum(A))
    y_diag = jnp.einsum('bclhn,bcshn,bhcls,bcshp->bclhp', Cc, Bc, Lmat, X)
    decay_states = jnp.exp(a_cum[..., -1:] - a_cum)
    chunk_states = jnp.einsum('bclhn,bhcl,bclhp->bchpn', Bc, decay_states, X)
    chunk_decay = jnp.exp(a_cum[..., -1])

    def step(carry, inp):
        st, dec = inp
        return carry * dec[..., None, None] + st, carry

    init = jnp.zeros((b, h, p, n), X.dtype)
    _, prev = lax.scan(step, init, (jnp.moveaxis(chunk_states, 1, 0), jnp.moveaxis(chunk_decay, 2, 0)))
    prev = jnp.moveaxis(prev, 0, 1)
    y_off = jnp.einsum('bclhn,bchpn,bhcl->bclhp', Cc, prev, jnp.exp(a_cum))
    return (y_diag + y_off).reshape(b, s, h, p)


def ssd_mixer(z, xbc, dt_raw, conv_w, conv_b, dt_bias, a_log, d_skip, norm_w):
    bsz, s, _ = xbc.shape
    f32 = jnp.float32
    xbc = jax.nn.silu(causal_depthwise_conv(xbc, conv_w, conv_b))
    gn = SSD_GROUPS * D_STATE
    xs = xbc[..., :SSD_WIDTH].reshape(bsz, s, SSD_HEADS, SSD_HEAD_DIM).astype(f32)
    bm = xbc[..., SSD_WIDTH:SSD_WIDTH + gn].reshape(bsz, s, SSD_GROUPS, D_STATE)
    cm = xbc[..., SSD_WIDTH + gn:].reshape(bsz, s, SSD_GROUPS, D_STATE)
    rep = SSD_HEADS // SSD_GROUPS
    bm = jnp.repeat(bm, rep, axis=2).astype(f32)
    cm = jnp.repeat(cm, rep, axis=2).astype(f32)
    dt = jax.nn.softplus(dt_raw.astype(f32) + dt_bias.astype(f32))
    a = -jnp.exp(a_log.astype(f32))
    y = ssd_chunked(xs * dt[..., None], dt * a, bm, cm)
    y = y + d_skip.astype(f32)[:, None] * xs
    y = y.reshape(bsz, s, SSD_WIDTH) * jax.nn.silu(z.astype(f32))
    y = y * lax.rsqrt(jnp.mean(jnp.square(y), -1, keepdims=True) + LN_EPS)
    return y.astype(z.dtype) * norm_w


def hier_moe(h, w_rg, b_rg, w_re, b_re, w_gate, w_up, w_down):
    bsz, s, _ = h.shape
    f32 = jnp.float32
    p_group = jax.nn.softmax((h @ w_rg + b_rg).astype(f32), -1)
    g_idx = jnp.argmax(p_group, -1)
    g_prob = jnp.max(p_group, -1)
    e_logits = (h @ w_re + b_re).astype(f32).reshape(bsz, s, N_GROUPS_MOE, EXPERTS_PER_GROUP)
    e_in_group = jnp.take_along_axis(e_logits, g_idx[..., None, None], axis=2)[..., 0, :]
    top_logits, top_idx = lax.top_k(e_in_group, TOP_K_IN_GROUP)
    top_w = jax.nn.softmax(top_logits, -1) * g_prob[..., None]
    expert_id = g_idx[..., None] * EXPERTS_PER_GROUP + top_idx
    gates = jnp.sum(jax.nn.one_hot(expert_id, N_EXPERTS, dtype=f32) * top_w[..., None], axis=-2).astype(h.dtype)
    y = jnp.zeros_like(h)
    for e in range(N_EXPERTS):
        hid = jax.nn.silu(h @ w_gate[e]) * (h @ w_up[e])
        y = y + gates[..., e:e + 1] * (hid @ w_down[e])
    return y


def setup_inputs(seed: int = 0) -> dict:
    key = jax.random.key(seed)
    ks = jax.random.split(key, 21)
    f32 = jnp.float32

    def nrm(k, shape, scale):
        return jax.random.normal(k, shape, f32) * scale

    x = nrm(ks[0], (BATCH, SEQ, D_MODEL), 1.0)
    w_in = nrm(ks[1], (DEPTH, D_MODEL, IN_WIDTH), D_MODEL ** -0.5)
    conv_w = nrm(ks[2], (DEPTH, CONV_WIDTH, CONV_CH), CONV_WIDTH ** -0.5)
    conv_b = nrm(ks[3], (DEPTH, CONV_CH), 0.02)
    dt0 = jnp.exp(jax.random.uniform(ks[4], (DEPTH, SSD_HEADS), f32, math.log(1e-3), math.log(1e-1)))
    dt_bias = dt0 + jnp.log(-jnp.expm1(-dt0))
    a_log = jnp.log(jax.random.uniform(ks[5], (DEPTH, SSD_HEADS), f32, 1.0, 16.0))
    d_skip = 1.0 + nrm(ks[6], (DEPTH, SSD_HEADS), 0.1)
    ssd_norm_w = 1.0 + nrm(ks[7], (DEPTH, SSD_WIDTH), 0.05)
    w_out = nrm(ks[8], (DEPTH, MIX_WIDTH, D_MODEL), (MIX_WIDTH ** -0.5) * BETA)
    ln1_g = 1.0 + nrm(ks[9], (DEPTH, D_MODEL), 0.05)
    ln1_b = nrm(ks[10], (DEPTH, D_MODEL), 0.02)
    w_route_group = nrm(ks[11], (DEPTH, D_MODEL, N_GROUPS_MOE), D_MODEL ** -0.5)
    b_route_group = nrm(ks[12], (DEPTH, N_GROUPS_MOE), 0.01)
    w_route_expert = nrm(ks[13], (DEPTH, D_MODEL, N_EXPERTS), D_MODEL ** -0.5)
    b_route_expert = nrm(ks[14], (DEPTH, N_EXPERTS), 0.01)
    w_gate = nrm(ks[15], (DEPTH, N_EXPERTS, D_MODEL, EXPERT_FF), D_MODEL ** -0.5)
    w_up = nrm(ks[16], (DEPTH, N_EXPERTS, D_MODEL, EXPERT_FF), D_MODEL ** -0.5)
    w_down = nrm(ks[17], (DEPTH, N_EXPERTS, EXPERT_FF, D_MODEL), (EXPERT_FF ** -0.5) * BETA)
    ln2_g = 1.0 + nrm(ks[18], (DEPTH, D_MODEL), 0.05)
    ln2_b = nrm(ks[19], (DEPTH, D_MODEL), 0.02)
    return {'x': x, 'w_in': w_in, 'conv_w': conv_w, 'conv_b': conv_b, 'dt_bias': dt_bias,
            'a_log': a_log, 'd_skip': d_skip, 'ssd_norm_w': ssd_norm_w, 'w_out': w_out,
            'ln1_g': ln1_g, 'ln1_b': ln1_b, 'w_route_group': w_route_group,
            'b_route_group': b_route_group, 'w_route_expert': w_route_expert,
            'b_route_expert': b_route_expert, 'w_gate': w_gate, 'w_up': w_up,
            'w_down': w_down, 'ln2_g': ln2_g, 'ln2_b': ln2_b}


def reference(x, w_in, conv_w, conv_b, dt_bias, a_log, d_skip, ssd_norm_w, w_out, ln1_g, ln1_b,
              w_route_group, b_route_group, w_route_expert, b_route_expert, w_gate, w_up,
              w_down, ln2_g, ln2_b):
    bsz, s, _ = x.shape
    topk = min(TOPK_MAX, s // 4)
    cos, sin = rope_tables(s)
    cos_h, sin_h = cos[:, None, :], sin[:, None, :]
    for l in range(DEPTH):
        proj = jnp.einsum('bsd,de->bse', x, w_in[l])
        q, k, v, iq, ik, iw, z, xbc, dt_raw = jnp.split(proj, IN_SPLIT_POINTS, axis=-1)
        q = apply_rope(q.reshape(bsz, s, ATT_HEADS, HEAD_DIM), cos_h, sin_h)
        k = apply_rope(k, cos, sin)
        iq = apply_rope(iq.reshape(bsz, s, IDX_HEADS, IDX_DIM), cos_h, sin_h)
        ik = apply_rope(ik, cos, sin)
        att = dsa_attention(q, k, v, iq, ik, iw * INDEXER_SCALE, topk)
        ssd = ssd_mixer(z, xbc, dt_raw, conv_w[l], conv_b[l], dt_bias[l], a_log[l],
                        d_skip[l], ssd_norm_w[l])
        mixed = jnp.einsum('bse,ed->bsd', jnp.concatenate([att, ssd], -1), w_out[l])
        x = layer_norm(ALPHA * x + mixed, ln1_g[l], ln1_b[l])
        ffn = hier_moe(x, w_route_group[l], b_route_group[l], w_route_expert[l],
                       b_route_expert[l], w_gate[l], w_up[l], w_down[l])
        x = layer_norm(ALPHA * x + ffn, ln2_g[l], ln2_b[l])
    return x
```

```python
import functools
import math

import jax
import jax.numpy as jnp
import numpy as np
from jax import lax
from jax.experimental import pallas as pl
from jax.experimental.pallas import tpu as pltpu

F32 = jnp.float32
BF16 = jnp.bfloat16

D_MODEL = 1024
HEAD_DIM = 64
ATT_WIDTH = 512
ATT_HEADS = 8
IDX_HEADS = 4
IDX_DIM = 64
TOPK_MAX = 256
ROPE_THETA = 10000.0
INDEXER_SCALE = (IDX_HEADS ** -0.5) * (IDX_DIM ** -0.5)
SSD_WIDTH = 512
SSD_HEADS = 8
SSD_HEAD_DIM = 64
SSD_GROUPS = 2
D_STATE = 64
CONV_WIDTH = 4
CONV_CH = SSD_WIDTH + 2 * SSD_GROUPS * D_STATE
CHUNK = 128
N_GROUPS_MOE = 4
EXPERTS_PER_GROUP = 4
N_EXPERTS = 16
EXPERT_FF = 256
DEPTH = 1
ALPHA = (2 * DEPTH) ** 0.25
LN_EPS = 1e-5

LANES = 128
VMEM_LIMIT = 56 * 1024 * 1024

C_Q = 0
C_IQ = 512
C_K = 768
C_IK = 832
C_MISC = 896
MISC_IW = 64
MISC_DT = 72
C_Z = 1024
C_XBC = 1536
IN_COLS = 2304

NT_DIMS = (((1,), (1,)), ((), ()))


def _nt_dot(a, b):
    return lax.dot_general(a, b, NT_DIMS, preferred_element_type=F32)


def _in_proj_kernel(x_ref, w_ref, cos_ref, s1_ref, s2_ref, mscale_ref,
                    q_ref, iq_ref, k_ref, ik_ref, vext_ref, misc_ref, z_ref, xbc_ref):
    xb = x_ref[...].astype(BF16)
    cos = cos_ref[...]
    s1 = s1_ref[...]
    s2 = s2_ref[...]

    def mm(c0, width):
        return jnp.dot(xb, w_ref[:, c0:c0 + width], preferred_element_type=F32)

    def rope(y):
        fwd = pltpu.roll(y, LANES - HEAD_DIM // 2, 1)
        bwd = pltpu.roll(y, HEAD_DIM // 2, 1)
        return y * cos + fwd * s1 + bwd * s2

    def rope_wide(y, scale):
        parts = []
        for c in range(y.shape[1] // LANES):
            r = rope(y[:, c * LANES:(c + 1) * LANES])
            parts.append(r * scale if scale != 1.0 else r)
        return parts

    for c, r in enumerate(rope_wide(mm(C_Q, ATT_WIDTH), HEAD_DIM ** -0.5)):
        q_ref[:, c * LANES:(c + 1) * LANES] = r.astype(BF16)
    for c, r in enumerate(rope_wide(mm(C_IQ, IDX_HEADS * IDX_DIM), 1.0)):
        iq_ref[:, c * LANES:(c + 1) * LANES] = r.astype(BF16)
    kk = rope(mm(C_K, LANES))
    k_ref[...] = kk[:, :HEAD_DIM].astype(BF16)
    ik_ref[...] = kk[:, HEAD_DIM:].astype(BF16)
    misc = mm(C_MISC, LANES) * mscale_ref[...]
    misc_ref[...] = misc
    lane = lax.broadcasted_iota(jnp.int32, misc.shape, 1)
    vext = jnp.where(lane < HEAD_DIM, misc, jnp.where(lane == HEAD_DIM, 1.0, 0.0))
    vext_ref[...] = vext.astype(BF16)
    z_ref[...] = mm(C_Z, SSD_WIDTH)
    xbc_ref[...] = mm(C_XBC, CONV_CH)


def _in_proj(x2, w_perm, cos_t, s1_t, s2_t, mscale, seq, tm):
    n = x2.shape[0]
    nblk_seq = seq // tm
    row = lambda i: (i, 0)
    tab = lambda i: (i % nblk_seq, 0)
    const = lambda i: (0, 0)
    outs = [
        (ATT_WIDTH, BF16), (IDX_HEADS * IDX_DIM, BF16), (HEAD_DIM, BF16), (IDX_DIM, BF16),
        (LANES, BF16), (LANES, F32), (SSD_WIDTH, F32), (CONV_CH, F32),
    ]
    return pl.pallas_call(
        _in_proj_kernel,
        grid=(n // tm,),
        in_specs=[
            pl.BlockSpec((tm, D_MODEL), row),
            pl.BlockSpec((D_MODEL, IN_COLS), const),
            pl.BlockSpec((tm, LANES), tab),
            pl.BlockSpec((tm, LANES), tab),
            pl.BlockSpec((tm, LANES), tab),
            pl.BlockSpec((1, LANES), const),
        ],
        out_specs=[pl.BlockSpec((tm, w), row) for w, _ in outs],
        out_shape=[jax.ShapeDtypeStruct((n, w), dt) for w, dt in outs],
        compiler_params=pltpu.CompilerParams(
            dimension_semantics=("parallel",), vmem_limit_bytes=VMEM_LIMIT),
        name="in_proj",
    )(x2, w_perm, cos_t, s1_t, s2_t, mscale)


TQ = 256
KC = 256
RG = 64


def _dsa_kernel(q_ref, iq_ref, misc_ref, k_ref, ik_ref, vext_ref, o_ref,
                sc_ref, bias_ref, lg_ref, *, topk):
    qi = pl.program_id(1)
    nj = qi + 1
    neg_inf = -jnp.inf
    kf = float(topk)

    iwb = [jnp.broadcast_to(misc_ref[:, MISC_IW + h:MISC_IW + h + 1], (TQ, KC))
           for h in range(IDX_HEADS)]

    def scores_body(j, carry):
        ikj = ik_ref[pl.ds(pl.multiple_of(j * KC, KC), KC), :]
        sc = jnp.zeros((TQ, KC), F32)
        for h in range(IDX_HEADS):
            d = _nt_dot(iq_ref[:, h * IDX_DIM:(h + 1) * IDX_DIM], ikj)
            sc = sc + iwb[h] * jnp.maximum(d, 0.0)
        sc_ref[j] = sc
        return carry

    lax.fori_loop(0, nj, scores_body, 0)
    row_i = lax.broadcasted_iota(jnp.int32, (TQ, KC), 0)
    col_i = lax.broadcasted_iota(jnp.int32, (TQ, KC), 1)
    causal = col_i <= row_i
    sc_ref[qi] = jnp.where(causal, sc_ref[qi], neg_inf)

    upper = (lax.broadcasted_iota(jnp.int32, (KC, KC), 0)
             < lax.broadcasted_iota(jnp.int32, (KC, KC), 1)).astype(BF16)
    sign_bit = jnp.int32(-2 ** 31)

    def search_group(rg, carry):
        r0 = pl.multiple_of(rg * RG, RG)

        def count_ge(cand):
            candb = jnp.broadcast_to(cand, (RG, KC))

            def body(j, acc):
                s = sc_ref[j, pl.ds(r0, RG), :]
                return acc + jnp.where(s >= candb, 1.0, 0.0)

            acc = lax.fori_loop(0, nj, body, jnp.zeros((RG, KC), F32))
            return jnp.sum(acc, axis=1, keepdims=True)

        pos = count_ge(jnp.zeros((RG, 1), F32)) >= kf
        sbits = jnp.where(pos, jnp.int32(0), sign_bit)

        def as_float(mag):
            return lax.bitcast_convert_type(mag | sbits, F32)

        def bit_body(b, mag):
            trial = mag | lax.shift_left(jnp.int32(1), jnp.int32(30) - b)
            ok = (count_ge(as_float(trial)) >= kf) == pos
            return jnp.where(ok, trial, mag)

        mag = lax.fori_loop(0, 31, bit_body, jnp.zeros((RG, 1), jnp.int32))
        lo = as_float(jnp.where(pos, mag, mag + 1))
        hi = as_float(jnp.where(pos, mag + 1, mag))
        need = kf - count_ge(hi)
        lob = jnp.broadcast_to(lo, (RG, KC))
        hib = jnp.broadcast_to(hi, (RG, KC))

        def bias_body(j, taken):
            s = sc_ref[j, pl.ds(r0, RG), :]
            gt = s >= hib
            eq = jnp.logical_and(s >= lob, jnp.logical_not(gt))
            eqf = jnp.where(eq, 1.0, 0.0)
            before = jnp.dot(eqf.astype(BF16), upper, preferred_element_type=F32) + taken
            sel = jnp.logical_or(gt, jnp.logical_and(eq, before < need))
            bias_ref[j, pl.ds(r0, RG), :] = jnp.where(sel, 0.0, neg_inf)
            return taken + jnp.sum(eqf, axis=1, keepdims=True)

        lax.fori_loop(0, nj, bias_body, jnp.zeros((RG, 1), F32))
        return carry

    lax.fori_loop(0, TQ // RG, search_group, 0)
    bias_ref[qi] = jnp.where(causal, bias_ref[qi], neg_inf)

    for h in range(ATT_HEADS):
        qh = q_ref[:, h * HEAD_DIM:(h + 1) * HEAD_DIM]

        def logits_body(j, m, qh=qh):
            kj = k_ref[pl.ds(pl.multiple_of(j * KC, KC), KC), :]
            lg = _nt_dot(qh, kj) + bias_ref[j]
            lg_ref[j] = lg
            return jnp.maximum(m, jnp.maximum(lg[:, :LANES], lg[:, LANES:]))

        m = lax.fori_loop(0, nj, logits_body, jnp.full((TQ, LANES), neg_inf, F32))
        mrow = jnp.max(m, axis=1, keepdims=True)

        def pv_body(j, acc, mrow=mrow):
            p = jnp.exp(lg_ref[j] - mrow)
            vj = vext_ref[pl.ds(pl.multiple_of(j * KC, KC), KC), :]
            return acc + jnp.dot(p.astype(BF16), vj, preferred_element_type=F32)

        acc = lax.fori_loop(0, nj, pv_body, jnp.zeros((TQ, LANES), F32))
        out = acc[:, :HEAD_DIM] * (1.0 / acc[:, HEAD_DIM:HEAD_DIM + 1])
        o_ref[:, h * HEAD_DIM:(h + 1) * HEAD_DIM] = out.astype(BF16)


def _dsa(q, iq, misc, k, ik, vext, topk):
    b, s, _ = q.shape
    nq = s // TQ
    tile = lambda bi, qi: (bi, qi, 0)
    full = lambda bi, qi: (bi, 0, 0)
    return pl.pallas_call(
        functools.partial(_dsa_kernel, topk=topk),
        grid=(b, nq),
        in_specs=[
            pl.BlockSpec((None, TQ, ATT_WIDTH), tile),
            pl.BlockSpec((None, TQ, IDX_HEADS * IDX_DIM), tile),
            pl.BlockSpec((None, TQ, LANES), tile),
            pl.BlockSpec((None, s, HEAD_DIM), full),
            pl.BlockSpec((None, s, IDX_DIM), full),
            pl.BlockSpec((None, s, LANES), full),
        ],
        out_specs=pl.BlockSpec((None, TQ, ATT_WIDTH), tile),
        out_shape=jax.ShapeDtypeStruct((b, s, ATT_WIDTH), BF16),
        scratch_shapes=[
            pltpu.VMEM((nq, TQ, KC), F32),
            pltpu.VMEM((nq, TQ, KC), F32),
            pltpu.VMEM((nq, TQ, KC), F32),
        ],
        compiler_params=pltpu.CompilerParams(
            dimension_semantics=("parallel", "arbitrary"), vmem_limit_bytes=VMEM_LIMIT),
        name="dsa_attention",
    )(q, iq, misc, k, ik, vext)


SSD_TT = 512
CONV_PAD = 8


def _silu(x):
    return x * (1.0 / (1.0 + jnp.exp(-x)))


def _ssd_kernel(xbc_ref, z_ref, misc_ref, cw_ref, cb_ref, dtb_ref, arow_ref, dskip_ref, nw_ref,
                o_ref, xpad_ref, xc_ref, y_ref, state_ref):
    t = pl.program_id(1)
    tt = xbc_ref.shape[0]

    @pl.when(t == 0)
    def _():
        xpad_ref[0:CONV_PAD, :] = jnp.zeros((CONV_PAD, CONV_CH), F32)
        state_ref[...] = jnp.zeros_like(state_ref)

    @pl.when(t > 0)
    def _():
        xpad_ref[0:CONV_PAD, :] = xpad_ref[tt:tt + CONV_PAD, :]

    xpad_ref[CONV_PAD:CONV_PAD + tt, :] = xbc_ref[...]

    conv = jnp.broadcast_to(cb_ref[...], (tt, CONV_CH))
    for j in range(CONV_WIDTH):
        off = CONV_PAD - (CONV_WIDTH - 1) + j
        conv = conv + cw_ref[j:j + 1, :] * xpad_ref[off:off + tt, :]
    xc_ref[...] = _silu(conv)

    li = lax.broadcasted_iota(jnp.int32, (CHUNK, CHUNK), 0)
    si = lax.broadcasted_iota(jnp.int32, (CHUNK, CHUNK), 1)
    tri = li >= si
    gn = SSD_GROUPS * D_STATE
    heads_per_group = SSD_HEADS // SSD_GROUPS

    def chunk_body(c, carry):
        l0 = pl.multiple_of(c * CHUNK, CHUNK)
        rows = pl.ds(l0, CHUNK)
        xs = xc_ref[rows, 0:SSD_WIDTH]
        bm = xc_ref[rows, SSD_WIDTH:SSD_WIDTH + gn]
        cm = xc_ref[rows, SSD_WIDTH + gn:SSD_WIDTH + 2 * gn]
        raw = misc_ref[rows, :] + dtb_ref[...]
        dt = jnp.maximum(raw, 0.0) + jnp.log1p(jnp.exp(-jnp.abs(raw)))
        adt = dt * arow_ref[...]
        acum = adt
        sh = 1
        while sh < CHUNK:
            acum = acum + jnp.where(li >= sh, pltpu.roll(acum, sh, 0), 0.0)
            sh *= 2
        acum_t = acum.T
        a_last = acum[CHUNK - 1:CHUNK, :]
        decay_all = jnp.exp(a_last - acum)
        ea_all = jnp.exp(acum)
        bm_t = bm.T
        bm16 = bm.astype(BF16)
        cm16 = cm.astype(BF16)
        for g in range(SSD_GROUPS):
            cg = cm16[:, g * D_STATE:(g + 1) * D_STATE]
            bg = bm16[:, g * D_STATE:(g + 1) * D_STATE]
            bg_t = bm_t[g * D_STATE:(g + 1) * D_STATE, :].astype(BF16)
            gmat = _nt_dot(cg, bg)
            for hh in range(heads_per_group):
                h = g * heads_per_group + hh
                lane = MISC_DT + h
                col = acum[:, lane:lane + 1]
                rowv = acum_t[lane:lane + 1, :]
                lmat = jnp.exp(jnp.where(tri, col - rowv, -jnp.inf))
                xh = xs[:, h * SSD_HEAD_DIM:(h + 1) * SSD_HEAD_DIM]
                xdt = xh * dt[:, lane:lane + 1]
                y_diag = jnp.dot((gmat * lmat).astype(BF16), xdt.astype(BF16),
                                 preferred_element_type=F32)
                prev = state_ref[h]
                y_off = jnp.dot(cg, prev.astype(BF16), preferred_element_type=F32) \
                    * ea_all[:, lane:lane + 1]
                xdec = (xdt * decay_all[:, lane:lane + 1]).astype(BF16)
                new = jnp.dot(bg_t, xdec, preferred_element_type=F32)
                state_ref[h] = prev * ea_all[CHUNK - 1:CHUNK, lane:lane + 1] + new
                y_ref[rows, h * SSD_HEAD_DIM:(h + 1) * SSD_HEAD_DIM] = y_diag + y_off
        y = y_ref[rows, :] + dskip_ref[...] * xs
        y = y * _silu(z_ref[rows, :])
        ms = jnp.mean(y * y, axis=1, keepdims=True)
        o_ref[rows, :] = (y * lax.rsqrt(ms + LN_EPS) * nw_ref[...]).astype(BF16)
        return carry

    lax.fori_loop(0, tt // CHUNK, chunk_body, 0)


def _ssd(xbc, z, misc, conv_w, conv_b, dtb_row, a_row, dskip_row, nw_row):
    b, s, _ = xbc.shape
    tt = min(SSD_TT, s)
    tile = lambda bi, ti: (bi, ti, 0)
    const = lambda bi, ti: (0, 0)
    return pl.pallas_call(
        _ssd_kernel,
        grid=(b, s // tt),
        in_specs=[
            pl.BlockSpec((None, tt, CONV_CH), tile),
            pl.BlockSpec((None, tt, SSD_WIDTH), tile),
            pl.BlockSpec((None, tt, LANES), tile),
            pl.BlockSpec((CONV_WIDTH, CONV_CH), const),
            pl.BlockSpec((1, CONV_CH), const),
            pl.BlockSpec((1, LANES), const),
            pl.BlockSpec((1, LANES), const),
            pl.BlockSpec((1, SSD_WIDTH), const),
            pl.BlockSpec((1, SSD_WIDTH), const),
        ],
        out_specs=pl.BlockSpec((None, tt, SSD_WIDTH), tile),
        out_shape=jax.ShapeDtypeStruct((b, s, SSD_WIDTH), BF16),
        scratch_shapes=[
            pltpu.VMEM((tt + CONV_PAD, CONV_CH), F32),
            pltpu.VMEM((tt, CONV_CH), F32),
            pltpu.VMEM((tt, SSD_WIDTH), F32),
            pltpu.VMEM((SSD_HEADS, D_STATE, SSD_HEAD_DIM), F32),
        ],
        compiler_params=pltpu.CompilerParams(
            dimension_semantics=("parallel", "arbitrary"), vmem_limit_bytes=VMEM_LIMIT),
        name="ssd_mixer",
    )(xbc, z, misc, conv_w, conv_b, dtb_row, a_row, dskip_row, nw_row)


def _layer_norm(y, g, b):
    mu = jnp.mean(y, axis=1, keepdims=True)
    yc = y - mu
    var = jnp.mean(yc * yc, axis=1, keepdims=True)
    return yc * lax.rsqrt(var + LN_EPS) * g + b


def _out_proj_kernel(att_ref, ssd_ref, x_ref, wa_ref, ws_ref, g_ref, b_ref, h_ref):
    mixed = jnp.dot(att_ref[...], wa_ref[...], preferred_element_type=F32)
    mixed = mixed + jnp.dot(ssd_ref[...], ws_ref[...], preferred_element_type=F32)
    h_ref[...] = _layer_norm(ALPHA * x_ref[...] + mixed, g_ref[...], b_ref[...])


def _out_proj(att2, ssd2, x2, w_att, w_ssd, g_row, b_row, tm):
    n = x2.shape[0]
    row = lambda i: (i, 0)
    const = lambda i: (0, 0)
    return pl.pallas_call(
        _out_proj_kernel,
        grid=(n // tm,),
        in_specs=[
            pl.BlockSpec((tm, ATT_WIDTH), row),
            pl.BlockSpec((tm, SSD_WIDTH), row),
            pl.BlockSpec((tm, D_MODEL), row),
            pl.BlockSpec((ATT_WIDTH, D_MODEL), const),
            pl.BlockSpec((SSD_WIDTH, D_MODEL), const),
            pl.BlockSpec((1, D_MODEL), const),
            pl.BlockSpec((1, D_MODEL), const),
        ],
        out_specs=pl.BlockSpec((tm, D_MODEL), row),
        out_shape=jax.ShapeDtypeStruct((n, D_MODEL), F32),
        compiler_params=pltpu.CompilerParams(
            dimension_semantics=("parallel",), vmem_limit_bytes=VMEM_LIMIT),
        name="out_proj_ln",
    )(att2, ssd2, x2, w_att, w_ssd, g_row, b_row)


ROUTE_E0 = N_GROUPS_MOE
BIG_LANE = 4 * LANES


def _moe_kernel(h_ref, wr_ref, br_ref, wg_ref, wu_ref, wd_ref, g_ref, b_ref, o_ref,
                hb_ref, gates_ref, acc_ref):
    g = pl.program_id(1)
    tm = h_ref.shape[0]
    lane = lax.broadcasted_iota(jnp.int32, (tm, LANES), 1)

    @pl.when(g == 0)
    def _():
        h = h_ref[...]
        hb_ref[...] = h.astype(BF16)
        acc_ref[...] = jnp.zeros_like(acc_ref)
        logits = jnp.dot(h, wr_ref[...], preferred_element_type=F32,
                         precision=lax.Precision.HIGHEST) + br_ref[...]
        is_grp = lane < N_GROUPS_MOE
        gl = jnp.where(is_grp, logits, -jnp.inf)
        gmax = jnp.max(gl, axis=1, keepdims=True)
        gidx = jnp.min(jnp.where(jnp.logical_and(is_grp, gl == gmax), lane, BIG_LANE),
                       axis=1, keepdims=True)
        gprob = 1.0 / jnp.sum(jnp.exp(gl - gmax), axis=1, keepdims=True)
        e_lo = ROUTE_E0 + EXPERTS_PER_GROUP * gidx
        in_grp = jnp.logical_and(lane >= e_lo, lane < e_lo + EXPERTS_PER_GROUP)
        el = jnp.where(in_grp, logits, -jnp.inf)
        l1 = jnp.max(el, axis=1, keepdims=True)
        i1 = jnp.min(jnp.where(jnp.logical_and(in_grp, el == l1), lane, BIG_LANE),
                     axis=1, keepdims=True)
        el2 = jnp.where(lane == i1, -jnp.inf, el)
        l2 = jnp.max(el2, axis=1, keepdims=True)
        i2 = jnp.min(jnp.where(jnp.logical_and(in_grp, el2 == l2), lane, BIG_LANE),
                     axis=1, keepdims=True)
        e2 = jnp.exp(l2 - l1)
        w1 = gprob / (1.0 + e2)
        w2 = gprob * e2 / (1.0 + e2)
        gates_ref[...] = jnp.where(lane == i1, w1, jnp.where(lane == i2, w2, 0.0))

    hb = hb_ref[...]
    gates = gates_ref[...]
    for k in range(EXPERTS_PER_GROUP):
        cols = slice(k * EXPERT_FF, (k + 1) * EXPERT_FF)
        a = jnp.dot(hb, wg_ref[:, cols], preferred_element_type=F32)
        u = jnp.dot(hb, wu_ref[:, cols], preferred_element_type=F32)
        e_lane = ROUTE_E0 + EXPERTS_PER_GROUP * g + k
        gate = jnp.sum(jnp.where(lane == e_lane, gates, 0.0), axis=1, keepdims=True)
        hid = (_silu(a) * u * gate).astype(BF16)
        acc_ref[...] += jnp.dot(hid, wd_ref[cols, :], preferred_element_type=F32)

    @pl.when(g == pl.num_programs(1) - 1)
    def _():
        o_ref[...] = _layer_norm(ALPHA * h_ref[...] + acc_ref[...], g_ref[...], b_ref[...])


def _moe(h2, wr, br, wg, wu, wd, g_row, b_row, tm):
    n = h2.shape[0]
    row = lambda i, g: (i, 0)
    const = lambda i, g: (0, 0)
    grp = lambda i, g: (g, 0, 0)
    gff = EXPERTS_PER_GROUP * EXPERT_FF
    return pl.pallas_call(
        _moe_kernel,
        grid=(n // tm, N_GROUPS_MOE),
        in_specs=[
            pl.BlockSpec((tm, D_MODEL), row),
            pl.BlockSpec((D_MODEL, LANES), const),
            pl.BlockSpec((1, LANES), const),
            pl.BlockSpec((None, D_MODEL, gff), grp),
            pl.BlockSpec((None, D_MODEL, gff), grp),
            pl.BlockSpec((None, gff, D_MODEL), grp),
            pl.BlockSpec((1, D_MODEL), const),
            pl.BlockSpec((1, D_MODEL), const),
        ],
        out_specs=pl.BlockSpec((tm, D_MODEL), row),
        out_shape=jax.ShapeDtypeStruct((n, D_MODEL), F32),
        scratch_shapes=[
            pltpu.VMEM((tm, D_MODEL), BF16),
            pltpu.VMEM((tm, LANES), F32),
            pltpu.VMEM((tm, D_MODEL), F32),
        ],
        compiler_params=pltpu.CompilerParams(
            dimension_semantics=("parallel", "arbitrary"), vmem_limit_bytes=VMEM_LIMIT),
        name="hier_moe_ln",
    )(h2, wr, br, wg, wu, wd, g_row, b_row)


def _rope_tables(seq):
    inv = ROPE_THETA ** (-jnp.arange(0, HEAD_DIM, 2, dtype=F32) / HEAD_DIM)
    ang = jnp.arange(seq, dtype=F32)[:, None] * inv[None, :]
    cos, sin = jnp.cos(ang), jnp.sin(ang)
    zero = jnp.zeros_like(sin)
    cos_t = jnp.tile(cos, (1, LANES // (HEAD_DIM // 2)))
    s1_t = jnp.tile(jnp.concatenate([-sin, zero], 1), (1, LANES // HEAD_DIM))
    s2_t = jnp.tile(jnp.concatenate([zero, sin], 1), (1, LANES // HEAD_DIM))
    return cos_t, s1_t, s2_t


def _permute_w_in(w):
    sizes = (ATT_WIDTH, HEAD_DIM, HEAD_DIM, IDX_HEADS * IDX_DIM, IDX_DIM, IDX_HEADS,
             SSD_WIDTH, CONV_CH, SSD_HEADS)
    pts = np.cumsum((0,) + sizes)
    q, k, v, iq, ik, iw, z, xbc, dt = [w[:, pts[i]:pts[i + 1]] for i in range(len(sizes))]
    d = w.shape[0]
    pad = lambda n: jnp.zeros((d, n), w.dtype)
    misc = jnp.concatenate([v, iw, pad(MISC_DT - MISC_IW - IDX_HEADS), dt,
                            pad(LANES - MISC_DT - SSD_HEADS)], 1)
    return jnp.concatenate([q, iq, k, ik, misc, z, xbc], 1).astype(BF16)


def _lane_row(vals, start):
    return jnp.zeros((1, LANES), F32).at[0, start:start + vals.shape[0]].set(vals.astype(F32))


def kernel(x, w_in, conv_w, conv_b, dt_bias, a_log, d_skip, ssd_norm_w, w_out, ln1_g, ln1_b,
           w_route_group, b_route_group, w_route_expert, b_route_expert, w_gate, w_up,
           w_down, ln2_g, ln2_b):
    bsz, seq, d = x.shape
    n = bsz * seq
    topk = min(TOPK_MAX, seq // 4)
    tm = 512
    assert d == D_MODEL and seq % TQ == 0 and seq % tm == 0 and topk <= KC
    cos_t, s1_t, s2_t = _rope_tables(seq)
    mscale = jnp.ones((1, LANES), F32).at[0, MISC_IW:MISC_IW + IDX_HEADS].set(INDEXER_SCALE)
    for l in range(DEPTH):
        x2 = x.reshape(n, d)
        q, iq, k, ik, vext, misc, z, xbc = _in_proj(
            x2, _permute_w_in(w_in[l]), cos_t, s1_t, s2_t, mscale, seq, tm)
        r3 = lambda a: a.reshape(bsz, seq, a.shape[-1])
        att = _dsa(r3(q), r3(iq), r3(misc), r3(k), r3(ik), r3(vext), topk)
        ssd = _ssd(
            r3(xbc), r3(z), r3(misc), conv_w[l], conv_b[l][None, :],
            _lane_row(dt_bias[l], MISC_DT), _lane_row(-jnp.exp(a_log[l].astype(F32)), MISC_DT),
            jnp.repeat(d_skip[l].astype(F32), SSD_HEAD_DIM)[None, :], ssd_norm_w[l][None, :])
        w_o = w_out[l].astype(BF16)
        h2 = _out_proj(att.reshape(n, ATT_WIDTH), ssd.reshape(n, SSD_WIDTH), x2,
                       w_o[:ATT_WIDTH], w_o[ATT_WIDTH:], ln1_g[l][None, :], ln1_b[l][None, :], tm)
        wr = jnp.zeros((d, LANES), F32)
        wr = wr.at[:, :N_GROUPS_MOE].set(w_route_group[l])
        wr = wr.at[:, ROUTE_E0:ROUTE_E0 + N_EXPERTS].set(w_route_expert[l])
        br = jnp.zeros((1, LANES), F32)
        br = br.at[0, :N_GROUPS_MOE].set(b_route_group[l])
        br = br.at[0, ROUTE_E0:ROUTE_E0 + N_EXPERTS].set(b_route_expert[l])
        gff = EXPERTS_PER_GROUP * EXPERT_FF
        wg = w_gate[l].reshape(N_GROUPS_MOE, EXPERTS_PER_GROUP, d, EXPERT_FF)
        wg = wg.transpose(0, 2, 1, 3).reshape(N_GROUPS_MOE, d, gff).astype(BF16)
        wu = w_up[l].reshape(N_GROUPS_MOE, EXPERTS_PER_GROUP, d, EXPERT_FF)
        wu = wu.transpose(0, 2, 1, 3).reshape(N_GROUPS_MOE, d, gff).astype(BF16)
        wd = w_down[l].reshape(N_GROUPS_MOE, gff, d).astype(BF16)
        x = _moe(h2, wr, br, wg, wu, wd, ln2_g[l][None, :], ln2_b[l][None, :], tm).reshape(bsz, seq, d)
    return x
```

```python
import functools

import jax
import jax.numpy as jnp
import numpy as np
from jax import lax
from jax.experimental import pallas as pl
from jax.experimental.pallas import tpu as pltpu

F32 = jnp.float32
BF16 = jnp.bfloat16

D_MODEL = 1024
HEAD_DIM = 64
ATT_WIDTH = 512
ATT_HEADS = 8
IDX_HEADS = 4
IDX_DIM = 64
TOPK_MAX = 256
ROPE_THETA = 10000.0
INDEXER_SCALE = (IDX_HEADS ** -0.5) * (IDX_DIM ** -0.5)
SSD_WIDTH = 512
SSD_HEADS = 8
SSD_HEAD_DIM = 64
SSD_GROUPS = 2
D_STATE = 64
CONV_WIDTH = 4
CONV_CH = SSD_WIDTH + 2 * SSD_GROUPS * D_STATE
CHUNK = 128
N_GROUPS_MOE = 4
EXPERTS_PER_GROUP = 4
N_EXPERTS = 16
EXPERT_FF = 256
DEPTH = 1
ALPHA = (2 * DEPTH) ** 0.25
LN_EPS = 1e-5

LANES = 128
SUBLANES = 8
VMEM_LIMIT = 56 * 1024 * 1024

C_Q = 0
C_IQ = 512
C_K = 768
C_IK = 832
C_MISC = 896
MISC_IW = 64
MISC_DT = 72
C_Z = 1024
C_XBC = 1536
IN_COLS = 2304

TQ = 256
KC = 256

NT_DIMS = (((1,), (1,)), ((), ()))


def _nt_dot(a, b):
    return lax.dot_general(a, b, NT_DIMS, preferred_element_type=F32)


def _fold_rows(x, op):
    slabs = [x[r * SUBLANES:(r + 1) * SUBLANES, :] for r in range(x.shape[0] // SUBLANES)]
    while len(slabs) > 1:
        nxt = [op(slabs[i], slabs[i + 1]) for i in range(0, len(slabs) - 1, 2)]
        if len(slabs) % 2:
            nxt.append(slabs[-1])
        slabs = nxt
    return slabs[0]


def _in_proj_kernel(x_ref, w_ref, cos_ref, s1_ref, s2_ref, mscale_ref,
                    q_ref, iq_ref, k_ref, ik_ref, vt_ref, misc_ref, z_ref, xbc_ref):
    xb = x_ref[...].astype(BF16)
    cos = cos_ref[...]
    s1 = s1_ref[...]
    s2 = s2_ref[...]

    def mm(c0, width):
        return jnp.dot(xb, w_ref[:, c0:c0 + width], preferred_element_type=F32)

    def rope(y):
        fwd = pltpu.roll(y, LANES - HEAD_DIM // 2, 1)
        bwd = pltpu.roll(y, HEAD_DIM // 2, 1)
        return y * cos + fwd * s1 + bwd * s2

    def rope_wide(y, scale):
        parts = []
        for c in range(y.shape[1] // LANES):
            r = rope(y[:, c * LANES:(c + 1) * LANES])
            parts.append(r * scale if scale != 1.0 else r)
        return parts

    for c, r in enumerate(rope_wide(mm(C_Q, ATT_WIDTH), HEAD_DIM ** -0.5)):
        q_ref[:, c * LANES:(c + 1) * LANES] = r.astype(BF16)
    for c, r in enumerate(rope_wide(mm(C_IQ, IDX_HEADS * IDX_DIM), 1.0)):
        iq_ref[:, c * LANES:(c + 1) * LANES] = r.astype(BF16)
    kk = rope(mm(C_K, LANES))
    k_ref[...] = kk[:, :HEAD_DIM].astype(BF16)
    ik_ref[...] = kk[:, HEAD_DIM:].astype(BF16)
    misc = mm(C_MISC, LANES) * mscale_ref[...]
    misc_ref[...] = misc
    lane = lax.broadcasted_iota(jnp.int32, misc.shape, 1)
    vext = jnp.where(lane < HEAD_DIM, misc, jnp.where(lane == HEAD_DIM, 1.0, 0.0))
    for c in range(vt_ref.shape[0]):
        vt_ref[c] = vext[c * KC:(c + 1) * KC, :].T.astype(BF16)
    z_ref[...] = mm(C_Z, SSD_WIDTH)
    xbc_ref[...] = mm(C_XBC, CONV_CH)


def _in_proj(x2, w_perm, cos_t, s1_t, s2_t, mscale, seq, tm):
    n = x2.shape[0]
    nblk_seq = seq // tm
    row = lambda i: (i, 0)
    tab = lambda i: (i % nblk_seq, 0)
    const = lambda i: (0, 0)
    outs = [
        ((n, ATT_WIDTH), BF16, pl.BlockSpec((tm, ATT_WIDTH), row)),
        ((n, IDX_HEADS * IDX_DIM), BF16, pl.BlockSpec((tm, IDX_HEADS * IDX_DIM), row)),
        ((n, HEAD_DIM), BF16, pl.BlockSpec((tm, HEAD_DIM), row)),
        ((n, IDX_DIM), BF16, pl.BlockSpec((tm, IDX_DIM), row)),
        ((n // KC, LANES, KC), BF16, pl.BlockSpec((tm // KC, LANES, KC), lambda i: (i, 0, 0))),
        ((n, LANES), F32, pl.BlockSpec((tm, LANES), row)),
        ((n, SSD_WIDTH), F32, pl.BlockSpec((tm, SSD_WIDTH), row)),
        ((n, CONV_CH), F32, pl.BlockSpec((tm, CONV_CH), row)),
    ]
    return pl.pallas_call(
        _in_proj_kernel,
        grid=(n // tm,),
        in_specs=[
            pl.BlockSpec((tm, D_MODEL), row),
            pl.BlockSpec((D_MODEL, IN_COLS), const),
            pl.BlockSpec((tm, LANES), tab),
            pl.BlockSpec((tm, LANES), tab),
            pl.BlockSpec((tm, LANES), tab),
            pl.BlockSpec((1, LANES), const),
        ],
        out_specs=[spec for _, _, spec in outs],
        out_shape=[jax.ShapeDtypeStruct(shape, dt) for shape, dt, _ in outs],
        compiler_params=pltpu.CompilerParams(
            dimension_semantics=("parallel",), vmem_limit_bytes=VMEM_LIMIT),
        name="in_proj",
    )(x2, w_perm, cos_t, s1_t, s2_t, mscale)


def _dsa_kernel(q_ref, iq_ref, misc_ref, k_ref, ik_ref, vt_ref, o_ref,
                sc_ref, bias_ref, lg_ref, acc_ref, *, topk):
    qi = pl.program_id(1)
    nj = qi + 1
    neg_inf = -jnp.inf
    kf = float(topk)
    key_i = lax.broadcasted_iota(jnp.int32, (KC, TQ), 0)
    qry_i = lax.broadcasted_iota(jnp.int32, (KC, TQ), 1)
    causal = key_i <= qry_i

    def key_rows(j):
        return pl.ds(pl.multiple_of(j * KC, KC), KC)

    iw_t = misc_ref[...].T[MISC_IW:MISC_IW + SUBLANES, :]

    def scores_body(j, carry):
        ikj = ik_ref[key_rows(j), :]
        sc = None
        for h in range(IDX_HEADS):
            d = _nt_dot(ikj, iq_ref[:, h * IDX_DIM:(h + 1) * IDX_DIM])
            term = iw_t[h:h + 1, :] * jnp.maximum(d, 0.0)
            sc = term if sc is None else sc + term
        sc_ref[j] = sc
        return carry

    lax.fori_loop(0, nj, scores_body, 0)
    sc_ref[qi] = jnp.where(causal, sc_ref[qi], neg_inf)

    def count_ge(cand):
        def body(j, acc):
            ind = jnp.where(sc_ref[j] >= cand, 1.0, 0.0)
            return acc + _fold_rows(ind, jnp.add)

        acc = lax.fori_loop(0, nj, body, jnp.zeros((SUBLANES, TQ), F32))
        return jnp.sum(acc, axis=0, keepdims=True)

    pos = count_ge(jnp.zeros((1, TQ), F32)) >= kf
    sbits = jnp.where(pos, jnp.int32(0), jnp.int32(-2 ** 31))

    def as_float(mag):
        return lax.bitcast_convert_type(mag | sbits, F32)

    def bit_body(b, mag):
        trial = mag | lax.shift_left(jnp.int32(1), jnp.int32(30) - b)
        ok = (count_ge(as_float(trial)) >= kf) == pos
        return jnp.where(ok, trial, mag)

    mag = lax.fori_loop(0, 31, bit_body, jnp.zeros((1, TQ), jnp.int32))
    lo = as_float(jnp.where(pos, mag, mag + 1))
    hi = as_float(jnp.where(pos, mag + 1, mag))
    need = kf - count_ge(hi)

    lower = (lax.broadcasted_iota(jnp.int32, (KC, KC), 0)
             > lax.broadcasted_iota(jnp.int32, (KC, KC), 1)).astype(BF16)

    def bias_body(j, taken):
        s = sc_ref[j]
        gt = s >= hi
        eq = jnp.logical_and(s >= lo, jnp.logical_not(gt))
        eqf = jnp.where(eq, 1.0, 0.0)
        before = jnp.dot(lower, eqf.astype(BF16), preferred_element_type=F32) + taken
        sel = jnp.logical_or(gt, jnp.logical_and(eq, before < need))
        bias_ref[j] = jnp.where(sel, 0.0, neg_inf)
        return taken + jnp.sum(_fold_rows(eqf, jnp.add), axis=0, keepdims=True)

    lax.fori_loop(0, nj, bias_body, jnp.zeros((1, TQ), F32))
    bias_ref[qi] = jnp.where(causal, bias_ref[qi], neg_inf)

    def logits_body(j, ms):
        kj = k_ref[key_rows(j), :]
        bias = bias_ref[j]
        out = []
        for h in range(ATT_HEADS):
            lg = _nt_dot(kj, q_ref[:, h * HEAD_DIM:(h + 1) * HEAD_DIM]) + bias
            lg_ref[h, j] = lg
            out.append(jnp.maximum(ms[h], _fold_rows(lg, jnp.maximum)))
        return tuple(out)

    ms = lax.fori_loop(0, nj, logits_body,
                       tuple(jnp.full((SUBLANES, TQ), neg_inf, F32) for _ in range(ATT_HEADS)))
    mrow = [jnp.max(m, axis=0, keepdims=True) for m in ms]
    acc_ref[...] = jnp.zeros_like(acc_ref)

    def pv_body(j, carry):
        vtj = vt_ref[j]
        for h in range(ATT_HEADS):
            p = jnp.exp(lg_ref[h, j] - mrow[h]).astype(BF16)
            acc_ref[h] += jnp.dot(vtj, p, preferred_element_type=F32)
        return carry

    lax.fori_loop(0, nj, pv_body, 0)
    for pair in range(ATT_HEADS // 2):
        halves = []
        for h in (2 * pair, 2 * pair + 1):
            a = acc_ref[h]
            halves.append(a[:HEAD_DIM, :] * (1.0 / a[HEAD_DIM:HEAD_DIM + 1, :]))
        blk = jnp.concatenate(halves, axis=0)
        o_ref[:, pair * LANES:(pair + 1) * LANES] = blk.T.astype(BF16)


def _dsa(q, iq, misc, k, ik, vt, topk):
    b, s, _ = q.shape
    nq = s // TQ
    tile = lambda bi, qi: (bi, qi, 0)
    full = lambda bi, qi: (bi, 0, 0)
    return pl.pallas_call(
        functools.partial(_dsa_kernel, topk=topk),
        grid=(b, nq),
        in_specs=[
            pl.BlockSpec((None, TQ, ATT_WIDTH), tile),
            pl.BlockSpec((None, TQ, IDX_HEADS * IDX_DIM), tile),
            pl.BlockSpec((None, TQ, LANES), tile),
            pl.BlockSpec((None, s, HEAD_DIM), full),
            pl.BlockSpec((None, s, IDX_DIM), full),
            pl.BlockSpec((None, s // KC, LANES, KC), lambda bi, qi: (bi, 0, 0, 0)),
        ],
        out_specs=pl.BlockSpec((None, TQ, ATT_WIDTH), tile),
        out_shape=jax.ShapeDtypeStruct((b, s, ATT_WIDTH), BF16),
        scratch_shapes=[
            pltpu.VMEM((nq, KC, TQ), F32),
            pltpu.VMEM((nq, KC, TQ), F32),
            pltpu.VMEM((ATT_HEADS, nq, KC, TQ), F32),
            pltpu.VMEM((ATT_HEADS, LANES, TQ), F32),
        ],
        compiler_params=pltpu.CompilerParams(
            dimension_semantics=("parallel", "arbitrary"), vmem_limit_bytes=VMEM_LIMIT),
        name="dsa_attention",
    )(q, iq, misc, k, ik, vt)


SSD_TT = 512
CONV_PAD = 8


def _silu(x):
    return x * (1.0 / (1.0 + jnp.exp(-x)))


def _ssd_kernel(xbc_ref, z_ref, misc_ref, cw_ref, cb_ref, dtb_ref, arow_ref, dskip_ref, nw_ref,
                o_ref, xpad_ref, xc_ref, y_ref, state_ref):
    t = pl.program_id(1)
    tt = xbc_ref.shape[0]

    @pl.when(t == 0)
    def _():
        xpad_ref[0:CONV_PAD, :] = jnp.zeros((CONV_PAD, CONV_CH), F32)
        state_ref[...] = jnp.zeros_like(state_ref)

    @pl.when(t > 0)
    def _():
        xpad_ref[0:CONV_PAD, :] = xpad_ref[tt:tt + CONV_PAD, :]

    xpad_ref[CONV_PAD:CONV_PAD + tt, :] = xbc_ref[...]

    conv = jnp.broadcast_to(cb_ref[...], (tt, CONV_CH))
    for j in range(CONV_WIDTH):
        off = CONV_PAD - (CONV_WIDTH - 1) + j
        conv = conv + cw_ref[j:j + 1, :] * xpad_ref[off:off + tt, :]
    xc_ref[...] = _silu(conv)

    li = lax.broadcasted_iota(jnp.int32, (CHUNK, CHUNK), 0)
    si = lax.broadcasted_iota(jnp.int32, (CHUNK, CHUNK), 1)
    tri = li >= si
    gn = SSD_GROUPS * D_STATE
    heads_per_group = SSD_HEADS // SSD_GROUPS

    def chunk_body(c, carry):
        l0 = pl.multiple_of(c * CHUNK, CHUNK)
        rows = pl.ds(l0, CHUNK)
        xs = xc_ref[rows, 0:SSD_WIDTH]
        bm = xc_ref[rows, SSD_WIDTH:SSD_WIDTH + gn]
        cm = xc_ref[rows, SSD_WIDTH + gn:SSD_WIDTH + 2 * gn]
        raw = misc_ref[rows, :] + dtb_ref[...]
        dt = jnp.maximum(raw, 0.0) + jnp.log1p(jnp.exp(-jnp.abs(raw)))
        adt = dt * arow_ref[...]
        acum = adt
        sh = 1
        while sh < CHUNK:
            acum = acum + jnp.where(li >= sh, pltpu.roll(acum, sh, 0), 0.0)
            sh *= 2
        acum_t = acum.T
        a_last = acum[CHUNK - 1:CHUNK, :]
        decay_all = jnp.exp(a_last - acum)
        ea_all = jnp.exp(acum)
        bm_t = bm.T
        bm16 = bm.astype(BF16)
        cm16 = cm.astype(BF16)
        for g in range(SSD_GROUPS):
            cg = cm16[:, g * D_STATE:(g + 1) * D_STATE]
            bg = bm16[:, g * D_STATE:(g + 1) * D_STATE]
            bg_t = bm_t[g * D_STATE:(g + 1) * D_STATE, :].astype(BF16)
            gmat = _nt_dot(cg, bg)
            for hh in range(heads_per_group):
                h = g * heads_per_group + hh
                lane = MISC_DT + h
                col = acum[:, lane:lane + 1]
                rowv = acum_t[lane:lane + 1, :]
                lmat = jnp.exp(jnp.where(tri, col - rowv, -jnp.inf))
                xh = xs[:, h * SSD_HEAD_DIM:(h + 1) * SSD_HEAD_DIM]
                xdt = xh * dt[:, lane:lane + 1]
                y_diag = jnp.dot((gmat * lmat).astype(BF16), xdt.astype(BF16),
                                 preferred_element_type=F32)
                prev = state_ref[h]
                y_off = jnp.dot(cg, prev.astype(BF16), preferred_element_type=F32) \
                    * ea_all[:, lane:lane + 1]
                xdec = (xdt * decay_all[:, lane:lane + 1]).astype(BF16)
                new = jnp.dot(bg_t, xdec, preferred_element_type=F32)
                state_ref[h] = prev * ea_all[CHUNK - 1:CHUNK, lane:lane + 1] + new
                y_ref[rows, h * SSD_HEAD_DIM:(h + 1) * SSD_HEAD_DIM] = y_diag + y_off
        y = y_ref[rows, :] + dskip_ref[...] * xs
        y = y * _silu(z_ref[rows, :])
        ms = jnp.mean(y * y, axis=1, keepdims=True)
        o_ref[rows, :] = (y * lax.rsqrt(ms + LN_EPS) * nw_ref[...]).astype(BF16)
        return carry

    lax.fori_loop(0, tt // CHUNK, chunk_body, 0)


def _ssd(xbc, z, misc, conv_w, conv_b, dtb_row, a_row, dskip_row, nw_row):
    b, s, _ = xbc.shape
    tt = min(SSD_TT, s)
    tile = lambda bi, ti: (bi, ti, 0)
    const = lambda bi, ti: (0, 0)
    return pl.pallas_call(
        _ssd_kernel,
        grid=(b, s // tt),
        in_specs=[
            pl.BlockSpec((None, tt, CONV_CH), tile),
            pl.BlockSpec((None, tt, SSD_WIDTH), tile),
            pl.BlockSpec((None, tt, LANES), tile),
            pl.BlockSpec((CONV_WIDTH, CONV_CH), const),
            pl.BlockSpec((1, CONV_CH), const),
            pl.BlockSpec((1, LANES), const),
            pl.BlockSpec((1, LANES), const),
            pl.BlockSpec((1, SSD_WIDTH), const),
            pl.BlockSpec((1, SSD_WIDTH), const),
        ],
        out_specs=pl.BlockSpec((None, tt, SSD_WIDTH), tile),
        out_shape=jax.ShapeDtypeStruct((b, s, SSD_WIDTH), BF16),
        scratch_shapes=[
            pltpu.VMEM((tt + CONV_PAD, CONV_CH), F32),
            pltpu.VMEM((tt, CONV_CH), F32),
            pltpu.VMEM((tt, SSD_WIDTH), F32),
            pltpu.VMEM((SSD_HEADS, D_STATE, SSD_HEAD_DIM), F32),
        ],
        compiler_params=pltpu.CompilerParams(
            dimension_semantics=("parallel", "arbitrary"), vmem_limit_bytes=VMEM_LIMIT),
        name="ssd_mixer",
    )(xbc, z, misc, conv_w, conv_b, dtb_row, a_row, dskip_row, nw_row)


def _layer_norm(y, g, b):
    mu = jnp.mean(y, axis=1, keepdims=True)
    yc = y - mu
    var = jnp.mean(yc * yc, axis=1, keepdims=True)
    return yc * lax.rsqrt(var + LN_EPS) * g + b


def _out_proj_kernel(att_ref, ssd_ref, x_ref, wa_ref, ws_ref, g_ref, b_ref, h_ref):
    mixed = jnp.dot(att_ref[...], wa_ref[...], preferred_element_type=F32)
    mixed = mixed + jnp.dot(ssd_ref[...], ws_ref[...], preferred_element_type=F32)
    h_ref[...] = _layer_norm(ALPHA * x_ref[...] + mixed, g_ref[...], b_ref[...])


def _out_proj(att2, ssd2, x2, w_att, w_ssd, g_row, b_row, tm):
    n = x2.shape[0]
    row = lambda i: (i, 0)
    const = lambda i: (0, 0)
    return pl.pallas_call(
        _out_proj_kernel,
        grid=(n // tm,),
        in_specs=[
            pl.BlockSpec((tm, ATT_WIDTH), row),
            pl.BlockSpec((tm, SSD_WIDTH), row),
            pl.BlockSpec((tm, D_MODEL), row),
            pl.BlockSpec((ATT_WIDTH, D_MODEL), const),
            pl.BlockSpec((SSD_WIDTH, D_MODEL), const),
            pl.BlockSpec((1, D_MODEL), const),
            pl.BlockSpec((1, D_MODEL), const),
        ],
        out_specs=pl.BlockSpec((tm, D_MODEL), row),
        out_shape=jax.ShapeDtypeStruct((n, D_MODEL), F32),
        compiler_params=pltpu.CompilerParams(
            dimension_semantics=("parallel",), vmem_limit_bytes=VMEM_LIMIT),
        name="out_proj_ln",
    )(att2, ssd2, x2, w_att, w_ssd, g_row, b_row)


ROUTE_E0 = N_GROUPS_MOE
BIG_LANE = 4 * LANES


def _moe_kernel(h_ref, wr_ref, br_ref, wg_ref, wu_ref, wd_ref, g_ref, b_ref, o_ref,
                hb_ref, gates_ref, acc_ref):
    g = pl.program_id(1)
    tm = h_ref.shape[0]
    lane = lax.broadcasted_iota(jnp.int32, (tm, LANES), 1)

    @pl.when(g == 0)
    def _():
        h = h_ref[...]
        hb_ref[...] = h.astype(BF16)
        acc_ref[...] = jnp.zeros_like(acc_ref)
        logits = jnp.dot(h, wr_ref[...], preferred_element_type=F32,
                         precision=lax.Precision.HIGHEST) + br_ref[...]
        is_grp = lane < N_GROUPS_MOE
        gl = jnp.where(is_grp, logits, -jnp.inf)
        gmax = jnp.max(gl, axis=1, keepdims=True)
        gidx = jnp.min(jnp.where(jnp.logical_and(is_grp, gl == gmax), lane, BIG_LANE),
                       axis=1, keepdims=True)
        gprob = 1.0 / jnp.sum(jnp.exp(gl - gmax), axis=1, keepdims=True)
        e_lo = ROUTE_E0 + EXPERTS_PER_GROUP * gidx
        in_grp = jnp.logical_and(lane >= e_lo, lane < e_lo + EXPERTS_PER_GROUP)
        el = jnp.where(in_grp, logits, -jnp.inf)
        l1 = jnp.max(el, axis=1, keepdims=True)
        i1 = jnp.min(jnp.where(jnp.logical_and(in_grp, el == l1), lane, BIG_LANE),
                     axis=1, keepdims=True)
        el2 = jnp.where(lane == i1, -jnp.inf, el)
        l2 = jnp.max(el2, axis=1, keepdims=True)
        i2 = jnp.min(jnp.where(jnp.logical_and(in_grp, el2 == l2), lane, BIG_LANE),
                     axis=1, keepdims=True)
        e2 = jnp.exp(l2 - l1)
        w1 = gprob / (1.0 + e2)
        w2 = gprob * e2 / (1.0 + e2)
        gates_ref[...] = jnp.where(lane == i1, w1, jnp.where(lane == i2, w2, 0.0))

    hb = hb_ref[...]
    gates = gates_ref[...]
    for k in range(EXPERTS_PER_GROUP):
        cols = slice(k * EXPERT_FF, (k + 1) * EXPERT_FF)
        a = jnp.dot(hb, wg_ref[:, cols], preferred_element_type=F32)
        u = jnp.dot(hb, wu_ref[:, cols], preferred_element_type=F32)
        e_lane = ROUTE_E0 + EXPERTS_PER_GROUP * g + k
        gate = jnp.sum(jnp.where(lane == e_lane, gates, 0.0), axis=1, keepdims=True)
        hid = (_silu(a) * u * gate).astype(BF16)
        acc_ref[...] += jnp.dot(hid, wd_ref[cols, :], preferred_element_type=F32)

    @pl.when(g == pl.num_programs(1) - 1)
    def _():
        o_ref[...] = _layer_norm(ALPHA * h_ref[...] + acc_ref[...], g_ref[...], b_ref[...])


def _moe(h2, wr, br, wg, wu, wd, g_row, b_row, tm):
    n = h2.shape[0]
    row = lambda i, g: (i, 0)
    const = lambda i, g: (0, 0)
    grp = lambda i, g: (g, 0, 0)
    gff = EXPERTS_PER_GROUP * EXPERT_FF
    return pl.pallas_call(
        _moe_kernel,
        grid=(n // tm, N_GROUPS_MOE),
        in_specs=[
            pl.BlockSpec((tm, D_MODEL), row),
            pl.BlockSpec((D_MODEL, LANES), const),
            pl.BlockSpec((1, LANES), const),
            pl.BlockSpec((None, D_MODEL, gff), grp),
            pl.BlockSpec((None, D_MODEL, gff), grp),
            pl.BlockSpec((None, gff, D_MODEL), grp),
            pl.BlockSpec((1, D_MODEL), const),
            pl.BlockSpec((1, D_MODEL), const),
        ],
        out_specs=pl.BlockSpec((tm, D_MODEL), row),
        out_shape=jax.ShapeDtypeStruct((n, D_MODEL), F32),
        scratch_shapes=[
            pltpu.VMEM((tm, D_MODEL), BF16),
            pltpu.VMEM((tm, LANES), F32),
            pltpu.VMEM((tm, D_MODEL), F32),
        ],
        compiler_params=pltpu.CompilerParams(
            dimension_semantics=("parallel", "arbitrary"), vmem_limit_bytes=VMEM_LIMIT),
        name="hier_moe_ln",
    )(h2, wr, br, wg, wu, wd, g_row, b_row)


def _rope_tables(seq):
    inv = ROPE_THETA ** (-jnp.arange(0, HEAD_DIM, 2, dtype=F32) / HEAD_DIM)
    ang = jnp.arange(seq, dtype=F32)[:, None] * inv[None, :]
    cos, sin = jnp.cos(ang), jnp.sin(ang)
    zero = jnp.zeros_like(sin)
    cos_t = jnp.tile(cos, (1, LANES // (HEAD_DIM // 2)))
    s1_t = jnp.tile(jnp.concatenate([-sin, zero], 1), (1, LANES // HEAD_DIM))
    s2_t = jnp.tile(jnp.concatenate([zero, sin], 1), (1, LANES // HEAD_DIM))
    return cos_t, s1_t, s2_t


def _permute_w_in(w):
    sizes = (ATT_WIDTH, HEAD_DIM, HEAD_DIM, IDX_HEADS * IDX_DIM, IDX_DIM, IDX_HEADS,
             SSD_WIDTH, CONV_CH, SSD_HEADS)
    pts = np.cumsum((0,) + sizes)
    q, k, v, iq, ik, iw, z, xbc, dt = [w[:, pts[i]:pts[i + 1]] for i in range(len(sizes))]
    d = w.shape[0]
    pad = lambda n: jnp.zeros((d, n), w.dtype)
    misc = jnp.concatenate([v, iw, pad(MISC_DT - MISC_IW - IDX_HEADS), dt,
                            pad(LANES - MISC_DT - SSD_HEADS)], 1)
    return jnp.concatenate([q, iq, k, ik, misc, z, xbc], 1).astype(BF16)


def _lane_row(vals, start):
    return jnp.zeros((1, LANES), F32).at[0, start:start + vals.shape[0]].set(vals.astype(F32))


def kernel(x, w_in, conv_w, conv_b, dt_bias, a_log, d_skip, ssd_norm_w, w_out, ln1_g, ln1_b,
           w_route_group, b_route_group, w_route_expert, b_route_expert, w_gate, w_up,
           w_down, ln2_g, ln2_b):
    bsz, seq, d = x.shape
    n = bsz * seq
    topk = min(TOPK_MAX, seq // 4)
    tm = 512
    assert d == D_MODEL and TQ == KC and seq % TQ == 0 and seq % tm == 0 and topk <= KC
    cos_t, s1_t, s2_t = _rope_tables(seq)
    mscale = jnp.ones((1, LANES), F32).at[0, MISC_IW:MISC_IW + IDX_HEADS].set(INDEXER_SCALE)
    for l in range(DEPTH):
        x2 = x.reshape(n, d)
        q, iq, k, ik, vt, misc, z, xbc = _in_proj(
            x2, _permute_w_in(w_in[l]), cos_t, s1_t, s2_t, mscale, seq, tm)
        r3 = lambda a: a.reshape(bsz, seq, a.shape[-1])
        att = _dsa(r3(q), r3(iq), r3(misc), r3(k), r3(ik),
                   vt.reshape(bsz, seq // KC, LANES, KC), topk)
        ssd = _ssd(
            r3(xbc), r3(z), r3(misc), conv_w[l], conv_b[l][None, :],
            _lane_row(dt_bias[l], MISC_DT), _lane_row(-jnp.exp(a_log[l].astype(F32)), MISC_DT),
            jnp.repeat(d_skip[l].astype(F32), SSD_HEAD_DIM)[None, :], ssd_norm_w[l][None, :])
        w_o = w_out[l].astype(BF16)
        h2 = _out_proj(att.reshape(n, ATT_WIDTH), ssd.reshape(n, SSD_WIDTH), x2,
                       w_o[:ATT_WIDTH], w_o[ATT_WIDTH:], ln1_g[l][None, :], ln1_b[l][None, :], tm)
        wr = jnp.zeros((d, LANES), F32)
        wr = wr.at[:, :N_GROUPS_MOE].set(w_route_group[l])
        wr = wr.at[:, ROUTE_E0:ROUTE_E0 + N_EXPERTS].set(w_route_expert[l])
        br = jnp.zeros((1, LANES), F32)
        br = br.at[0, :N_GROUPS_MOE].set(b_route_group[l])
        br = br.at[0, ROUTE_E0:ROUTE_E0 + N_EXPERTS].set(b_route_expert[l])
        gff = EXPERTS_PER_GROUP * EXPERT_FF
        wg = w_gate[l].reshape(N_GROUPS_MOE, EXPERTS_PER_GROUP, d, EXPERT_FF)
        wg = wg.transpose(0, 2, 1, 3).reshape(N_GROUPS_MOE, d, gff).astype(BF16)
        wu = w_up[l].reshape(N_GROUPS_MOE, EXPERTS_PER_GROUP, d, EXPERT_FF)
        wu = wu.transpose(0, 2, 1, 3).reshape(N_GROUPS_MOE, d, gff).astype(BF16)
        wd = w_down[l].reshape(N_GROUPS_MOE, gff, d).astype(BF16)
        x = _moe(h2, wr, br, wg, wu, wd, ln2_g[l][None, :], ln2_b[l][None, :], tm).reshape(bsz, seq, d)
    return x
```

```python
import functools

import jax
import jax.numpy as jnp
import numpy as np
from jax import lax
from jax.experimental import pallas as pl
from jax.experimental.pallas import tpu as pltpu

F32 = jnp.float32
BF16 = jnp.bfloat16

D_MODEL = 1024
HEAD_DIM = 64
ATT_WIDTH = 512
ATT_HEADS = 8
IDX_HEADS = 4
IDX_DIM = 64
TOPK_MAX = 256
ROPE_THETA = 10000.0
INDEXER_SCALE = (IDX_HEADS ** -0.5) * (IDX_DIM ** -0.5)
SSD_WIDTH = 512
SSD_HEADS = 8
SSD_HEAD_DIM = 64
SSD_GROUPS = 2
D_STATE = 64
CONV_WIDTH = 4
CONV_CH = SSD_WIDTH + 2 * SSD_GROUPS * D_STATE
CHUNK = 128
N_GROUPS_MOE = 4
EXPERTS_PER_GROUP = 4
N_EXPERTS = 16
EXPERT_FF = 256
DEPTH = 1
ALPHA = (2 * DEPTH) ** 0.25
LN_EPS = 1e-5

LANES = 128
SUBLANES = 8
VMEM_LIMIT = 56 * 1024 * 1024

C_Q = 0
C_IQ = 512
C_K = 768
C_IK = 832
C_MISC = 896
MISC_IW = 64
MISC_DT = 72
C_Z = 1024
C_XBC = 1536
IN_COLS = 2304

TQ = 256
KC = 256

NT_DIMS = (((1,), (1,)), ((), ()))


def _nt_dot(a, b):
    return lax.dot_general(a, b, NT_DIMS, preferred_element_type=F32)


def _fold_rows(x, op):
    slabs = [x[r * SUBLANES:(r + 1) * SUBLANES, :] for r in range(x.shape[0] // SUBLANES)]
    while len(slabs) > 1:
        nxt = [op(slabs[i], slabs[i + 1]) for i in range(0, len(slabs) - 1, 2)]
        if len(slabs) % 2:
            nxt.append(slabs[-1])
        slabs = nxt
    return slabs[0]


CONV_PAD = 8


def _silu(x):
    return x * (1.0 / (1.0 + jnp.exp(-x)))


def _in_proj_kernel(x_ref, w_ref, cos_ref, s1_ref, s2_ref, mscale_ref, cw_ref, cb_ref,
                    q_ref, iq_ref, k_ref, ik_ref, vt_ref, misc_ref, zs_ref, xc_ref, hist_ref,
                    *, nblk_seq):
    xb = x_ref[...].astype(BF16)
    cos = cos_ref[...]
    s1 = s1_ref[...]
    s2 = s2_ref[...]

    def mm(c0, width):
        return jnp.dot(xb, w_ref[:, c0:c0 + width], preferred_element_type=F32)

    def rope(y):
        fwd = pltpu.roll(y, LANES - HEAD_DIM // 2, 1)
        bwd = pltpu.roll(y, HEAD_DIM // 2, 1)
        return y * cos + fwd * s1 + bwd * s2

    def rope_wide(y, scale):
        parts = []
        for c in range(y.shape[1] // LANES):
            r = rope(y[:, c * LANES:(c + 1) * LANES])
            parts.append(r * scale if scale != 1.0 else r)
        return parts

    for c, r in enumerate(rope_wide(mm(C_Q, ATT_WIDTH), HEAD_DIM ** -0.5)):
        q_ref[:, c * LANES:(c + 1) * LANES] = r.astype(BF16)
    for c, r in enumerate(rope_wide(mm(C_IQ, IDX_HEADS * IDX_DIM), 1.0)):
        iq_ref[:, c * LANES:(c + 1) * LANES] = r.astype(BF16)
    kk = rope(mm(C_K, LANES))
    k_ref[...] = kk[:, :HEAD_DIM].astype(BF16)
    ik_ref[...] = kk[:, HEAD_DIM:].astype(BF16)
    misc = mm(C_MISC, LANES) * mscale_ref[...]
    misc_ref[...] = misc
    lane = lax.broadcasted_iota(jnp.int32, misc.shape, 1)
    vext = jnp.where(lane < HEAD_DIM, misc, jnp.where(lane == HEAD_DIM, 1.0, 0.0))
    for c in range(vt_ref.shape[0]):
        vt_ref[c] = vext[c * KC:(c + 1) * KC, :].T.astype(BF16)
    zs_ref[...] = _silu(mm(C_Z, SSD_WIDTH))

    xbc = mm(C_XBC, CONV_CH)
    tm = xbc.shape[0]
    @pl.when(pl.program_id(0) == 0)
    def _():
        hist_ref[...] = jnp.zeros_like(hist_ref)

    seq_start = pl.program_id(0) % nblk_seq == 0
    hist = jnp.where(seq_start, 0.0, hist_ref[...])
    xp = jnp.concatenate([hist, xbc], axis=0)
    acc = cw_ref[0:1, :] * xp
    for j in range(1, CONV_WIDTH):
        acc = pltpu.roll(acc, 1, 0) + cw_ref[j:j + 1, :] * xp
    xc_ref[...] = _silu(acc[CONV_PAD:, :] + cb_ref[...])
    hist_ref[...] = xbc[tm - CONV_PAD:, :]


def _in_proj(x2, w_perm, cos_t, s1_t, s2_t, mscale, conv_w, conv_b, seq, tm):
    n = x2.shape[0]
    nblk_seq = seq // tm
    row = lambda i: (i, 0)
    tab = lambda i: (i % nblk_seq, 0)
    const = lambda i: (0, 0)
    outs = [
        ((n, ATT_WIDTH), BF16, pl.BlockSpec((tm, ATT_WIDTH), row)),
        ((n, IDX_HEADS * IDX_DIM), BF16, pl.BlockSpec((tm, IDX_HEADS * IDX_DIM), row)),
        ((n, HEAD_DIM), BF16, pl.BlockSpec((tm, HEAD_DIM), row)),
        ((n, IDX_DIM), BF16, pl.BlockSpec((tm, IDX_DIM), row)),
        ((n // KC, LANES, KC), BF16, pl.BlockSpec((tm // KC, LANES, KC), lambda i: (i, 0, 0))),
        ((n, LANES), F32, pl.BlockSpec((tm, LANES), row)),
        ((n, SSD_WIDTH), F32, pl.BlockSpec((tm, SSD_WIDTH), row)),
        ((n, CONV_CH), F32, pl.BlockSpec((tm, CONV_CH), row)),
    ]
    return pl.pallas_call(
        functools.partial(_in_proj_kernel, nblk_seq=nblk_seq),
        grid=(n // tm,),
        in_specs=[
            pl.BlockSpec((tm, D_MODEL), row),
            pl.BlockSpec((D_MODEL, IN_COLS), const),
            pl.BlockSpec((tm, LANES), tab),
            pl.BlockSpec((tm, LANES), tab),
            pl.BlockSpec((tm, LANES), tab),
            pl.BlockSpec((1, LANES), const),
            pl.BlockSpec((CONV_WIDTH, CONV_CH), const),
            pl.BlockSpec((1, CONV_CH), const),
        ],
        out_specs=[spec for _, _, spec in outs],
        out_shape=[jax.ShapeDtypeStruct(shape, dt) for shape, dt, _ in outs],
        scratch_shapes=[pltpu.VMEM((CONV_PAD, CONV_CH), F32)],
        compiler_params=pltpu.CompilerParams(
            dimension_semantics=("arbitrary",), vmem_limit_bytes=VMEM_LIMIT),
        name="in_proj",
    )(x2, w_perm, cos_t, s1_t, s2_t, mscale, conv_w, conv_b)


def _dsa_kernel(q_ref, iq_ref, misc_ref, k_ref, ik_ref, vt_ref, o_ref,
                sc_ref, bias_ref, lg_ref, acc_ref, *, topk):
    qi = pl.program_id(1)
    nj = qi + 1
    neg_inf = -jnp.inf
    kf = float(topk)
    key_i = lax.broadcasted_iota(jnp.int32, (KC, TQ), 0)
    qry_i = lax.broadcasted_iota(jnp.int32, (KC, TQ), 1)
    causal = key_i <= qry_i

    def key_rows(j):
        return pl.ds(pl.multiple_of(j * KC, KC), KC)

    iw_t = misc_ref[...].T[MISC_IW:MISC_IW + SUBLANES, :]

    def scores_body(j, carry):
        ikj = ik_ref[key_rows(j), :]
        sc = None
        for h in range(IDX_HEADS):
            d = _nt_dot(ikj, iq_ref[:, h * IDX_DIM:(h + 1) * IDX_DIM])
            term = iw_t[h:h + 1, :] * jnp.maximum(d, 0.0)
            sc = term if sc is None else sc + term
        sc_ref[j] = sc
        return carry

    lax.fori_loop(0, nj, scores_body, 0)
    sc_ref[qi] = jnp.where(causal, sc_ref[qi], neg_inf)

    def count_ge(cand):
        def body(j, acc):
            ind = jnp.where(sc_ref[j] >= cand, 1.0, 0.0)
            return acc + _fold_rows(ind, jnp.add)

        acc = lax.fori_loop(0, nj, body, jnp.zeros((SUBLANES, TQ), F32))
        return jnp.sum(acc, axis=0, keepdims=True)

    pos = count_ge(jnp.zeros((1, TQ), F32)) >= kf
    sbits = jnp.where(pos, jnp.int32(0), jnp.int32(-2 ** 31))

    def as_float(mag):
        return lax.bitcast_convert_type(mag | sbits, F32)

    def bit_body(b, mag):
        trial = mag | lax.shift_left(jnp.int32(1), jnp.int32(30) - b)
        ok = (count_ge(as_float(trial)) >= kf) == pos
        return jnp.where(ok, trial, mag)

    mag = lax.fori_loop(0, 31, bit_body, jnp.zeros((1, TQ), jnp.int32))
    lo = as_float(jnp.where(pos, mag, mag + 1))
    hi = as_float(jnp.where(pos, mag + 1, mag))
    need = kf - count_ge(hi)

    lower = (lax.broadcasted_iota(jnp.int32, (KC, KC), 0)
             > lax.broadcasted_iota(jnp.int32, (KC, KC), 1)).astype(BF16)

    def bias_body(j, taken):
        s = sc_ref[j]
        gt = s >= hi
        eq = jnp.logical_and(s >= lo, jnp.logical_not(gt))
        eqf = jnp.where(eq, 1.0, 0.0)
        before = jnp.dot(lower, eqf.astype(BF16), preferred_element_type=F32) + taken
        sel = jnp.logical_or(gt, jnp.logical_and(eq, before < need))
        bias_ref[j] = jnp.where(sel, 0.0, neg_inf)
        return taken + jnp.sum(_fold_rows(eqf, jnp.add), axis=0, keepdims=True)

    lax.fori_loop(0, nj, bias_body, jnp.zeros((1, TQ), F32))
    bias_ref[qi] = jnp.where(causal, bias_ref[qi], neg_inf)

    def logits_body(j, ms):
        kj = k_ref[key_rows(j), :]
        bias = bias_ref[j]
        out = []
        for h in range(ATT_HEADS):
            lg = _nt_dot(kj, q_ref[:, h * HEAD_DIM:(h + 1) * HEAD_DIM]) + bias
            lg_ref[h, j] = lg
            out.append(jnp.maximum(ms[h], _fold_rows(lg, jnp.maximum)))
        return tuple(out)

    ms = lax.fori_loop(0, nj, logits_body,
                       tuple(jnp.full((SUBLANES, TQ), neg_inf, F32) for _ in range(ATT_HEADS)))
    mrow = [jnp.max(m, axis=0, keepdims=True) for m in ms]
    acc_ref[...] = jnp.zeros_like(acc_ref)

    def pv_body(j, carry):
        vtj = vt_ref[j]
        for h in range(ATT_HEADS):
            p = jnp.exp(lg_ref[h, j] - mrow[h]).astype(BF16)
            acc_ref[h] += jnp.dot(vtj, p, preferred_element_type=F32)
        return carry

    lax.fori_loop(0, nj, pv_body, 0)
    for pair in range(ATT_HEADS // 2):
        halves = []
        for h in (2 * pair, 2 * pair + 1):
            a = acc_ref[h]
            halves.append(a[:HEAD_DIM, :] * (1.0 / a[HEAD_DIM:HEAD_DIM + 1, :]))
        blk = jnp.concatenate(halves, axis=0)
        o_ref[:, pair * LANES:(pair + 1) * LANES] = blk.T.astype(BF16)


def _dsa(q, iq, misc, k, ik, vt, topk):
    b, s, _ = q.shape
    nq = s // TQ
    tile = lambda bi, qi: (bi, qi, 0)
    full = lambda bi, qi: (bi, 0, 0)
    return pl.pallas_call(
        functools.partial(_dsa_kernel, topk=topk),
        grid=(b, nq),
        in_specs=[
            pl.BlockSpec((None, TQ, ATT_WIDTH), tile),
            pl.BlockSpec((None, TQ, IDX_HEADS * IDX_DIM), tile),
            pl.BlockSpec((None, TQ, LANES), tile),
            pl.BlockSpec((None, s, HEAD_DIM), full),
            pl.BlockSpec((None, s, IDX_DIM), full),
            pl.BlockSpec((None, s // KC, LANES, KC), lambda bi, qi: (bi, 0, 0, 0)),
        ],
        out_specs=pl.BlockSpec((None, TQ, ATT_WIDTH), tile),
        out_shape=jax.ShapeDtypeStruct((b, s, ATT_WIDTH), BF16),
        scratch_shapes=[
            pltpu.VMEM((nq, KC, TQ), F32),
            pltpu.VMEM((nq, KC, TQ), F32),
            pltpu.VMEM((ATT_HEADS, nq, KC, TQ), F32),
            pltpu.VMEM((ATT_HEADS, LANES, TQ), F32),
        ],
        compiler_params=pltpu.CompilerParams(
            dimension_semantics=("parallel", "arbitrary"), vmem_limit_bytes=VMEM_LIMIT),
        name="dsa_attention",
    )(q, iq, misc, k, ik, vt)


SSD_TT = 512


def _ssd_kernel(xc_ref, zs_ref, misc_ref, dtb_ref, arep_ref, dskip_ref, nw_ref,
                expand_ref, triu_ref, o_ref, state_ref):
    t = pl.program_id(1)
    tt = xc_ref.shape[0]

    @pl.when(t == 0)
    def _():
        state_ref[...] = jnp.zeros_like(state_ref)

    tri = (lax.broadcasted_iota(jnp.int32, (CHUNK, CHUNK), 0)
           >= lax.broadcasted_iota(jnp.int32, (CHUNK, CHUNK), 1))
    left_head = lax.broadcasted_iota(jnp.int32, (CHUNK, LANES), 1) < SSD_HEAD_DIM
    left_head_n = lax.broadcasted_iota(jnp.int32, (D_STATE, LANES), 1) < SSD_HEAD_DIM
    gn = SSD_GROUPS * D_STATE
    pairs_per_group = SSD_HEADS // SSD_GROUPS // 2
    expand = expand_ref[...]
    triu = triu_ref[...]
    zpad = jnp.zeros((SUBLANES, CHUNK), F32)

    for c in range(tt // CHUNK):
        rows = slice(c * CHUNK, (c + 1) * CHUNK)
        xs = xc_ref[rows, 0:SSD_WIDTH]
        bm = xc_ref[rows, SSD_WIDTH:SSD_WIDTH + gn]
        cm = xc_ref[rows, SSD_WIDTH + gn:SSD_WIDTH + 2 * gn]
        raw = misc_ref[rows, :].T[MISC_DT:MISC_DT + SSD_HEADS, :] + dtb_ref[...]
        dt_t = jnp.maximum(raw, 0.0) + jnp.log1p(jnp.exp(-jnp.abs(raw)))
        adt = dt_t * arep_ref[...]
        hi = adt.astype(BF16).astype(F32)
        r1 = adt - hi
        mid = r1.astype(BF16).astype(F32)
        pieces = jnp.concatenate([hi, mid, r1 - mid, zpad], axis=0).astype(BF16)
        cs = jnp.dot(pieces, triu, preferred_element_type=F32)
        acum_t = (cs[0:SUBLANES] + cs[SUBLANES:2 * SUBLANES]) + cs[2 * SUBLANES:3 * SUBLANES]
        a_last = acum_t[:, CHUNK - 1:CHUNK]
        ddt_t = jnp.exp(a_last - acum_t) * dt_t
        acum = jnp.concatenate(
            [acum_t, jnp.zeros((LANES - SSD_HEADS, CHUNK), F32)], axis=0).T
        ea = jnp.exp(acum)
        ea_hi = ea.astype(BF16)
        ea_lo = (ea - ea_hi.astype(F32)).astype(BF16)
        ea_x = (jnp.dot(ea_hi, expand, preferred_element_type=F32)
                + jnp.dot(ea_lo, expand, preferred_element_type=F32))
        bm_t = bm.T
        xs16 = xs.astype(BF16)
        bm16 = bm.astype(BF16)
        cm16 = cm.astype(BF16)
        y_pairs = [None] * (SSD_HEADS // 2)
        for g in range(SSD_GROUPS):
            cg = cm16[:, g * D_STATE:(g + 1) * D_STATE]
            bg = bm16[:, g * D_STATE:(g + 1) * D_STATE]
            bg_t = bm_t[g * D_STATE:(g + 1) * D_STATE, :]
            gmat = _nt_dot(cg, bg)
            for pp in range(pairs_per_group):
                pair = g * pairs_per_group + pp
                lanes = slice(pair * LANES, (pair + 1) * LANES)
                xp = xs16[:, lanes]
                y_halves, s_halves = [], []
                for h in (2 * pair, 2 * pair + 1):
                    col = acum[:, h:h + 1]
                    rowv = acum_t[h:h + 1, :]
                    lmat = jnp.exp(jnp.where(tri, col - rowv, -jnp.inf))
                    mmat = (gmat * lmat * dt_t[h:h + 1, :]).astype(BF16)
                    y_halves.append(jnp.dot(mmat, xp, preferred_element_type=F32))
                    bs = (bg_t * ddt_t[h:h + 1, :]).astype(BF16)
                    s_halves.append(jnp.dot(bs, xp, preferred_element_type=F32))
                y_diag = jnp.where(left_head, y_halves[0], y_halves[1])
                new = jnp.where(left_head_n, s_halves[0], s_halves[1])
                prev = state_ref[pair]
                y_off = jnp.dot(cg, prev.astype(BF16), preferred_element_type=F32) * ea_x[:, lanes]
                state_ref[pair] = prev * ea_x[CHUNK - 1:CHUNK, lanes] + new
                y_pairs[pair] = y_diag + y_off
        y = jnp.concatenate(y_pairs, axis=1) + dskip_ref[...] * xs
        y = y * zs_ref[rows, :]
        ms = jnp.mean(y * y, axis=1, keepdims=True)
        o_ref[rows, :] = (y * lax.rsqrt(ms + LN_EPS) * nw_ref[...]).astype(BF16)


def _ssd(xc, zs, misc, dtb_rep, a_rep, dskip_row, nw_row):
    b, s, _ = xc.shape
    tt = min(SSD_TT, s)
    tile = lambda bi, ti: (bi, ti, 0)
    const = lambda bi, ti: (0, 0)
    expand = (jnp.arange(LANES)[:, None] == jnp.arange(SSD_WIDTH)[None, :] // SSD_HEAD_DIM).astype(BF16)
    triu = (jnp.arange(CHUNK)[:, None] <= jnp.arange(CHUNK)[None, :]).astype(BF16)
    return pl.pallas_call(
        _ssd_kernel,
        grid=(b, s // tt),
        in_specs=[
            pl.BlockSpec((None, tt, CONV_CH), tile),
            pl.BlockSpec((None, tt, SSD_WIDTH), tile),
            pl.BlockSpec((None, tt, LANES), tile),
            pl.BlockSpec((SSD_HEADS, CHUNK), const),
            pl.BlockSpec((SSD_HEADS, CHUNK), const),
            pl.BlockSpec((1, SSD_WIDTH), const),
            pl.BlockSpec((1, SSD_WIDTH), const),
            pl.BlockSpec((LANES, SSD_WIDTH), const),
            pl.BlockSpec((CHUNK, CHUNK), const),
        ],
        out_specs=pl.BlockSpec((None, tt, SSD_WIDTH), tile),
        out_shape=jax.ShapeDtypeStruct((b, s, SSD_WIDTH), BF16),
        scratch_shapes=[
            pltpu.VMEM((SSD_HEADS // 2, D_STATE, LANES), F32),
        ],
        compiler_params=pltpu.CompilerParams(
            dimension_semantics=("parallel", "arbitrary"), vmem_limit_bytes=VMEM_LIMIT),
        name="ssd_mixer",
    )(xc, zs, misc, dtb_rep, a_rep, dskip_row, nw_row, expand, triu)


def _layer_norm(y, g, b):
    mu = jnp.mean(y, axis=1, keepdims=True)
    yc = y - mu
    var = jnp.mean(yc * yc, axis=1, keepdims=True)
    return yc * lax.rsqrt(var + LN_EPS) * g + b


def _out_proj_kernel(att_ref, ssd_ref, x_ref, wa_ref, ws_ref, g_ref, b_ref, h_ref):
    mixed = jnp.dot(att_ref[...], wa_ref[...], preferred_element_type=F32)
    mixed = mixed + jnp.dot(ssd_ref[...], ws_ref[...], preferred_element_type=F32)
    h_ref[...] = _layer_norm(ALPHA * x_ref[...] + mixed, g_ref[...], b_ref[...])


def _out_proj(att2, ssd2, x2, w_att, w_ssd, g_row, b_row, tm):
    n = x2.shape[0]
    row = lambda i: (i, 0)
    const = lambda i: (0, 0)
    return pl.pallas_call(
        _out_proj_kernel,
        grid=(n // tm,),
        in_specs=[
            pl.BlockSpec((tm, ATT_WIDTH), row),
            pl.BlockSpec((tm, SSD_WIDTH), row),
            pl.BlockSpec((tm, D_MODEL), row),
            pl.BlockSpec((ATT_WIDTH, D_MODEL), const),
            pl.BlockSpec((SSD_WIDTH, D_MODEL), const),
            pl.BlockSpec((1, D_MODEL), const),
            pl.BlockSpec((1, D_MODEL), const),
        ],
        out_specs=pl.BlockSpec((tm, D_MODEL), row),
        out_shape=jax.ShapeDtypeStruct((n, D_MODEL), F32),
        compiler_params=pltpu.CompilerParams(
            dimension_semantics=("parallel",), vmem_limit_bytes=VMEM_LIMIT),
        name="out_proj_ln",
    )(att2, ssd2, x2, w_att, w_ssd, g_row, b_row)


ROUTE_E0 = N_GROUPS_MOE
BIG_LANE = 4 * LANES


def _moe_kernel(h_ref, wr_ref, br_ref, wg_ref, wu_ref, wd_ref, g_ref, b_ref, o_ref,
                hb_ref, gates_ref, acc_ref):
    g = pl.program_id(1)
    tm = h_ref.shape[0]
    lane = lax.broadcasted_iota(jnp.int32, (tm, LANES), 1)

    @pl.when(g == 0)
    def _():
        h = h_ref[...]
        hb_ref[...] = h.astype(BF16)
        acc_ref[...] = jnp.zeros_like(acc_ref)
        logits = jnp.dot(h, wr_ref[...], preferred_element_type=F32,
                         precision=lax.Precision.HIGHEST) + br_ref[...]
        is_grp = lane < N_GROUPS_MOE
        gl = jnp.where(is_grp, logits, -jnp.inf)
        gmax = jnp.max(gl, axis=1, keepdims=True)
        gidx = jnp.min(jnp.where(jnp.logical_and(is_grp, gl == gmax), lane, BIG_LANE),
                       axis=1, keepdims=True)
        gprob = 1.0 / jnp.sum(jnp.exp(gl - gmax), axis=1, keepdims=True)
        e_lo = ROUTE_E0 + EXPERTS_PER_GROUP * gidx
        in_grp = jnp.logical_and(lane >= e_lo, lane < e_lo + EXPERTS_PER_GROUP)
        el = jnp.where(in_grp, logits, -jnp.inf)
        l1 = jnp.max(el, axis=1, keepdims=True)
        i1 = jnp.min(jnp.where(jnp.logical_and(in_grp, el == l1), lane, BIG_LANE),
                     axis=1, keepdims=True)
        el2 = jnp.where(lane == i1, -jnp.inf, el)
        l2 = jnp.max(el2, axis=1, keepdims=True)
        i2 = jnp.min(jnp.where(jnp.logical_and(in_grp, el2 == l2), lane, BIG_LANE),
                     axis=1, keepdims=True)
        e2 = jnp.exp(l2 - l1)
        w1 = gprob / (1.0 + e2)
        w2 = gprob * e2 / (1.0 + e2)
        gates_ref[...] = jnp.where(lane == i1, w1, jnp.where(lane == i2, w2, 0.0))

    hb = hb_ref[...]
    gates = gates_ref[...]
    for k in range(EXPERTS_PER_GROUP):
        cols = slice(k * EXPERT_FF, (k + 1) * EXPERT_FF)
        a = jnp.dot(hb, wg_ref[:, cols], preferred_element_type=F32)
        u = jnp.dot(hb, wu_ref[:, cols], preferred_element_type=F32)
        e_lane = ROUTE_E0 + EXPERTS_PER_GROUP * g + k
        gate = jnp.sum(jnp.where(lane == e_lane, gates, 0.0), axis=1, keepdims=True)
        hid = (_silu(a) * u * gate).astype(BF16)
        acc_ref[...] += jnp.dot(hid, wd_ref[cols, :], preferred_element_type=F32)

    @pl.when(g == pl.num_programs(1) - 1)
    def _():
        o_ref[...] = _layer_norm(ALPHA * h_ref[...] + acc_ref[...], g_ref[...], b_ref[...])


def _moe(h2, wr, br, wg, wu, wd, g_row, b_row, tm):
    n = h2.shape[0]
    row = lambda i, g: (i, 0)
    const = lambda i, g: (0, 0)
    grp = lambda i, g: (g, 0, 0)
    gff = EXPERTS_PER_GROUP * EXPERT_FF
    return pl.pallas_call(
        _moe_kernel,
        grid=(n // tm, N_GROUPS_MOE),
        in_specs=[
            pl.BlockSpec((tm, D_MODEL), row),
            pl.BlockSpec((D_MODEL, LANES), const),
            pl.BlockSpec((1, LANES), const),
            pl.BlockSpec((None, D_MODEL, gff), grp),
            pl.BlockSpec((None, D_MODEL, gff), grp),
            pl.BlockSpec((None, gff, D_MODEL), grp),
            pl.BlockSpec((1, D_MODEL), const),
            pl.BlockSpec((1, D_MODEL), const),
        ],
        out_specs=pl.BlockSpec((tm, D_MODEL), row),
        out_shape=jax.ShapeDtypeStruct((n, D_MODEL), F32),
        scratch_shapes=[
            pltpu.VMEM((tm, D_MODEL), BF16),
            pltpu.VMEM((tm, LANES), F32),
            pltpu.VMEM((tm, D_MODEL), F32),
        ],
        compiler_params=pltpu.CompilerParams(
            dimension_semantics=("parallel", "arbitrary"), vmem_limit_bytes=VMEM_LIMIT),
        name="hier_moe_ln",
    )(h2, wr, br, wg, wu, wd, g_row, b_row)


def _rope_tables(seq):
    inv = ROPE_THETA ** (-jnp.arange(0, HEAD_DIM, 2, dtype=F32) / HEAD_DIM)
    ang = jnp.arange(seq, dtype=F32)[:, None] * inv[None, :]
    cos, sin = jnp.cos(ang), jnp.sin(ang)
    zero = jnp.zeros_like(sin)
    cos_t = jnp.tile(cos, (1, LANES // (HEAD_DIM // 2)))
    s1_t = jnp.tile(jnp.concatenate([-sin, zero], 1), (1, LANES // HEAD_DIM))
    s2_t = jnp.tile(jnp.concatenate([zero, sin], 1), (1, LANES // HEAD_DIM))
    return cos_t, s1_t, s2_t


def _permute_w_in(w):
    sizes = (ATT_WIDTH, HEAD_DIM, HEAD_DIM, IDX_HEADS * IDX_DIM, IDX_DIM, IDX_HEADS,
             SSD_WIDTH, CONV_CH, SSD_HEADS)
    pts = np.cumsum((0,) + sizes)
    q, k, v, iq, ik, iw, z, xbc, dt = [w[:, pts[i]:pts[i + 1]] for i in range(len(sizes))]
    d = w.shape[0]
    pad = lambda n: jnp.zeros((d, n), w.dtype)
    misc = jnp.concatenate([v, iw, pad(MISC_DT - MISC_IW - IDX_HEADS), dt,
                            pad(LANES - MISC_DT - SSD_HEADS)], 1)
    return jnp.concatenate([q, iq, k, ik, misc, z, xbc], 1).astype(BF16)


def _head_rep(vals):
    return jnp.broadcast_to(vals.astype(F32)[:, None], (vals.shape[0], CHUNK))


def kernel(x, w_in, conv_w, conv_b, dt_bias, a_log, d_skip, ssd_norm_w, w_out, ln1_g, ln1_b,
           w_route_group, b_route_group, w_route_expert, b_route_expert, w_gate, w_up,
           w_down, ln2_g, ln2_b):
    bsz, seq, d = x.shape
    n = bsz * seq
    topk = min(TOPK_MAX, seq // 4)
    tm = 512
    assert d == D_MODEL and TQ == KC and seq % TQ == 0 and seq % tm == 0 and topk <= KC
    cos_t, s1_t, s2_t = _rope_tables(seq)
    mscale = jnp.ones((1, LANES), F32).at[0, MISC_IW:MISC_IW + IDX_HEADS].set(INDEXER_SCALE)
    for l in range(DEPTH):
        x2 = x.reshape(n, d)
        q, iq, k, ik, vt, misc, zs, xc = _in_proj(
            x2, _permute_w_in(w_in[l]), cos_t, s1_t, s2_t, mscale,
            conv_w[l].astype(F32), conv_b[l].astype(F32)[None, :], seq, tm)
        r3 = lambda a: a.reshape(bsz, seq, a.shape[-1])
        att = _dsa(r3(q), r3(iq), r3(misc), r3(k), r3(ik),
                   vt.reshape(bsz, seq // KC, LANES, KC), topk)
        ssd = _ssd(
            r3(xc), r3(zs), r3(misc),
            _head_rep(dt_bias[l]), _head_rep(-jnp.exp(a_log[l].astype(F32))),
            jnp.repeat(d_skip[l].astype(F32), SSD_HEAD_DIM)[None, :], ssd_norm_w[l][None, :])
        w_o = w_out[l].astype(BF16)
        h2 = _out_proj(att.reshape(n, ATT_WIDTH), ssd.reshape(n, SSD_WIDTH), x2,
                       w_o[:ATT_WIDTH], w_o[ATT_WIDTH:], ln1_g[l][None, :], ln1_b[l][None, :], tm)
        wr = jnp.zeros((d, LANES), F32)
        wr = wr.at[:, :N_GROUPS_MOE].set(w_route_group[l])
        wr = wr.at[:, ROUTE_E0:ROUTE_E0 + N_EXPERTS].set(w_route_expert[l])
        br = jnp.zeros((1, LANES), F32)
        br = br.at[0, :N_GROUPS_MOE].set(b_route_group[l])
        br = br.at[0, ROUTE_E0:ROUTE_E0 + N_EXPERTS].set(b_route_expert[l])
        gff = EXPERTS_PER_GROUP * EXPERT_FF
        wg = w_gate[l].reshape(N_GROUPS_MOE, EXPERTS_PER_GROUP, d, EXPERT_FF)
        wg = wg.transpose(0, 2, 1, 3).reshape(N_GROUPS_MOE, d, gff).astype(BF16)
        wu = w_up[l].reshape(N_GROUPS_MOE, EXPERTS_PER_GROUP, d, EXPERT_FF)
        wu = wu.transpose(0, 2, 1, 3).reshape(N_GROUPS_MOE, d, gff).astype(BF16)
        wd = w_down[l].reshape(N_GROUPS_MOE, gff, d).astype(BF16)
        x = _moe(h2, wr, br, wg, wu, wd, ln2_g[l][None, :], ln2_b[l][None, :], tm).reshape(bsz, seq, d)
    return x
```

```python
import functools

import jax
import jax.numpy as jnp
import numpy as np
from jax import lax
from jax.experimental import pallas as pl
from jax.experimental.pallas import tpu as pltpu

F32 = jnp.float32
BF16 = jnp.bfloat16

D_MODEL = 1024
HEAD_DIM = 64
ATT_WIDTH = 512
ATT_HEADS = 8
IDX_HEADS = 4
IDX_DIM = 64
TOPK_MAX = 256
ROPE_THETA = 10000.0
INDEXER_SCALE = (IDX_HEADS ** -0.5) * (IDX_DIM ** -0.5)
SSD_WIDTH = 512
SSD_HEADS = 8
SSD_HEAD_DIM = 64
SSD_GROUPS = 2
D_STATE = 64
CONV_WIDTH = 4
CONV_CH = SSD_WIDTH + 2 * SSD_GROUPS * D_STATE
CHUNK = 128
N_GROUPS_MOE = 4
EXPERTS_PER_GROUP = 4
N_EXPERTS = 16
EXPERT_FF = 256
DEPTH = 1
ALPHA = (2 * DEPTH) ** 0.25
LN_EPS = 1e-5

LANES = 128
SUBLANES = 8
VMEM_LIMIT = 56 * 1024 * 1024

C_Q = 0
C_IQ = 512
C_K = 768
C_IK = 832
C_MISC = 896
MISC_IW = 64
MISC_DT = 72
C_Z = 1024
C_XBC = 1536
IN_COLS = 2304

TQ = 256
KC = 256

NT_DIMS = (((1,), (1,)), ((), ()))


def _nt_dot(a, b):
    return lax.dot_general(a, b, NT_DIMS, preferred_element_type=F32)


def _fold_rows(x, op):
    slabs = [x[r * SUBLANES:(r + 1) * SUBLANES, :] for r in range(x.shape[0] // SUBLANES)]
    while len(slabs) > 1:
        nxt = [op(slabs[i], slabs[i + 1]) for i in range(0, len(slabs) - 1, 2)]
        if len(slabs) % 2:
            nxt.append(slabs[-1])
        slabs = nxt
    return slabs[0]


CONV_PAD = 8


def _silu(x):
    return x * (1.0 / (1.0 + jnp.exp(-x)))


def _in_proj_kernel(x_ref, w_ref, cos_ref, s1_ref, s2_ref, mscale_ref, cw_ref, cb_ref,
                    q_ref, iq_ref, k_ref, ik_ref, vt_ref, misc_ref, zs_ref, xc_ref, hist_ref,
                    *, nblk_seq):
    xb = x_ref[...].astype(BF16)
    cos = cos_ref[...]
    s1 = s1_ref[...]
    s2 = s2_ref[...]

    def mm(c0, width):
        return jnp.dot(xb, w_ref[:, c0:c0 + width], preferred_element_type=F32)

    def rope(y):
        fwd = pltpu.roll(y, LANES - HEAD_DIM // 2, 1)
        bwd = pltpu.roll(y, HEAD_DIM // 2, 1)
        return y * cos + fwd * s1 + bwd * s2

    def rope_wide(y, scale):
        parts = []
        for c in range(y.shape[1] // LANES):
            r = rope(y[:, c * LANES:(c + 1) * LANES])
            parts.append(r * scale if scale != 1.0 else r)
        return parts

    for c, r in enumerate(rope_wide(mm(C_Q, ATT_WIDTH), HEAD_DIM ** -0.5)):
        q_ref[:, c * LANES:(c + 1) * LANES] = r.astype(BF16)
    for c, r in enumerate(rope_wide(mm(C_IQ, IDX_HEADS * IDX_DIM), 1.0)):
        iq_ref[:, c * LANES:(c + 1) * LANES] = r.astype(BF16)
    kk = rope(mm(C_K, LANES))
    k_ref[...] = kk[:, :HEAD_DIM].astype(BF16)
    ik_ref[...] = kk[:, HEAD_DIM:].astype(BF16)
    misc = mm(C_MISC, LANES) * mscale_ref[...]
    misc_ref[...] = misc
    lane = lax.broadcasted_iota(jnp.int32, misc.shape, 1)
    vext = jnp.where(lane < HEAD_DIM, misc, jnp.where(lane == HEAD_DIM, 1.0, 0.0))
    for c in range(vt_ref.shape[0]):
        vt_ref[c] = vext[c * KC:(c + 1) * KC, :].T.astype(BF16)
    zs_ref[...] = _silu(mm(C_Z, SSD_WIDTH))

    xbc = mm(C_XBC, CONV_CH)
    tm = xbc.shape[0]
    @pl.when(pl.program_id(0) == 0)
    def _():
        hist_ref[...] = jnp.zeros_like(hist_ref)

    seq_start = pl.program_id(0) % nblk_seq == 0
    hist = jnp.where(seq_start, 0.0, hist_ref[...])
    xp = jnp.concatenate([hist, xbc], axis=0)
    acc = cw_ref[0:1, :] * xp
    for j in range(1, CONV_WIDTH):
        acc = pltpu.roll(acc, 1, 0) + cw_ref[j:j + 1, :] * xp
    xc_ref[...] = _silu(acc[CONV_PAD:, :] + cb_ref[...])
    hist_ref[...] = xbc[tm - CONV_PAD:, :]


def _in_proj(x2, w_perm, cos_t, s1_t, s2_t, mscale, conv_w, conv_b, seq, tm):
    n = x2.shape[0]
    nblk_seq = seq // tm
    row = lambda i: (i, 0)
    tab = lambda i: (i % nblk_seq, 0)
    const = lambda i: (0, 0)
    outs = [
        ((n, ATT_WIDTH), BF16, pl.BlockSpec((tm, ATT_WIDTH), row)),
        ((n, IDX_HEADS * IDX_DIM), BF16, pl.BlockSpec((tm, IDX_HEADS * IDX_DIM), row)),
        ((n, HEAD_DIM), BF16, pl.BlockSpec((tm, HEAD_DIM), row)),
        ((n, IDX_DIM), BF16, pl.BlockSpec((tm, IDX_DIM), row)),
        ((n // KC, LANES, KC), BF16, pl.BlockSpec((tm // KC, LANES, KC), lambda i: (i, 0, 0))),
        ((n, LANES), F32, pl.BlockSpec((tm, LANES), row)),
        ((n, SSD_WIDTH), F32, pl.BlockSpec((tm, SSD_WIDTH), row)),
        ((n, CONV_CH), F32, pl.BlockSpec((tm, CONV_CH), row)),
    ]
    return pl.pallas_call(
        functools.partial(_in_proj_kernel, nblk_seq=nblk_seq),
        grid=(n // tm,),
        in_specs=[
            pl.BlockSpec((tm, D_MODEL), row),
            pl.BlockSpec((D_MODEL, IN_COLS), const),
            pl.BlockSpec((tm, LANES), tab),
            pl.BlockSpec((tm, LANES), tab),
            pl.BlockSpec((tm, LANES), tab),
            pl.BlockSpec((1, LANES), const),
            pl.BlockSpec((CONV_WIDTH, CONV_CH), const),
            pl.BlockSpec((1, CONV_CH), const),
        ],
        out_specs=[spec for _, _, spec in outs],
        out_shape=[jax.ShapeDtypeStruct(shape, dt) for shape, dt, _ in outs],
        scratch_shapes=[pltpu.VMEM((CONV_PAD, CONV_CH), F32)],
        compiler_params=pltpu.CompilerParams(
            dimension_semantics=("arbitrary",), vmem_limit_bytes=VMEM_LIMIT),
        name="in_proj",
    )(x2, w_perm, cos_t, s1_t, s2_t, mscale, conv_w, conv_b)


def _dsa_kernel(q_ref, iq_ref, misc_ref, k_ref, ik_ref, vt_ref, o_ref,
                sc_ref, bias_ref, lg_ref, acc_ref, *, topk):
    qi = pl.program_id(1)
    nj = qi + 1
    neg_inf = -jnp.inf
    kf = float(topk)
    key_i = lax.broadcasted_iota(jnp.int32, (KC, TQ), 0)
    qry_i = lax.broadcasted_iota(jnp.int32, (KC, TQ), 1)
    causal = key_i <= qry_i

    def key_rows(j):
        return pl.ds(pl.multiple_of(j * KC, KC), KC)

    iw_t = misc_ref[...].T[MISC_IW:MISC_IW + SUBLANES, :]

    def scores_body(j, carry):
        ikj = ik_ref[key_rows(j), :]
        sc = None
        for h in range(IDX_HEADS):
            d = _nt_dot(ikj, iq_ref[:, h * IDX_DIM:(h + 1) * IDX_DIM])
            term = iw_t[h:h + 1, :] * jnp.maximum(d, 0.0)
            sc = term if sc is None else sc + term
        sc_ref[j] = sc
        return carry

    lax.fori_loop(0, nj, scores_body, 0)
    sc_ref[qi] = jnp.where(causal, sc_ref[qi], neg_inf)

    def count_ge(cand):
        def body(j, acc):
            ind = jnp.where(sc_ref[j] >= cand, 1.0, 0.0)
            return acc + _fold_rows(ind, jnp.add)

        acc = lax.fori_loop(0, nj, body, jnp.zeros((SUBLANES, TQ), F32))
        return jnp.sum(acc, axis=0, keepdims=True)

    pos = count_ge(jnp.zeros((1, TQ), F32)) >= kf
    sbits = jnp.where(pos, jnp.int32(0), jnp.int32(-2 ** 31))

    def as_float(mag):
        return lax.bitcast_convert_type(mag | sbits, F32)

    def bit_body(b, mag):
        trial = mag | lax.shift_left(jnp.int32(1), jnp.int32(30) - b)
        ok = (count_ge(as_float(trial)) >= kf) == pos
        return jnp.where(ok, trial, mag)

    mag = lax.fori_loop(0, 31, bit_body, jnp.zeros((1, TQ), jnp.int32))
    lo = as_float(jnp.where(pos, mag, mag + 1))
    hi = as_float(jnp.where(pos, mag + 1, mag))
    need = kf - count_ge(hi)

    lower = (lax.broadcasted_iota(jnp.int32, (KC, KC), 0)
             > lax.broadcasted_iota(jnp.int32, (KC, KC), 1)).astype(BF16)

    def bias_body(j, taken):
        s = sc_ref[j]
        gt = s >= hi
        eq = jnp.logical_and(s >= lo, jnp.logical_not(gt))
        eqf = jnp.where(eq, 1.0, 0.0)
        before = jnp.dot(lower, eqf.astype(BF16), preferred_element_type=F32) + taken
        sel = jnp.logical_or(gt, jnp.logical_and(eq, before < need))
        bias_ref[j] = jnp.where(sel, 0.0, neg_inf)
        return taken + jnp.sum(_fold_rows(eqf, jnp.add), axis=0, keepdims=True)

    lax.fori_loop(0, nj, bias_body, jnp.zeros((1, TQ), F32))
    bias_ref[qi] = jnp.where(causal, bias_ref[qi], neg_inf)

    def logits_body(j, ms):
        kj = k_ref[key_rows(j), :]
        bias = bias_ref[j]
        out = []
        for h in range(ATT_HEADS):
            lg = _nt_dot(kj, q_ref[:, h * HEAD_DIM:(h + 1) * HEAD_DIM]) + bias
            lg_ref[h, j] = lg
            out.append(jnp.maximum(ms[h], _fold_rows(lg, jnp.maximum)))
        return tuple(out)

    ms = lax.fori_loop(0, nj, logits_body,
                       tuple(jnp.full((SUBLANES, TQ), neg_inf, F32) for _ in range(ATT_HEADS)))
    mrow = [jnp.max(m, axis=0, keepdims=True) for m in ms]
    acc_ref[...] = jnp.zeros_like(acc_ref)

    def pv_body(j, carry):
        vtj = vt_ref[j]
        for h in range(ATT_HEADS):
            p = jnp.exp(lg_ref[h, j] - mrow[h]).astype(BF16)
            acc_ref[h] += jnp.dot(vtj, p, preferred_element_type=F32)
        return carry

    lax.fori_loop(0, nj, pv_body, 0)
    for pair in range(ATT_HEADS // 2):
        halves = []
        for h in (2 * pair, 2 * pair + 1):
            a = acc_ref[h]
            halves.append(a[:HEAD_DIM, :] * (1.0 / a[HEAD_DIM:HEAD_DIM + 1, :]))
        blk = jnp.concatenate(halves, axis=0)
        o_ref[:, pair * LANES:(pair + 1) * LANES] = blk.T.astype(BF16)


def _dsa(q, iq, misc, k, ik, vt, b, s, topk):
    nq = s // TQ
    tile = lambda bi, qi: (bi * nq + qi, 0)
    full = lambda bi, qi: (bi, 0)
    return pl.pallas_call(
        functools.partial(_dsa_kernel, topk=topk),
        grid=(b, nq),
        in_specs=[
            pl.BlockSpec((TQ, ATT_WIDTH), tile),
            pl.BlockSpec((TQ, IDX_HEADS * IDX_DIM), tile),
            pl.BlockSpec((TQ, LANES), tile),
            pl.BlockSpec((s, HEAD_DIM), full),
            pl.BlockSpec((s, IDX_DIM), full),
            pl.BlockSpec((s // KC, LANES, KC), lambda bi, qi: (bi, 0, 0)),
        ],
        out_specs=pl.BlockSpec((TQ, ATT_WIDTH), tile),
        out_shape=jax.ShapeDtypeStruct((b * s, ATT_WIDTH), BF16),
        scratch_shapes=[
            pltpu.VMEM((nq, KC, TQ), F32),
            pltpu.VMEM((nq, KC, TQ), F32),
            pltpu.VMEM((ATT_HEADS, nq, KC, TQ), F32),
            pltpu.VMEM((ATT_HEADS, LANES, TQ), F32),
        ],
        compiler_params=pltpu.CompilerParams(
            dimension_semantics=("parallel", "arbitrary"), vmem_limit_bytes=VMEM_LIMIT),
        name="dsa_attention",
    )(q, iq, misc, k, ik, vt)


SSD_TT = 512


def _ssd_kernel(xc_ref, zs_ref, misc_ref, dtb_ref, arep_ref, dskip_ref, nw_ref,
                expand_ref, triu_ref, o_ref, state_ref):
    t = pl.program_id(1)
    tt = xc_ref.shape[0]

    @pl.when(t == 0)
    def _():
        state_ref[...] = jnp.zeros_like(state_ref)

    tri = (lax.broadcasted_iota(jnp.int32, (CHUNK, CHUNK), 0)
           >= lax.broadcasted_iota(jnp.int32, (CHUNK, CHUNK), 1))
    left_head = lax.broadcasted_iota(jnp.int32, (CHUNK, LANES), 1) < SSD_HEAD_DIM
    left_head_n = lax.broadcasted_iota(jnp.int32, (D_STATE, LANES), 1) < SSD_HEAD_DIM
    gn = SSD_GROUPS * D_STATE
    pairs_per_group = SSD_HEADS // SSD_GROUPS // 2
    expand = expand_ref[...]
    triu = triu_ref[...]
    zpad = jnp.zeros((SUBLANES, CHUNK), F32)

    for c in range(tt // CHUNK):
        rows = slice(c * CHUNK, (c + 1) * CHUNK)
        xs = xc_ref[rows, 0:SSD_WIDTH]
        bm = xc_ref[rows, SSD_WIDTH:SSD_WIDTH + gn]
        cm = xc_ref[rows, SSD_WIDTH + gn:SSD_WIDTH + 2 * gn]
        raw = misc_ref[rows, :].T[MISC_DT:MISC_DT + SSD_HEADS, :] + dtb_ref[...]
        dt_t = jnp.maximum(raw, 0.0) + jnp.log1p(jnp.exp(-jnp.abs(raw)))
        adt = dt_t * arep_ref[...]
        hi = adt.astype(BF16).astype(F32)
        r1 = adt - hi
        mid = r1.astype(BF16).astype(F32)
        pieces = jnp.concatenate([hi, mid, r1 - mid, zpad], axis=0).astype(BF16)
        cs = jnp.dot(pieces, triu, preferred_element_type=F32)
        acum_t = (cs[0:SUBLANES] + cs[SUBLANES:2 * SUBLANES]) + cs[2 * SUBLANES:3 * SUBLANES]
        a_last = acum_t[:, CHUNK - 1:CHUNK]
        ddt_t = jnp.exp(a_last - acum_t) * dt_t
        acum = jnp.concatenate(
            [acum_t, jnp.zeros((LANES - SSD_HEADS, CHUNK), F32)], axis=0).T
        ea = jnp.exp(acum)
        ea_hi = ea.astype(BF16)
        ea_lo = (ea - ea_hi.astype(F32)).astype(BF16)
        ea_x = (jnp.dot(ea_hi, expand, preferred_element_type=F32)
                + jnp.dot(ea_lo, expand, preferred_element_type=F32))
        bm_t = bm.T
        xs16 = xs.astype(BF16)
        bm16 = bm.astype(BF16)
        cm16 = cm.astype(BF16)
        y_pairs = [None] * (SSD_HEADS // 2)
        for g in range(SSD_GROUPS):
            cg = cm16[:, g * D_STATE:(g + 1) * D_STATE]
            bg = bm16[:, g * D_STATE:(g + 1) * D_STATE]
            bg_t = bm_t[g * D_STATE:(g + 1) * D_STATE, :]
            gmat = _nt_dot(cg, bg)
            for pp in range(pairs_per_group):
                pair = g * pairs_per_group + pp
                lanes = slice(pair * LANES, (pair + 1) * LANES)
                xp = xs16[:, lanes]
                y_halves, s_halves = [], []
                for h in (2 * pair, 2 * pair + 1):
                    col = acum[:, h:h + 1]
                    rowv = acum_t[h:h + 1, :]
                    lmat = jnp.exp(jnp.where(tri, col - rowv, -jnp.inf))
                    mmat = (gmat * lmat * dt_t[h:h + 1, :]).astype(BF16)
                    y_halves.append(jnp.dot(mmat, xp, preferred_element_type=F32))
                    bs = (bg_t * ddt_t[h:h + 1, :]).astype(BF16)
                    s_halves.append(jnp.dot(bs, xp, preferred_element_type=F32))
                y_diag = jnp.where(left_head, y_halves[0], y_halves[1])
                new = jnp.where(left_head_n, s_halves[0], s_halves[1])
                prev = state_ref[pair]
                y_off = jnp.dot(cg, prev.astype(BF16), preferred_element_type=F32) * ea_x[:, lanes]
                state_ref[pair] = prev * ea_x[CHUNK - 1:CHUNK, lanes] + new
                y_pairs[pair] = y_diag + y_off
        y = jnp.concatenate(y_pairs, axis=1) + dskip_ref[...] * xs
        y = y * zs_ref[rows, :]
        ms = jnp.mean(y * y, axis=1, keepdims=True)
        o_ref[rows, :] = (y * lax.rsqrt(ms + LN_EPS) * nw_ref[...]).astype(BF16)


def _ssd(xc, zs, misc, dtb_rep, a_rep, dskip_row, nw_row, b, s):
    tt = min(SSD_TT, s)
    nt = s // tt
    tile = lambda bi, ti: (bi * nt + ti, 0)
    const = lambda bi, ti: (0, 0)
    expand = (jnp.arange(LANES)[:, None] == jnp.arange(SSD_WIDTH)[None, :] // SSD_HEAD_DIM).astype(BF16)
    triu = (jnp.arange(CHUNK)[:, None] <= jnp.arange(CHUNK)[None, :]).astype(BF16)
    return pl.pallas_call(
        _ssd_kernel,
        grid=(b, s // tt),
        in_specs=[
            pl.BlockSpec((tt, CONV_CH), tile),
            pl.BlockSpec((tt, SSD_WIDTH), tile),
            pl.BlockSpec((tt, LANES), tile),
            pl.BlockSpec((SSD_HEADS, CHUNK), const),
            pl.BlockSpec((SSD_HEADS, CHUNK), const),
            pl.BlockSpec((1, SSD_WIDTH), const),
            pl.BlockSpec((1, SSD_WIDTH), const),
            pl.BlockSpec((LANES, SSD_WIDTH), const),
            pl.BlockSpec((CHUNK, CHUNK), const),
        ],
        out_specs=pl.BlockSpec((tt, SSD_WIDTH), tile),
        out_shape=jax.ShapeDtypeStruct((b * s, SSD_WIDTH), BF16),
        scratch_shapes=[
            pltpu.VMEM((SSD_HEADS // 2, D_STATE, LANES), F32),
        ],
        compiler_params=pltpu.CompilerParams(
            dimension_semantics=("parallel", "arbitrary"), vmem_limit_bytes=VMEM_LIMIT),
        name="ssd_mixer",
    )(xc, zs, misc, dtb_rep, a_rep, dskip_row, nw_row, expand, triu)


def _layer_norm(y, g, b):
    mu = jnp.mean(y, axis=1, keepdims=True)
    yc = y - mu
    var = jnp.mean(yc * yc, axis=1, keepdims=True)
    return yc * lax.rsqrt(var + LN_EPS) * g + b


def _out_proj_kernel(att_ref, ssd_ref, x_ref, wa_ref, ws_ref, g_ref, b_ref, h_ref):
    mixed = jnp.dot(att_ref[...], wa_ref[...], preferred_element_type=F32)
    mixed = mixed + jnp.dot(ssd_ref[...], ws_ref[...], preferred_element_type=F32)
    h_ref[...] = _layer_norm(ALPHA * x_ref[...] + mixed, g_ref[...], b_ref[...])


def _out_proj(att2, ssd2, x2, w_att, w_ssd, g_row, b_row, tm):
    n = x2.shape[0]
    row = lambda i: (i, 0)
    const = lambda i: (0, 0)
    return pl.pallas_call(
        _out_proj_kernel,
        grid=(n // tm,),
        in_specs=[
            pl.BlockSpec((tm, ATT_WIDTH), row),
            pl.BlockSpec((tm, SSD_WIDTH), row),
            pl.BlockSpec((tm, D_MODEL), row),
            pl.BlockSpec((ATT_WIDTH, D_MODEL), const),
            pl.BlockSpec((SSD_WIDTH, D_MODEL), const),
            pl.BlockSpec((1, D_MODEL), const),
            pl.BlockSpec((1, D_MODEL), const),
        ],
        out_specs=pl.BlockSpec((tm, D_MODEL), row),
        out_shape=jax.ShapeDtypeStruct((n, D_MODEL), F32),
        compiler_params=pltpu.CompilerParams(
            dimension_semantics=("parallel",), vmem_limit_bytes=VMEM_LIMIT),
        name="out_proj_ln",
    )(att2, ssd2, x2, w_att, w_ssd, g_row, b_row)


ROUTE_E0 = N_GROUPS_MOE
MOE_TM = 1024


def _first_max(vals):
    best = vals[0]
    for v in vals[1:]:
        best = jnp.maximum(best, v)
    idx = jnp.full(best.shape, len(vals) - 1, jnp.int32)
    for i in range(len(vals) - 2, -1, -1):
        idx = jnp.where(vals[i] == best, i, idx)
    return best, idx


def _moe_kernel(h_ref, wrh_ref, wrl_ref, br_ref, wg_ref, wu_ref, wd_ref, g_ref, b_ref, o_ref,
                hb_ref, gates_ref, acc_ref):
    g = pl.program_id(1)
    tm = h_ref.shape[0]

    @pl.when(g == 0)
    def _():
        h = h_ref[...]
        h_hi = h.astype(BF16)
        hb_ref[...] = h_hi
        acc_ref[...] = jnp.zeros_like(acc_ref)
        h_lo = (h - h_hi.astype(F32)).astype(BF16)
        wrh = wrh_ref[...]
        logits = (jnp.dot(h_hi, wrh, preferred_element_type=F32)
                  + jnp.dot(h_lo, wrh, preferred_element_type=F32)
                  + jnp.dot(h_hi, wrl_ref[...], preferred_element_type=F32)) + br_ref[...]
        lt = logits.T
        row = lambda r: lt[r:r + 1, :]
        gl = [row(r) for r in range(N_GROUPS_MOE)]
        gmax, gidx = _first_max(gl)
        denom = jnp.exp(gl[0] - gmax)
        for v in gl[1:]:
            denom = denom + jnp.exp(v - gmax)
        gprob = 1.0 / denom
        el = []
        for k in range(EXPERTS_PER_GROUP):
            v = row(ROUTE_E0 + (N_GROUPS_MOE - 1) * EXPERTS_PER_GROUP + k)
            for gg in range(N_GROUPS_MOE - 2, -1, -1):
                v = jnp.where(gidx == gg, row(ROUTE_E0 + gg * EXPERTS_PER_GROUP + k), v)
            el.append(v)
        l1, i1 = _first_max(el)
        l2, i2 = _first_max([jnp.where(i1 == k, -jnp.inf, el[k]) for k in range(EXPERTS_PER_GROUP)])
        e2 = jnp.exp(l2 - l1)
        w1 = gprob / (1.0 + e2)
        w2 = gprob * e2 / (1.0 + e2)
        rows = []
        for gg in range(N_GROUPS_MOE):
            for k in range(EXPERTS_PER_GROUP):
                wk = jnp.where(i1 == k, w1, jnp.where(i2 == k, w2, 0.0))
                rows.append(jnp.where(gidx == gg, wk, 0.0))
        rows.append(jnp.zeros((LANES - N_EXPERTS, tm), F32))
        gates_ref[...] = jnp.concatenate(rows, axis=0).T

    hb = hb_ref[...]
    gates = gates_ref[...]
    lane = lax.broadcasted_iota(jnp.int32, (tm, LANES), 1)
    for k in range(EXPERTS_PER_GROUP):
        a = jnp.dot(hb, wg_ref[k], preferred_element_type=F32)
        u = jnp.dot(hb, wu_ref[k], preferred_element_type=F32)
        e_lane = EXPERTS_PER_GROUP * g + k
        gate = jnp.sum(jnp.where(lane == e_lane, gates, 0.0), axis=1, keepdims=True)
        hid = (_silu(a) * u * gate).astype(BF16)
        acc_ref[...] += jnp.dot(hid, wd_ref[k], preferred_element_type=F32)

    @pl.when(g == pl.num_programs(1) - 1)
    def _():
        o_ref[...] = _layer_norm(ALPHA * h_ref[...] + acc_ref[...], g_ref[...], b_ref[...])


def _moe(h2, wr_hi, wr_lo, br, wg, wu, wd, g_row, b_row):
    n = h2.shape[0]
    tm = min(MOE_TM, n)
    row = lambda i, g: (i, 0)
    const = lambda i, g: (0, 0)
    grp = lambda i, g: (g, 0, 0)
    return pl.pallas_call(
        _moe_kernel,
        grid=(n // tm, N_GROUPS_MOE),
        in_specs=[
            pl.BlockSpec((tm, D_MODEL), row),
            pl.BlockSpec((D_MODEL, LANES), const),
            pl.BlockSpec((D_MODEL, LANES), const),
            pl.BlockSpec((1, LANES), const),
            pl.BlockSpec((EXPERTS_PER_GROUP, D_MODEL, EXPERT_FF), grp),
            pl.BlockSpec((EXPERTS_PER_GROUP, D_MODEL, EXPERT_FF), grp),
            pl.BlockSpec((EXPERTS_PER_GROUP, EXPERT_FF, D_MODEL), grp),
            pl.BlockSpec((1, D_MODEL), const),
            pl.BlockSpec((1, D_MODEL), const),
        ],
        out_specs=pl.BlockSpec((tm, D_MODEL), row),
        out_shape=jax.ShapeDtypeStruct((n, D_MODEL), F32),
        scratch_shapes=[
            pltpu.VMEM((tm, D_MODEL), BF16),
            pltpu.VMEM((tm, LANES), F32),
            pltpu.VMEM((tm, D_MODEL), F32),
        ],
        compiler_params=pltpu.CompilerParams(
            dimension_semantics=("parallel", "arbitrary"), vmem_limit_bytes=VMEM_LIMIT),
        name="hier_moe_ln",
    )(h2, wr_hi, wr_lo, br, wg, wu, wd, g_row, b_row)


def _rope_tables(seq):
    inv = ROPE_THETA ** (-jnp.arange(0, HEAD_DIM, 2, dtype=F32) / HEAD_DIM)
    ang = jnp.arange(seq, dtype=F32)[:, None] * inv[None, :]
    cos, sin = jnp.cos(ang), jnp.sin(ang)
    zero = jnp.zeros_like(sin)
    cos_t = jnp.tile(cos, (1, LANES // (HEAD_DIM // 2)))
    s1_t = jnp.tile(jnp.concatenate([-sin, zero], 1), (1, LANES // HEAD_DIM))
    s2_t = jnp.tile(jnp.concatenate([zero, sin], 1), (1, LANES // HEAD_DIM))
    return cos_t, s1_t, s2_t


def _permute_w_in(w):
    sizes = (ATT_WIDTH, HEAD_DIM, HEAD_DIM, IDX_HEADS * IDX_DIM, IDX_DIM, IDX_HEADS,
             SSD_WIDTH, CONV_CH, SSD_HEADS)
    pts = np.cumsum((0,) + sizes)
    q, k, v, iq, ik, iw, z, xbc, dt = [w[:, pts[i]:pts[i + 1]] for i in range(len(sizes))]
    d = w.shape[0]
    pad = lambda n: jnp.zeros((d, n), w.dtype)
    misc = jnp.concatenate([v, iw, pad(MISC_DT - MISC_IW - IDX_HEADS), dt,
                            pad(LANES - MISC_DT - SSD_HEADS)], 1)
    return jnp.concatenate([q, iq, k, ik, misc, z, xbc], 1).astype(BF16)


def _head_rep(vals):
    return jnp.broadcast_to(vals.astype(F32)[:, None], (vals.shape[0], CHUNK))


def kernel(x, w_in, conv_w, conv_b, dt_bias, a_log, d_skip, ssd_norm_w, w_out, ln1_g, ln1_b,
           w_route_group, b_route_group, w_route_expert, b_route_expert, w_gate, w_up,
           w_down, ln2_g, ln2_b):
    bsz, seq, d = x.shape
    n = bsz * seq
    topk = min(TOPK_MAX, seq // 4)
    tm = 512
    assert d == D_MODEL and TQ == KC and seq % TQ == 0 and seq % tm == 0 and topk <= KC
    cos_t, s1_t, s2_t = _rope_tables(seq)
    mscale = jnp.ones((1, LANES), F32).at[0, MISC_IW:MISC_IW + IDX_HEADS].set(INDEXER_SCALE)
    for l in range(DEPTH):
        x2 = x.reshape(n, d)
        q, iq, k, ik, vt, misc, zs, xc = _in_proj(
            x2, _permute_w_in(w_in[l]), cos_t, s1_t, s2_t, mscale,
            conv_w[l].astype(F32), conv_b[l].astype(F32)[None, :], seq, tm)
        att = _dsa(q, iq, misc, k, ik, vt, bsz, seq, topk)
        ssd = _ssd(
            xc, zs, misc,
            _head_rep(dt_bias[l]), _head_rep(-jnp.exp(a_log[l].astype(F32))),
            jnp.repeat(d_skip[l].astype(F32), SSD_HEAD_DIM)[None, :], ssd_norm_w[l][None, :],
            bsz, seq)
        w_o = w_out[l].astype(BF16)
        h2 = _out_proj(att, ssd, x2, w_o[:ATT_WIDTH], w_o[ATT_WIDTH:],
                       ln1_g[l][None, :], ln1_b[l][None, :], tm)
        pad = jnp.zeros((d, LANES - ROUTE_E0 - N_EXPERTS), F32)
        wr = jnp.concatenate([w_route_group[l].astype(F32), w_route_expert[l].astype(F32), pad], 1)
        wr_hi = wr.astype(BF16)
        wr_lo = (wr - wr_hi.astype(F32)).astype(BF16)
        br = jnp.concatenate([b_route_group[l].astype(F32), b_route_expert[l].astype(F32),
                              pad[0]])[None, :]
        x = _moe(h2, wr_hi, wr_lo, br, w_gate[l].astype(BF16), w_up[l].astype(BF16),
                 w_down[l].astype(BF16), ln2_g[l][None, :], ln2_b[l][None, :]).reshape(bsz, seq, d)
    return x
```

```python
import functools

import jax
import jax.numpy as jnp
import numpy as np
from jax import lax
from jax.experimental import pallas as pl
from jax.experimental.pallas import tpu as pltpu

F32 = jnp.float32
BF16 = jnp.bfloat16

D_MODEL = 1024
HEAD_DIM = 64
ATT_WIDTH = 512
ATT_HEADS = 8
IDX_HEADS = 4
IDX_DIM = 64
TOPK_MAX = 256
ROPE_THETA = 10000.0
INDEXER_SCALE = (IDX_HEADS ** -0.5) * (IDX_DIM ** -0.5)
SSD_WIDTH = 512
SSD_HEADS = 8
SSD_HEAD_DIM = 64
SSD_GROUPS = 2
D_STATE = 64
CONV_WIDTH = 4
CONV_CH = SSD_WIDTH + 2 * SSD_GROUPS * D_STATE
CHUNK = 128
N_GROUPS_MOE = 4
EXPERTS_PER_GROUP = 4
N_EXPERTS = 16
EXPERT_FF = 256
DEPTH = 1
ALPHA = (2 * DEPTH) ** 0.25
LN_EPS = 1e-5

LANES = 128
SUBLANES = 8
VMEM_LIMIT = 56 * 1024 * 1024

C_Q = 0
C_IQ = 512
C_K = 768
C_IK = 832
C_MISC = 896
MISC_IW = 64
MISC_DT = 72
C_Z = 1024
C_XBC = 1536
IN_COLS = 2304

TQ = 256
KC = 256

Q_SCALE = HEAD_DIM ** -0.5 * float(np.log2(np.e))

NT_DIMS = (((1,), (1,)), ((), ()))


def _nt_dot(a, b):
    return lax.dot_general(a, b, NT_DIMS, preferred_element_type=F32)


def _fold_rows(x, op):
    slabs = [x[r * SUBLANES:(r + 1) * SUBLANES, :] for r in range(x.shape[0] // SUBLANES)]
    while len(slabs) > 1:
        nxt = [op(slabs[i], slabs[i + 1]) for i in range(0, len(slabs) - 1, 2)]
        if len(slabs) % 2:
            nxt.append(slabs[-1])
        slabs = nxt
    return slabs[0]


CONV_PAD = 8


def _silu(x):
    return x * (1.0 / (1.0 + jnp.exp(-x)))


def _in_proj_kernel(x_ref, w_ref, cos_ref, s1_ref, s2_ref, mscale_ref, cw_ref, cb_ref,
                    q_ref, iq_ref, k_ref, ik_ref, vt_ref, misc_ref, zs_ref, xc_ref, hist_ref,
                    *, nblk_seq):
    xb = x_ref[...].astype(BF16)
    cos = cos_ref[...]
    s1 = s1_ref[...]
    s2 = s2_ref[...]

    def mm(c0, width):
        return jnp.dot(xb, w_ref[:, c0:c0 + width], preferred_element_type=F32)

    def rope(y):
        fwd = pltpu.roll(y, LANES - HEAD_DIM // 2, 1)
        bwd = pltpu.roll(y, HEAD_DIM // 2, 1)
        return y * cos + fwd * s1 + bwd * s2

    def rope_wide(y, scale):
        parts = []
        for c in range(y.shape[1] // LANES):
            r = rope(y[:, c * LANES:(c + 1) * LANES])
            parts.append(r * scale if scale != 1.0 else r)
        return parts

    for c, r in enumerate(rope_wide(mm(C_Q, ATT_WIDTH), Q_SCALE)):
        q_ref[:, c * LANES:(c + 1) * LANES] = r.astype(BF16)
    for c, r in enumerate(rope_wide(mm(C_IQ, IDX_HEADS * IDX_DIM), 1.0)):
        iq_ref[:, c * LANES:(c + 1) * LANES] = r.astype(BF16)
    kk = rope(mm(C_K, LANES))
    k_ref[...] = kk[:, :HEAD_DIM].astype(BF16)
    ik_ref[...] = kk[:, HEAD_DIM:].astype(BF16)
    misc = mm(C_MISC, LANES) * mscale_ref[...]
    misc_ref[...] = misc
    lane = lax.broadcasted_iota(jnp.int32, misc.shape, 1)
    vext = jnp.where(lane < HEAD_DIM, misc, jnp.where(lane == HEAD_DIM, 1.0, 0.0))
    for c in range(vt_ref.shape[0]):
        vt_ref[c] = vext[c * KC:(c + 1) * KC, :].T.astype(BF16)
    zs_ref[...] = _silu(mm(C_Z, SSD_WIDTH))

    xbc = mm(C_XBC, CONV_CH)
    tm = xbc.shape[0]
    @pl.when(pl.program_id(0) == 0)
    def _():
        hist_ref[...] = jnp.zeros_like(hist_ref)

    seq_start = pl.program_id(0) % nblk_seq == 0
    hist = jnp.where(seq_start, 0.0, hist_ref[...])
    xp = jnp.concatenate([hist, xbc], axis=0)
    acc = cw_ref[0:1, :] * xp
    for j in range(1, CONV_WIDTH):
        acc = pltpu.roll(acc, 1, 0) + cw_ref[j:j + 1, :] * xp
    xc_ref[...] = _silu(acc[CONV_PAD:, :] + cb_ref[...])
    hist_ref[...] = xbc[tm - CONV_PAD:, :]


def _in_proj(x2, w_perm, cos_t, s1_t, s2_t, mscale, conv_w, conv_b, seq, tm):
    n = x2.shape[0]
    nblk_seq = seq // tm
    row = lambda i: (i, 0)
    tab = lambda i: (i % nblk_seq, 0)
    const = lambda i: (0, 0)
    outs = [
        ((n, ATT_WIDTH), BF16, pl.BlockSpec((tm, ATT_WIDTH), row)),
        ((n, IDX_HEADS * IDX_DIM), BF16, pl.BlockSpec((tm, IDX_HEADS * IDX_DIM), row)),
        ((n, HEAD_DIM), BF16, pl.BlockSpec((tm, HEAD_DIM), row)),
        ((n, IDX_DIM), BF16, pl.BlockSpec((tm, IDX_DIM), row)),
        ((n // KC, LANES, KC), BF16, pl.BlockSpec((tm // KC, LANES, KC), lambda i: (i, 0, 0))),
        ((n, LANES), F32, pl.BlockSpec((tm, LANES), row)),
        ((n, SSD_WIDTH), F32, pl.BlockSpec((tm, SSD_WIDTH), row)),
        ((n, CONV_CH), F32, pl.BlockSpec((tm, CONV_CH), row)),
    ]
    return pl.pallas_call(
        functools.partial(_in_proj_kernel, nblk_seq=nblk_seq),
        grid=(n // tm,),
        in_specs=[
            pl.BlockSpec((tm, D_MODEL), row),
            pl.BlockSpec((D_MODEL, IN_COLS), const),
            pl.BlockSpec((tm, LANES), tab),
            pl.BlockSpec((tm, LANES), tab),
            pl.BlockSpec((tm, LANES), tab),
            pl.BlockSpec((1, LANES), const),
            pl.BlockSpec((CONV_WIDTH, CONV_CH), const),
            pl.BlockSpec((1, CONV_CH), const),
        ],
        out_specs=[spec for _, _, spec in outs],
        out_shape=[jax.ShapeDtypeStruct(shape, dt) for shape, dt, _ in outs],
        scratch_shapes=[pltpu.VMEM((CONV_PAD, CONV_CH), F32)],
        compiler_params=pltpu.CompilerParams(
            dimension_semantics=("arbitrary",), vmem_limit_bytes=VMEM_LIMIT),
        name="in_proj",
    )(x2, w_perm, cos_t, s1_t, s2_t, mscale, conv_w, conv_b)


def _dsa_kernel(q_ref, iq_ref, misc_ref, k_ref, ik_ref, vt_ref, o_ref,
                sc_ref, lg_ref, acc_ref, *, topk):
    qi = pl.program_id(1)
    nj = qi + 1
    neg_inf = -jnp.inf
    kf = float(topk)
    key_i = lax.broadcasted_iota(jnp.int32, (KC, TQ), 0)
    qry_i = lax.broadcasted_iota(jnp.int32, (KC, TQ), 1)
    causal = key_i <= qry_i

    def key_rows(j):
        return pl.ds(pl.multiple_of(j * KC, KC), KC)

    iw_t = misc_ref[...].T[MISC_IW:MISC_IW + SUBLANES, :]

    def scores_body(j, carry):
        ikj = ik_ref[key_rows(j), :]
        sc = None
        for h in range(IDX_HEADS):
            d = _nt_dot(ikj, iq_ref[:, h * IDX_DIM:(h + 1) * IDX_DIM])
            term = iw_t[h:h + 1, :] * jnp.maximum(d, 0.0)
            sc = term if sc is None else sc + term
        sc_ref[j] = sc
        return carry

    lax.fori_loop(0, nj, scores_body, 0)
    sc_ref[qi] = jnp.where(causal, sc_ref[qi], neg_inf)

    def count_ge(cand):
        def body(j, acc):
            ind = jnp.where(sc_ref[j] >= cand, 1.0, 0.0)
            return acc + _fold_rows(ind, jnp.add)

        acc = lax.fori_loop(0, nj, body, jnp.zeros((SUBLANES, TQ), F32))
        return jnp.sum(acc, axis=0, keepdims=True)

    pos = count_ge(jnp.zeros((1, TQ), F32)) >= kf
    sbits = jnp.where(pos, jnp.int32(0), jnp.int32(-2 ** 31))

    def as_float(mag):
        return lax.bitcast_convert_type(mag | sbits, F32)

    def bit_body(b, mag):
        trial = mag | lax.shift_left(jnp.int32(1), jnp.int32(30) - b)
        ok = (count_ge(as_float(trial)) >= kf) == pos
        return jnp.where(ok, trial, mag)

    mag = lax.fori_loop(0, 31, bit_body, jnp.zeros((1, TQ), jnp.int32))
    lo = as_float(jnp.where(pos, mag, mag + 1))
    hi = as_float(jnp.where(pos, mag + 1, mag))
    need = kf - count_ge(hi)

    lower = (lax.broadcasted_iota(jnp.int32, (KC, KC), 0)
             > lax.broadcasted_iota(jnp.int32, (KC, KC), 1)).astype(BF16)

    acc_ref[...] = jnp.zeros_like(acc_ref)
    lg_ref[1] = jnp.full(lg_ref.shape[1:], neg_inf, F32)
    no_max = tuple(jnp.minimum(need, neg_inf) for _ in range(ATT_HEADS))

    def logits_stage(j, taken):
        s = sc_ref[j]
        gt = s >= hi
        eq = jnp.logical_and(s >= lo, jnp.logical_not(gt))
        eqf = jnp.where(eq, 1.0, 0.0)
        before = jnp.dot(lower, eqf.astype(BF16), preferred_element_type=F32) + taken
        sel = jnp.logical_or(gt, jnp.logical_and(eq, before < need))
        sel = jnp.logical_and(sel, jnp.logical_or(causal, j < qi))
        bias = jnp.where(sel, 0.0, neg_inf)
        taken = taken + jnp.sum(_fold_rows(eqf, jnp.add), axis=0, keepdims=True)
        kj = k_ref[key_rows(j), :]
        slot = j % 2
        cms = []
        for h in range(ATT_HEADS):
            lg = _nt_dot(kj, q_ref[:, h * HEAD_DIM:(h + 1) * HEAD_DIM]) + bias
            lg_ref[slot, h] = lg
            cms.append(jnp.max(_fold_rows(lg, jnp.maximum), axis=0, keepdims=True))
        return taken, tuple(cms)

    def pv_stage(jp, cms, ms):
        vtj = vt_ref[jp]
        slot = jp % 2
        new_ms = []
        for h in range(ATT_HEADS):
            m_new = jnp.maximum(ms[h], cms[h])
            shift = jnp.where(m_new == neg_inf, 0.0, m_new)
            p = jnp.exp2(lg_ref[slot, h] - shift).astype(BF16)
            alpha = jnp.exp2(ms[h] - shift)
            acc_ref[h] = alpha * acc_ref[h] + jnp.dot(vtj, p, preferred_element_type=F32)
            new_ms.append(m_new)
        return tuple(new_ms)

    def att_body(j, carry):
        taken, cms_prev, ms = carry
        ms = pv_stage(jnp.maximum(j - 1, 0) + (j == 0).astype(jnp.int32), cms_prev, ms)
        taken, cms = logits_stage(j, taken)
        return taken, cms, ms

    _, cms_last, ms = lax.fori_loop(0, nj, att_body, (jnp.zeros((1, TQ), F32), no_max, no_max))
    pv_stage(qi, cms_last, ms)
    for pair in range(ATT_HEADS // 2):
        halves = []
        for h in (2 * pair, 2 * pair + 1):
            a = acc_ref[h]
            halves.append(a[:HEAD_DIM, :] * (1.0 / a[HEAD_DIM:HEAD_DIM + 1, :]))
        blk = jnp.concatenate(halves, axis=0)
        o_ref[:, pair * LANES:(pair + 1) * LANES] = blk.T.astype(BF16)


def _dsa(q, iq, misc, k, ik, vt, b, s, topk):
    nq = s // TQ
    tile = lambda bi, qi: (bi * nq + qi, 0)
    full = lambda bi, qi: (bi, 0)
    return pl.pallas_call(
        functools.partial(_dsa_kernel, topk=topk),
        grid=(b, nq),
        in_specs=[
            pl.BlockSpec((TQ, ATT_WIDTH), tile),
            pl.BlockSpec((TQ, IDX_HEADS * IDX_DIM), tile),
            pl.BlockSpec((TQ, LANES), tile),
            pl.BlockSpec((s, HEAD_DIM), full),
            pl.BlockSpec((s, IDX_DIM), full),
            pl.BlockSpec((s // KC, LANES, KC), lambda bi, qi: (bi, 0, 0)),
        ],
        out_specs=pl.BlockSpec((TQ, ATT_WIDTH), tile),
        out_shape=jax.ShapeDtypeStruct((b * s, ATT_WIDTH), BF16),
        scratch_shapes=[
            pltpu.VMEM((nq, KC, TQ), F32),
            pltpu.VMEM((2, ATT_HEADS, KC, TQ), F32),
            pltpu.VMEM((ATT_HEADS, LANES, TQ), F32),
        ],
        compiler_params=pltpu.CompilerParams(
            dimension_semantics=("parallel", "arbitrary"), vmem_limit_bytes=VMEM_LIMIT),
        name="dsa_attention",
    )(q, iq, misc, k, ik, vt)


SSD_TT = 512


def _ssd_kernel(xc_ref, zs_ref, misc_ref, dtb_ref, arep_ref, dskip_ref, nw_ref,
                expand_ref, triu_ref, o_ref, state_ref):
    t = pl.program_id(1)
    tt = xc_ref.shape[0]

    @pl.when(t == 0)
    def _():
        state_ref[...] = jnp.zeros_like(state_ref)

    tri = (lax.broadcasted_iota(jnp.int32, (CHUNK, CHUNK), 0)
           >= lax.broadcasted_iota(jnp.int32, (CHUNK, CHUNK), 1))
    left_head = lax.broadcasted_iota(jnp.int32, (CHUNK, LANES), 1) < SSD_HEAD_DIM
    left_head_n = lax.broadcasted_iota(jnp.int32, (D_STATE, LANES), 1) < SSD_HEAD_DIM
    gn = SSD_GROUPS * D_STATE
    pairs_per_group = SSD_HEADS // SSD_GROUPS // 2
    expand = expand_ref[...]
    triu = triu_ref[...]
    zpad = jnp.zeros((SUBLANES, CHUNK), F32)

    for c in range(tt // CHUNK):
        rows = slice(c * CHUNK, (c + 1) * CHUNK)
        xs = xc_ref[rows, 0:SSD_WIDTH]
        bm = xc_ref[rows, SSD_WIDTH:SSD_WIDTH + gn]
        cm = xc_ref[rows, SSD_WIDTH + gn:SSD_WIDTH + 2 * gn]
        raw = misc_ref[rows, :].T[MISC_DT:MISC_DT + SSD_HEADS, :] + dtb_ref[...]
        dt_t = jnp.maximum(raw, 0.0) + jnp.log1p(jnp.exp(-jnp.abs(raw)))
        adt = dt_t * arep_ref[...]
        hi = adt.astype(BF16).astype(F32)
        r1 = adt - hi
        mid = r1.astype(BF16).astype(F32)
        pieces = jnp.concatenate([hi, mid, r1 - mid, zpad], axis=0).astype(BF16)
        cs = jnp.dot(pieces, triu, preferred_element_type=F32)
        acum_t = (cs[0:SUBLANES] + cs[SUBLANES:2 * SUBLANES]) + cs[2 * SUBLANES:3 * SUBLANES]
        a_last = acum_t[:, CHUNK - 1:CHUNK]
        ddt_t = jnp.exp(a_last - acum_t) * dt_t
        acum = jnp.concatenate(
            [acum_t, jnp.zeros((LANES - SSD_HEADS, CHUNK), F32)], axis=0).T
        ea = jnp.exp(acum)
        ea_hi = ea.astype(BF16)
        ea_lo = (ea - ea_hi.astype(F32)).astype(BF16)
        ea_x = (jnp.dot(ea_hi, expand, preferred_element_type=F32)
                + jnp.dot(ea_lo, expand, preferred_element_type=F32))
        bm_t = bm.T
        xs16 = xs.astype(BF16)
        bm16 = bm.astype(BF16)
        cm16 = cm.astype(BF16)
        y_pairs = [None] * (SSD_HEADS // 2)
        for g in range(SSD_GROUPS):
            cg = cm16[:, g * D_STATE:(g + 1) * D_STATE]
            bg = bm16[:, g * D_STATE:(g + 1) * D_STATE]
            bg_t = bm_t[g * D_STATE:(g + 1) * D_STATE, :]
            gmat = _nt_dot(cg, bg)
            for pp in range(pairs_per_group):
                pair = g * pairs_per_group + pp
                lanes = slice(pair * LANES, (pair + 1) * LANES)
                xp = xs16[:, lanes]
                y_halves, s_halves = [], []
                for h in (2 * pair, 2 * pair + 1):
                    col = acum[:, h:h + 1]
                    rowv = acum_t[h:h + 1, :]
                    lmat = jnp.exp(jnp.where(tri, col - rowv, -jnp.inf))
                    mmat = (gmat * lmat * dt_t[h:h + 1, :]).astype(BF16)
                    y_halves.append(jnp.dot(mmat, xp, preferred_element_type=F32))
                    bs = (bg_t * ddt_t[h:h + 1, :]).astype(BF16)
                    s_halves.append(jnp.dot(bs, xp, preferred_element_type=F32))
                y_diag = jnp.where(left_head, y_halves[0], y_halves[1])
                new = jnp.where(left_head_n, s_halves[0], s_halves[1])
                prev = state_ref[pair]
                y_off = jnp.dot(cg, prev.astype(BF16), preferred_element_type=F32) * ea_x[:, lanes]
                state_ref[pair] = prev * ea_x[CHUNK - 1:CHUNK, lanes] + new
                y_pairs[pair] = y_diag + y_off
        y = jnp.concatenate(y_pairs, axis=1) + dskip_ref[...] * xs
        y = y * zs_ref[rows, :]
        ms = jnp.mean(y * y, axis=1, keepdims=True)
        o_ref[rows, :] = (y * lax.rsqrt(ms + LN_EPS) * nw_ref[...]).astype(BF16)


def _ssd(xc, zs, misc, dtb_rep, a_rep, dskip_row, nw_row, b, s):
    tt = min(SSD_TT, s)
    nt = s // tt
    tile = lambda bi, ti: (bi * nt + ti, 0)
    const = lambda bi, ti: (0, 0)
    expand = (jnp.arange(LANES)[:, None] == jnp.arange(SSD_WIDTH)[None, :] // SSD_HEAD_DIM).astype(BF16)
    triu = (jnp.arange(CHUNK)[:, None] <= jnp.arange(CHUNK)[None, :]).astype(BF16)
    return pl.pallas_call(
        _ssd_kernel,
        grid=(b, s // tt),
        in_specs=[
            pl.BlockSpec((tt, CONV_CH), tile),
            pl.BlockSpec((tt, SSD_WIDTH), tile),
            pl.BlockSpec((tt, LANES), tile),
            pl.BlockSpec((SSD_HEADS, CHUNK), const),
            pl.BlockSpec((SSD_HEADS, CHUNK), const),
            pl.BlockSpec((1, SSD_WIDTH), const),
            pl.BlockSpec((1, SSD_WIDTH), const),
            pl.BlockSpec((LANES, SSD_WIDTH), const),
            pl.BlockSpec((CHUNK, CHUNK), const),
        ],
        out_specs=pl.BlockSpec((tt, SSD_WIDTH), tile),
        out_shape=jax.ShapeDtypeStruct((b * s, SSD_WIDTH), BF16),
        scratch_shapes=[
            pltpu.VMEM((SSD_HEADS // 2, D_STATE, LANES), F32),
        ],
        compiler_params=pltpu.CompilerParams(
            dimension_semantics=("parallel", "arbitrary"), vmem_limit_bytes=VMEM_LIMIT),
        name="ssd_mixer",
    )(xc, zs, misc, dtb_rep, a_rep, dskip_row, nw_row, expand, triu)


def _layer_norm(y, g, b):
    mu = jnp.mean(y, axis=1, keepdims=True)
    yc = y - mu
    var = jnp.mean(yc * yc, axis=1, keepdims=True)
    return yc * lax.rsqrt(var + LN_EPS) * g + b


def _out_proj_kernel(att_ref, ssd_ref, x_ref, wa_ref, ws_ref, g_ref, b_ref, h_ref):
    mixed = jnp.dot(att_ref[...], wa_ref[...], preferred_element_type=F32)
    mixed = mixed + jnp.dot(ssd_ref[...], ws_ref[...], preferred_element_type=F32)
    h_ref[...] = _layer_norm(ALPHA * x_ref[...] + mixed, g_ref[...], b_ref[...])


def _out_proj(att2, ssd2, x2, w_att, w_ssd, g_row, b_row, tm):
    n = x2.shape[0]
    row = lambda i: (i, 0)
    const = lambda i: (0, 0)
    return pl.pallas_call(
        _out_proj_kernel,
        grid=(n // tm,),
        in_specs=[
            pl.BlockSpec((tm, ATT_WIDTH), row),
            pl.BlockSpec((tm, SSD_WIDTH), row),
            pl.BlockSpec((tm, D_MODEL), row),
            pl.BlockSpec((ATT_WIDTH, D_MODEL), const),
            pl.BlockSpec((SSD_WIDTH, D_MODEL), const),
            pl.BlockSpec((1, D_MODEL), const),
            pl.BlockSpec((1, D_MODEL), const),
        ],
        out_specs=pl.BlockSpec((tm, D_MODEL), row),
        out_shape=jax.ShapeDtypeStruct((n, D_MODEL), F32),
        compiler_params=pltpu.CompilerParams(
            dimension_semantics=("parallel",), vmem_limit_bytes=VMEM_LIMIT),
        name="out_proj_ln",
    )(att2, ssd2, x2, w_att, w_ssd, g_row, b_row)


ROUTE_E0 = N_GROUPS_MOE
MOE_TM = 1024


def _first_max(vals):
    best = vals[0]
    for v in vals[1:]:
        best = jnp.maximum(best, v)
    idx = jnp.full(best.shape, len(vals) - 1, jnp.int32)
    for i in range(len(vals) - 2, -1, -1):
        idx = jnp.where(vals[i] == best, i, idx)
    return best, idx


def _moe_kernel(h_ref, wrh_ref, wrl_ref, br_ref, wg_ref, wu_ref, wd_ref, g_ref, b_ref, o_ref,
                hb_ref, gates_ref, acc_ref):
    g = pl.program_id(1)
    tm = h_ref.shape[0]

    @pl.when(g == 0)
    def _():
        h = h_ref[...]
        h_hi = h.astype(BF16)
        hb_ref[...] = h_hi
        acc_ref[...] = jnp.zeros_like(acc_ref)
        h_lo = (h - h_hi.astype(F32)).astype(BF16)
        wrh = wrh_ref[...]
        logits = (jnp.dot(h_hi, wrh, preferred_element_type=F32)
                  + jnp.dot(h_lo, wrh, preferred_element_type=F32)
                  + jnp.dot(h_hi, wrl_ref[...], preferred_element_type=F32)) + br_ref[...]
        lt = logits.T
        row = lambda r: lt[r:r + 1, :]
        gl = [row(r) for r in range(N_GROUPS_MOE)]
        gmax, gidx = _first_max(gl)
        denom = jnp.exp(gl[0] - gmax)
        for v in gl[1:]:
            denom = denom + jnp.exp(v - gmax)
        gprob = 1.0 / denom
        el = []
        for k in range(EXPERTS_PER_GROUP):
            v = row(ROUTE_E0 + (N_GROUPS_MOE - 1) * EXPERTS_PER_GROUP + k)
            for gg in range(N_GROUPS_MOE - 2, -1, -1):
                v = jnp.where(gidx == gg, row(ROUTE_E0 + gg * EXPERTS_PER_GROUP + k), v)
            el.append(v)
        l1, i1 = _first_max(el)
        l2, i2 = _first_max([jnp.where(i1 == k, -jnp.inf, el[k]) for k in range(EXPERTS_PER_GROUP)])
        e2 = jnp.exp(l2 - l1)
        w1 = gprob / (1.0 + e2)
        w2 = gprob * e2 / (1.0 + e2)
        rows = []
        for gg in range(N_GROUPS_MOE):
            for k in range(EXPERTS_PER_GROUP):
                wk = jnp.where(i1 == k, w1, jnp.where(i2 == k, w2, 0.0))
                rows.append(jnp.where(gidx == gg, wk, 0.0))
        rows.append(jnp.zeros((LANES - N_EXPERTS, tm), F32))
        gates_ref[...] = jnp.concatenate(rows, axis=0).T

    hb = hb_ref[...]
    gates = gates_ref[...]
    lane = lax.broadcasted_iota(jnp.int32, (tm, LANES), 1)
    for k in range(EXPERTS_PER_GROUP):
        a = jnp.dot(hb, wg_ref[k], preferred_element_type=F32)
        u = jnp.dot(hb, wu_ref[k], preferred_element_type=F32)
        e_lane = EXPERTS_PER_GROUP * g + k
        gate = jnp.sum(jnp.where(lane == e_lane, gates, 0.0), axis=1, keepdims=True)
        hid = (_silu(a) * u * gate).astype(BF16)
        acc_ref[...] += jnp.dot(hid, wd_ref[k], preferred_element_type=F32)

    @pl.when(g == pl.num_programs(1) - 1)
    def _():
        o_ref[...] = _layer_norm(ALPHA * h_ref[...] + acc_ref[...], g_ref[...], b_ref[...])


def _moe(h2, wr_hi, wr_lo, br, wg, wu, wd, g_row, b_row):
    n = h2.shape[0]
    tm = min(MOE_TM, n)
    row = lambda i, g: (i, 0)
    const = lambda i, g: (0, 0)
    grp = lambda i, g: (g, 0, 0)
    return pl.pallas_call(
        _moe_kernel,
        grid=(n // tm, N_GROUPS_MOE),
        in_specs=[
            pl.BlockSpec((tm, D_MODEL), row),
            pl.BlockSpec((D_MODEL, LANES), const),
            pl.BlockSpec((D_MODEL, LANES), const),
            pl.BlockSpec((1, LANES), const),
            pl.BlockSpec((EXPERTS_PER_GROUP, D_MODEL, EXPERT_FF), grp),
            pl.BlockSpec((EXPERTS_PER_GROUP, D_MODEL, EXPERT_FF), grp),
            pl.BlockSpec((EXPERTS_PER_GROUP, EXPERT_FF, D_MODEL), grp),
            pl.BlockSpec((1, D_MODEL), const),
            pl.BlockSpec((1, D_MODEL), const),
        ],
        out_specs=pl.BlockSpec((tm, D_MODEL), row),
        out_shape=jax.ShapeDtypeStruct((n, D_MODEL), F32),
        scratch_shapes=[
            pltpu.VMEM((tm, D_MODEL), BF16),
            pltpu.VMEM((tm, LANES), F32),
            pltpu.VMEM((tm, D_MODEL), F32),
        ],
        compiler_params=pltpu.CompilerParams(
            dimension_semantics=("parallel", "arbitrary"), vmem_limit_bytes=VMEM_LIMIT),
        name="hier_moe_ln",
    )(h2, wr_hi, wr_lo, br, wg, wu, wd, g_row, b_row)


def _rope_tables(seq):
    inv = ROPE_THETA ** (-jnp.arange(0, HEAD_DIM, 2, dtype=F32) / HEAD_DIM)
    ang = jnp.arange(seq, dtype=F32)[:, None] * inv[None, :]
    cos, sin = jnp.cos(ang), jnp.sin(ang)
    zero = jnp.zeros_like(sin)
    cos_t = jnp.tile(cos, (1, LANES // (HEAD_DIM // 2)))
    s1_t = jnp.tile(jnp.concatenate([-sin, zero], 1), (1, LANES // HEAD_DIM))
    s2_t = jnp.tile(jnp.concatenate([zero, sin], 1), (1, LANES // HEAD_DIM))
    return cos_t, s1_t, s2_t


def _permute_w_in(w):
    sizes = (ATT_WIDTH, HEAD_DIM, HEAD_DIM, IDX_HEADS * IDX_DIM, IDX_DIM, IDX_HEADS,
             SSD_WIDTH, CONV_CH, SSD_HEADS)
    pts = np.cumsum((0,) + sizes)
    q, k, v, iq, ik, iw, z, xbc, dt = [w[:, pts[i]:pts[i + 1]] for i in range(len(sizes))]
    d = w.shape[0]
    pad = lambda n: jnp.zeros((d, n), w.dtype)
    misc = jnp.concatenate([v, iw, pad(MISC_DT - MISC_IW - IDX_HEADS), dt,
                            pad(LANES - MISC_DT - SSD_HEADS)], 1)
    return jnp.concatenate([q, iq, k, ik, misc, z, xbc], 1).astype(BF16)


def _head_rep(vals):
    return jnp.broadcast_to(vals.astype(F32)[:, None], (vals.shape[0], CHUNK))


def kernel(x, w_in, conv_w, conv_b, dt_bias, a_log, d_skip, ssd_norm_w, w_out, ln1_g, ln1_b,
           w_route_group, b_route_group, w_route_expert, b_route_expert, w_gate, w_up,
           w_down, ln2_g, ln2_b):
    bsz, seq, d = x.shape
    n = bsz * seq
    topk = min(TOPK_MAX, seq // 4)
    tm = 512
    assert d == D_MODEL and TQ == KC and seq % TQ == 0 and seq % tm == 0 and topk <= KC
    cos_t, s1_t, s2_t = _rope_tables(seq)
    mscale = jnp.ones((1, LANES), F32).at[0, MISC_IW:MISC_IW + IDX_HEADS].set(INDEXER_SCALE)
    for l in range(DEPTH):
        x2 = x.reshape(n, d)
        q, iq, k, ik, vt, misc, zs, xc = _in_proj(
            x2, _permute_w_in(w_in[l]), cos_t, s1_t, s2_t, mscale,
            conv_w[l].astype(F32), conv_b[l].astype(F32)[None, :], seq, tm)
        att = _dsa(q, iq, misc, k, ik, vt, bsz, seq, topk)
        ssd = _ssd(
            xc, zs, misc,
            _head_rep(dt_bias[l]), _head_rep(-jnp.exp(a_log[l].astype(F32))),
            jnp.repeat(d_skip[l].astype(F32), SSD_HEAD_DIM)[None, :], ssd_norm_w[l][None, :],
            bsz, seq)
        w_o = w_out[l].astype(BF16)
        h2 = _out_proj(att, ssd, x2, w_o[:ATT_WIDTH], w_o[ATT_WIDTH:],
                       ln1_g[l][None, :], ln1_b[l][None, :], tm)
        pad = jnp.zeros((d, LANES - ROUTE_E0 - N_EXPERTS), F32)
        wr = jnp.concatenate([w_route_group[l].astype(F32), w_route_expert[l].astype(F32), pad], 1)
        wr_hi = wr.astype(BF16)
        wr_lo = (wr - wr_hi.astype(F32)).astype(BF16)
        br = jnp.concatenate([b_route_group[l].astype(F32), b_route_expert[l].astype(F32),
                              pad[0]])[None, :]
        x = _moe(h2, wr_hi, wr_lo, br, w_gate[l].astype(BF16), w_up[l].astype(BF16),
                 w_down[l].astype(BF16), ln2_g[l][None, :], ln2_b[l][None, :]).reshape(bsz, seq, d)
    return x
```

```python
import functools

import jax
import jax.numpy as jnp
import numpy as np
from jax import lax
from jax.experimental import pallas as pl
from jax.experimental.pallas import tpu as pltpu

F32 = jnp.float32
BF16 = jnp.bfloat16

D_MODEL = 1024
HEAD_DIM = 64
ATT_WIDTH = 512
ATT_HEADS = 8
IDX_HEADS = 4
IDX_DIM = 64
TOPK_MAX = 256
ROPE_THETA = 10000.0
INDEXER_SCALE = (IDX_HEADS ** -0.5) * (IDX_DIM ** -0.5)
SSD_WIDTH = 512
SSD_HEADS = 8
SSD_HEAD_DIM = 64
SSD_GROUPS = 2
D_STATE = 64
CONV_WIDTH = 4
CONV_CH = SSD_WIDTH + 2 * SSD_GROUPS * D_STATE
CHUNK = 128
N_GROUPS_MOE = 4
EXPERTS_PER_GROUP = 4
N_EXPERTS = 16
EXPERT_FF = 256
DEPTH = 1
ALPHA = (2 * DEPTH) ** 0.25
LN_EPS = 1e-5

LANES = 128
SUBLANES = 8
VMEM_LIMIT = 56 * 1024 * 1024

C_Q = 0
C_IQ = 512
C_K = 768
C_IK = 832
C_MISC = 896
MISC_IW = 64
MISC_DT = 72
C_Z = 1024
C_XBC = 1536
IN_COLS = 2304

TQ = 256
KC = 256

Q_SCALE = HEAD_DIM ** -0.5 * float(np.log2(np.e))

NT_DIMS = (((1,), (1,)), ((), ()))


def _nt_dot(a, b):
    return lax.dot_general(a, b, NT_DIMS, preferred_element_type=F32)


def _fold_rows(x, op):
    slabs = [x[r * SUBLANES:(r + 1) * SUBLANES, :] for r in range(x.shape[0] // SUBLANES)]
    while len(slabs) > 1:
        nxt = [op(slabs[i], slabs[i + 1]) for i in range(0, len(slabs) - 1, 2)]
        if len(slabs) % 2:
            nxt.append(slabs[-1])
        slabs = nxt
    return slabs[0]


CONV_PAD = 8


def _silu(x):
    return x * (1.0 / (1.0 + jnp.exp(-x)))


def _in_proj_kernel(x_ref, w_ref, cos_ref, s1_ref, s2_ref, mscale_ref, cw_ref, cb_ref,
                    q_ref, iq_ref, k_ref, ik_ref, vt_ref, misc_ref, zs_ref, xc_ref, hist_ref,
                    *, nblk_seq):
    @pl.when(pl.program_id(0) == 0)
    def _():
        hist_ref[...] = jnp.zeros_like(hist_ref)

    xb = x_ref[...].astype(BF16)
    cos = cos_ref[...]
    s1 = s1_ref[...]
    s2 = s2_ref[...]

    def mm(c0, width):
        return jnp.dot(xb, w_ref[:, c0:c0 + width], preferred_element_type=F32)

    def rope(y):
        fwd = pltpu.roll(y, LANES - HEAD_DIM // 2, 1)
        bwd = pltpu.roll(y, HEAD_DIM // 2, 1)
        return y * cos + fwd * s1 + bwd * s2

    def rope_wide(y, scale):
        parts = []
        for c in range(y.shape[1] // LANES):
            r = rope(y[:, c * LANES:(c + 1) * LANES])
            parts.append(r * scale if scale != 1.0 else r)
        return parts

    xbc = mm(C_XBC, CONV_CH)
    tm = xbc.shape[0]
    seq_start = pl.program_id(0) % nblk_seq == 0
    hist = jnp.where(seq_start, 0.0, hist_ref[...])
    xp = jnp.concatenate([hist, xbc], axis=0)
    acc = cw_ref[0:1, :] * xp
    for j in range(1, CONV_WIDTH):
        acc = pltpu.roll(acc, 1, 0) + cw_ref[j:j + 1, :] * xp
    xc_ref[...] = _silu(acc[CONV_PAD:, :] + cb_ref[...])
    hist_ref[...] = xbc[tm - CONV_PAD:, :]
    zs_ref[...] = _silu(mm(C_Z, SSD_WIDTH))

    for c, r in enumerate(rope_wide(mm(C_Q, ATT_WIDTH), Q_SCALE)):
        q_ref[:, c * LANES:(c + 1) * LANES] = r.astype(BF16)
    for c, r in enumerate(rope_wide(mm(C_IQ, IDX_HEADS * IDX_DIM), 1.0)):
        iq_ref[:, c * LANES:(c + 1) * LANES] = r.astype(BF16)
    kk = rope(mm(C_K, LANES))
    k_ref[...] = kk[:, :HEAD_DIM].astype(BF16)
    ik_ref[...] = kk[:, HEAD_DIM:].astype(BF16)
    misc = mm(C_MISC, LANES) * mscale_ref[...]
    misc_ref[...] = misc
    lane = lax.broadcasted_iota(jnp.int32, misc.shape, 1)
    vext = jnp.where(lane < HEAD_DIM, misc, jnp.where(lane == HEAD_DIM, 1.0, 0.0))
    for c in range(vt_ref.shape[0]):
        vt_ref[c] = vext[c * KC:(c + 1) * KC, :].T.astype(BF16)


def _in_proj(x2, w_perm, cos_t, s1_t, s2_t, mscale, conv_w, conv_b, seq, tm):
    n = x2.shape[0]
    nblk_seq = seq // tm
    row = lambda i: (i, 0)
    tab = lambda i: (i % nblk_seq, 0)
    const = lambda i: (0, 0)
    outs = [
        ((n, ATT_WIDTH), BF16, pl.BlockSpec((tm, ATT_WIDTH), row)),
        ((n, IDX_HEADS * IDX_DIM), BF16, pl.BlockSpec((tm, IDX_HEADS * IDX_DIM), row)),
        ((n, HEAD_DIM), BF16, pl.BlockSpec((tm, HEAD_DIM), row)),
        ((n, IDX_DIM), BF16, pl.BlockSpec((tm, IDX_DIM), row)),
        ((n // KC, LANES, KC), BF16, pl.BlockSpec((tm // KC, LANES, KC), lambda i: (i, 0, 0))),
        ((n, LANES), F32, pl.BlockSpec((tm, LANES), row)),
        ((n, SSD_WIDTH), F32, pl.BlockSpec((tm, SSD_WIDTH), row)),
        ((n, CONV_CH), F32, pl.BlockSpec((tm, CONV_CH), row)),
    ]
    return pl.pallas_call(
        functools.partial(_in_proj_kernel, nblk_seq=nblk_seq),
        grid=(n // tm,),
        in_specs=[
            pl.BlockSpec((tm, D_MODEL), row),
            pl.BlockSpec((D_MODEL, IN_COLS), const),
            pl.BlockSpec((tm, LANES), tab),
            pl.BlockSpec((tm, LANES), tab),
            pl.BlockSpec((tm, LANES), tab),
            pl.BlockSpec((1, LANES), const),
            pl.BlockSpec((CONV_WIDTH, CONV_CH), const),
            pl.BlockSpec((1, CONV_CH), const),
        ],
        out_specs=[spec for _, _, spec in outs],
        out_shape=[jax.ShapeDtypeStruct(shape, dt) for shape, dt, _ in outs],
        scratch_shapes=[pltpu.VMEM((CONV_PAD, CONV_CH), F32)],
        compiler_params=pltpu.CompilerParams(
            dimension_semantics=("arbitrary",), vmem_limit_bytes=VMEM_LIMIT),
        name="in_proj",
    )(x2, w_perm, cos_t, s1_t, s2_t, mscale, conv_w, conv_b)


def _dsa_kernel(q_ref, iq_ref, misc_ref, k_ref, ik_ref, vt_ref, o_ref,
                sc_ref, lg_ref, acc_ref, *, topk):
    qi = pl.program_id(1)
    nj = qi + 1
    neg_inf = -jnp.inf
    kf = float(topk)
    key_i = lax.broadcasted_iota(jnp.int32, (KC, TQ), 0)
    qry_i = lax.broadcasted_iota(jnp.int32, (KC, TQ), 1)
    causal = key_i <= qry_i

    def key_rows(j):
        return pl.ds(pl.multiple_of(j * KC, KC), KC)

    iw_t = misc_ref[...].T[MISC_IW:MISC_IW + SUBLANES, :]

    def scores_body(j, carry):
        ikj = ik_ref[key_rows(j), :]
        sc = None
        for h in range(IDX_HEADS):
            d = _nt_dot(ikj, iq_ref[:, h * IDX_DIM:(h + 1) * IDX_DIM])
            term = iw_t[h:h + 1, :] * jnp.maximum(d, 0.0)
            sc = term if sc is None else sc + term
        sc_ref[j] = sc
        return carry

    lax.fori_loop(0, nj, scores_body, 0)
    sc_ref[qi] = jnp.where(causal, sc_ref[qi], neg_inf)

    def count_ge(cand):
        def body(j, acc):
            ind = jnp.where(sc_ref[j] >= cand, 1.0, 0.0)
            return acc + _fold_rows(ind, jnp.add)

        acc = lax.fori_loop(0, nj, body, jnp.zeros((SUBLANES, TQ), F32))
        return jnp.sum(acc, axis=0, keepdims=True)

    pos = count_ge(jnp.zeros((1, TQ), F32)) >= kf
    sbits = jnp.where(pos, jnp.int32(0), jnp.int32(-2 ** 31))

    def as_float(mag):
        return lax.bitcast_convert_type(mag | sbits, F32)

    def bit_body(b, mag):
        trial = mag | lax.shift_left(jnp.int32(1), jnp.int32(30) - b)
        ok = (count_ge(as_float(trial)) >= kf) == pos
        return jnp.where(ok, trial, mag)

    mag = lax.fori_loop(0, 31, bit_body, jnp.zeros((1, TQ), jnp.int32))
    lo = as_float(jnp.where(pos, mag, mag + 1))
    hi = as_float(jnp.where(pos, mag + 1, mag))
    need = kf - count_ge(hi)

    lower = (lax.broadcasted_iota(jnp.int32, (KC, KC), 0)
             > lax.broadcasted_iota(jnp.int32, (KC, KC), 1)).astype(BF16)

    acc_ref[...] = jnp.zeros_like(acc_ref)
    lg_ref[1] = jnp.full(lg_ref.shape[1:], neg_inf, F32)
    no_max = tuple(jnp.minimum(need, neg_inf) for _ in range(ATT_HEADS))

    def logits_stage(j, taken):
        s = sc_ref[j]
        gt = s >= hi
        eq = jnp.logical_and(s >= lo, jnp.logical_not(gt))
        eqf = jnp.where(eq, 1.0, 0.0)
        before = jnp.dot(lower, eqf.astype(BF16), preferred_element_type=F32) + taken
        sel = jnp.logical_or(gt, jnp.logical_and(eq, before < need))
        sel = jnp.logical_and(sel, jnp.logical_or(causal, j < qi))
        bias = jnp.where(sel, 0.0, neg_inf)
        taken = taken + jnp.sum(_fold_rows(eqf, jnp.add), axis=0, keepdims=True)
        kj = k_ref[key_rows(j), :]
        slot = j % 2
        cms = []
        for h in range(ATT_HEADS):
            lg = _nt_dot(kj, q_ref[:, h * HEAD_DIM:(h + 1) * HEAD_DIM]) + bias
            lg_ref[slot, h] = lg
            cms.append(jnp.max(_fold_rows(lg, jnp.maximum), axis=0, keepdims=True))
        return taken, tuple(cms)

    def pv_stage(jp, cms, ms):
        vtj = vt_ref[jp]
        slot = jp % 2
        new_ms = []
        for h in range(ATT_HEADS):
            m_new = jnp.maximum(ms[h], cms[h])
            shift = jnp.where(m_new == neg_inf, 0.0, m_new)
            p = jnp.exp2(lg_ref[slot, h] - shift).astype(BF16)
            alpha = jnp.exp2(ms[h] - shift)
            acc_ref[h] = alpha * acc_ref[h] + jnp.dot(vtj, p, preferred_element_type=F32)
            new_ms.append(m_new)
        return tuple(new_ms)

    def att_body(j, carry):
        taken, cms_prev, ms = carry
        ms = pv_stage(jnp.maximum(j - 1, 0) + (j == 0).astype(jnp.int32), cms_prev, ms)
        taken, cms = logits_stage(j, taken)
        return taken, cms, ms

    _, cms_last, ms = lax.fori_loop(0, nj, att_body, (jnp.zeros((1, TQ), F32), no_max, no_max))
    pv_stage(qi, cms_last, ms)
    for pair in range(ATT_HEADS // 2):
        halves = []
        for h in (2 * pair, 2 * pair + 1):
            a = acc_ref[h]
            halves.append(a[:HEAD_DIM, :] * (1.0 / a[HEAD_DIM:HEAD_DIM + 1, :]))
        blk = jnp.concatenate(halves, axis=0)
        o_ref[:, pair * LANES:(pair + 1) * LANES] = blk.T.astype(BF16)


def _dsa(q, iq, misc, k, ik, vt, b, s, topk):
    nq = s // TQ
    tile = lambda bi, qi: (bi * nq + qi, 0)
    full = lambda bi, qi: (bi, 0)
    return pl.pallas_call(
        functools.partial(_dsa_kernel, topk=topk),
        grid=(b, nq),
        in_specs=[
            pl.BlockSpec((TQ, ATT_WIDTH), tile),
            pl.BlockSpec((TQ, IDX_HEADS * IDX_DIM), tile),
            pl.BlockSpec((TQ, LANES), tile),
            pl.BlockSpec((s, HEAD_DIM), full),
            pl.BlockSpec((s, IDX_DIM), full),
            pl.BlockSpec((s // KC, LANES, KC), lambda bi, qi: (bi, 0, 0)),
        ],
        out_specs=pl.BlockSpec((TQ, ATT_WIDTH), tile),
        out_shape=jax.ShapeDtypeStruct((b * s, ATT_WIDTH), BF16),
        scratch_shapes=[
            pltpu.VMEM((nq, KC, TQ), F32),
            pltpu.VMEM((2, ATT_HEADS, KC, TQ), F32),
            pltpu.VMEM((ATT_HEADS, LANES, TQ), F32),
        ],
        compiler_params=pltpu.CompilerParams(
            dimension_semantics=("parallel", "arbitrary"), vmem_limit_bytes=VMEM_LIMIT),
        name="dsa_attention",
    )(q, iq, misc, k, ik, vt)


SSD_TT = 512


def _ssd_kernel(xc_ref, zs_ref, misc_ref, dtb_ref, arep_ref, dskip_ref, nw_ref,
                expand_ref, triu_ref, o_ref, state_ref):
    t = pl.program_id(1)
    tt = xc_ref.shape[0]

    @pl.when(t == 0)
    def _():
        state_ref[...] = jnp.zeros_like(state_ref)

    tri = (lax.broadcasted_iota(jnp.int32, (CHUNK, CHUNK), 0)
           >= lax.broadcasted_iota(jnp.int32, (CHUNK, CHUNK), 1))
    left_head = lax.broadcasted_iota(jnp.int32, (CHUNK, LANES), 1) < SSD_HEAD_DIM
    left_head_n = lax.broadcasted_iota(jnp.int32, (D_STATE, LANES), 1) < SSD_HEAD_DIM
    gn = SSD_GROUPS * D_STATE
    pairs_per_group = SSD_HEADS // SSD_GROUPS // 2
    expand = expand_ref[...]
    triu = triu_ref[...]
    zpad = jnp.zeros((SUBLANES, CHUNK), F32)

    for c in range(tt // CHUNK):
        rows = slice(c * CHUNK, (c + 1) * CHUNK)
        xs = xc_ref[rows, 0:SSD_WIDTH]
        bm = xc_ref[rows, SSD_WIDTH:SSD_WIDTH + gn]
        cm = xc_ref[rows, SSD_WIDTH + gn:SSD_WIDTH + 2 * gn]
        raw = misc_ref[rows, :].T[MISC_DT:MISC_DT + SSD_HEADS, :] + dtb_ref[...]
        dt_t = jnp.maximum(raw, 0.0) + jnp.log1p(jnp.exp(-jnp.abs(raw)))
        adt = dt_t * arep_ref[...]
        hi = adt.astype(BF16).astype(F32)
        r1 = adt - hi
        mid = r1.astype(BF16).astype(F32)
        pieces = jnp.concatenate([hi, mid, r1 - mid, zpad], axis=0).astype(BF16)
        cs = jnp.dot(pieces, triu, preferred_element_type=F32)
        acum_t = (cs[0:SUBLANES] + cs[SUBLANES:2 * SUBLANES]) + cs[2 * SUBLANES:3 * SUBLANES]
        a_last = acum_t[:, CHUNK - 1:CHUNK]
        ddt_t = jnp.exp(a_last - acum_t) * dt_t
        acum = jnp.concatenate(
            [acum_t, jnp.zeros((LANES - SSD_HEADS, CHUNK), F32)], axis=0).T
        ea = jnp.exp(acum)
        ea_hi = ea.astype(BF16)
        ea_lo = (ea - ea_hi.astype(F32)).astype(BF16)
        ea_x = (jnp.dot(ea_hi, expand, preferred_element_type=F32)
                + jnp.dot(ea_lo, expand, preferred_element_type=F32))
        bm_t = bm.T
        xs16 = xs.astype(BF16)
        bm16 = bm.astype(BF16)
        cm16 = cm.astype(BF16)
        y_pairs = [None] * (SSD_HEADS // 2)
        for g in range(SSD_GROUPS):
            cg = cm16[:, g * D_STATE:(g + 1) * D_STATE]
            bg = bm16[:, g * D_STATE:(g + 1) * D_STATE]
            bg_t = bm_t[g * D_STATE:(g + 1) * D_STATE, :]
            gmat = _nt_dot(cg, bg)
            for pp in range(pairs_per_group):
                pair = g * pairs_per_group + pp
                lanes = slice(pair * LANES, (pair + 1) * LANES)
                xp = xs16[:, lanes]
                y_halves, s_halves = [], []
                for h in (2 * pair, 2 * pair + 1):
                    col = acum[:, h:h + 1]
                    rowv = acum_t[h:h + 1, :]
                    lmat = jnp.exp(jnp.where(tri, col - rowv, -jnp.inf))
                    mmat = (gmat * lmat * dt_t[h:h + 1, :]).astype(BF16)
                    y_halves.append(jnp.dot(mmat, xp, preferred_element_type=F32))
                    bs = (bg_t * ddt_t[h:h + 1, :]).astype(BF16)
                    s_halves.append(jnp.dot(bs, xp, preferred_element_type=F32))
                y_diag = jnp.where(left_head, y_halves[0], y_halves[1])
                new = jnp.where(left_head_n, s_halves[0], s_halves[1])
                prev = state_ref[pair]
                y_off = jnp.dot(cg, prev.astype(BF16), preferred_element_type=F32) * ea_x[:, lanes]
                state_ref[pair] = prev * ea_x[CHUNK - 1:CHUNK, lanes] + new
                y_pairs[pair] = y_diag + y_off
        y = jnp.concatenate(y_pairs, axis=1) + dskip_ref[...] * xs
        y = y * zs_ref[rows, :]
        ms = jnp.mean(y * y, axis=1, keepdims=True)
        o_ref[rows, :] = (y * lax.rsqrt(ms + LN_EPS) * nw_ref[...]).astype(BF16)


def _ssd(xc, zs, misc, dtb_rep, a_rep, dskip_row, nw_row, b, s):
    tt = min(SSD_TT, s)
    nt = s // tt
    tile = lambda bi, ti: (bi * nt + ti, 0)
    const = lambda bi, ti: (0, 0)
    expand = (jnp.arange(LANES)[:, None] == jnp.arange(SSD_WIDTH)[None, :] // SSD_HEAD_DIM).astype(BF16)
    triu = (jnp.arange(CHUNK)[:, None] <= jnp.arange(CHUNK)[None, :]).astype(BF16)
    return pl.pallas_call(
        _ssd_kernel,
        grid=(b, s // tt),
        in_specs=[
            pl.BlockSpec((tt, CONV_CH), tile),
            pl.BlockSpec((tt, SSD_WIDTH), tile),
            pl.BlockSpec((tt, LANES), tile),
            pl.BlockSpec((SSD_HEADS, CHUNK), const),
            pl.BlockSpec((SSD_HEADS, CHUNK), const),
            pl.BlockSpec((1, SSD_WIDTH), const),
            pl.BlockSpec((1, SSD_WIDTH), const),
            pl.BlockSpec((LANES, SSD_WIDTH), const),
            pl.BlockSpec((CHUNK, CHUNK), const),
        ],
        out_specs=pl.BlockSpec((tt, SSD_WIDTH), tile),
        out_shape=jax.ShapeDtypeStruct((b * s, SSD_WIDTH), BF16),
        scratch_shapes=[
            pltpu.VMEM((SSD_HEADS // 2, D_STATE, LANES), F32),
        ],
        compiler_params=pltpu.CompilerParams(
            dimension_semantics=("parallel", "arbitrary"), vmem_limit_bytes=VMEM_LIMIT),
        name="ssd_mixer",
    )(xc, zs, misc, dtb_rep, a_rep, dskip_row, nw_row, expand, triu)


def _layer_norm(y, g, b):
    mu = jnp.mean(y, axis=1, keepdims=True)
    yc = y - mu
    var = jnp.mean(yc * yc, axis=1, keepdims=True)
    return yc * lax.rsqrt(var + LN_EPS) * g + b


OUT_TM = 1024
OUT_SUB = 256


def _out_proj_kernel(att_ref, ssd_ref, x_ref, wa_ref, ws_ref, g_ref, b_ref, h_ref):
    for r0 in range(0, x_ref.shape[0], OUT_SUB):
        rows = slice(r0, r0 + OUT_SUB)
        mixed = jnp.dot(att_ref[rows, :], wa_ref[...], preferred_element_type=F32)
        mixed = mixed + jnp.dot(ssd_ref[rows, :], ws_ref[...], preferred_element_type=F32)
        h_ref[rows, :] = _layer_norm(ALPHA * x_ref[rows, :] + mixed, g_ref[...], b_ref[...])


def _out_proj(att2, ssd2, x2, w_att, w_ssd, g_row, b_row, tm):
    n = x2.shape[0]
    row = lambda i: (i, 0)
    const = lambda i: (0, 0)
    return pl.pallas_call(
        _out_proj_kernel,
        grid=(n // tm,),
        in_specs=[
            pl.BlockSpec((tm, ATT_WIDTH), row),
            pl.BlockSpec((tm, SSD_WIDTH), row),
            pl.BlockSpec((tm, D_MODEL), row),
            pl.BlockSpec((ATT_WIDTH, D_MODEL), const),
            pl.BlockSpec((SSD_WIDTH, D_MODEL), const),
            pl.BlockSpec((1, D_MODEL), const),
            pl.BlockSpec((1, D_MODEL), const),
        ],
        out_specs=pl.BlockSpec((tm, D_MODEL), row),
        out_shape=jax.ShapeDtypeStruct((n, D_MODEL), F32),
        compiler_params=pltpu.CompilerParams(
            dimension_semantics=("parallel",), vmem_limit_bytes=VMEM_LIMIT),
        name="out_proj_ln",
    )(att2, ssd2, x2, w_att, w_ssd, g_row, b_row)


ROUTE_E0 = N_GROUPS_MOE
MOE_TM = 1024


def _first_max(vals):
    best = vals[0]
    for v in vals[1:]:
        best = jnp.maximum(best, v)
    idx = jnp.full(best.shape, len(vals) - 1, jnp.int32)
    for i in range(len(vals) - 2, -1, -1):
        idx = jnp.where(vals[i] == best, i, idx)
    return best, idx


def _moe_kernel(h_ref, wrh_ref, wrl_ref, br_ref, wg_ref, wu_ref, wd_ref, g_ref, b_ref, o_ref,
                hb_ref, gates_ref, acc_ref):
    g = pl.program_id(1)
    tm = h_ref.shape[0]

    @pl.when(g == 0)
    def _():
        h = h_ref[...]
        h_hi = h.astype(BF16)
        hb_ref[...] = h_hi
        acc_ref[...] = jnp.zeros_like(acc_ref)
        h_lo = (h - h_hi.astype(F32)).astype(BF16)
        wrh = wrh_ref[...]
        logits = (jnp.dot(h_hi, wrh, preferred_element_type=F32)
                  + jnp.dot(h_lo, wrh, preferred_element_type=F32)
                  + jnp.dot(h_hi, wrl_ref[...], preferred_element_type=F32)) + br_ref[...]
        lt = logits.T
        row = lambda r: lt[r:r + 1, :]
        gl = [row(r) for r in range(N_GROUPS_MOE)]
        gmax, gidx = _first_max(gl)
        denom = jnp.exp(gl[0] - gmax)
        for v in gl[1:]:
            denom = denom + jnp.exp(v - gmax)
        gprob = 1.0 / denom
        el = []
        for k in range(EXPERTS_PER_GROUP):
            v = row(ROUTE_E0 + (N_GROUPS_MOE - 1) * EXPERTS_PER_GROUP + k)
            for gg in range(N_GROUPS_MOE - 2, -1, -1):
                v = jnp.where(gidx == gg, row(ROUTE_E0 + gg * EXPERTS_PER_GROUP + k), v)
            el.append(v)
        l1, i1 = _first_max(el)
        l2, i2 = _first_max([jnp.where(i1 == k, -jnp.inf, el[k]) for k in range(EXPERTS_PER_GROUP)])
        e2 = jnp.exp(l2 - l1)
        w1 = gprob / (1.0 + e2)
        w2 = gprob * e2 / (1.0 + e2)
        rows = []
        for gg in range(N_GROUPS_MOE):
            for k in range(EXPERTS_PER_GROUP):
                wk = jnp.where(i1 == k, w1, jnp.where(i2 == k, w2, 0.0))
                rows.append(jnp.where(gidx == gg, wk, 0.0))
        rows.append(jnp.zeros((LANES - N_EXPERTS, tm), F32))
        gates_ref[...] = jnp.concatenate(rows, axis=0).T

    hb = hb_ref[...]
    gates = gates_ref[...]
    lane = lax.broadcasted_iota(jnp.int32, (tm, LANES), 1)
    for k in range(EXPERTS_PER_GROUP):
        a = jnp.dot(hb, wg_ref[k], preferred_element_type=F32)
        u = jnp.dot(hb, wu_ref[k], preferred_element_type=F32)
        e_lane = EXPERTS_PER_GROUP * g + k
        gate = jnp.sum(jnp.where(lane == e_lane, gates, 0.0), axis=1, keepdims=True)
        hid = (_silu(a) * u * gate).astype(BF16)
        acc_ref[...] += jnp.dot(hid, wd_ref[k], preferred_element_type=F32)

    @pl.when(g == pl.num_programs(1) - 1)
    def _():
        o_ref[...] = _layer_norm(ALPHA * h_ref[...] + acc_ref[...], g_ref[...], b_ref[...])


def _moe(h2, wr_hi, wr_lo, br, wg, wu, wd, g_row, b_row):
    n = h2.shape[0]
    tm = min(MOE_TM, n)
    row = lambda i, g: (i, 0)
    const = lambda i, g: (0, 0)
    grp = lambda i, g: (g, 0, 0)
    return pl.pallas_call(
        _moe_kernel,
        grid=(n // tm, N_GROUPS_MOE),
        in_specs=[
            pl.BlockSpec((tm, D_MODEL), row),
            pl.BlockSpec((D_MODEL, LANES), const),
            pl.BlockSpec((D_MODEL, LANES), const),
            pl.BlockSpec((1, LANES), const),
            pl.BlockSpec((EXPERTS_PER_GROUP, D_MODEL, EXPERT_FF), grp),
            pl.BlockSpec((EXPERTS_PER_GROUP, D_MODEL, EXPERT_FF), grp),
            pl.BlockSpec((EXPERTS_PER_GROUP, EXPERT_FF, D_MODEL), grp),
            pl.BlockSpec((1, D_MODEL), const),
            pl.BlockSpec((1, D_MODEL), const),
        ],
        out_specs=pl.BlockSpec((tm, D_MODEL), row),
        out_shape=jax.ShapeDtypeStruct((n, D_MODEL), F32),
        scratch_shapes=[
            pltpu.VMEM((tm, D_MODEL), BF16),
            pltpu.VMEM((tm, LANES), F32),
            pltpu.VMEM((tm, D_MODEL), F32),
        ],
        compiler_params=pltpu.CompilerParams(
            dimension_semantics=("parallel", "arbitrary"), vmem_limit_bytes=VMEM_LIMIT),
        name="hier_moe_ln",
    )(h2, wr_hi, wr_lo, br, wg, wu, wd, g_row, b_row)


def _rope_tables(seq):
    inv = ROPE_THETA ** (-jnp.arange(0, HEAD_DIM, 2, dtype=F32) / HEAD_DIM)
    ang = jnp.arange(seq, dtype=F32)[:, None] * inv[None, :]
    cos, sin = jnp.cos(ang), jnp.sin(ang)
    zero = jnp.zeros_like(sin)
    cos_t = jnp.tile(cos, (1, LANES // (HEAD_DIM // 2)))
    s1_t = jnp.tile(jnp.concatenate([-sin, zero], 1), (1, LANES // HEAD_DIM))
    s2_t = jnp.tile(jnp.concatenate([zero, sin], 1), (1, LANES // HEAD_DIM))
    return cos_t, s1_t, s2_t


def _permute_w_in(w):
    sizes = (ATT_WIDTH, HEAD_DIM, HEAD_DIM, IDX_HEADS * IDX_DIM, IDX_DIM, IDX_HEADS,
             SSD_WIDTH, CONV_CH, SSD_HEADS)
    pts = np.cumsum((0,) + sizes)
    q, k, v, iq, ik, iw, z, xbc, dt = [w[:, pts[i]:pts[i + 1]] for i in range(len(sizes))]
    d = w.shape[0]
    pad = lambda n: jnp.zeros((d, n), w.dtype)
    misc = jnp.concatenate([v, iw, pad(MISC_DT - MISC_IW - IDX_HEADS), dt,
                            pad(LANES - MISC_DT - SSD_HEADS)], 1)
    return jnp.concatenate([q, iq, k, ik, misc, z, xbc], 1).astype(BF16)


def _head_rep(vals):
    return jnp.broadcast_to(vals.astype(F32)[:, None], (vals.shape[0], CHUNK))


def kernel(x, w_in, conv_w, conv_b, dt_bias, a_log, d_skip, ssd_norm_w, w_out, ln1_g, ln1_b,
           w_route_group, b_route_group, w_route_expert, b_route_expert, w_gate, w_up,
           w_down, ln2_g, ln2_b):
    bsz, seq, d = x.shape
    n = bsz * seq
    topk = min(TOPK_MAX, seq // 4)
    tm = 512
    assert d == D_MODEL and TQ == KC and seq % TQ == 0 and seq % tm == 0 and topk <= KC
    cos_t, s1_t, s2_t = _rope_tables(seq)
    mscale = jnp.ones((1, LANES), F32).at[0, MISC_IW:MISC_IW + IDX_HEADS].set(INDEXER_SCALE)
    for l in range(DEPTH):
        x2 = x.reshape(n, d)
        q, iq, k, ik, vt, misc, zs, xc = _in_proj(
            x2, _permute_w_in(w_in[l]), cos_t, s1_t, s2_t, mscale,
            conv_w[l].astype(F32), conv_b[l].astype(F32)[None, :], seq, tm)
        att = _dsa(q, iq, misc, k, ik, vt, bsz, seq, topk)
        ssd = _ssd(
            xc, zs, misc,
            _head_rep(dt_bias[l]), _head_rep(-jnp.exp(a_log[l].astype(F32))),
            jnp.repeat(d_skip[l].astype(F32), SSD_HEAD_DIM)[None, :], ssd_norm_w[l][None, :],
            bsz, seq)
        w_o = w_out[l].astype(BF16)
        h2 = _out_proj(att, ssd, x2, w_o[:ATT_WIDTH], w_o[ATT_WIDTH:],
                       ln1_g[l][None, :], ln1_b[l][None, :], min(OUT_TM, n))
        pad = jnp.zeros((d, LANES - ROUTE_E0 - N_EXPERTS), F32)
        wr = jnp.concatenate([w_route_group[l].astype(F32), w_route_expert[l].astype(F32), pad], 1)
        wr_hi = wr.astype(BF16)
        wr_lo = (wr - wr_hi.astype(F32)).astype(BF16)
        br = jnp.concatenate([b_route_group[l].astype(F32), b_route_expert[l].astype(F32),
                              pad[0]])[None, :]
        x = _moe(h2, wr_hi, wr_lo, br, w_gate[l].astype(BF16), w_up[l].astype(BF16),
                 w_down[l].astype(BF16), ln2_g[l][None, :], ln2_b[l][None, :]).reshape(bsz, seq, d)
    return x
```

```python
import functools

import jax
import jax.numpy as jnp
import numpy as np
from jax import lax
from jax.experimental import pallas as pl
from jax.experimental.pallas import tpu as pltpu

F32 = jnp.float32
BF16 = jnp.bfloat16

D_MODEL = 1024
HEAD_DIM = 64
ATT_WIDTH = 512
ATT_HEADS = 8
IDX_HEADS = 4
IDX_DIM = 64
TOPK_MAX = 256
ROPE_THETA = 10000.0
INDEXER_SCALE = (IDX_HEADS ** -0.5) * (IDX_DIM ** -0.5)
SSD_WIDTH = 512
SSD_HEADS = 8
SSD_HEAD_DIM = 64
SSD_GROUPS = 2
D_STATE = 64
CONV_WIDTH = 4
CONV_CH = SSD_WIDTH + 2 * SSD_GROUPS * D_STATE
CHUNK = 128
N_GROUPS_MOE = 4
EXPERTS_PER_GROUP = 4
N_EXPERTS = 16
EXPERT_FF = 256
DEPTH = 1
ALPHA = (2 * DEPTH) ** 0.25
LN_EPS = 1e-5

LANES = 128
SUBLANES = 8
VMEM_LIMIT = 56 * 1024 * 1024

C_Q = 0
C_IQ = 512
C_K = 768
C_IK = 832
C_MISC = 896
MISC_IW = 64
MISC_DT = 72
C_Z = 1024
C_XBC = 1536
IN_COLS = 2304

TQ = 256
KC = 256

Q_SCALE = HEAD_DIM ** -0.5 * float(np.log2(np.e))

NT_DIMS = (((1,), (1,)), ((), ()))


def _nt_dot(a, b):
    return lax.dot_general(a, b, NT_DIMS, preferred_element_type=F32)


def _fold_rows(x, op):
    slabs = [x[r * SUBLANES:(r + 1) * SUBLANES, :] for r in range(x.shape[0] // SUBLANES)]
    while len(slabs) > 1:
        nxt = [op(slabs[i], slabs[i + 1]) for i in range(0, len(slabs) - 1, 2)]
        if len(slabs) % 2:
            nxt.append(slabs[-1])
        slabs = nxt
    return slabs[0]


CONV_PAD = 8


def _silu(x):
    return x * (1.0 / (1.0 + jnp.exp(-x)))


def _in_proj_kernel(x_ref, w_ref, cos_ref, s1_ref, s2_ref, mscale_ref, cw_ref, cb_ref,
                    q_ref, iq_ref, k_ref, ik_ref, vt_ref, misc_ref, zs_ref, xc_ref, hist_ref,
                    *, nblk_seq):
    @pl.when(pl.program_id(0) == 0)
    def _():
        hist_ref[...] = jnp.zeros_like(hist_ref)

    xb = x_ref[...].astype(BF16)
    cos = cos_ref[...]
    s1 = s1_ref[...]
    s2 = s2_ref[...]

    def mm(c0, width):
        return jnp.dot(xb, w_ref[:, c0:c0 + width], preferred_element_type=F32)

    def rope(y):
        fwd = pltpu.roll(y, LANES - HEAD_DIM // 2, 1)
        bwd = pltpu.roll(y, HEAD_DIM // 2, 1)
        return y * cos + fwd * s1 + bwd * s2

    def rope_wide(y, scale):
        parts = []
        for c in range(y.shape[1] // LANES):
            r = rope(y[:, c * LANES:(c + 1) * LANES])
            parts.append(r * scale if scale != 1.0 else r)
        return parts

    xbc = mm(C_XBC, CONV_CH)
    tm = xbc.shape[0]
    seq_start = pl.program_id(0) % nblk_seq == 0
    hist = jnp.where(seq_start, 0.0, hist_ref[...])
    xp = jnp.concatenate([hist, xbc], axis=0)
    acc = cw_ref[0:1, :] * xp
    for j in range(1, CONV_WIDTH):
        acc = pltpu.roll(acc, 1, 0) + cw_ref[j:j + 1, :] * xp
    xc_ref[...] = _silu(acc[CONV_PAD:, :] + cb_ref[...])
    hist_ref[...] = xbc[tm - CONV_PAD:, :]
    zs_ref[...] = _silu(mm(C_Z, SSD_WIDTH))

    for c, r in enumerate(rope_wide(mm(C_Q, ATT_WIDTH), Q_SCALE)):
        q_ref[:, c * LANES:(c + 1) * LANES] = r.astype(BF16)
    for c, r in enumerate(rope_wide(mm(C_IQ, IDX_HEADS * IDX_DIM), 1.0)):
        iq_ref[:, c * LANES:(c + 1) * LANES] = r.astype(BF16)
    kk = rope(mm(C_K, LANES))
    k_ref[...] = kk[:, :HEAD_DIM].astype(BF16)
    ik_ref[...] = kk[:, HEAD_DIM:].astype(BF16)
    misc = mm(C_MISC, LANES) * mscale_ref[...]
    misc_ref[...] = misc
    lane = lax.broadcasted_iota(jnp.int32, misc.shape, 1)
    vext = jnp.where(lane < HEAD_DIM, misc, jnp.where(lane == HEAD_DIM, 1.0, 0.0))
    for c in range(vt_ref.shape[0]):
        vt_ref[c] = vext[c * KC:(c + 1) * KC, :].T.astype(BF16)


def _in_proj(x2, w_perm, cos_t, s1_t, s2_t, mscale, conv_w, conv_b, seq, tm):
    n = x2.shape[0]
    nblk_seq = seq // tm
    row = lambda i: (i, 0)
    tab = lambda i: (i % nblk_seq, 0)
    const = lambda i: (0, 0)
    outs = [
        ((n, ATT_WIDTH), BF16, pl.BlockSpec((tm, ATT_WIDTH), row)),
        ((n, IDX_HEADS * IDX_DIM), BF16, pl.BlockSpec((tm, IDX_HEADS * IDX_DIM), row)),
        ((n, HEAD_DIM), BF16, pl.BlockSpec((tm, HEAD_DIM), row)),
        ((n, IDX_DIM), BF16, pl.BlockSpec((tm, IDX_DIM), row)),
        ((n // KC, LANES, KC), BF16, pl.BlockSpec((tm // KC, LANES, KC), lambda i: (i, 0, 0))),
        ((n, LANES), F32, pl.BlockSpec((tm, LANES), row)),
        ((n, SSD_WIDTH), F32, pl.BlockSpec((tm, SSD_WIDTH), row)),
        ((n, CONV_CH), F32, pl.BlockSpec((tm, CONV_CH), row)),
    ]
    return pl.pallas_call(
        functools.partial(_in_proj_kernel, nblk_seq=nblk_seq),
        grid=(n // tm,),
        in_specs=[
            pl.BlockSpec((tm, D_MODEL), row),
            pl.BlockSpec((D_MODEL, IN_COLS), const),
            pl.BlockSpec((tm, LANES), tab),
            pl.BlockSpec((tm, LANES), tab),
            pl.BlockSpec((tm, LANES), tab),
            pl.BlockSpec((1, LANES), const),
            pl.BlockSpec((CONV_WIDTH, CONV_CH), const),
            pl.BlockSpec((1, CONV_CH), const),
        ],
        out_specs=[spec for _, _, spec in outs],
        out_shape=[jax.ShapeDtypeStruct(shape, dt) for shape, dt, _ in outs],
        scratch_shapes=[pltpu.VMEM((CONV_PAD, CONV_CH), F32)],
        compiler_params=pltpu.CompilerParams(
            dimension_semantics=("arbitrary",), vmem_limit_bytes=VMEM_LIMIT),
        name="in_proj",
    )(x2, w_perm, cos_t, s1_t, s2_t, mscale, conv_w, conv_b)


def _dsa_kernel(q_ref, iq_ref, misc_ref, k_ref, ik_ref, vt_ref, o_ref,
                sc_ref, lg_ref, acc_ref, *, topk):
    qi = pl.program_id(1)
    nj = qi + 1
    neg_inf = -jnp.inf
    kf = float(topk)
    key_i = lax.broadcasted_iota(jnp.int32, (KC, TQ), 0)
    qry_i = lax.broadcasted_iota(jnp.int32, (KC, TQ), 1)
    causal = key_i <= qry_i

    def key_rows(j):
        return pl.ds(pl.multiple_of(j * KC, KC), KC)

    iw_t = misc_ref[...].T[MISC_IW:MISC_IW + SUBLANES, :]

    def scores_body(j, carry):
        ikj = ik_ref[key_rows(j), :]
        sc = None
        for h in range(IDX_HEADS):
            d = _nt_dot(ikj, iq_ref[:, h * IDX_DIM:(h + 1) * IDX_DIM])
            term = iw_t[h:h + 1, :] * jnp.maximum(d, 0.0)
            sc = term if sc is None else sc + term
        sc_ref[j] = sc
        return carry

    lax.fori_loop(0, nj, scores_body, 0)
    sc_ref[qi] = jnp.where(causal, sc_ref[qi], neg_inf)

    def count_ge(cand):
        def body(j, acc):
            ind = jnp.where(sc_ref[j] >= cand, 1.0, 0.0)
            return acc + _fold_rows(ind, jnp.add)

        acc = lax.fori_loop(0, nj, body, jnp.zeros((SUBLANES, TQ), F32))
        return jnp.sum(acc, axis=0, keepdims=True)

    pos = count_ge(jnp.zeros((1, TQ), F32)) >= kf
    sbits = jnp.where(pos, jnp.int32(0), jnp.int32(-2 ** 31))

    def as_float(mag):
        return lax.bitcast_convert_type(mag | sbits, F32)

    def bit_body(b, mag):
        trial = mag | lax.shift_left(jnp.int32(1), jnp.int32(30) - b)
        ok = (count_ge(as_float(trial)) >= kf) == pos
        return jnp.where(ok, trial, mag)

    mag = lax.fori_loop(0, 31, bit_body, jnp.zeros((1, TQ), jnp.int32))
    lo = as_float(jnp.where(pos, mag, mag + 1))
    hi = as_float(jnp.where(pos, mag + 1, mag))
    need = kf - count_ge(hi)

    lower = (lax.broadcasted_iota(jnp.int32, (KC, KC), 0)
             > lax.broadcasted_iota(jnp.int32, (KC, KC), 1)).astype(BF16)

    acc_ref[...] = jnp.zeros_like(acc_ref)
    lg_ref[1] = jnp.full(lg_ref.shape[1:], neg_inf, F32)
    no_max = tuple(jnp.minimum(need, neg_inf) for _ in range(ATT_HEADS))

    def logits_stage(j, taken):
        s = sc_ref[j]
        gt = s >= hi
        eq = jnp.logical_and(s >= lo, jnp.logical_not(gt))
        eqf = jnp.where(eq, 1.0, 0.0)
        before = jnp.dot(lower, eqf.astype(BF16), preferred_element_type=F32) + taken
        sel = jnp.logical_or(gt, jnp.logical_and(eq, before < need))
        sel = jnp.logical_and(sel, jnp.logical_or(causal, j < qi))
        bias = jnp.where(sel, 0.0, neg_inf)
        taken = taken + jnp.sum(_fold_rows(eqf, jnp.add), axis=0, keepdims=True)
        kj = k_ref[key_rows(j), :]
        slot = j % 2
        cms = []
        for h in range(ATT_HEADS):
            lg = _nt_dot(kj, q_ref[:, h * HEAD_DIM:(h + 1) * HEAD_DIM]) + bias
            lg_ref[slot, h] = lg
            cms.append(jnp.max(_fold_rows(lg, jnp.maximum), axis=0, keepdims=True))
        return taken, tuple(cms)

    def pv_stage(jp, cms, ms):
        vtj = vt_ref[jp]
        slot = jp % 2
        new_ms = []
        for h in range(ATT_HEADS):
            m_new = jnp.maximum(ms[h], cms[h])
            shift = jnp.where(m_new == neg_inf, 0.0, m_new)
            p = jnp.exp2(lg_ref[slot, h] - shift).astype(BF16)
            alpha = jnp.exp2(ms[h] - shift)
            acc_ref[h] = alpha * acc_ref[h] + jnp.dot(vtj, p, preferred_element_type=F32)
            new_ms.append(m_new)
        return tuple(new_ms)

    def att_body(j, carry):
        taken, cms_prev, ms = carry
        ms = pv_stage(jnp.maximum(j - 1, 0) + (j == 0).astype(jnp.int32), cms_prev, ms)
        taken, cms = logits_stage(j, taken)
        return taken, cms, ms

    _, cms_last, ms = lax.fori_loop(0, nj, att_body, (jnp.zeros((1, TQ), F32), no_max, no_max))
    pv_stage(qi, cms_last, ms)
    for pair in range(ATT_HEADS // 2):
        halves = []
        for h in (2 * pair, 2 * pair + 1):
            a = acc_ref[h]
            halves.append(a[:HEAD_DIM, :] * (1.0 / a[HEAD_DIM:HEAD_DIM + 1, :]))
        blk = jnp.concatenate(halves, axis=0)
        o_ref[:, pair * LANES:(pair + 1) * LANES] = blk.T.astype(BF16)


def _dsa(q, iq, misc, k, ik, vt, b, s, topk):
    nq = s // TQ
    tile = lambda bi, qi: (bi * nq + qi, 0)
    full = lambda bi, qi: (bi, 0)
    return pl.pallas_call(
        functools.partial(_dsa_kernel, topk=topk),
        grid=(b, nq),
        in_specs=[
            pl.BlockSpec((TQ, ATT_WIDTH), tile),
            pl.BlockSpec((TQ, IDX_HEADS * IDX_DIM), tile),
            pl.BlockSpec((TQ, LANES), tile),
            pl.BlockSpec((s, HEAD_DIM), full),
            pl.BlockSpec((s, IDX_DIM), full),
            pl.BlockSpec((s // KC, LANES, KC), lambda bi, qi: (bi, 0, 0)),
        ],
        out_specs=pl.BlockSpec((TQ, ATT_WIDTH), tile),
        out_shape=jax.ShapeDtypeStruct((b * s, ATT_WIDTH), BF16),
        scratch_shapes=[
            pltpu.VMEM((nq, KC, TQ), F32),
            pltpu.VMEM((2, ATT_HEADS, KC, TQ), F32),
            pltpu.VMEM((ATT_HEADS, LANES, TQ), F32),
        ],
        compiler_params=pltpu.CompilerParams(
            dimension_semantics=("parallel", "arbitrary"), vmem_limit_bytes=VMEM_LIMIT),
        name="dsa_attention",
    )(q, iq, misc, k, ik, vt)


SSD_TT = 512


def _ssd_kernel(xc_ref, zs_ref, misc_ref, dtb_ref, arep_ref, dskip_ref, nw_ref,
                expand_ref, triu_ref, o_ref, state_ref):
    t = pl.program_id(1)
    tt = xc_ref.shape[0]

    @pl.when(t == 0)
    def _():
        state_ref[...] = jnp.zeros_like(state_ref)

    tri = (lax.broadcasted_iota(jnp.int32, (CHUNK, CHUNK), 0)
           >= lax.broadcasted_iota(jnp.int32, (CHUNK, CHUNK), 1))
    left_head = lax.broadcasted_iota(jnp.int32, (CHUNK, LANES), 1) < SSD_HEAD_DIM
    left_head_n = lax.broadcasted_iota(jnp.int32, (D_STATE, LANES), 1) < SSD_HEAD_DIM
    gn = SSD_GROUPS * D_STATE
    pairs_per_group = SSD_HEADS // SSD_GROUPS // 2
    expand = expand_ref[...]
    triu = triu_ref[...]
    zpad = jnp.zeros((SUBLANES, CHUNK), F32)

    for c in range(tt // CHUNK):
        rows = slice(c * CHUNK, (c + 1) * CHUNK)
        xs = xc_ref[rows, 0:SSD_WIDTH]
        bm = xc_ref[rows, SSD_WIDTH:SSD_WIDTH + gn]
        cm = xc_ref[rows, SSD_WIDTH + gn:SSD_WIDTH + 2 * gn]
        raw = misc_ref[rows, :].T[MISC_DT:MISC_DT + SSD_HEADS, :] + dtb_ref[...]
        dt_t = jnp.maximum(raw, 0.0) + jnp.log1p(jnp.exp(-jnp.abs(raw)))
        adt = dt_t * arep_ref[...]
        hi = adt.astype(BF16).astype(F32)
        r1 = adt - hi
        mid = r1.astype(BF16).astype(F32)
        pieces = jnp.concatenate([hi, mid, r1 - mid, zpad], axis=0).astype(BF16)
        cs = jnp.dot(pieces, triu, preferred_element_type=F32)
        acum_t = (cs[0:SUBLANES] + cs[SUBLANES:2 * SUBLANES]) + cs[2 * SUBLANES:3 * SUBLANES]
        a_last = acum_t[:, CHUNK - 1:CHUNK]
        ddt_t = jnp.exp(a_last - acum_t) * dt_t
        acum = jnp.concatenate(
            [acum_t, jnp.zeros((LANES - SSD_HEADS, CHUNK), F32)], axis=0).T
        ea = jnp.exp(acum)
        ea_hi = ea.astype(BF16)
        ea_lo = (ea - ea_hi.astype(F32)).astype(BF16)
        ea_x = (jnp.dot(ea_hi, expand, preferred_element_type=F32)
                + jnp.dot(ea_lo, expand, preferred_element_type=F32))
        bm_t = bm.T
        xs16 = xs.astype(BF16)
        bm16 = bm.astype(BF16)
        cm16 = cm.astype(BF16)
        y_pairs = [None] * (SSD_HEADS // 2)
        for g in range(SSD_GROUPS):
            cg = cm16[:, g * D_STATE:(g + 1) * D_STATE]
            bg = bm16[:, g * D_STATE:(g + 1) * D_STATE]
            bg_t = bm_t[g * D_STATE:(g + 1) * D_STATE, :]
            gmat = _nt_dot(cg, bg)
            for pp in range(pairs_per_group):
                pair = g * pairs_per_group + pp
                lanes = slice(pair * LANES, (pair + 1) * LANES)
                xp = xs16[:, lanes]
                y_halves, s_halves = [], []
                for h in (2 * pair, 2 * pair + 1):
                    col = acum[:, h:h + 1]
                    rowv = acum_t[h:h + 1, :]
                    lmat = jnp.exp(jnp.where(tri, col - rowv, -jnp.inf))
                    mmat = (gmat * lmat * dt_t[h:h + 1, :]).astype(BF16)
                    y_halves.append(jnp.dot(mmat, xp, preferred_element_type=F32))
                    bs = (bg_t * ddt_t[h:h + 1, :]).astype(BF16)
                    s_halves.append(jnp.dot(bs, xp, preferred_element_type=F32))
                y_diag = jnp.where(left_head, y_halves[0], y_halves[1])
                new = jnp.where(left_head_n, s_halves[0], s_halves[1])
                prev = state_ref[pair]
                y_off = jnp.dot(cg, prev.astype(BF16), preferred_element_type=F32) * ea_x[:, lanes]
                state_ref[pair] = prev * ea_x[CHUNK - 1:CHUNK, lanes] + new
                y_pairs[pair] = y_diag + y_off
        y = jnp.concatenate(y_pairs, axis=1) + dskip_ref[...] * xs
        y = y * zs_ref[rows, :]
        ms = jnp.mean(y * y, axis=1, keepdims=True)
        o_ref[rows, :] = (y * lax.rsqrt(ms + LN_EPS) * nw_ref[...]).astype(BF16)


def _ssd(xc, zs, misc, dtb_rep, a_rep, dskip_row, nw_row, b, s):
    tt = min(SSD_TT, s)
    nt = s // tt
    tile = lambda bi, ti: (bi * nt + ti, 0)
    const = lambda bi, ti: (0, 0)
    expand = (jnp.arange(LANES)[:, None] == jnp.arange(SSD_WIDTH)[None, :] // SSD_HEAD_DIM).astype(BF16)
    triu = (jnp.arange(CHUNK)[:, None] <= jnp.arange(CHUNK)[None, :]).astype(BF16)
    return pl.pallas_call(
        _ssd_kernel,
        grid=(b, s // tt),
        in_specs=[
            pl.BlockSpec((tt, CONV_CH), tile),
            pl.BlockSpec((tt, SSD_WIDTH), tile),
            pl.BlockSpec((tt, LANES), tile),
            pl.BlockSpec((SSD_HEADS, CHUNK), const),
            pl.BlockSpec((SSD_HEADS, CHUNK), const),
            pl.BlockSpec((1, SSD_WIDTH), const),
            pl.BlockSpec((1, SSD_WIDTH), const),
            pl.BlockSpec((LANES, SSD_WIDTH), const),
            pl.BlockSpec((CHUNK, CHUNK), const),
        ],
        out_specs=pl.BlockSpec((tt, SSD_WIDTH), tile),
        out_shape=jax.ShapeDtypeStruct((b * s, SSD_WIDTH), BF16),
        scratch_shapes=[
            pltpu.VMEM((SSD_HEADS // 2, D_STATE, LANES), F32),
        ],
        compiler_params=pltpu.CompilerParams(
            dimension_semantics=("parallel", "arbitrary"), vmem_limit_bytes=VMEM_LIMIT),
        name="ssd_mixer",
    )(xc, zs, misc, dtb_rep, a_rep, dskip_row, nw_row, expand, triu)


def _layer_norm(y, g, b):
    mu = jnp.mean(y, axis=1, keepdims=True)
    yc = y - mu
    var = jnp.mean(yc * yc, axis=1, keepdims=True)
    return yc * lax.rsqrt(var + LN_EPS) * g + b


OUT_TM = 1024
OUT_SUB = 256


def _out_proj_kernel(att_ref, ssd_ref, x_ref, wa_ref, ws_ref, g_ref, b_ref, h_ref):
    for r0 in range(0, x_ref.shape[0], OUT_SUB):
        rows = slice(r0, r0 + OUT_SUB)
        mixed = jnp.dot(att_ref[rows, :], wa_ref[...], preferred_element_type=F32)
        mixed = mixed + jnp.dot(ssd_ref[rows, :], ws_ref[...], preferred_element_type=F32)
        h_ref[rows, :] = _layer_norm(ALPHA * x_ref[rows, :] + mixed, g_ref[...], b_ref[...])


def _out_proj(att2, ssd2, x2, w_att, w_ssd, g_row, b_row, tm):
    n = x2.shape[0]
    row = lambda i: (i, 0)
    const = lambda i: (0, 0)
    return pl.pallas_call(
        _out_proj_kernel,
        grid=(n // tm,),
        in_specs=[
            pl.BlockSpec((tm, ATT_WIDTH), row),
            pl.BlockSpec((tm, SSD_WIDTH), row),
            pl.BlockSpec((tm, D_MODEL), row),
            pl.BlockSpec((ATT_WIDTH, D_MODEL), const),
            pl.BlockSpec((SSD_WIDTH, D_MODEL), const),
            pl.BlockSpec((1, D_MODEL), const),
            pl.BlockSpec((1, D_MODEL), const),
        ],
        out_specs=pl.BlockSpec((tm, D_MODEL), row),
        out_shape=jax.ShapeDtypeStruct((n, D_MODEL), F32),
        compiler_params=pltpu.CompilerParams(
            dimension_semantics=("parallel",), vmem_limit_bytes=VMEM_LIMIT),
        name="out_proj_ln",
    )(att2, ssd2, x2, w_att, w_ssd, g_row, b_row)


ROUTE_E0 = N_GROUPS_MOE
MOE_TM = 512
MOE_CAP = 192


def _first_max(vals):
    best = vals[0]
    for v in vals[1:]:
        best = jnp.maximum(best, v)
    idx = jnp.full(best.shape, len(vals) - 1, jnp.int32)
    for i in range(len(vals) - 2, -1, -1):
        idx = jnp.where(vals[i] == best, i, idx)
    return best, idx


def _moe_kernel(h_ref, wrh_ref, wrl_ref, br_ref, wg_ref, wu_ref, wd_ref, g_ref, b_ref, upper_ref,
                o_ref, hb_ref, hp_ref, gp_ref, yp_ref, y_ref):
    tm = h_ref.shape[0]
    h = h_ref[...]
    h_hi = h.astype(BF16)
    hb_ref[...] = h_hi
    h_lo = (h - h_hi.astype(F32)).astype(BF16)
    wrh = wrh_ref[...]
    logits = (jnp.dot(h_hi, wrh, preferred_element_type=F32)
              + jnp.dot(h_lo, wrh, preferred_element_type=F32)
              + jnp.dot(h_hi, wrl_ref[...], preferred_element_type=F32)) + br_ref[...]
    lt = logits.T
    row = lambda r: lt[r:r + 1, :]
    gl = [row(r) for r in range(N_GROUPS_MOE)]
    gmax, gidx = _first_max(gl)
    denom = jnp.exp(gl[0] - gmax)
    for v in gl[1:]:
        denom = denom + jnp.exp(v - gmax)
    gprob = 1.0 / denom
    el = []
    for k in range(EXPERTS_PER_GROUP):
        v = row(ROUTE_E0 + (N_GROUPS_MOE - 1) * EXPERTS_PER_GROUP + k)
        for gg in range(N_GROUPS_MOE - 2, -1, -1):
            v = jnp.where(gidx == gg, row(ROUTE_E0 + gg * EXPERTS_PER_GROUP + k), v)
        el.append(v)
    l1, i1 = _first_max(el)
    l2, i2 = _first_max([jnp.where(i1 == k, -jnp.inf, el[k]) for k in range(EXPERTS_PER_GROUP)])
    e2 = jnp.exp(l2 - l1)
    w1 = gprob / (1.0 + e2)
    w2 = gprob * e2 / (1.0 + e2)
    gate4 = [jnp.where(i1 == k, w1, jnp.where(i2 == k, w2, 0.0)) for k in range(EXPERTS_PER_GROUP)]

    member = [jnp.where(gidx == gg, 1.0, 0.0) for gg in range(N_GROUPS_MOE)]
    member_blk = jnp.concatenate(
        member + [jnp.zeros((2 * SUBLANES - N_GROUPS_MOE, tm), F32)], axis=0).astype(BF16)
    earlier = jnp.dot(member_blk, upper_ref[...], preferred_element_type=F32)
    counts = [jnp.sum(m, axis=1, keepdims=True) for m in member]
    most = counts[0]
    for c in counts[1:]:
        most = jnp.maximum(most, c)
    fits = jnp.max(most) <= float(MOE_CAP)
    slot = member[0] * earlier[0:1, :]
    for gg in range(1, N_GROUPS_MOE):
        slot = slot + member[gg] * (earlier[gg:gg + 1, :] + float(gg * MOE_CAP))
    tok = jnp.concatenate(
        gate4 + [slot, gidx.astype(F32), jnp.zeros((LANES - EXPERTS_PER_GROUP - 2, tm), F32)],
        axis=0).T
    slot_lane, grp_lane = EXPERTS_PER_GROUP, EXPERTS_PER_GROUP + 1
    rows_p = N_GROUPS_MOE * MOE_CAP

    @pl.when(fits)
    def _():
        hb = hb_ref[...]
        place = jnp.where(lax.broadcasted_iota(jnp.int32, (rows_p, tm), 0) == slot.astype(jnp.int32),
                          1.0, 0.0).astype(BF16)
        hp_ref[...] = jnp.dot(place, hb, preferred_element_type=F32).astype(BF16)
        t_hi = tok.astype(BF16)
        t_lo = (tok - t_hi.astype(F32)).astype(BF16)
        gp_ref[...] = (jnp.dot(place, t_hi, preferred_element_type=F32)
                       + jnp.dot(place, t_lo, preferred_element_type=F32))
        for gg in range(N_GROUPS_MOE):
            rows = slice(gg * MOE_CAP, (gg + 1) * MOE_CAP)
            xg = hp_ref[rows, :]
            acc = None
            for k in range(EXPERTS_PER_GROUP):
                e = gg * EXPERTS_PER_GROUP + k
                a = jnp.dot(xg, wg_ref[e], preferred_element_type=F32)
                u = jnp.dot(xg, wu_ref[e], preferred_element_type=F32)
                hid = (_silu(a) * u * gp_ref[rows, k:k + 1]).astype(BF16)
                part = jnp.dot(hid, wd_ref[e], preferred_element_type=F32)
                acc = part if acc is None else acc + part
            yp_ref[rows, :] = acc.astype(BF16)
        back = jnp.where(lax.broadcasted_iota(jnp.int32, (tm, rows_p), 1)
                         == tok[:, slot_lane:slot_lane + 1].astype(jnp.int32), 1.0, 0.0).astype(BF16)
        y_ref[...] = jnp.dot(back, yp_ref[...], preferred_element_type=F32)

    @pl.when(jnp.logical_not(fits))
    def _():
        hb = hb_ref[...]
        acc = None
        for gg in range(N_GROUPS_MOE):
            in_grp = tok[:, grp_lane:grp_lane + 1] == float(gg)
            for k in range(EXPERTS_PER_GROUP):
                e = gg * EXPERTS_PER_GROUP + k
                a = jnp.dot(hb, wg_ref[e], preferred_element_type=F32)
                u = jnp.dot(hb, wu_ref[e], preferred_element_type=F32)
                gate = jnp.where(in_grp, tok[:, k:k + 1], 0.0)
                hid = (_silu(a) * u * gate).astype(BF16)
                part = jnp.dot(hid, wd_ref[e], preferred_element_type=F32)
                acc = part if acc is None else acc + part
        y_ref[...] = acc

    o_ref[...] = _layer_norm(ALPHA * h_ref[...] + y_ref[...], g_ref[...], b_ref[...])


def _moe(h2, wr_hi, wr_lo, br, wg, wu, wd, g_row, b_row):
    n = h2.shape[0]
    tm = min(MOE_TM, n)
    rows_p = N_GROUPS_MOE * MOE_CAP
    row = lambda i: (i, 0)
    const = lambda i: (0, 0)
    whole = lambda i: (0, 0, 0)
    once = pl.Buffered(1)
    upper = (jnp.arange(tm)[:, None] < jnp.arange(tm)[None, :]).astype(BF16)
    return pl.pallas_call(
        _moe_kernel,
        grid=(n // tm,),
        in_specs=[
            pl.BlockSpec((tm, D_MODEL), row),
            pl.BlockSpec((D_MODEL, LANES), const),
            pl.BlockSpec((D_MODEL, LANES), const),
            pl.BlockSpec((1, LANES), const),
            pl.BlockSpec((N_EXPERTS, D_MODEL, EXPERT_FF), whole, pipeline_mode=once),
            pl.BlockSpec((N_EXPERTS, D_MODEL, EXPERT_FF), whole, pipeline_mode=once),
            pl.BlockSpec((N_EXPERTS, EXPERT_FF, D_MODEL), whole, pipeline_mode=once),
            pl.BlockSpec((1, D_MODEL), const),
            pl.BlockSpec((1, D_MODEL), const),
            pl.BlockSpec((tm, tm), const, pipeline_mode=once),
        ],
        out_specs=pl.BlockSpec((tm, D_MODEL), row),
        out_shape=jax.ShapeDtypeStruct((n, D_MODEL), F32),
        scratch_shapes=[
            pltpu.VMEM((tm, D_MODEL), BF16),
            pltpu.VMEM((rows_p, D_MODEL), BF16),
            pltpu.VMEM((rows_p, LANES), F32),
            pltpu.VMEM((rows_p, D_MODEL), BF16),
            pltpu.VMEM((tm, D_MODEL), F32),
        ],
        compiler_params=pltpu.CompilerParams(
            dimension_semantics=("parallel",), vmem_limit_bytes=VMEM_LIMIT),
        name="hier_moe_ln",
    )(h2, wr_hi, wr_lo, br, wg, wu, wd, g_row, b_row, upper)


def _rope_tables(seq):
    inv = ROPE_THETA ** (-jnp.arange(0, HEAD_DIM, 2, dtype=F32) / HEAD_DIM)
    ang = jnp.arange(seq, dtype=F32)[:, None] * inv[None, :]
    cos, sin = jnp.cos(ang), jnp.sin(ang)
    zero = jnp.zeros_like(sin)
    cos_t = jnp.tile(cos, (1, LANES // (HEAD_DIM // 2)))
    s1_t = jnp.tile(jnp.concatenate([-sin, zero], 1), (1, LANES // HEAD_DIM))
    s2_t = jnp.tile(jnp.concatenate([zero, sin], 1), (1, LANES // HEAD_DIM))
    return cos_t, s1_t, s2_t


def _permute_w_in(w):
    sizes = (ATT_WIDTH, HEAD_DIM, HEAD_DIM, IDX_HEADS * IDX_DIM, IDX_DIM, IDX_HEADS,
             SSD_WIDTH, CONV_CH, SSD_HEADS)
    pts = np.cumsum((0,) + sizes)
    q, k, v, iq, ik, iw, z, xbc, dt = [w[:, pts[i]:pts[i + 1]] for i in range(len(sizes))]
    d = w.shape[0]
    pad = lambda n: jnp.zeros((d, n), w.dtype)
    misc = jnp.concatenate([v, iw, pad(MISC_DT - MISC_IW - IDX_HEADS), dt,
                            pad(LANES - MISC_DT - SSD_HEADS)], 1)
    return jnp.concatenate([q, iq, k, ik, misc, z, xbc], 1).astype(BF16)


def _head_rep(vals):
    return jnp.broadcast_to(vals.astype(F32)[:, None], (vals.shape[0], CHUNK))


def kernel(x, w_in, conv_w, conv_b, dt_bias, a_log, d_skip, ssd_norm_w, w_out, ln1_g, ln1_b,
           w_route_group, b_route_group, w_route_expert, b_route_expert, w_gate, w_up,
           w_down, ln2_g, ln2_b):
    bsz, seq, d = x.shape
    n = bsz * seq
    topk = min(TOPK_MAX, seq // 4)
    tm = 512
    assert d == D_MODEL and TQ == KC and seq % TQ == 0 and seq % tm == 0 and topk <= KC
    cos_t, s1_t, s2_t = _rope_tables(seq)
    mscale = jnp.ones((1, LANES), F32).at[0, MISC_IW:MISC_IW + IDX_HEADS].set(INDEXER_SCALE)
    for l in range(DEPTH):
        x2 = x.reshape(n, d)
        q, iq, k, ik, vt, misc, zs, xc = _in_proj(
            x2, _permute_w_in(w_in[l]), cos_t, s1_t, s2_t, mscale,
            conv_w[l].astype(F32), conv_b[l].astype(F32)[None, :], seq, tm)
        att = _dsa(q, iq, misc, k, ik, vt, bsz, seq, topk)
        ssd = _ssd(
            xc, zs, misc,
            _head_rep(dt_bias[l]), _head_rep(-jnp.exp(a_log[l].astype(F32))),
            jnp.repeat(d_skip[l].astype(F32), SSD_HEAD_DIM)[None, :], ssd_norm_w[l][None, :],
            bsz, seq)
        w_o = w_out[l].astype(BF16)
        h2 = _out_proj(att, ssd, x2, w_o[:ATT_WIDTH], w_o[ATT_WIDTH:],
                       ln1_g[l][None, :], ln1_b[l][None, :], min(OUT_TM, n))
        pad = jnp.zeros((d, LANES - ROUTE_E0 - N_EXPERTS), F32)
        wr = jnp.concatenate([w_route_group[l].astype(F32), w_route_expert[l].astype(F32), pad], 1)
        wr_hi = wr.astype(BF16)
        wr_lo = (wr - wr_hi.astype(F32)).astype(BF16)
        br = jnp.concatenate([b_route_group[l].astype(F32), b_route_expert[l].astype(F32),
                              pad[0]])[None, :]
        x = _moe(h2, wr_hi, wr_lo, br, w_gate[l].astype(BF16), w_up[l].astype(BF16),
                 w_down[l].astype(BF16), ln2_g[l][None, :], ln2_b[l][None, :]).reshape(bsz, seq, d)
    return x
```

```python
import functools

import jax
import jax.numpy as jnp
import numpy as np
from jax import lax
from jax.experimental import pallas as pl
from jax.experimental.pallas import tpu as pltpu

F32 = jnp.float32
BF16 = jnp.bfloat16

D_MODEL = 1024
HEAD_DIM = 64
ATT_WIDTH = 512
ATT_HEADS = 8
IDX_HEADS = 4
IDX_DIM = 64
TOPK_MAX = 256
ROPE_THETA = 10000.0
INDEXER_SCALE = (IDX_HEADS ** -0.5) * (IDX_DIM ** -0.5)
SSD_WIDTH = 512
SSD_HEADS = 8
SSD_HEAD_DIM = 64
SSD_GROUPS = 2
D_STATE = 64
CONV_WIDTH = 4
CONV_CH = SSD_WIDTH + 2 * SSD_GROUPS * D_STATE
CHUNK = 128
N_GROUPS_MOE = 4
EXPERTS_PER_GROUP = 4
N_EXPERTS = 16
EXPERT_FF = 256
DEPTH = 1
ALPHA = (2 * DEPTH) ** 0.25
LN_EPS = 1e-5

LANES = 128
SUBLANES = 8
BF16_ROWS = 16
INF_KEY = 0x7F800000
KEY_STEP16 = 1 << 16
VMEM_LIMIT = 56 * 1024 * 1024

C_Q = 0
C_IQ = 512
C_K = 768
C_IK = 832
C_MISC = 896
MISC_IW = 64
MISC_DT = 72
C_Z = 1024
C_XBC = 1536
IN_COLS = 2304

TQ = 256
KC = 256

Q_SCALE = HEAD_DIM ** -0.5 * float(np.log2(np.e))

NT_DIMS = (((1,), (1,)), ((), ()))


def _nt_dot(a, b):
    return lax.dot_general(a, b, NT_DIMS, preferred_element_type=F32)


def _fold_rows(x, op, rows=SUBLANES):
    slabs = [x[r * rows:(r + 1) * rows, :] for r in range(x.shape[0] // rows)]
    while len(slabs) > 1:
        nxt = [op(slabs[i], slabs[i + 1]) for i in range(0, len(slabs) - 1, 2)]
        if len(slabs) % 2:
            nxt.append(slabs[-1])
        slabs = nxt
    return slabs[0]


CONV_PAD = 8


def _silu(x):
    return x * (1.0 / (1.0 + jnp.exp(-x)))


def _in_proj_kernel(x_ref, w_ref, cos_ref, s1_ref, s2_ref, mscale_ref, cw_ref, cb_ref,
                    q_ref, iq_ref, k_ref, ik_ref, vt_ref, misc_ref, zs_ref, xc_ref, hist_ref,
                    *, nblk_seq):
    @pl.when(pl.program_id(0) == 0)
    def _():
        hist_ref[...] = jnp.zeros_like(hist_ref)

    xb = x_ref[...].astype(BF16)
    cos = cos_ref[...]
    s1 = s1_ref[...]
    s2 = s2_ref[...]

    def mm(c0, width):
        return jnp.dot(xb, w_ref[:, c0:c0 + width], preferred_element_type=F32)

    def rope(y):
        fwd = pltpu.roll(y, LANES - HEAD_DIM // 2, 1)
        bwd = pltpu.roll(y, HEAD_DIM // 2, 1)
        return y * cos + fwd * s1 + bwd * s2

    def rope_wide(y, scale):
        parts = []
        for c in range(y.shape[1] // LANES):
            r = rope(y[:, c * LANES:(c + 1) * LANES])
            parts.append(r * scale if scale != 1.0 else r)
        return parts

    xbc = mm(C_XBC, CONV_CH)
    tm = xbc.shape[0]
    seq_start = pl.program_id(0) % nblk_seq == 0
    hist = jnp.where(seq_start, 0.0, hist_ref[...])
    xp = jnp.concatenate([hist, xbc], axis=0)
    acc = cw_ref[0:1, :] * xp
    for j in range(1, CONV_WIDTH):
        acc = pltpu.roll(acc, 1, 0) + cw_ref[j:j + 1, :] * xp
    xc_ref[...] = _silu(acc[CONV_PAD:, :] + cb_ref[...])
    hist_ref[...] = xbc[tm - CONV_PAD:, :]
    zs_ref[...] = _silu(mm(C_Z, SSD_WIDTH))

    for c, r in enumerate(rope_wide(mm(C_Q, ATT_WIDTH), Q_SCALE)):
        q_ref[:, c * LANES:(c + 1) * LANES] = r.astype(BF16)
    for c, r in enumerate(rope_wide(mm(C_IQ, IDX_HEADS * IDX_DIM), 1.0)):
        iq_ref[:, c * LANES:(c + 1) * LANES] = r.astype(BF16)
    kk = rope(mm(C_K, LANES))
    k_ref[...] = kk[:, :HEAD_DIM].astype(BF16)
    ik_ref[...] = kk[:, HEAD_DIM:].astype(BF16)
    misc = mm(C_MISC, LANES) * mscale_ref[...]
    misc_ref[...] = misc
    lane = lax.broadcasted_iota(jnp.int32, misc.shape, 1)
    vext = jnp.where(lane < HEAD_DIM, misc, jnp.where(lane == HEAD_DIM, 1.0, 0.0))
    for c in range(vt_ref.shape[0]):
        vt_ref[c] = vext[c * KC:(c + 1) * KC, :].T.astype(BF16)


def _in_proj(x2, w_perm, cos_t, s1_t, s2_t, mscale, conv_w, conv_b, seq, tm):
    n = x2.shape[0]
    nblk_seq = seq // tm
    row = lambda i: (i, 0)
    tab = lambda i: (i % nblk_seq, 0)
    const = lambda i: (0, 0)
    outs = [
        ((n, ATT_WIDTH), BF16, pl.BlockSpec((tm, ATT_WIDTH), row)),
        ((n, IDX_HEADS * IDX_DIM), BF16, pl.BlockSpec((tm, IDX_HEADS * IDX_DIM), row)),
        ((n, HEAD_DIM), BF16, pl.BlockSpec((tm, HEAD_DIM), row)),
        ((n, IDX_DIM), BF16, pl.BlockSpec((tm, IDX_DIM), row)),
        ((n // KC, LANES, KC), BF16, pl.BlockSpec((tm // KC, LANES, KC), lambda i: (i, 0, 0))),
        ((n, LANES), F32, pl.BlockSpec((tm, LANES), row)),
        ((n, SSD_WIDTH), F32, pl.BlockSpec((tm, SSD_WIDTH), row)),
        ((n, CONV_CH), F32, pl.BlockSpec((tm, CONV_CH), row)),
    ]
    return pl.pallas_call(
        functools.partial(_in_proj_kernel, nblk_seq=nblk_seq),
        grid=(n // tm,),
        in_specs=[
            pl.BlockSpec((tm, D_MODEL), row),
            pl.BlockSpec((D_MODEL, IN_COLS), const),
            pl.BlockSpec((tm, LANES), tab),
            pl.BlockSpec((tm, LANES), tab),
            pl.BlockSpec((tm, LANES), tab),
            pl.BlockSpec((1, LANES), const),
            pl.BlockSpec((CONV_WIDTH, CONV_CH), const),
            pl.BlockSpec((1, CONV_CH), const),
        ],
        out_specs=[spec for _, _, spec in outs],
        out_shape=[jax.ShapeDtypeStruct(shape, dt) for shape, dt, _ in outs],
        scratch_shapes=[pltpu.VMEM((CONV_PAD, CONV_CH), F32)],
        compiler_params=pltpu.CompilerParams(
            dimension_semantics=("arbitrary",), vmem_limit_bytes=VMEM_LIMIT),
        name="in_proj",
    )(x2, w_perm, cos_t, s1_t, s2_t, mscale, conv_w, conv_b)


def _dsa_kernel(q_ref, iq_ref, misc_ref, k_ref, ik_ref, vt_ref, o_ref,
                sc_ref, sc16_ref, lg_ref, acc_ref, *, topk):
    qi = pl.program_id(1)
    nj = qi + 1
    neg_inf = -jnp.inf
    kf = float(topk)
    key_i = lax.broadcasted_iota(jnp.int32, (KC, TQ), 0)
    qry_i = lax.broadcasted_iota(jnp.int32, (KC, TQ), 1)
    causal = key_i <= qry_i

    def key_rows(j):
        return pl.ds(pl.multiple_of(j * KC, KC), KC)

    iw_t = misc_ref[...].T[MISC_IW:MISC_IW + SUBLANES, :]

    def scores_body(j, carry):
        ikj = ik_ref[key_rows(j), :]
        sc = None
        for h in range(IDX_HEADS):
            d = _nt_dot(ikj, iq_ref[:, h * IDX_DIM:(h + 1) * IDX_DIM])
            term = iw_t[h:h + 1, :] * jnp.maximum(d, 0.0)
            sc = term if sc is None else sc + term
        sc_ref[j] = sc
        sc16_ref[j] = sc.astype(BF16)
        return carry

    lax.fori_loop(0, nj, scores_body, 0)
    sc_diag = jnp.where(causal, sc_ref[qi], neg_inf)
    sc_ref[qi] = sc_diag
    sc16_ref[qi] = sc_diag.astype(BF16)

    def key_to_float(key):
        f = lax.bitcast_convert_type(jnp.where(key < 0, -key, key), F32)
        return jnp.where(key < 0, -f, f)

    one16 = jnp.ones((), BF16)
    zero16 = jnp.zeros((), BF16)

    def count16_ge(cand16):
        def body(j, acc):
            ind = jnp.where(sc16_ref[j] >= cand16, one16, zero16)
            return acc + _fold_rows(ind, jnp.add, BF16_ROWS).astype(F32)

        acc = lax.fori_loop(0, nj, body, jnp.zeros((BF16_ROWS, TQ), F32))
        return jnp.sum(acc, axis=0, keepdims=True)

    def count_ge(cand):
        def body(j, acc):
            ind = jnp.where(sc_ref[j] >= cand, 1.0, 0.0)
            return acc + _fold_rows(ind, jnp.add)

        acc = lax.fori_loop(0, nj, body, jnp.zeros((SUBLANES, TQ), F32))
        return jnp.sum(acc, axis=0, keepdims=True)

    def coarse_body(b, m):
        trial = m + lax.shift_left(jnp.int32(1), jnp.int32(15) - b)
        cand16 = key_to_float(trial * KEY_STEP16).astype(BF16)
        return jnp.where(count16_ge(cand16) >= kf, trial, m)

    m16 = lax.fori_loop(0, 16, coarse_body, jnp.full((1, TQ), -(INF_KEY // KEY_STEP16), jnp.int32))
    key_base = jnp.maximum(m16 * KEY_STEP16 - (KEY_STEP16 // 2 + 1), -INF_KEY)

    def fine_body(b, off):
        trial = off + lax.shift_left(jnp.int32(1), jnp.int32(16) - b)
        ok = count_ge(key_to_float(key_base + trial)) >= kf
        return jnp.where(ok, trial, off)

    off = lax.fori_loop(0, 17, fine_body, jnp.zeros((1, TQ), jnp.int32))
    lo = key_to_float(key_base + off)
    hi = key_to_float(key_base + off + 1)
    need = kf - count_ge(hi)

    lower = (lax.broadcasted_iota(jnp.int32, (KC, KC), 0)
             > lax.broadcasted_iota(jnp.int32, (KC, KC), 1)).astype(BF16)

    acc_ref[...] = jnp.zeros_like(acc_ref)
    lg_ref[1] = jnp.full(lg_ref.shape[1:], neg_inf, F32)
    no_max = tuple(jnp.minimum(need, neg_inf) for _ in range(ATT_HEADS))

    def logits_stage(j, taken):
        s = sc_ref[j]
        gt = s >= hi
        eq = jnp.logical_and(s >= lo, jnp.logical_not(gt))
        eqf = jnp.where(eq, 1.0, 0.0)
        before = jnp.dot(lower, eqf.astype(BF16), preferred_element_type=F32) + taken
        sel = jnp.logical_or(gt, jnp.logical_and(eq, before < need))
        sel = jnp.logical_and(sel, jnp.logical_or(causal, j < qi))
        bias = jnp.where(sel, 0.0, neg_inf)
        taken = taken + jnp.sum(_fold_rows(eqf, jnp.add), axis=0, keepdims=True)
        kj = k_ref[key_rows(j), :]
        slot = j % 2
        cms = []
        for h in range(ATT_HEADS):
            lg = _nt_dot(kj, q_ref[:, h * HEAD_DIM:(h + 1) * HEAD_DIM]) + bias
            lg_ref[slot, h] = lg
            cms.append(jnp.max(_fold_rows(lg, jnp.maximum), axis=0, keepdims=True))
        return taken, tuple(cms)

    def pv_stage(jp, cms, ms):
        vtj = vt_ref[jp]
        slot = jp % 2
        new_ms = []
        for h in range(ATT_HEADS):
            m_new = jnp.maximum(ms[h], cms[h])
            shift = jnp.where(m_new == neg_inf, 0.0, m_new)
            p = jnp.exp2(lg_ref[slot, h] - shift).astype(BF16)
            alpha = jnp.exp2(ms[h] - shift)
            acc_ref[h] = alpha * acc_ref[h] + jnp.dot(vtj, p, preferred_element_type=F32)
            new_ms.append(m_new)
        return tuple(new_ms)

    def att_body(j, carry):
        taken, cms_prev, ms = carry
        ms = pv_stage(jnp.maximum(j - 1, 0) + (j == 0).astype(jnp.int32), cms_prev, ms)
        taken, cms = logits_stage(j, taken)
        return taken, cms, ms

    _, cms_last, ms = lax.fori_loop(0, nj, att_body, (jnp.zeros((1, TQ), F32), no_max, no_max))
    pv_stage(qi, cms_last, ms)
    for pair in range(ATT_HEADS // 2):
        halves = []
        for h in (2 * pair, 2 * pair + 1):
            a = acc_ref[h]
            halves.append(a[:HEAD_DIM, :] * (1.0 / a[HEAD_DIM:HEAD_DIM + 1, :]))
        blk = jnp.concatenate(halves, axis=0)
        o_ref[:, pair * LANES:(pair + 1) * LANES] = blk.T.astype(BF16)


def _dsa(q, iq, misc, k, ik, vt, b, s, topk):
    nq = s // TQ
    tile = lambda bi, qi: (bi * nq + qi, 0)
    full = lambda bi, qi: (bi, 0)
    return pl.pallas_call(
        functools.partial(_dsa_kernel, topk=topk),
        grid=(b, nq),
        in_specs=[
            pl.BlockSpec((TQ, ATT_WIDTH), tile),
            pl.BlockSpec((TQ, IDX_HEADS * IDX_DIM), tile),
            pl.BlockSpec((TQ, LANES), tile),
            pl.BlockSpec((s, HEAD_DIM), full),
            pl.BlockSpec((s, IDX_DIM), full),
            pl.BlockSpec((s // KC, LANES, KC), lambda bi, qi: (bi, 0, 0)),
        ],
        out_specs=pl.BlockSpec((TQ, ATT_WIDTH), tile),
        out_shape=jax.ShapeDtypeStruct((b * s, ATT_WIDTH), BF16),
        scratch_shapes=[
            pltpu.VMEM((nq, KC, TQ), F32),
            pltpu.VMEM((nq, KC, TQ), BF16),
            pltpu.VMEM((2, ATT_HEADS, KC, TQ), F32),
            pltpu.VMEM((ATT_HEADS, LANES, TQ), F32),
        ],
        compiler_params=pltpu.CompilerParams(
            dimension_semantics=("parallel", "arbitrary"), vmem_limit_bytes=VMEM_LIMIT),
        name="dsa_attention",
    )(q, iq, misc, k, ik, vt)


SSD_TT = 512


def _ssd_kernel(xc_ref, zs_ref, misc_ref, dtb_ref, arep_ref, dskip_ref, nw_ref,
                expand_ref, triu_ref, o_ref, state_ref):
    t = pl.program_id(1)
    tt = xc_ref.shape[0]

    @pl.when(t == 0)
    def _():
        state_ref[...] = jnp.zeros_like(state_ref)

    tri = (lax.broadcasted_iota(jnp.int32, (CHUNK, CHUNK), 0)
           >= lax.broadcasted_iota(jnp.int32, (CHUNK, CHUNK), 1))
    left_head = lax.broadcasted_iota(jnp.int32, (CHUNK, LANES), 1) < SSD_HEAD_DIM
    left_head_n = lax.broadcasted_iota(jnp.int32, (D_STATE, LANES), 1) < SSD_HEAD_DIM
    gn = SSD_GROUPS * D_STATE
    pairs_per_group = SSD_HEADS // SSD_GROUPS // 2
    expand = expand_ref[...]
    triu = triu_ref[...]
    zpad = jnp.zeros((SUBLANES, CHUNK), F32)

    for c in range(tt // CHUNK):
        rows = slice(c * CHUNK, (c + 1) * CHUNK)
        xs = xc_ref[rows, 0:SSD_WIDTH]
        bm = xc_ref[rows, SSD_WIDTH:SSD_WIDTH + gn]
        cm = xc_ref[rows, SSD_WIDTH + gn:SSD_WIDTH + 2 * gn]
        raw = misc_ref[rows, :].T[MISC_DT:MISC_DT + SSD_HEADS, :] + dtb_ref[...]
        dt_t = jnp.maximum(raw, 0.0) + jnp.log1p(jnp.exp(-jnp.abs(raw)))
        adt = dt_t * arep_ref[...]
        hi = adt.astype(BF16).astype(F32)
        r1 = adt - hi
        mid = r1.astype(BF16).astype(F32)
        pieces = jnp.concatenate([hi, mid, r1 - mid, zpad], axis=0).astype(BF16)
        cs = jnp.dot(pieces, triu, preferred_element_type=F32)
        acum_t = (cs[0:SUBLANES] + cs[SUBLANES:2 * SUBLANES]) + cs[2 * SUBLANES:3 * SUBLANES]
        a_last = acum_t[:, CHUNK - 1:CHUNK]
        ddt_t = jnp.exp(a_last - acum_t) * dt_t
        acum = jnp.concatenate(
            [acum_t, jnp.zeros((LANES - SSD_HEADS, CHUNK), F32)], axis=0).T
        ea = jnp.exp(acum)
        ea_hi = ea.astype(BF16)
        ea_lo = (ea - ea_hi.astype(F32)).astype(BF16)
        ea_x = (jnp.dot(ea_hi, expand, preferred_element_type=F32)
                + jnp.dot(ea_lo, expand, preferred_element_type=F32))
        bm_t = bm.T
        xs16 = xs.astype(BF16)
        bm16 = bm.astype(BF16)
        cm16 = cm.astype(BF16)
        y_pairs = [None] * (SSD_HEADS // 2)
        for g in range(SSD_GROUPS):
            cg = cm16[:, g * D_STATE:(g + 1) * D_STATE]
            bg = bm16[:, g * D_STATE:(g + 1) * D_STATE]
            bg_t = bm_t[g * D_STATE:(g + 1) * D_STATE, :]
            gmat = _nt_dot(cg, bg)
            for pp in range(pairs_per_group):
                pair = g * pairs_per_group + pp
                lanes = slice(pair * LANES, (pair + 1) * LANES)
                xp = xs16[:, lanes]
                y_halves, s_halves = [], []
                for h in (2 * pair, 2 * pair + 1):
                    col = acum[:, h:h + 1]
                    rowv = acum_t[h:h + 1, :]
                    lmat = jnp.exp(jnp.where(tri, col - rowv, -jnp.inf))
                    mmat = (gmat * lmat * dt_t[h:h + 1, :]).astype(BF16)
                    y_halves.append(jnp.dot(mmat, xp, preferred_element_type=F32))
                    bs = (bg_t * ddt_t[h:h + 1, :]).astype(BF16)
                    s_halves.append(jnp.dot(bs, xp, preferred_element_type=F32))
                y_diag = jnp.where(left_head, y_halves[0], y_halves[1])
                new = jnp.where(left_head_n, s_halves[0], s_halves[1])
                prev = state_ref[pair]
                y_off = jnp.dot(cg, prev.astype(BF16), preferred_element_type=F32) * ea_x[:, lanes]
                state_ref[pair] = prev * ea_x[CHUNK - 1:CHUNK, lanes] + new
                y_pairs[pair] = y_diag + y_off
        y = jnp.concatenate(y_pairs, axis=1) + dskip_ref[...] * xs
        y = y * zs_ref[rows, :]
        ms = jnp.mean(y * y, axis=1, keepdims=True)
        o_ref[rows, :] = (y * lax.rsqrt(ms + LN_EPS) * nw_ref[...]).astype(BF16)


def _ssd(xc, zs, misc, dtb_rep, a_rep, dskip_row, nw_row, b, s):
    tt = min(SSD_TT, s)
    nt = s // tt
    tile = lambda bi, ti: (bi * nt + ti, 0)
    const = lambda bi, ti: (0, 0)
    expand = (jnp.arange(LANES)[:, None] == jnp.arange(SSD_WIDTH)[None, :] // SSD_HEAD_DIM).astype(BF16)
    triu = (jnp.arange(CHUNK)[:, None] <= jnp.arange(CHUNK)[None, :]).astype(BF16)
    return pl.pallas_call(
        _ssd_kernel,
        grid=(b, s // tt),
        in_specs=[
            pl.BlockSpec((tt, CONV_CH), tile),
            pl.BlockSpec((tt, SSD_WIDTH), tile),
            pl.BlockSpec((tt, LANES), tile),
            pl.BlockSpec((SSD_HEADS, CHUNK), const),
            pl.BlockSpec((SSD_HEADS, CHUNK), const),
            pl.BlockSpec((1, SSD_WIDTH), const),
            pl.BlockSpec((1, SSD_WIDTH), const),
            pl.BlockSpec((LANES, SSD_WIDTH), const),
            pl.BlockSpec((CHUNK, CHUNK), const),
        ],
        out_specs=pl.BlockSpec((tt, SSD_WIDTH), tile),
        out_shape=jax.ShapeDtypeStruct((b * s, SSD_WIDTH), BF16),
        scratch_shapes=[
            pltpu.VMEM((SSD_HEADS // 2, D_STATE, LANES), F32),
        ],
        compiler_params=pltpu.CompilerParams(
            dimension_semantics=("parallel", "arbitrary"), vmem_limit_bytes=VMEM_LIMIT),
        name="ssd_mixer",
    )(xc, zs, misc, dtb_rep, a_rep, dskip_row, nw_row, expand, triu)


def _layer_norm(y, g, b):
    mu = jnp.mean(y, axis=1, keepdims=True)
    yc = y - mu
    var = jnp.mean(yc * yc, axis=1, keepdims=True)
    return yc * lax.rsqrt(var + LN_EPS) * g + b


OUT_TM = 1024
OUT_SUB = 256


def _out_proj_kernel(att_ref, ssd_ref, x_ref, wa_ref, ws_ref, g_ref, b_ref, h_ref):
    for r0 in range(0, x_ref.shape[0], OUT_SUB):
        rows = slice(r0, r0 + OUT_SUB)
        mixed = jnp.dot(att_ref[rows, :], wa_ref[...], preferred_element_type=F32)
        mixed = mixed + jnp.dot(ssd_ref[rows, :], ws_ref[...], preferred_element_type=F32)
        h_ref[rows, :] = _layer_norm(ALPHA * x_ref[rows, :] + mixed, g_ref[...], b_ref[...])


def _out_proj(att2, ssd2, x2, w_att, w_ssd, g_row, b_row, tm):
    n = x2.shape[0]
    row = lambda i: (i, 0)
    const = lambda i: (0, 0)
    return pl.pallas_call(
        _out_proj_kernel,
        grid=(n // tm,),
        in_specs=[
            pl.BlockSpec((tm, ATT_WIDTH), row),
            pl.BlockSpec((tm, SSD_WIDTH), row),
            pl.BlockSpec((tm, D_MODEL), row),
            pl.BlockSpec((ATT_WIDTH, D_MODEL), const),
            pl.BlockSpec((SSD_WIDTH, D_MODEL), const),
            pl.BlockSpec((1, D_MODEL), const),
            pl.BlockSpec((1, D_MODEL), const),
        ],
        out_specs=pl.BlockSpec((tm, D_MODEL), row),
        out_shape=jax.ShapeDtypeStruct((n, D_MODEL), F32),
        compiler_params=pltpu.CompilerParams(
            dimension_semantics=("parallel",), vmem_limit_bytes=VMEM_LIMIT),
        name="out_proj_ln",
    )(att2, ssd2, x2, w_att, w_ssd, g_row, b_row)


ROUTE_E0 = N_GROUPS_MOE
MOE_TM = 512
MOE_CAP = 160


def _first_max(vals):
    best = vals[0]
    for v in vals[1:]:
        best = jnp.maximum(best, v)
    idx = jnp.full(best.shape, len(vals) - 1, jnp.int32)
    for i in range(len(vals) - 2, -1, -1):
        idx = jnp.where(vals[i] == best, i, idx)
    return best, idx


def _moe_kernel(h_ref, wrh_ref, wrl_ref, br_ref, wg_ref, wu_ref, wd_ref, g_ref, b_ref, upper_ref,
                o_ref, hb_ref, hp_ref, gp_ref, yp_ref, y_ref):
    tm = h_ref.shape[0]
    h = h_ref[...]
    h_hi = h.astype(BF16)
    hb_ref[...] = h_hi
    h_lo = (h - h_hi.astype(F32)).astype(BF16)
    wrh = wrh_ref[...]
    logits = (jnp.dot(h_hi, wrh, preferred_element_type=F32)
              + jnp.dot(h_lo, wrh, preferred_element_type=F32)
              + jnp.dot(h_hi, wrl_ref[...], preferred_element_type=F32)) + br_ref[...]
    lt = logits.T
    row = lambda r: lt[r:r + 1, :]
    gl = [row(r) for r in range(N_GROUPS_MOE)]
    gmax, gidx = _first_max(gl)
    denom = jnp.exp(gl[0] - gmax)
    for v in gl[1:]:
        denom = denom + jnp.exp(v - gmax)
    gprob = 1.0 / denom
    el = []
    for k in range(EXPERTS_PER_GROUP):
        v = row(ROUTE_E0 + (N_GROUPS_MOE - 1) * EXPERTS_PER_GROUP + k)
        for gg in range(N_GROUPS_MOE - 2, -1, -1):
            v = jnp.where(gidx == gg, row(ROUTE_E0 + gg * EXPERTS_PER_GROUP + k), v)
        el.append(v)
    l1, i1 = _first_max(el)
    l2, i2 = _first_max([jnp.where(i1 == k, -jnp.inf, el[k]) for k in range(EXPERTS_PER_GROUP)])
    e2 = jnp.exp(l2 - l1)
    w1 = gprob / (1.0 + e2)
    w2 = gprob * e2 / (1.0 + e2)
    gate4 = [jnp.where(i1 == k, w1, jnp.where(i2 == k, w2, 0.0)) for k in range(EXPERTS_PER_GROUP)]

    member = [jnp.where(gidx == gg, 1.0, 0.0) for gg in range(N_GROUPS_MOE)]
    member_blk = jnp.concatenate(
        member + [jnp.zeros((2 * SUBLANES - N_GROUPS_MOE, tm), F32)], axis=0).astype(BF16)
    earlier = jnp.dot(member_blk, upper_ref[...], preferred_element_type=F32)
    counts = [jnp.sum(m, axis=1, keepdims=True) for m in member]
    most = counts[0]
    for c in counts[1:]:
        most = jnp.maximum(most, c)
    fits = jnp.max(most) <= float(MOE_CAP)
    slot = member[0] * earlier[0:1, :]
    for gg in range(1, N_GROUPS_MOE):
        slot = slot + member[gg] * (earlier[gg:gg + 1, :] + float(gg * MOE_CAP))
    tok = jnp.concatenate(
        gate4 + [slot, gidx.astype(F32), jnp.zeros((LANES - EXPERTS_PER_GROUP - 2, tm), F32)],
        axis=0).T
    slot_lane, grp_lane = EXPERTS_PER_GROUP, EXPERTS_PER_GROUP + 1
    rows_p = N_GROUPS_MOE * MOE_CAP

    @pl.when(fits)
    def _():
        hb = hb_ref[...]
        place = jnp.where(lax.broadcasted_iota(jnp.int32, (rows_p, tm), 0) == slot.astype(jnp.int32),
                          1.0, 0.0).astype(BF16)
        hp_ref[...] = jnp.dot(place, hb, preferred_element_type=F32).astype(BF16)
        t_hi = tok.astype(BF16)
        t_lo = (tok - t_hi.astype(F32)).astype(BF16)
        both = jnp.dot(place, jnp.concatenate([t_hi, t_lo], axis=1), preferred_element_type=F32)
        gp_ref[...] = both[:, :LANES] + both[:, LANES:]
        for gg in range(N_GROUPS_MOE):
            rows = slice(gg * MOE_CAP, (gg + 1) * MOE_CAP)
            xg = hp_ref[rows, :]
            acc = None
            for k in range(EXPERTS_PER_GROUP):
                e = gg * EXPERTS_PER_GROUP + k
                a = jnp.dot(xg, wg_ref[e], preferred_element_type=F32)
                u = jnp.dot(xg, wu_ref[e], preferred_element_type=F32)
                hid = (_silu(a) * u * gp_ref[rows, k:k + 1]).astype(BF16)
                part = jnp.dot(hid, wd_ref[e], preferred_element_type=F32)
                acc = part if acc is None else acc + part
            yp_ref[rows, :] = acc.astype(BF16)
        back = jnp.where(lax.broadcasted_iota(jnp.int32, (tm, rows_p), 1)
                         == tok[:, slot_lane:slot_lane + 1].astype(jnp.int32), 1.0, 0.0).astype(BF16)
        y_ref[...] = jnp.dot(back, yp_ref[...], preferred_element_type=F32)

    @pl.when(jnp.logical_not(fits))
    def _():
        hb = hb_ref[...]
        acc = None
        for gg in range(N_GROUPS_MOE):
            in_grp = tok[:, grp_lane:grp_lane + 1] == float(gg)
            for k in range(EXPERTS_PER_GROUP):
                e = gg * EXPERTS_PER_GROUP + k
                a = jnp.dot(hb, wg_ref[e], preferred_element_type=F32)
                u = jnp.dot(hb, wu_ref[e], preferred_element_type=F32)
                gate = jnp.where(in_grp, tok[:, k:k + 1], 0.0)
                hid = (_silu(a) * u * gate).astype(BF16)
                part = jnp.dot(hid, wd_ref[e], preferred_element_type=F32)
                acc = part if acc is None else acc + part
        y_ref[...] = acc

    o_ref[...] = _layer_norm(ALPHA * h_ref[...] + y_ref[...], g_ref[...], b_ref[...])


def _moe(h2, wr_hi, wr_lo, br, wg, wu, wd, g_row, b_row):
    n = h2.shape[0]
    tm = min(MOE_TM, n)
    rows_p = N_GROUPS_MOE * MOE_CAP
    row = lambda i: (i, 0)
    const = lambda i: (0, 0)
    whole = lambda i: (0, 0, 0)
    once = pl.Buffered(1)
    upper = (jnp.arange(tm)[:, None] < jnp.arange(tm)[None, :]).astype(BF16)
    return pl.pallas_call(
        _moe_kernel,
        grid=(n // tm,),
        in_specs=[
            pl.BlockSpec((tm, D_MODEL), row),
            pl.BlockSpec((D_MODEL, LANES), const),
            pl.BlockSpec((D_MODEL, LANES), const),
            pl.BlockSpec((1, LANES), const),
            pl.BlockSpec((N_EXPERTS, D_MODEL, EXPERT_FF), whole, pipeline_mode=once),
            pl.BlockSpec((N_EXPERTS, D_MODEL, EXPERT_FF), whole, pipeline_mode=once),
            pl.BlockSpec((N_EXPERTS, EXPERT_FF, D_MODEL), whole, pipeline_mode=once),
            pl.BlockSpec((1, D_MODEL), const),
            pl.BlockSpec((1, D_MODEL), const),
            pl.BlockSpec((tm, tm), const, pipeline_mode=once),
        ],
        out_specs=pl.BlockSpec((tm, D_MODEL), row),
        out_shape=jax.ShapeDtypeStruct((n, D_MODEL), F32),
        scratch_shapes=[
            pltpu.VMEM((tm, D_MODEL), BF16),
            pltpu.VMEM((rows_p, D_MODEL), BF16),
            pltpu.VMEM((rows_p, LANES), F32),
            pltpu.VMEM((rows_p, D_MODEL), BF16),
            pltpu.VMEM((tm, D_MODEL), F32),
        ],
        compiler_params=pltpu.CompilerParams(
            dimension_semantics=("parallel",), vmem_limit_bytes=VMEM_LIMIT),
        name="hier_moe_ln",
    )(h2, wr_hi, wr_lo, br, wg, wu, wd, g_row, b_row, upper)


def _rope_tables(seq):
    inv = ROPE_THETA ** (-jnp.arange(0, HEAD_DIM, 2, dtype=F32) / HEAD_DIM)
    ang = jnp.arange(seq, dtype=F32)[:, None] * inv[None, :]
    cos, sin = jnp.cos(ang), jnp.sin(ang)
    zero = jnp.zeros_like(sin)
    cos_t = jnp.tile(cos, (1, LANES // (HEAD_DIM // 2)))
    s1_t = jnp.tile(jnp.concatenate([-sin, zero], 1), (1, LANES // HEAD_DIM))
    s2_t = jnp.tile(jnp.concatenate([zero, sin], 1), (1, LANES // HEAD_DIM))
    return cos_t, s1_t, s2_t


def _permute_w_in(w):
    sizes = (ATT_WIDTH, HEAD_DIM, HEAD_DIM, IDX_HEADS * IDX_DIM, IDX_DIM, IDX_HEADS,
             SSD_WIDTH, CONV_CH, SSD_HEADS)
    pts = np.cumsum((0,) + sizes)
    q, k, v, iq, ik, iw, z, xbc, dt = [w[:, pts[i]:pts[i + 1]] for i in range(len(sizes))]
    d = w.shape[0]
    pad = lambda n: jnp.zeros((d, n), w.dtype)
    misc = jnp.concatenate([v, iw, pad(MISC_DT - MISC_IW - IDX_HEADS), dt,
                            pad(LANES - MISC_DT - SSD_HEADS)], 1)
    return jnp.concatenate([q, iq, k, ik, misc, z, xbc], 1).astype(BF16)


def _head_rep(vals):
    return jnp.broadcast_to(vals.astype(F32)[:, None], (vals.shape[0], CHUNK))


def kernel(x, w_in, conv_w, conv_b, dt_bias, a_log, d_skip, ssd_norm_w, w_out, ln1_g, ln1_b,
           w_route_group, b_route_group, w_route_expert, b_route_expert, w_gate, w_up,
           w_down, ln2_g, ln2_b):
    bsz, seq, d = x.shape
    n = bsz * seq
    topk = min(TOPK_MAX, seq // 4)
    tm = 512
    assert d == D_MODEL and TQ == KC and seq % TQ == 0 and seq % tm == 0 and topk <= KC
    cos_t, s1_t, s2_t = _rope_tables(seq)
    mscale = jnp.ones((1, LANES), F32).at[0, MISC_IW:MISC_IW + IDX_HEADS].set(INDEXER_SCALE)
    for l in range(DEPTH):
        x2 = x.reshape(n, d)
        q, iq, k, ik, vt, misc, zs, xc = _in_proj(
            x2, _permute_w_in(w_in[l]), cos_t, s1_t, s2_t, mscale,
            conv_w[l].astype(F32), conv_b[l].astype(F32)[None, :], seq, tm)
        att = _dsa(q, iq, misc, k, ik, vt, bsz, seq, topk)
        ssd = _ssd(
            xc, zs, misc,
            _head_rep(dt_bias[l]), _head_rep(-jnp.exp(a_log[l].astype(F32))),
            jnp.repeat(d_skip[l].astype(F32), SSD_HEAD_DIM)[None, :], ssd_norm_w[l][None, :],
            bsz, seq)
        w_o = w_out[l].astype(BF16)
        h2 = _out_proj(att, ssd, x2, w_o[:ATT_WIDTH], w_o[ATT_WIDTH:],
                       ln1_g[l][None, :], ln1_b[l][None, :], min(OUT_TM, n))
        pad = jnp.zeros((d, LANES - ROUTE_E0 - N_EXPERTS), F32)
        wr = jnp.concatenate([w_route_group[l].astype(F32), w_route_expert[l].astype(F32), pad], 1)
        wr_hi = wr.astype(BF16)
        wr_lo = (wr - wr_hi.astype(F32)).astype(BF16)
        br = jnp.concatenate([b_route_group[l].astype(F32), b_route_expert[l].astype(F32),
                              pad[0]])[None, :]
        x = _moe(h2, wr_hi, wr_lo, br, w_gate[l].astype(BF16), w_up[l].astype(BF16),
                 w_down[l].astype(BF16), ln2_g[l][None, :], ln2_b[l][None, :]).reshape(bsz, seq, d)
    return x
```

```python
import functools

import jax
import jax.numpy as jnp
import numpy as np
from jax import lax
from jax.experimental import pallas as pl
from jax.experimental.pallas import tpu as pltpu

F32 = jnp.float32
BF16 = jnp.bfloat16

D_MODEL = 1024
HEAD_DIM = 64
ATT_WIDTH = 512
ATT_HEADS = 8
IDX_HEADS = 4
IDX_DIM = 64
TOPK_MAX = 256
ROPE_THETA = 10000.0
INDEXER_SCALE = (IDX_HEADS ** -0.5) * (IDX_DIM ** -0.5)
SSD_WIDTH = 512
SSD_HEADS = 8
SSD_HEAD_DIM = 64
SSD_GROUPS = 2
D_STATE = 64
CONV_WIDTH = 4
CONV_CH = SSD_WIDTH + 2 * SSD_GROUPS * D_STATE
CHUNK = 128
N_GROUPS_MOE = 4
EXPERTS_PER_GROUP = 4
N_EXPERTS = 16
EXPERT_FF = 256
DEPTH = 1
ALPHA = (2 * DEPTH) ** 0.25
LN_EPS = 1e-5

LANES = 128
SUBLANES = 8
BF16_ROWS = 16
INF_KEY = 0x7F800000
KEY_STEP16 = 1 << 16
VMEM_LIMIT = 56 * 1024 * 1024

C_Q = 0
C_IQ = 512
C_K = 768
C_IK = 832
C_MISC = 896
MISC_IW = 64
MISC_DT = 72
C_Z = 1024
C_XBC = 1536
IN_COLS = 2304

TQ = 256
KC = 256

Q_SCALE = HEAD_DIM ** -0.5 * float(np.log2(np.e))

NT_DIMS = (((1,), (1,)), ((), ()))


def _nt_dot(a, b):
    return lax.dot_general(a, b, NT_DIMS, preferred_element_type=F32)


def _fold_rows(x, op, rows=SUBLANES):
    slabs = [x[r * rows:(r + 1) * rows, :] for r in range(x.shape[0] // rows)]
    while len(slabs) > 1:
        nxt = [op(slabs[i], slabs[i + 1]) for i in range(0, len(slabs) - 1, 2)]
        if len(slabs) % 2:
            nxt.append(slabs[-1])
        slabs = nxt
    return slabs[0]


CONV_PAD = 8
PROJ_COLS = 256


def _silu(x):
    return x * (1.0 / (1.0 + jnp.exp(-x)))


def _in_proj_kernel(x_ref, w_ref, cos_ref, s1_ref, s2_ref, mscale_ref, cw_ref, cb_ref,
                    q_ref, iq_ref, k_ref, ik_ref, vt_ref, misc_ref, zs_ref, xc_ref, hist_ref,
                    *, nblk_seq):
    @pl.when(pl.program_id(0) == 0)
    def _():
        hist_ref[...] = jnp.zeros_like(hist_ref)

    xb = x_ref[...].astype(BF16)
    cos = cos_ref[...]
    s1 = s1_ref[...]
    s2 = s2_ref[...]

    def mm(c0, width):
        return jnp.dot(xb, w_ref[:, c0:c0 + width], preferred_element_type=F32)

    def rope(y):
        fwd = pltpu.roll(y, LANES - HEAD_DIM // 2, 1)
        bwd = pltpu.roll(y, HEAD_DIM // 2, 1)
        return y * cos + fwd * s1 + bwd * s2

    def rope_wide(y, scale):
        parts = []
        for c in range(y.shape[1] // LANES):
            r = rope(y[:, c * LANES:(c + 1) * LANES])
            parts.append(r * scale if scale != 1.0 else r)
        return parts

    tm = x_ref.shape[0]
    seq_start = pl.program_id(0) % nblk_seq == 0
    for c0 in range(0, CONV_CH, PROJ_COLS):
        cols = slice(c0, c0 + PROJ_COLS)
        xbc = mm(C_XBC + c0, PROJ_COLS)
        hist = jnp.where(seq_start, 0.0, hist_ref[:, cols])
        xp = jnp.concatenate([hist, xbc], axis=0)
        acc = cw_ref[0:1, cols] * xp
        for j in range(1, CONV_WIDTH):
            acc = pltpu.roll(acc, 1, 0) + cw_ref[j:j + 1, cols] * xp
        xc_ref[:, cols] = _silu(acc[CONV_PAD:, :] + cb_ref[:, cols])
        hist_ref[:, cols] = xbc[tm - CONV_PAD:, :]
    for c0 in range(0, SSD_WIDTH, PROJ_COLS):
        zs_ref[:, c0:c0 + PROJ_COLS] = _silu(mm(C_Z + c0, PROJ_COLS))

    for c, r in enumerate(rope_wide(mm(C_Q, ATT_WIDTH), Q_SCALE)):
        q_ref[:, c * LANES:(c + 1) * LANES] = r.astype(BF16)
    for c, r in enumerate(rope_wide(mm(C_IQ, IDX_HEADS * IDX_DIM), 1.0)):
        iq_ref[:, c * LANES:(c + 1) * LANES] = r.astype(BF16)
    kk = rope(mm(C_K, LANES))
    k_ref[...] = kk[:, :HEAD_DIM].astype(BF16)
    ik_ref[...] = kk[:, HEAD_DIM:].astype(BF16)
    misc = mm(C_MISC, LANES) * mscale_ref[...]
    misc_ref[...] = misc
    lane = lax.broadcasted_iota(jnp.int32, misc.shape, 1)
    vext = jnp.where(lane < HEAD_DIM, misc, jnp.where(lane == HEAD_DIM, 1.0, 0.0))
    for c in range(vt_ref.shape[0]):
        vt_ref[c] = vext[c * KC:(c + 1) * KC, :].T.astype(BF16)


def _in_proj(x2, w_perm, cos_t, s1_t, s2_t, mscale, conv_w, conv_b, seq, tm):
    n = x2.shape[0]
    nblk_seq = seq // tm
    row = lambda i: (i, 0)
    tab = lambda i: (i % nblk_seq, 0)
    const = lambda i: (0, 0)
    outs = [
        ((n, ATT_WIDTH), BF16, pl.BlockSpec((tm, ATT_WIDTH), row)),
        ((n, IDX_HEADS * IDX_DIM), BF16, pl.BlockSpec((tm, IDX_HEADS * IDX_DIM), row)),
        ((n, HEAD_DIM), BF16, pl.BlockSpec((tm, HEAD_DIM), row)),
        ((n, IDX_DIM), BF16, pl.BlockSpec((tm, IDX_DIM), row)),
        ((n // KC, LANES, KC), BF16, pl.BlockSpec((tm // KC, LANES, KC), lambda i: (i, 0, 0))),
        ((n, LANES), F32, pl.BlockSpec((tm, LANES), row)),
        ((n, SSD_WIDTH), F32, pl.BlockSpec((tm, SSD_WIDTH), row)),
        ((n, CONV_CH), F32, pl.BlockSpec((tm, CONV_CH), row)),
    ]
    return pl.pallas_call(
        functools.partial(_in_proj_kernel, nblk_seq=nblk_seq),
        grid=(n // tm,),
        in_specs=[
            pl.BlockSpec((tm, D_MODEL), row),
            pl.BlockSpec((D_MODEL, IN_COLS), const),
            pl.BlockSpec((tm, LANES), tab),
            pl.BlockSpec((tm, LANES), tab),
            pl.BlockSpec((tm, LANES), tab),
            pl.BlockSpec((1, LANES), const),
            pl.BlockSpec((CONV_WIDTH, CONV_CH), const),
            pl.BlockSpec((1, CONV_CH), const),
        ],
        out_specs=[spec for _, _, spec in outs],
        out_shape=[jax.ShapeDtypeStruct(shape, dt) for shape, dt, _ in outs],
        scratch_shapes=[pltpu.VMEM((CONV_PAD, CONV_CH), F32)],
        compiler_params=pltpu.CompilerParams(
            dimension_semantics=("arbitrary",), vmem_limit_bytes=VMEM_LIMIT),
        name="in_proj",
    )(x2, w_perm, cos_t, s1_t, s2_t, mscale, conv_w, conv_b)


def _dsa_kernel(q_ref, iq_ref, misc_ref, k_ref, ik_ref, vt_ref, o_ref,
                sc_ref, sc16_ref, lg_ref, acc_ref, *, topk):
    qi = pl.program_id(1)
    nj = qi + 1
    neg_inf = -jnp.inf
    kf = float(topk)
    key_i = lax.broadcasted_iota(jnp.int32, (KC, TQ), 0)
    qry_i = lax.broadcasted_iota(jnp.int32, (KC, TQ), 1)
    causal = key_i <= qry_i

    def key_rows(j):
        return pl.ds(pl.multiple_of(j * KC, KC), KC)

    iw_t = misc_ref[...].T[MISC_IW:MISC_IW + SUBLANES, :]

    def scores_body(j, carry):
        ikj = ik_ref[key_rows(j), :]
        sc = None
        for h in range(IDX_HEADS):
            d = _nt_dot(ikj, iq_ref[:, h * IDX_DIM:(h + 1) * IDX_DIM])
            term = iw_t[h:h + 1, :] * jnp.maximum(d, 0.0)
            sc = term if sc is None else sc + term
        sc_ref[j] = sc
        sc16_ref[j] = sc.astype(BF16)
        return carry

    lax.fori_loop(0, nj, scores_body, 0)
    sc_diag = jnp.where(causal, sc_ref[qi], neg_inf)
    sc_ref[qi] = sc_diag
    sc16_ref[qi] = sc_diag.astype(BF16)

    def key_to_float(key):
        f = lax.bitcast_convert_type(jnp.where(key < 0, -key, key), F32)
        return jnp.where(key < 0, -f, f)

    one16 = jnp.ones((), BF16)
    zero16 = jnp.zeros((), BF16)

    def count16_ge(cand16):
        def body(j, acc):
            ind = jnp.where(sc16_ref[j] >= cand16, one16, zero16)
            return acc + _fold_rows(ind, jnp.add, BF16_ROWS).astype(F32)

        acc = lax.fori_loop(0, nj, body, jnp.zeros((BF16_ROWS, TQ), F32))
        return jnp.sum(acc, axis=0, keepdims=True)

    def count_ge(cand):
        def body(j, acc):
            ind = jnp.where(sc_ref[j] >= cand, 1.0, 0.0)
            return acc + _fold_rows(ind, jnp.add)

        acc = lax.fori_loop(0, nj, body, jnp.zeros((SUBLANES, TQ), F32))
        return jnp.sum(acc, axis=0, keepdims=True)

    def coarse_body(b, m):
        trial = m + lax.shift_left(jnp.int32(1), jnp.int32(15) - b)
        cand16 = key_to_float(trial * KEY_STEP16).astype(BF16)
        return jnp.where(count16_ge(cand16) >= kf, trial, m)

    m16 = lax.fori_loop(0, 16, coarse_body, jnp.full((1, TQ), -(INF_KEY // KEY_STEP16), jnp.int32))
    key_base = jnp.maximum(m16 * KEY_STEP16 - (KEY_STEP16 // 2 + 1), -INF_KEY)

    def fine_body(b, off):
        trial = off + lax.shift_left(jnp.int32(1), jnp.int32(16) - b)
        ok = count_ge(key_to_float(key_base + trial)) >= kf
        return jnp.where(ok, trial, off)

    off = lax.fori_loop(0, 17, fine_body, jnp.zeros((1, TQ), jnp.int32))
    lo = key_to_float(key_base + off)
    hi = key_to_float(key_base + off + 1)
    need = kf - count_ge(hi)

    lower = (lax.broadcasted_iota(jnp.int32, (KC, KC), 0)
             > lax.broadcasted_iota(jnp.int32, (KC, KC), 1)).astype(BF16)

    acc_ref[...] = jnp.zeros_like(acc_ref)
    lg_ref[1] = jnp.full(lg_ref.shape[1:], neg_inf, F32)
    no_max = tuple(jnp.minimum(need, neg_inf) for _ in range(ATT_HEADS))

    def logits_stage(j, taken):
        s = sc_ref[j]
        gt = s >= hi
        eq = jnp.logical_and(s >= lo, jnp.logical_not(gt))
        eqf = jnp.where(eq, 1.0, 0.0)
        before = jnp.dot(lower, eqf.astype(BF16), preferred_element_type=F32) + taken
        sel = jnp.logical_or(gt, jnp.logical_and(eq, before < need))
        sel = jnp.logical_and(sel, jnp.logical_or(causal, j < qi))
        bias = jnp.where(sel, 0.0, neg_inf)
        taken = taken + jnp.sum(_fold_rows(eqf, jnp.add), axis=0, keepdims=True)
        kj = k_ref[key_rows(j), :]
        slot = j % 2
        cms = []
        for h in range(ATT_HEADS):
            lg = _nt_dot(kj, q_ref[:, h * HEAD_DIM:(h + 1) * HEAD_DIM]) + bias
            lg_ref[slot, h] = lg
            cms.append(jnp.max(_fold_rows(lg, jnp.maximum), axis=0, keepdims=True))
        return taken, tuple(cms)

    def pv_stage(jp, cms, ms):
        vtj = vt_ref[jp]
        slot = jp % 2
        new_ms = []
        for h in range(ATT_HEADS):
            m_new = jnp.maximum(ms[h], cms[h])
            shift = jnp.where(m_new == neg_inf, 0.0, m_new)
            p = jnp.exp2(lg_ref[slot, h] - shift).astype(BF16)
            alpha = jnp.exp2(ms[h] - shift)
            acc_ref[h] = alpha * acc_ref[h] + jnp.dot(vtj, p, preferred_element_type=F32)
            new_ms.append(m_new)
        return tuple(new_ms)

    def att_body(j, carry):
        taken, cms_prev, ms = carry
        ms = pv_stage(jnp.maximum(j - 1, 0) + (j == 0).astype(jnp.int32), cms_prev, ms)
        taken, cms = logits_stage(j, taken)
        return taken, cms, ms

    _, cms_last, ms = lax.fori_loop(0, nj, att_body, (jnp.zeros((1, TQ), F32), no_max, no_max))
    pv_stage(qi, cms_last, ms)
    for pair in range(ATT_HEADS // 2):
        halves = []
        for h in (2 * pair, 2 * pair + 1):
            a = acc_ref[h]
            halves.append(a[:HEAD_DIM, :] * (1.0 / a[HEAD_DIM:HEAD_DIM + 1, :]))
        blk = jnp.concatenate(halves, axis=0)
        o_ref[:, pair * LANES:(pair + 1) * LANES] = blk.T.astype(BF16)


def _dsa(q, iq, misc, k, ik, vt, b, s, topk):
    nq = s // TQ
    tile = lambda bi, qi: (bi * nq + qi, 0)
    full = lambda bi, qi: (bi, 0)
    return pl.pallas_call(
        functools.partial(_dsa_kernel, topk=topk),
        grid=(b, nq),
        in_specs=[
            pl.BlockSpec((TQ, ATT_WIDTH), tile),
            pl.BlockSpec((TQ, IDX_HEADS * IDX_DIM), tile),
            pl.BlockSpec((TQ, LANES), tile),
            pl.BlockSpec((s, HEAD_DIM), full),
            pl.BlockSpec((s, IDX_DIM), full),
            pl.BlockSpec((s // KC, LANES, KC), lambda bi, qi: (bi, 0, 0)),
        ],
        out_specs=pl.BlockSpec((TQ, ATT_WIDTH), tile),
        out_shape=jax.ShapeDtypeStruct((b * s, ATT_WIDTH), BF16),
        scratch_shapes=[
            pltpu.VMEM((nq, KC, TQ), F32),
            pltpu.VMEM((nq, KC, TQ), BF16),
            pltpu.VMEM((2, ATT_HEADS, KC, TQ), F32),
            pltpu.VMEM((ATT_HEADS, LANES, TQ), F32),
        ],
        compiler_params=pltpu.CompilerParams(
            dimension_semantics=("parallel", "arbitrary"), vmem_limit_bytes=VMEM_LIMIT),
        name="dsa_attention",
    )(q, iq, misc, k, ik, vt)


SSD_TT = 512


def _ssd_kernel(xc_ref, zs_ref, misc_ref, dtb_ref, arep_ref, dskip_ref, nw_ref,
                expand_ref, triu_ref, o_ref, state_ref):
    t = pl.program_id(1)
    tt = xc_ref.shape[0]

    @pl.when(t == 0)
    def _():
        state_ref[...] = jnp.zeros_like(state_ref)

    tri = (lax.broadcasted_iota(jnp.int32, (CHUNK, CHUNK), 0)
           >= lax.broadcasted_iota(jnp.int32, (CHUNK, CHUNK), 1))
    left_head = lax.broadcasted_iota(jnp.int32, (CHUNK, LANES), 1) < SSD_HEAD_DIM
    left_head_n = lax.broadcasted_iota(jnp.int32, (D_STATE, LANES), 1) < SSD_HEAD_DIM
    gn = SSD_GROUPS * D_STATE
    pairs_per_group = SSD_HEADS // SSD_GROUPS // 2
    expand = expand_ref[...]
    triu = triu_ref[...]
    zpad = jnp.zeros((SUBLANES, CHUNK), F32)

    for c in range(tt // CHUNK):
        rows = slice(c * CHUNK, (c + 1) * CHUNK)
        raw = misc_ref[rows, :].T[MISC_DT:MISC_DT + SSD_HEADS, :] + dtb_ref[...]
        dt_t = jnp.maximum(raw, 0.0) + jnp.log1p(jnp.exp(-jnp.abs(raw)))
        adt = dt_t * arep_ref[...]
        hi = adt.astype(BF16).astype(F32)
        r1 = adt - hi
        mid = r1.astype(BF16).astype(F32)
        pieces = jnp.concatenate([hi, mid, r1 - mid, zpad], axis=0).astype(BF16)
        cs = jnp.dot(pieces, triu, preferred_element_type=F32)
        acum_t = (cs[0:SUBLANES] + cs[SUBLANES:2 * SUBLANES]) + cs[2 * SUBLANES:3 * SUBLANES]
        a_last = acum_t[:, CHUNK - 1:CHUNK]
        ddt_t = jnp.exp(a_last - acum_t) * dt_t
        acum = jnp.concatenate(
            [acum_t, jnp.zeros((LANES - SSD_HEADS, CHUNK), F32)], axis=0).T
        ea = jnp.exp(acum)
        ea_hi = ea.astype(BF16)
        ea_lo = (ea - ea_hi.astype(F32)).astype(BF16)
        ea_x = (jnp.dot(ea_hi, expand, preferred_element_type=F32)
                + jnp.dot(ea_lo, expand, preferred_element_type=F32))
        xs = xc_ref[rows, 0:SSD_WIDTH]
        bm = xc_ref[rows, SSD_WIDTH:SSD_WIDTH + gn]
        cm = xc_ref[rows, SSD_WIDTH + gn:SSD_WIDTH + 2 * gn]
        bm_t = bm.T
        xs16 = xs.astype(BF16)
        bm16 = bm.astype(BF16)
        cm16 = cm.astype(BF16)
        y_pairs = [None] * (SSD_HEADS // 2)
        for g in range(SSD_GROUPS):
            cg = cm16[:, g * D_STATE:(g + 1) * D_STATE]
            bg = bm16[:, g * D_STATE:(g + 1) * D_STATE]
            bg_t = bm_t[g * D_STATE:(g + 1) * D_STATE, :]
            gmat = _nt_dot(cg, bg)
            for pp in range(pairs_per_group):
                pair = g * pairs_per_group + pp
                lanes = slice(pair * LANES, (pair + 1) * LANES)
                xp = xs16[:, lanes]
                y_halves, s_halves = [], []
                for h in (2 * pair, 2 * pair + 1):
                    col = acum[:, h:h + 1]
                    rowv = acum_t[h:h + 1, :]
                    lmat = jnp.exp(jnp.where(tri, col - rowv, -jnp.inf))
                    mmat = (gmat * lmat * dt_t[h:h + 1, :]).astype(BF16)
                    y_halves.append(jnp.dot(mmat, xp, preferred_element_type=F32))
                    bs = (bg_t * ddt_t[h:h + 1, :]).astype(BF16)
                    s_halves.append(jnp.dot(bs, xp, preferred_element_type=F32))
                y_diag = jnp.where(left_head, y_halves[0], y_halves[1])
                new = jnp.where(left_head_n, s_halves[0], s_halves[1])
                prev = state_ref[pair]
                y_off = jnp.dot(cg, prev.astype(BF16), preferred_element_type=F32) * ea_x[:, lanes]
                state_ref[pair] = prev * ea_x[CHUNK - 1:CHUNK, lanes] + new
                y_pairs[pair] = y_diag + y_off
        y = jnp.concatenate(y_pairs, axis=1) + dskip_ref[...] * xs
        y = y * zs_ref[rows, :]
        ms = jnp.mean(y * y, axis=1, keepdims=True)
        o_ref[rows, :] = (y * lax.rsqrt(ms + LN_EPS) * nw_ref[...]).astype(BF16)


def _ssd(xc, zs, misc, dtb_rep, a_rep, dskip_row, nw_row, b, s):
    tt = min(SSD_TT, s)
    nt = s // tt
    tile = lambda bi, ti: (bi * nt + ti, 0)
    const = lambda bi, ti: (0, 0)
    expand = (jnp.arange(LANES)[:, None] == jnp.arange(SSD_WIDTH)[None, :] // SSD_HEAD_DIM).astype(BF16)
    triu = (jnp.arange(CHUNK)[:, None] <= jnp.arange(CHUNK)[None, :]).astype(BF16)
    return pl.pallas_call(
        _ssd_kernel,
        grid=(b, s // tt),
        in_specs=[
            pl.BlockSpec((tt, CONV_CH), tile),
            pl.BlockSpec((tt, SSD_WIDTH), tile),
            pl.BlockSpec((tt, LANES), tile),
            pl.BlockSpec((SSD_HEADS, CHUNK), const),
            pl.BlockSpec((SSD_HEADS, CHUNK), const),
            pl.BlockSpec((1, SSD_WIDTH), const),
            pl.BlockSpec((1, SSD_WIDTH), const),
            pl.BlockSpec((LANES, SSD_WIDTH), const),
            pl.BlockSpec((CHUNK, CHUNK), const),
        ],
        out_specs=pl.BlockSpec((tt, SSD_WIDTH), tile),
        out_shape=jax.ShapeDtypeStruct((b * s, SSD_WIDTH), BF16),
        scratch_shapes=[
            pltpu.VMEM((SSD_HEADS // 2, D_STATE, LANES), F32),
        ],
        compiler_params=pltpu.CompilerParams(
            dimension_semantics=("parallel", "arbitrary"), vmem_limit_bytes=VMEM_LIMIT),
        name="ssd_mixer",
    )(xc, zs, misc, dtb_rep, a_rep, dskip_row, nw_row, expand, triu)


def _layer_norm(y, g, b):
    mu = jnp.mean(y, axis=1, keepdims=True)
    yc = y - mu
    var = jnp.mean(yc * yc, axis=1, keepdims=True)
    return yc * lax.rsqrt(var + LN_EPS) * g + b


OUT_TM = 1024
OUT_SUB = 256


def _out_proj_kernel(att_ref, ssd_ref, x_ref, wa_ref, ws_ref, g_ref, b_ref, h_ref):
    for r0 in range(0, x_ref.shape[0], OUT_SUB):
        rows = slice(r0, r0 + OUT_SUB)
        mixed = jnp.dot(att_ref[rows, :], wa_ref[...], preferred_element_type=F32)
        mixed = mixed + jnp.dot(ssd_ref[rows, :], ws_ref[...], preferred_element_type=F32)
        h_ref[rows, :] = _layer_norm(ALPHA * x_ref[rows, :] + mixed, g_ref[...], b_ref[...])


def _out_proj(att2, ssd2, x2, w_att, w_ssd, g_row, b_row, tm):
    n = x2.shape[0]
    row = lambda i: (i, 0)
    const = lambda i: (0, 0)
    return pl.pallas_call(
        _out_proj_kernel,
        grid=(n // tm,),
        in_specs=[
            pl.BlockSpec((tm, ATT_WIDTH), row),
            pl.BlockSpec((tm, SSD_WIDTH), row),
            pl.BlockSpec((tm, D_MODEL), row),
            pl.BlockSpec((ATT_WIDTH, D_MODEL), const),
            pl.BlockSpec((SSD_WIDTH, D_MODEL), const),
            pl.BlockSpec((1, D_MODEL), const),
            pl.BlockSpec((1, D_MODEL), const),
        ],
        out_specs=pl.BlockSpec((tm, D_MODEL), row),
        out_shape=jax.ShapeDtypeStruct((n, D_MODEL), F32),
        compiler_params=pltpu.CompilerParams(
            dimension_semantics=("parallel",), vmem_limit_bytes=VMEM_LIMIT),
        name="out_proj_ln",
    )(att2, ssd2, x2, w_att, w_ssd, g_row, b_row)


ROUTE_E0 = N_GROUPS_MOE
MOE_TM = 512
MOE_CAP = 160


def _first_max(vals):
    best = vals[0]
    for v in vals[1:]:
        best = jnp.maximum(best, v)
    idx = jnp.full(best.shape, len(vals) - 1, jnp.int32)
    for i in range(len(vals) - 2, -1, -1):
        idx = jnp.where(vals[i] == best, i, idx)
    return best, idx


def _moe_kernel(h_ref, wrh_ref, wrl_ref, br_ref, wg_ref, wu_ref, wd_ref, g_ref, b_ref, upper_ref,
                o_ref, hb_ref, hp_ref, gp_ref, yp_ref):
    tm = h_ref.shape[0]
    h = h_ref[...]
    h_hi = h.astype(BF16)
    hb_ref[...] = h_hi
    h_lo = (h - h_hi.astype(F32)).astype(BF16)
    wrh = wrh_ref[...]
    logits = (jnp.dot(h_hi, wrh, preferred_element_type=F32)
              + jnp.dot(h_lo, wrh, preferred_element_type=F32)
              + jnp.dot(h_hi, wrl_ref[...], preferred_element_type=F32)) + br_ref[...]
    lt = logits.T
    row = lambda r: lt[r:r + 1, :]
    gl = [row(r) for r in range(N_GROUPS_MOE)]
    gmax, gidx = _first_max(gl)
    denom = jnp.exp(gl[0] - gmax)
    for v in gl[1:]:
        denom = denom + jnp.exp(v - gmax)
    gprob = 1.0 / denom
    el = []
    for k in range(EXPERTS_PER_GROUP):
        v = row(ROUTE_E0 + (N_GROUPS_MOE - 1) * EXPERTS_PER_GROUP + k)
        for gg in range(N_GROUPS_MOE - 2, -1, -1):
            v = jnp.where(gidx == gg, row(ROUTE_E0 + gg * EXPERTS_PER_GROUP + k), v)
        el.append(v)
    l1, i1 = _first_max(el)
    l2, i2 = _first_max([jnp.where(i1 == k, -jnp.inf, el[k]) for k in range(EXPERTS_PER_GROUP)])
    e2 = jnp.exp(l2 - l1)
    w1 = gprob / (1.0 + e2)
    w2 = gprob * e2 / (1.0 + e2)
    gate4 = [jnp.where(i1 == k, w1, jnp.where(i2 == k, w2, 0.0)) for k in range(EXPERTS_PER_GROUP)]

    member = [jnp.where(gidx == gg, 1.0, 0.0) for gg in range(N_GROUPS_MOE)]
    member_blk = jnp.concatenate(
        member + [jnp.zeros((2 * SUBLANES - N_GROUPS_MOE, tm), F32)], axis=0).astype(BF16)
    earlier = jnp.dot(member_blk, upper_ref[...], preferred_element_type=F32)
    counts = [jnp.sum(m, axis=1, keepdims=True) for m in member]
    most = counts[0]
    for c in counts[1:]:
        most = jnp.maximum(most, c)
    fits = jnp.max(most) <= float(MOE_CAP)
    slot = member[0] * earlier[0:1, :]
    for gg in range(1, N_GROUPS_MOE):
        slot = slot + member[gg] * (earlier[gg:gg + 1, :] + float(gg * MOE_CAP))
    tok = jnp.concatenate(
        gate4 + [slot, gidx.astype(F32), jnp.zeros((LANES - EXPERTS_PER_GROUP - 2, tm), F32)],
        axis=0).T
    slot_lane, grp_lane = EXPERTS_PER_GROUP, EXPERTS_PER_GROUP + 1
    rows_p = N_GROUPS_MOE * MOE_CAP

    @pl.when(fits)
    def _():
        hb = hb_ref[...]
        place = jnp.where(lax.broadcasted_iota(jnp.int32, (rows_p, tm), 0) == slot.astype(jnp.int32),
                          1.0, 0.0).astype(BF16)
        hp_ref[...] = jnp.dot(place, hb, preferred_element_type=F32).astype(BF16)
        t_hi = tok.astype(BF16)
        t_lo = (tok - t_hi.astype(F32)).astype(BF16)
        both = jnp.dot(place, jnp.concatenate([t_hi, t_lo], axis=1), preferred_element_type=F32)
        gp_ref[...] = both[:, :LANES] + both[:, LANES:]
        for gg in range(N_GROUPS_MOE):
            rows = slice(gg * MOE_CAP, (gg + 1) * MOE_CAP)
            xg = hp_ref[rows, :]
            acc = None
            for k in range(EXPERTS_PER_GROUP):
                e = gg * EXPERTS_PER_GROUP + k
                a = jnp.dot(xg, wg_ref[e], preferred_element_type=F32)
                u = jnp.dot(xg, wu_ref[e], preferred_element_type=F32)
                hid = (_silu(a) * u * gp_ref[rows, k:k + 1]).astype(BF16)
                part = jnp.dot(hid, wd_ref[e], preferred_element_type=F32)
                acc = part if acc is None else acc + part
            yp_ref[rows, :] = acc.astype(BF16)
        back = jnp.where(lax.broadcasted_iota(jnp.int32, (tm, rows_p), 1)
                         == tok[:, slot_lane:slot_lane + 1].astype(jnp.int32), 1.0, 0.0).astype(BF16)
        for r0 in range(0, tm, OUT_SUB):
            rr = slice(r0, r0 + OUT_SUB)
            y = jnp.dot(back[rr, :], yp_ref[...], preferred_element_type=F32)
            o_ref[rr, :] = _layer_norm(ALPHA * h_ref[rr, :] + y, g_ref[...], b_ref[...])

    @pl.when(jnp.logical_not(fits))
    def _():
        hb = hb_ref[...]
        acc = None
        for gg in range(N_GROUPS_MOE):
            in_grp = tok[:, grp_lane:grp_lane + 1] == float(gg)
            for k in range(EXPERTS_PER_GROUP):
                e = gg * EXPERTS_PER_GROUP + k
                a = jnp.dot(hb, wg_ref[e], preferred_element_type=F32)
                u = jnp.dot(hb, wu_ref[e], preferred_element_type=F32)
                gate = jnp.where(in_grp, tok[:, k:k + 1], 0.0)
                hid = (_silu(a) * u * gate).astype(BF16)
                part = jnp.dot(hid, wd_ref[e], preferred_element_type=F32)
                acc = part if acc is None else acc + part
        o_ref[...] = _layer_norm(ALPHA * h_ref[...] + acc, g_ref[...], b_ref[...])


def _moe(h2, wr_hi, wr_lo, br, wg, wu, wd, g_row, b_row):
    n = h2.shape[0]
    tm = min(MOE_TM, n)
    rows_p = N_GROUPS_MOE * MOE_CAP
    row = lambda i: (i, 0)
    const = lambda i: (0, 0)
    whole = lambda i: (0, 0, 0)
    once = pl.Buffered(1)
    upper = (jnp.arange(tm)[:, None] < jnp.arange(tm)[None, :]).astype(BF16)
    return pl.pallas_call(
        _moe_kernel,
        grid=(n // tm,),
        in_specs=[
            pl.BlockSpec((tm, D_MODEL), row),
            pl.BlockSpec((D_MODEL, LANES), const),
            pl.BlockSpec((D_MODEL, LANES), const),
            pl.BlockSpec((1, LANES), const),
            pl.BlockSpec((N_EXPERTS, D_MODEL, EXPERT_FF), whole, pipeline_mode=once),
            pl.BlockSpec((N_EXPERTS, D_MODEL, EXPERT_FF), whole, pipeline_mode=once),
            pl.BlockSpec((N_EXPERTS, EXPERT_FF, D_MODEL), whole, pipeline_mode=once),
            pl.BlockSpec((1, D_MODEL), const),
            pl.BlockSpec((1, D_MODEL), const),
            pl.BlockSpec((tm, tm), const, pipeline_mode=once),
        ],
        out_specs=pl.BlockSpec((tm, D_MODEL), row),
        out_shape=jax.ShapeDtypeStruct((n, D_MODEL), F32),
        scratch_shapes=[
            pltpu.VMEM((tm, D_MODEL), BF16),
            pltpu.VMEM((rows_p, D_MODEL), BF16),
            pltpu.VMEM((rows_p, LANES), F32),
            pltpu.VMEM((rows_p, D_MODEL), BF16),
        ],
        compiler_params=pltpu.CompilerParams(
            dimension_semantics=("parallel",), vmem_limit_bytes=VMEM_LIMIT),
        name="hier_moe_ln",
    )(h2, wr_hi, wr_lo, br, wg, wu, wd, g_row, b_row, upper)


def _rope_tables(seq):
    inv = ROPE_THETA ** (-jnp.arange(0, HEAD_DIM, 2, dtype=F32) / HEAD_DIM)
    ang = jnp.arange(seq, dtype=F32)[:, None] * inv[None, :]
    cos, sin = jnp.cos(ang), jnp.sin(ang)
    zero = jnp.zeros_like(sin)
    cos_t = jnp.tile(cos, (1, LANES // (HEAD_DIM // 2)))
    s1_t = jnp.tile(jnp.concatenate([-sin, zero], 1), (1, LANES // HEAD_DIM))
    s2_t = jnp.tile(jnp.concatenate([zero, sin], 1), (1, LANES // HEAD_DIM))
    return cos_t, s1_t, s2_t


def _permute_w_in(w):
    sizes = (ATT_WIDTH, HEAD_DIM, HEAD_DIM, IDX_HEADS * IDX_DIM, IDX_DIM, IDX_HEADS,
             SSD_WIDTH, CONV_CH, SSD_HEADS)
    pts = np.cumsum((0,) + sizes)
    q, k, v, iq, ik, iw, z, xbc, dt = [w[:, pts[i]:pts[i + 1]] for i in range(len(sizes))]
    d = w.shape[0]
    pad = lambda n: jnp.zeros((d, n), w.dtype)
    misc = jnp.concatenate([v, iw, pad(MISC_DT - MISC_IW - IDX_HEADS), dt,
                            pad(LANES - MISC_DT - SSD_HEADS)], 1)
    return jnp.concatenate([q, iq, k, ik, misc, z, xbc], 1).astype(BF16)


def _head_rep(vals):
    return jnp.broadcast_to(vals.astype(F32)[:, None], (vals.shape[0], CHUNK))


def kernel(x, w_in, conv_w, conv_b, dt_bias, a_log, d_skip, ssd_norm_w, w_out, ln1_g, ln1_b,
           w_route_group, b_route_group, w_route_expert, b_route_expert, w_gate, w_up,
           w_down, ln2_g, ln2_b):
    bsz, seq, d = x.shape
    n = bsz * seq
    topk = min(TOPK_MAX, seq // 4)
    tm = 512
    assert d == D_MODEL and TQ == KC and seq % TQ == 0 and seq % tm == 0 and topk <= KC
    cos_t, s1_t, s2_t = _rope_tables(seq)
    mscale = jnp.ones((1, LANES), F32).at[0, MISC_IW:MISC_IW + IDX_HEADS].set(INDEXER_SCALE)
    for l in range(DEPTH):
        x2 = x.reshape(n, d)
        q, iq, k, ik, vt, misc, zs, xc = _in_proj(
            x2, _permute_w_in(w_in[l]), cos_t, s1_t, s2_t, mscale,
            conv_w[l].astype(F32), conv_b[l].astype(F32)[None, :], seq, tm)
        att = _dsa(q, iq, misc, k, ik, vt, bsz, seq, topk)
        ssd = _ssd(
            xc, zs, misc,
            _head_rep(dt_bias[l]), _head_rep(-jnp.exp(a_log[l].astype(F32))),
            jnp.repeat(d_skip[l].astype(F32), SSD_HEAD_DIM)[None, :], ssd_norm_w[l][None, :],
            bsz, seq)
        w_o = w_out[l].astype(BF16)
        h2 = _out_proj(att, ssd, x2, w_o[:ATT_WIDTH], w_o[ATT_WIDTH:],
                       ln1_g[l][None, :], ln1_b[l][None, :], min(OUT_TM, n))
        pad = jnp.zeros((d, LANES - ROUTE_E0 - N_EXPERTS), F32)
        wr = jnp.concatenate([w_route_group[l].astype(F32), w_route_expert[l].astype(F32), pad], 1)
        wr_hi = wr.astype(BF16)
        wr_lo = (wr - wr_hi.astype(F32)).astype(BF16)
        br = jnp.concatenate([b_route_group[l].astype(F32), b_route_expert[l].astype(F32),
                              pad[0]])[None, :]
        x = _moe(h2, wr_hi, wr_lo, br, w_gate[l].astype(BF16), w_up[l].astype(BF16),
                 w_down[l].astype(BF16), ln2_g[l][None, :], ln2_b[l][None, :]).reshape(bsz, seq, d)
    return x
```

```python
import functools

import jax
import jax.numpy as jnp
import numpy as np
from jax import lax
from jax.experimental import pallas as pl
from jax.experimental.pallas import tpu as pltpu

F32 = jnp.float32
BF16 = jnp.bfloat16

D_MODEL = 1024
HEAD_DIM = 64
ATT_WIDTH = 512
ATT_HEADS = 8
IDX_HEADS = 4
IDX_DIM = 64
TOPK_MAX = 256
ROPE_THETA = 10000.0
INDEXER_SCALE = (IDX_HEADS ** -0.5) * (IDX_DIM ** -0.5)
SSD_WIDTH = 512
SSD_HEADS = 8
SSD_HEAD_DIM = 64
SSD_GROUPS = 2
D_STATE = 64
CONV_WIDTH = 4
CONV_CH = SSD_WIDTH + 2 * SSD_GROUPS * D_STATE
CHUNK = 128
N_GROUPS_MOE = 4
EXPERTS_PER_GROUP = 4
N_EXPERTS = 16
EXPERT_FF = 256
DEPTH = 1
ALPHA = (2 * DEPTH) ** 0.25
LN_EPS = 1e-5

LANES = 128
SUBLANES = 8
BF16_ROWS = 16
INF_KEY = 0x7F800000
MIN_NORMAL_KEY = 0x00800000
KEY_STEP16 = 1 << 16
VMEM_LIMIT = 56 * 1024 * 1024

C_Q = 0
C_IQ = 512
C_K = 768
C_IK = 832
C_MISC = 896
MISC_IW = 64
MISC_DT = 72
C_Z = 1024
C_XBC = 1536
IN_COLS = 2304

TQ = 256
KC = 256

Q_SCALE = HEAD_DIM ** -0.5 * float(np.log2(np.e))

NT_DIMS = (((1,), (1,)), ((), ()))


def _nt_dot(a, b):
    return lax.dot_general(a, b, NT_DIMS, preferred_element_type=F32)


def _fold_rows(x, op, rows=SUBLANES):
    slabs = [x[r * rows:(r + 1) * rows, :] for r in range(x.shape[0] // rows)]
    while len(slabs) > 1:
        nxt = [op(slabs[i], slabs[i + 1]) for i in range(0, len(slabs) - 1, 2)]
        if len(slabs) % 2:
            nxt.append(slabs[-1])
        slabs = nxt
    return slabs[0]


CONV_PAD = 8
PROJ_COLS = 256


def _silu(x):
    return x * (1.0 / (1.0 + jnp.exp(-x)))


def _in_proj_kernel(x_ref, w_ref, cos_ref, s1_ref, s2_ref, mscale_ref, cw_ref, cb_ref,
                    q_ref, iq_ref, k_ref, ik_ref, vt_ref, misc_ref, zs_ref, xc_ref, hist_ref,
                    *, nblk_seq):
    @pl.when(pl.program_id(0) == 0)
    def _():
        hist_ref[...] = jnp.zeros_like(hist_ref)

    xb = x_ref[...].astype(BF16)
    cos = cos_ref[...]
    s1 = s1_ref[...]
    s2 = s2_ref[...]

    def mm(c0, width):
        return jnp.dot(xb, w_ref[:, c0:c0 + width], preferred_element_type=F32)

    def rope(y):
        fwd = pltpu.roll(y, LANES - HEAD_DIM // 2, 1)
        bwd = pltpu.roll(y, HEAD_DIM // 2, 1)
        return y * cos + fwd * s1 + bwd * s2

    def rope_wide(y, scale):
        parts = []
        for c in range(y.shape[1] // LANES):
            r = rope(y[:, c * LANES:(c + 1) * LANES])
            parts.append(r * scale if scale != 1.0 else r)
        return parts

    tm = x_ref.shape[0]
    seq_start = pl.program_id(0) % nblk_seq == 0
    for c0 in range(0, CONV_CH, PROJ_COLS):
        cols = slice(c0, c0 + PROJ_COLS)
        xbc = mm(C_XBC + c0, PROJ_COLS)
        hist = jnp.where(seq_start, 0.0, hist_ref[:, cols])
        xp = jnp.concatenate([hist, xbc], axis=0)
        acc = cw_ref[0:1, cols] * xp
        for j in range(1, CONV_WIDTH):
            acc = pltpu.roll(acc, 1, 0) + cw_ref[j:j + 1, cols] * xp
        xc_ref[:, cols] = _silu(acc[CONV_PAD:, :] + cb_ref[:, cols])
        hist_ref[:, cols] = xbc[tm - CONV_PAD:, :]
    for c0 in range(0, SSD_WIDTH, PROJ_COLS):
        zs_ref[:, c0:c0 + PROJ_COLS] = _silu(mm(C_Z + c0, PROJ_COLS))

    for c, r in enumerate(rope_wide(mm(C_Q, ATT_WIDTH), Q_SCALE)):
        q_ref[:, c * LANES:(c + 1) * LANES] = r.astype(BF16)
    for c, r in enumerate(rope_wide(mm(C_IQ, IDX_HEADS * IDX_DIM), 1.0)):
        iq_ref[:, c * LANES:(c + 1) * LANES] = r.astype(BF16)
    kk = rope(mm(C_K, LANES))
    k_ref[...] = kk[:, :HEAD_DIM].astype(BF16)
    ik_ref[...] = kk[:, HEAD_DIM:].astype(BF16)
    misc = mm(C_MISC, LANES) * mscale_ref[...]
    misc_ref[...] = misc
    lane = lax.broadcasted_iota(jnp.int32, misc.shape, 1)
    vext = jnp.where(lane < HEAD_DIM, misc, jnp.where(lane == HEAD_DIM, 1.0, 0.0))
    for c in range(vt_ref.shape[0]):
        vt_ref[c] = vext[c * KC:(c + 1) * KC, :].T.astype(BF16)


def _in_proj(x2, w_perm, cos_t, s1_t, s2_t, mscale, conv_w, conv_b, seq, tm):
    n = x2.shape[0]
    nblk_seq = seq // tm
    row = lambda i: (i, 0)
    tab = lambda i: (i % nblk_seq, 0)
    const = lambda i: (0, 0)
    outs = [
        ((n, ATT_WIDTH), BF16, pl.BlockSpec((tm, ATT_WIDTH), row)),
        ((n, IDX_HEADS * IDX_DIM), BF16, pl.BlockSpec((tm, IDX_HEADS * IDX_DIM), row)),
        ((n, HEAD_DIM), BF16, pl.BlockSpec((tm, HEAD_DIM), row)),
        ((n, IDX_DIM), BF16, pl.BlockSpec((tm, IDX_DIM), row)),
        ((n // KC, LANES, KC), BF16, pl.BlockSpec((tm // KC, LANES, KC), lambda i: (i, 0, 0))),
        ((n, LANES), F32, pl.BlockSpec((tm, LANES), row)),
        ((n, SSD_WIDTH), F32, pl.BlockSpec((tm, SSD_WIDTH), row)),
        ((n, CONV_CH), F32, pl.BlockSpec((tm, CONV_CH), row)),
    ]
    return pl.pallas_call(
        functools.partial(_in_proj_kernel, nblk_seq=nblk_seq),
        grid=(n // tm,),
        in_specs=[
            pl.BlockSpec((tm, D_MODEL), row),
            pl.BlockSpec((D_MODEL, IN_COLS), const),
            pl.BlockSpec((tm, LANES), tab),
            pl.BlockSpec((tm, LANES), tab),
            pl.BlockSpec((tm, LANES), tab),
            pl.BlockSpec((1, LANES), const),
            pl.BlockSpec((CONV_WIDTH, CONV_CH), const),
            pl.BlockSpec((1, CONV_CH), const),
        ],
        out_specs=[spec for _, _, spec in outs],
        out_shape=[jax.ShapeDtypeStruct(shape, dt) for shape, dt, _ in outs],
        scratch_shapes=[pltpu.VMEM((CONV_PAD, CONV_CH), F32)],
        compiler_params=pltpu.CompilerParams(
            dimension_semantics=("arbitrary",), vmem_limit_bytes=VMEM_LIMIT),
        name="in_proj",
    )(x2, w_perm, cos_t, s1_t, s2_t, mscale, conv_w, conv_b)


def _dsa_kernel(q_ref, iq_ref, misc_ref, k_ref, ik_ref, vt_ref, o_ref,
                sc_ref, sc16_ref, lg_ref, acc_ref, *, topk):
    qi = pl.program_id(1)
    nj = qi + 1
    neg_inf = -jnp.inf
    kf = float(topk)
    key_i = lax.broadcasted_iota(jnp.int32, (KC, TQ), 0)
    qry_i = lax.broadcasted_iota(jnp.int32, (KC, TQ), 1)
    causal = key_i <= qry_i

    def key_rows(j):
        return pl.ds(pl.multiple_of(j * KC, KC), KC)

    iw_t = misc_ref[...].T[MISC_IW:MISC_IW + SUBLANES, :]

    def scores_body(j, carry):
        ikj = ik_ref[key_rows(j), :]
        sc = None
        for h in range(IDX_HEADS):
            d = _nt_dot(ikj, iq_ref[:, h * IDX_DIM:(h + 1) * IDX_DIM])
            term = iw_t[h:h + 1, :] * jnp.maximum(d, 0.0)
            sc = term if sc is None else sc + term
        sc_ref[j] = sc
        sc16_ref[j] = sc.astype(BF16)
        return carry

    lax.fori_loop(0, nj, scores_body, 0)
    sc_diag = jnp.where(causal, sc_ref[qi], neg_inf)
    sc_ref[qi] = sc_diag
    sc16_ref[qi] = sc_diag.astype(BF16)

    def key_to_float(key):
        mag = jnp.where(key < 0, -key, key)
        sub = jnp.logical_and(mag > 0, mag < MIN_NORMAL_KEY)
        mag = jnp.where(sub, jnp.where(key < 0, 0, MIN_NORMAL_KEY), mag)
        f = lax.bitcast_convert_type(mag, F32)
        return jnp.where(key < 0, -f, f)

    one16 = jnp.ones((), BF16)
    zero16 = jnp.zeros((), BF16)

    def count16_ge(cand16):
        def body(j, acc):
            ind = jnp.where(sc16_ref[j] >= cand16, one16, zero16)
            return acc + _fold_rows(ind, jnp.add, BF16_ROWS).astype(F32)

        acc = lax.fori_loop(0, nj, body, jnp.zeros((BF16_ROWS, TQ), F32))
        return jnp.sum(acc, axis=0, keepdims=True)

    def count_ge(cand):
        def body(j, acc):
            ind = jnp.where(sc_ref[j] >= cand, 1.0, 0.0)
            return acc + _fold_rows(ind, jnp.add)

        acc = lax.fori_loop(0, nj, body, jnp.zeros((SUBLANES, TQ), F32))
        return jnp.sum(acc, axis=0, keepdims=True)

    def coarse_body(b, m):
        trial = m + lax.shift_left(jnp.int32(1), jnp.int32(15) - b)
        cand16 = key_to_float(trial * KEY_STEP16).astype(BF16)
        return jnp.where(count16_ge(cand16) >= kf, trial, m)

    m16 = lax.fori_loop(0, 16, coarse_body, jnp.full((1, TQ), -(INF_KEY // KEY_STEP16), jnp.int32))
    key_base = jnp.maximum(m16 * KEY_STEP16 - (KEY_STEP16 // 2 + 1), -INF_KEY)

    def fine_body(b, off):
        trial = off + lax.shift_left(jnp.int32(1), jnp.int32(16) - b)
        ok = count_ge(key_to_float(key_base + trial)) >= kf
        return jnp.where(ok, trial, off)

    off = lax.fori_loop(0, 17, fine_body, jnp.zeros((1, TQ), jnp.int32))
    lo = key_to_float(key_base + off)
    hi = key_to_float(key_base + off + 1)
    need = kf - count_ge(hi)

    lower = (lax.broadcasted_iota(jnp.int32, (KC, KC), 0)
             > lax.broadcasted_iota(jnp.int32, (KC, KC), 1)).astype(BF16)

    acc_ref[...] = jnp.zeros_like(acc_ref)
    no_max = tuple(jnp.minimum(need, neg_inf) for _ in range(ATT_HEADS))

    def logits_stage(j, taken):
        s = sc_ref[j]
        gt = s >= hi
        eq = jnp.logical_and(s >= lo, jnp.logical_not(gt))
        eqf = jnp.where(eq, 1.0, 0.0)
        before = jnp.dot(lower, eqf.astype(BF16), preferred_element_type=F32) + taken
        sel = jnp.logical_or(gt, jnp.logical_and(eq, before < need))
        sel = jnp.logical_and(sel, jnp.logical_or(causal, j < qi))
        bias = jnp.where(sel, 0.0, neg_inf)
        taken = taken + jnp.sum(_fold_rows(eqf, jnp.add), axis=0, keepdims=True)
        kj = k_ref[key_rows(j), :]
        slot = j % 2
        cms = []
        for h in range(ATT_HEADS):
            lg = _nt_dot(kj, q_ref[:, h * HEAD_DIM:(h + 1) * HEAD_DIM]) + bias
            lg_ref[slot, h] = lg
            cms.append(jnp.max(_fold_rows(lg, jnp.maximum), axis=0, keepdims=True))
        return taken, tuple(cms)

    def pv_stage(jp, cms, ms):
        vtj = vt_ref[jp]
        slot = jp % 2
        new_ms = []
        for h in range(ATT_HEADS):
            m_new = jnp.maximum(ms[h], cms[h])
            shift = jnp.where(m_new == neg_inf, 0.0, m_new)
            p = jnp.exp2(lg_ref[slot, h] - shift).astype(BF16)
            alpha = jnp.exp2(ms[h] - shift)
            acc_ref[h] = alpha * acc_ref[h] + jnp.dot(vtj, p, preferred_element_type=F32)
            new_ms.append(m_new)
        return tuple(new_ms)

    def att_body(j, carry):
        taken, cms_prev, ms = carry
        ms = pv_stage(j - 1, cms_prev, ms)
        taken, cms = logits_stage(j, taken)
        return taken, cms, ms

    taken0, cms0 = logits_stage(0, jnp.zeros((1, TQ), F32))
    _, cms_last, ms = lax.fori_loop(1, nj, att_body, (taken0, cms0, no_max))
    pv_stage(qi, cms_last, ms)
    for pair in range(ATT_HEADS // 2):
        halves = []
        for h in (2 * pair, 2 * pair + 1):
            a = acc_ref[h]
            halves.append(a[:HEAD_DIM, :] * (1.0 / a[HEAD_DIM:HEAD_DIM + 1, :]))
        blk = jnp.concatenate(halves, axis=0)
        o_ref[:, pair * LANES:(pair + 1) * LANES] = blk.T.astype(BF16)


def _dsa(q, iq, misc, k, ik, vt, b, s, topk):
    nq = s // TQ
    tile = lambda bi, qi: (bi * nq + qi, 0)
    full = lambda bi, qi: (bi, 0)
    return pl.pallas_call(
        functools.partial(_dsa_kernel, topk=topk),
        grid=(b, nq),
        in_specs=[
            pl.BlockSpec((TQ, ATT_WIDTH), tile),
            pl.BlockSpec((TQ, IDX_HEADS * IDX_DIM), tile),
            pl.BlockSpec((TQ, LANES), tile),
            pl.BlockSpec((s, HEAD_DIM), full),
            pl.BlockSpec((s, IDX_DIM), full),
            pl.BlockSpec((s // KC, LANES, KC), lambda bi, qi: (bi, 0, 0)),
        ],
        out_specs=pl.BlockSpec((TQ, ATT_WIDTH), tile),
        out_shape=jax.ShapeDtypeStruct((b * s, ATT_WIDTH), BF16),
        scratch_shapes=[
            pltpu.VMEM((nq, KC, TQ), F32),
            pltpu.VMEM((nq, KC, TQ), BF16),
            pltpu.VMEM((2, ATT_HEADS, KC, TQ), F32),
            pltpu.VMEM((ATT_HEADS, LANES, TQ), F32),
        ],
        compiler_params=pltpu.CompilerParams(
            dimension_semantics=("parallel", "arbitrary"), vmem_limit_bytes=VMEM_LIMIT),
        name="dsa_attention",
    )(q, iq, misc, k, ik, vt)


SSD_TT = 512


def _ssd_kernel(xc_ref, zs_ref, misc_ref, dtb_ref, arep_ref, dskip_ref, nw_ref,
                expand_ref, triu_ref, o_ref, state_ref):
    t = pl.program_id(1)
    tt = xc_ref.shape[0]

    @pl.when(t == 0)
    def _():
        state_ref[...] = jnp.zeros_like(state_ref)

    tri = (lax.broadcasted_iota(jnp.int32, (CHUNK, CHUNK), 0)
           >= lax.broadcasted_iota(jnp.int32, (CHUNK, CHUNK), 1))
    left_head = lax.broadcasted_iota(jnp.int32, (CHUNK, LANES), 1) < SSD_HEAD_DIM
    left_head_n = lax.broadcasted_iota(jnp.int32, (D_STATE, LANES), 1) < SSD_HEAD_DIM
    gn = SSD_GROUPS * D_STATE
    pairs_per_group = SSD_HEADS // SSD_GROUPS // 2
    expand = expand_ref[...]
    triu = triu_ref[...]
    zpad = jnp.zeros((SUBLANES, CHUNK), F32)

    for c in range(tt // CHUNK):
        rows = slice(c * CHUNK, (c + 1) * CHUNK)
        raw = misc_ref[rows, :].T[MISC_DT:MISC_DT + SSD_HEADS, :] + dtb_ref[...]
        dt_t = jnp.maximum(raw, 0.0) + jnp.log1p(jnp.exp(-jnp.abs(raw)))
        adt = dt_t * arep_ref[...]
        hi = adt.astype(BF16).astype(F32)
        r1 = adt - hi
        mid = r1.astype(BF16).astype(F32)
        pieces = jnp.concatenate([hi, mid, r1 - mid, zpad], axis=0).astype(BF16)
        cs = jnp.dot(pieces, triu, preferred_element_type=F32)
        acum_t = (cs[0:SUBLANES] + cs[SUBLANES:2 * SUBLANES]) + cs[2 * SUBLANES:3 * SUBLANES]
        a_last = acum_t[:, CHUNK - 1:CHUNK]
        ddt_t = jnp.exp(a_last - acum_t) * dt_t
        acum = jnp.concatenate(
            [acum_t, jnp.zeros((LANES - SSD_HEADS, CHUNK), F32)], axis=0).T
        ea = jnp.exp(acum)
        ea_hi = ea.astype(BF16)
        ea_lo = (ea - ea_hi.astype(F32)).astype(BF16)
        ea_x = (jnp.dot(ea_hi, expand, preferred_element_type=F32)
                + jnp.dot(ea_lo, expand, preferred_element_type=F32))
        xs = xc_ref[rows, 0:SSD_WIDTH]
        bm = xc_ref[rows, SSD_WIDTH:SSD_WIDTH + gn]
        cm = xc_ref[rows, SSD_WIDTH + gn:SSD_WIDTH + 2 * gn]
        bm_t = bm.T
        xs16 = xs.astype(BF16)
        bm16 = bm.astype(BF16)
        cm16 = cm.astype(BF16)
        y_pairs = [None] * (SSD_HEADS // 2)
        for g in range(SSD_GROUPS):
            cg = cm16[:, g * D_STATE:(g + 1) * D_STATE]
            bg = bm16[:, g * D_STATE:(g + 1) * D_STATE]
            bg_t = bm_t[g * D_STATE:(g + 1) * D_STATE, :]
            gmat = _nt_dot(cg, bg)
            for pp in range(pairs_per_group):
                pair = g * pairs_per_group + pp
                lanes = slice(pair * LANES, (pair + 1) * LANES)
                xp = xs16[:, lanes]
                y_halves, s_halves = [], []
                for h in (2 * pair, 2 * pair + 1):
                    col = acum[:, h:h + 1]
                    rowv = acum_t[h:h + 1, :]
                    lmat = jnp.exp(jnp.where(tri, col - rowv, -jnp.inf))
                    mmat = (gmat * lmat * dt_t[h:h + 1, :]).astype(BF16)
                    y_halves.append(jnp.dot(mmat, xp, preferred_element_type=F32))
                    bs = (bg_t * ddt_t[h:h + 1, :]).astype(BF16)
                    s_halves.append(jnp.dot(bs, xp, preferred_element_type=F32))
                y_diag = jnp.where(left_head, y_halves[0], y_halves[1])
                new = jnp.where(left_head_n, s_halves[0], s_halves[1])
                prev = state_ref[pair]
                y_off = jnp.dot(cg, prev.astype(BF16), preferred_element_type=F32) * ea_x[:, lanes]
                state_ref[pair] = prev * ea_x[CHUNK - 1:CHUNK, lanes] + new
                y_pairs[pair] = y_diag + y_off
        y = jnp.concatenate(y_pairs, axis=1) + dskip_ref[...] * xs
        y = y * zs_ref[rows, :]
        ms = jnp.mean(y * y, axis=1, keepdims=True)
        o_ref[rows, :] = (y * lax.rsqrt(ms + LN_EPS) * nw_ref[...]).astype(BF16)


def _ssd(xc, zs, misc, dtb_rep, a_rep, dskip_row, nw_row, b, s):
    tt = min(SSD_TT, s)
    nt = s // tt
    tile = lambda bi, ti: (bi * nt + ti, 0)
    const = lambda bi, ti: (0, 0)
    expand = (jnp.arange(LANES)[:, None] == jnp.arange(SSD_WIDTH)[None, :] // SSD_HEAD_DIM).astype(BF16)
    triu = (jnp.arange(CHUNK)[:, None] <= jnp.arange(CHUNK)[None, :]).astype(BF16)
    return pl.pallas_call(
        _ssd_kernel,
        grid=(b, s // tt),
        in_specs=[
            pl.BlockSpec((tt, CONV_CH), tile),
            pl.BlockSpec((tt, SSD_WIDTH), tile),
            pl.BlockSpec((tt, LANES), tile),
            pl.BlockSpec((SSD_HEADS, CHUNK), const),
            pl.BlockSpec((SSD_HEADS, CHUNK), const),
            pl.BlockSpec((1, SSD_WIDTH), const),
            pl.BlockSpec((1, SSD_WIDTH), const),
            pl.BlockSpec((LANES, SSD_WIDTH), const),
            pl.BlockSpec((CHUNK, CHUNK), const),
        ],
        out_specs=pl.BlockSpec((tt, SSD_WIDTH), tile),
        out_shape=jax.ShapeDtypeStruct((b * s, SSD_WIDTH), BF16),
        scratch_shapes=[
            pltpu.VMEM((SSD_HEADS // 2, D_STATE, LANES), F32),
        ],
        compiler_params=pltpu.CompilerParams(
            dimension_semantics=("parallel", "arbitrary"), vmem_limit_bytes=VMEM_LIMIT),
        name="ssd_mixer",
    )(xc, zs, misc, dtb_rep, a_rep, dskip_row, nw_row, expand, triu)


def _layer_norm(y, g, b):
    mu = jnp.mean(y, axis=1, keepdims=True)
    yc = y - mu
    var = jnp.mean(yc * yc, axis=1, keepdims=True)
    return yc * lax.rsqrt(var + LN_EPS) * g + b


OUT_TM = 1024
OUT_SUB = 256


def _out_proj_kernel(att_ref, ssd_ref, x_ref, wa_ref, ws_ref, g_ref, b_ref, h_ref):
    for r0 in range(0, x_ref.shape[0], OUT_SUB):
        rows = slice(r0, r0 + OUT_SUB)
        mixed = jnp.dot(att_ref[rows, :], wa_ref[...], preferred_element_type=F32)
        mixed = mixed + jnp.dot(ssd_ref[rows, :], ws_ref[...], preferred_element_type=F32)
        h_ref[rows, :] = _layer_norm(ALPHA * x_ref[rows, :] + mixed, g_ref[...], b_ref[...])


def _out_proj(att2, ssd2, x2, w_att, w_ssd, g_row, b_row, tm):
    n = x2.shape[0]
    row = lambda i: (i, 0)
    const = lambda i: (0, 0)
    return pl.pallas_call(
        _out_proj_kernel,
        grid=(n // tm,),
        in_specs=[
            pl.BlockSpec((tm, ATT_WIDTH), row),
            pl.BlockSpec((tm, SSD_WIDTH), row),
            pl.BlockSpec((tm, D_MODEL), row),
            pl.BlockSpec((ATT_WIDTH, D_MODEL), const),
            pl.BlockSpec((SSD_WIDTH, D_MODEL), const),
            pl.BlockSpec((1, D_MODEL), const),
            pl.BlockSpec((1, D_MODEL), const),
        ],
        out_specs=pl.BlockSpec((tm, D_MODEL), row),
        out_shape=jax.ShapeDtypeStruct((n, D_MODEL), F32),
        compiler_params=pltpu.CompilerParams(
            dimension_semantics=("parallel",), vmem_limit_bytes=VMEM_LIMIT),
        name="out_proj_ln",
    )(att2, ssd2, x2, w_att, w_ssd, g_row, b_row)


ROUTE_E0 = N_GROUPS_MOE
MOE_TM = 512
MOE_CAP = 160


def _first_max(vals):
    best = vals[0]
    for v in vals[1:]:
        best = jnp.maximum(best, v)
    idx = jnp.full(best.shape, len(vals) - 1, jnp.int32)
    for i in range(len(vals) - 2, -1, -1):
        idx = jnp.where(vals[i] == best, i, idx)
    return best, idx


def _moe_kernel(h_ref, wrh_ref, wrl_ref, br_ref, wg_ref, wu_ref, wd_ref, g_ref, b_ref, upper_ref,
                o_ref, hb_ref, hp_ref, gp_ref, yp_ref):
    tm = h_ref.shape[0]
    h = h_ref[...]
    h_hi = h.astype(BF16)
    hb_ref[...] = h_hi
    h_lo = (h - h_hi.astype(F32)).astype(BF16)
    wrh = wrh_ref[...]
    logits = (jnp.dot(h_hi, wrh, preferred_element_type=F32)
              + jnp.dot(h_lo, wrh, preferred_element_type=F32)
              + jnp.dot(h_hi, wrl_ref[...], preferred_element_type=F32)) + br_ref[...]
    lt = logits.T
    row = lambda r: lt[r:r + 1, :]
    gl = [row(r) for r in range(N_GROUPS_MOE)]
    gmax, gidx = _first_max(gl)
    denom = jnp.exp(gl[0] - gmax)
    for v in gl[1:]:
        denom = denom + jnp.exp(v - gmax)
    gprob = 1.0 / denom
    el = []
    for k in range(EXPERTS_PER_GROUP):
        v = row(ROUTE_E0 + (N_GROUPS_MOE - 1) * EXPERTS_PER_GROUP + k)
        for gg in range(N_GROUPS_MOE - 2, -1, -1):
            v = jnp.where(gidx == gg, row(ROUTE_E0 + gg * EXPERTS_PER_GROUP + k), v)
        el.append(v)
    l1, i1 = _first_max(el)
    l2, i2 = _first_max([jnp.where(i1 == k, -jnp.inf, el[k]) for k in range(EXPERTS_PER_GROUP)])
    e2 = jnp.exp(l2 - l1)
    w1 = gprob / (1.0 + e2)
    w2 = gprob * e2 / (1.0 + e2)
    gate4 = [jnp.where(i1 == k, w1, jnp.where(i2 == k, w2, 0.0)) for k in range(EXPERTS_PER_GROUP)]

    member = [jnp.where(gidx == gg, 1.0, 0.0) for gg in range(N_GROUPS_MOE)]
    member_blk = jnp.concatenate(
        member + [jnp.zeros((2 * SUBLANES - N_GROUPS_MOE, tm), F32)], axis=0).astype(BF16)
    earlier = jnp.dot(member_blk, upper_ref[...], preferred_element_type=F32)
    counts = [jnp.sum(m, axis=1, keepdims=True) for m in member]
    most = counts[0]
    for c in counts[1:]:
        most = jnp.maximum(most, c)
    fits = jnp.max(most) <= float(MOE_CAP)
    slot = member[0] * earlier[0:1, :]
    for gg in range(1, N_GROUPS_MOE):
        slot = slot + member[gg] * (earlier[gg:gg + 1, :] + float(gg * MOE_CAP))
    tok = jnp.concatenate(
        gate4 + [slot, gidx.astype(F32), jnp.zeros((LANES - EXPERTS_PER_GROUP - 2, tm), F32)],
        axis=0).T
    slot_lane, grp_lane = EXPERTS_PER_GROUP, EXPERTS_PER_GROUP + 1
    rows_p = N_GROUPS_MOE * MOE_CAP

    @pl.when(fits)
    def _():
        hb = hb_ref[...]
        place = jnp.where(lax.broadcasted_iota(jnp.int32, (rows_p, tm), 0) == slot.astype(jnp.int32),
                          1.0, 0.0).astype(BF16)
        hp_ref[...] = jnp.dot(place, hb, preferred_element_type=F32).astype(BF16)
        t_hi = tok.astype(BF16)
        t_lo = (tok - t_hi.astype(F32)).astype(BF16)
        both = jnp.dot(place, jnp.concatenate([t_hi, t_lo], axis=1), preferred_element_type=F32)
        gp_ref[...] = both[:, :LANES] + both[:, LANES:]
        for gg in range(N_GROUPS_MOE):
            rows = slice(gg * MOE_CAP, (gg + 1) * MOE_CAP)
            xg = hp_ref[rows, :]
            acc = None
            for k in range(EXPERTS_PER_GROUP):
                e = gg * EXPERTS_PER_GROUP + k
                a = jnp.dot(xg, wg_ref[e], preferred_element_type=F32)
                u = jnp.dot(xg, wu_ref[e], preferred_element_type=F32)
                hid = (_silu(a) * u * gp_ref[rows, k:k + 1]).astype(BF16)
                part = jnp.dot(hid, wd_ref[e], preferred_element_type=F32)
                acc = part if acc is None else acc + part
            yp_ref[rows, :] = acc.astype(BF16)
        back = jnp.where(lax.broadcasted_iota(jnp.int32, (tm, rows_p), 1)
                         == tok[:, slot_lane:slot_lane + 1].astype(jnp.int32), 1.0, 0.0).astype(BF16)
        for r0 in range(0, tm, OUT_SUB):
            rr = slice(r0, r0 + OUT_SUB)
            y = jnp.dot(back[rr, :], yp_ref[...], preferred_element_type=F32)
            o_ref[rr, :] = _layer_norm(ALPHA * h_ref[rr, :] + y, g_ref[...], b_ref[...])

    @pl.when(jnp.logical_not(fits))
    def _():
        hb = hb_ref[...]
        acc = None
        for gg in range(N_GROUPS_MOE):
            in_grp = tok[:, grp_lane:grp_lane + 1] == float(gg)
            for k in range(EXPERTS_PER_GROUP):
                e = gg * EXPERTS_PER_GROUP + k
                a = jnp.dot(hb, wg_ref[e], preferred_element_type=F32)
                u = jnp.dot(hb, wu_ref[e], preferred_element_type=F32)
                gate = jnp.where(in_grp, tok[:, k:k + 1], 0.0)
                hid = (_silu(a) * u * gate).astype(BF16)
                part = jnp.dot(hid, wd_ref[e], preferred_element_type=F32)
                acc = part if acc is None else acc + part
        o_ref[...] = _layer_norm(ALPHA * h_ref[...] + acc, g_ref[...], b_ref[...])


def _moe(h2, wr_hi, wr_lo, br, wg, wu, wd, g_row, b_row):
    n = h2.shape[0]
    tm = min(MOE_TM, n)
    rows_p = N_GROUPS_MOE * MOE_CAP
    row = lambda i: (i, 0)
    const = lambda i: (0, 0)
    whole = lambda i: (0, 0, 0)
    once = pl.Buffered(1)
    upper = (jnp.arange(tm)[:, None] < jnp.arange(tm)[None, :]).astype(BF16)
    return pl.pallas_call(
        _moe_kernel,
        grid=(n // tm,),
        in_specs=[
            pl.BlockSpec((tm, D_MODEL), row),
            pl.BlockSpec((D_MODEL, LANES), const),
            pl.BlockSpec((D_MODEL, LANES), const),
            pl.BlockSpec((1, LANES), const),
            pl.BlockSpec((N_EXPERTS, D_MODEL, EXPERT_FF), whole, pipeline_mode=once),
            pl.BlockSpec((N_EXPERTS, D_MODEL, EXPERT_FF), whole, pipeline_mode=once),
            pl.BlockSpec((N_EXPERTS, EXPERT_FF, D_MODEL), whole, pipeline_mode=once),
            pl.BlockSpec((1, D_MODEL), const),
            pl.BlockSpec((1, D_MODEL), const),
            pl.BlockSpec((tm, tm), const, pipeline_mode=once),
        ],
        out_specs=pl.BlockSpec((tm, D_MODEL), row),
        out_shape=jax.ShapeDtypeStruct((n, D_MODEL), F32),
        scratch_shapes=[
            pltpu.VMEM((tm, D_MODEL), BF16),
            pltpu.VMEM((rows_p, D_MODEL), BF16),
            pltpu.VMEM((rows_p, LANES), F32),
            pltpu.VMEM((rows_p, D_MODEL), BF16),
        ],
        compiler_params=pltpu.CompilerParams(
            dimension_semantics=("parallel",), vmem_limit_bytes=VMEM_LIMIT),
        name="hier_moe_ln",
    )(h2, wr_hi, wr_lo, br, wg, wu, wd, g_row, b_row, upper)


def _rope_tables(seq):
    inv = ROPE_THETA ** (-jnp.arange(0, HEAD_DIM, 2, dtype=F32) / HEAD_DIM)
    ang = jnp.arange(seq, dtype=F32)[:, None] * inv[None, :]
    cos, sin = jnp.cos(ang), jnp.sin(ang)
    zero = jnp.zeros_like(sin)
    cos_t = jnp.tile(cos, (1, LANES // (HEAD_DIM // 2)))
    s1_t = jnp.tile(jnp.concatenate([-sin, zero], 1), (1, LANES // HEAD_DIM))
    s2_t = jnp.tile(jnp.concatenate([zero, sin], 1), (1, LANES // HEAD_DIM))
    return cos_t, s1_t, s2_t


def _permute_w_in(w):
    sizes = (ATT_WIDTH, HEAD_DIM, HEAD_DIM, IDX_HEADS * IDX_DIM, IDX_DIM, IDX_HEADS,
             SSD_WIDTH, CONV_CH, SSD_HEADS)
    pts = np.cumsum((0,) + sizes)
    q, k, v, iq, ik, iw, z, xbc, dt = [w[:, pts[i]:pts[i + 1]] for i in range(len(sizes))]
    d = w.shape[0]
    pad = lambda n: jnp.zeros((d, n), w.dtype)
    misc = jnp.concatenate([v, iw, pad(MISC_DT - MISC_IW - IDX_HEADS), dt,
                            pad(LANES - MISC_DT - SSD_HEADS)], 1)
    return jnp.concatenate([q, iq, k, ik, misc, z, xbc], 1).astype(BF16)


def _head_rep(vals):
    return jnp.broadcast_to(vals.astype(F32)[:, None], (vals.shape[0], CHUNK))


def kernel(x, w_in, conv_w, conv_b, dt_bias, a_log, d_skip, ssd_norm_w, w_out, ln1_g, ln1_b,
           w_route_group, b_route_group, w_route_expert, b_route_expert, w_gate, w_up,
           w_down, ln2_g, ln2_b):
    bsz, seq, d = x.shape
    n = bsz * seq
    topk = min(TOPK_MAX, seq // 4)
    tm = 512
    assert d == D_MODEL and TQ == KC and seq % TQ == 0 and seq % tm == 0 and topk <= KC
    cos_t, s1_t, s2_t = _rope_tables(seq)
    mscale = jnp.ones((1, LANES), F32).at[0, MISC_IW:MISC_IW + IDX_HEADS].set(INDEXER_SCALE)
    for l in range(DEPTH):
        x2 = x.reshape(n, d)
        q, iq, k, ik, vt, misc, zs, xc = _in_proj(
            x2, _permute_w_in(w_in[l]), cos_t, s1_t, s2_t, mscale,
            conv_w[l].astype(F32), conv_b[l].astype(F32)[None, :], seq, tm)
        att = _dsa(q, iq, misc, k, ik, vt, bsz, seq, topk)
        ssd = _ssd(
            xc, zs, misc,
            _head_rep(dt_bias[l]), _head_rep(-jnp.exp(a_log[l].astype(F32))),
            jnp.repeat(d_skip[l].astype(F32), SSD_HEAD_DIM)[None, :], ssd_norm_w[l][None, :],
            bsz, seq)
        w_o = w_out[l].astype(BF16)
        h2 = _out_proj(att, ssd, x2, w_o[:ATT_WIDTH], w_o[ATT_WIDTH:],
                       ln1_g[l][None, :], ln1_b[l][None, :], min(OUT_TM, n))
        pad = jnp.zeros((d, LANES - ROUTE_E0 - N_EXPERTS), F32)
        wr = jnp.concatenate([w_route_group[l].astype(F32), w_route_expert[l].astype(F32), pad], 1)
        wr_hi = wr.astype(BF16)
        wr_lo = (wr - wr_hi.astype(F32)).astype(BF16)
        br = jnp.concatenate([b_route_group[l].astype(F32), b_route_expert[l].astype(F32),
                              pad[0]])[None, :]
        x = _moe(h2, wr_hi, wr_lo, br, w_gate[l].astype(BF16), w_up[l].astype(BF16),
                 w_down[l].astype(BF16), ln2_g[l][None, :], ln2_b[l][None, :]).reshape(bsz, seq, d)
    return x
```

```python
import functools

import jax
import jax.numpy as jnp
import numpy as np
from jax import lax
from jax.experimental import pallas as pl
from jax.experimental.pallas import tpu as pltpu

F32 = jnp.float32
BF16 = jnp.bfloat16

D_MODEL = 1024
HEAD_DIM = 64
ATT_WIDTH = 512
ATT_HEADS = 8
IDX_HEADS = 4
IDX_DIM = 64
TOPK_MAX = 256
ROPE_THETA = 10000.0
INDEXER_SCALE = (IDX_HEADS ** -0.5) * (IDX_DIM ** -0.5)
SSD_WIDTH = 512
SSD_HEADS = 8
SSD_HEAD_DIM = 64
SSD_GROUPS = 2
D_STATE = 64
CONV_WIDTH = 4
CONV_CH = SSD_WIDTH + 2 * SSD_GROUPS * D_STATE
CHUNK = 128
N_GROUPS_MOE = 4
EXPERTS_PER_GROUP = 4
N_EXPERTS = 16
EXPERT_FF = 256
DEPTH = 1
ALPHA = (2 * DEPTH) ** 0.25
LN_EPS = 1e-5

LANES = 128
SUBLANES = 8
BF16_ROWS = 16
INF_KEY = 0x7F800000
MIN_NORMAL_KEY = 0x00800000
KEY_STEP16 = 1 << 16
VMEM_LIMIT = 56 * 1024 * 1024

C_Q = 0
C_IQ = 512
C_K = 768
C_IK = 832
C_MISC = 896
MISC_IW = 64
MISC_DT = 72
C_Z = 1024
C_XBC = 1536
IN_COLS = 2304

TQ = 256
KC = 256

Q_SCALE = HEAD_DIM ** -0.5 * float(np.log2(np.e))

NT_DIMS = (((1,), (1,)), ((), ()))


def _nt_dot(a, b):
    return lax.dot_general(a, b, NT_DIMS, preferred_element_type=F32)


def _fold_rows(x, op, rows=SUBLANES):
    slabs = [x[r * rows:(r + 1) * rows, :] for r in range(x.shape[0] // rows)]
    while len(slabs) > 1:
        nxt = [op(slabs[i], slabs[i + 1]) for i in range(0, len(slabs) - 1, 2)]
        if len(slabs) % 2:
            nxt.append(slabs[-1])
        slabs = nxt
    return slabs[0]


CONV_PAD = 8
PROJ_COLS = 256


def _silu(x):
    return x * (1.0 / (1.0 + jnp.exp(-x)))


def _in_proj_kernel(x_ref, w_ref, cos_ref, s1_ref, s2_ref, mscale_ref, cw_ref, cb_ref,
                    q_ref, iq_ref, k_ref, ik_ref, vt_ref, misc_ref, zs_ref, xc_ref, hist_ref,
                    *, nblk_seq):
    @pl.when(pl.program_id(0) == 0)
    def _():
        hist_ref[...] = jnp.zeros_like(hist_ref)

    xb = x_ref[...].astype(BF16)
    cos = cos_ref[...]
    s1 = s1_ref[...]
    s2 = s2_ref[...]

    def mm(c0, width):
        return jnp.dot(xb, w_ref[:, c0:c0 + width], preferred_element_type=F32)

    def rope(y):
        fwd = pltpu.roll(y, LANES - HEAD_DIM // 2, 1)
        bwd = pltpu.roll(y, HEAD_DIM // 2, 1)
        return y * cos + fwd * s1 + bwd * s2

    def rope_wide(y, scale):
        parts = []
        for c in range(y.shape[1] // LANES):
            r = rope(y[:, c * LANES:(c + 1) * LANES])
            parts.append(r * scale if scale != 1.0 else r)
        return parts

    tm = x_ref.shape[0]
    seq_start = pl.program_id(0) % nblk_seq == 0
    for c0 in range(0, CONV_CH, PROJ_COLS):
        cols = slice(c0, c0 + PROJ_COLS)
        xbc = mm(C_XBC + c0, PROJ_COLS)
        hist = jnp.where(seq_start, 0.0, hist_ref[:, cols])
        xp = jnp.concatenate([hist, xbc], axis=0)
        acc = cw_ref[0:1, cols] * xp
        for j in range(1, CONV_WIDTH):
            acc = pltpu.roll(acc, 1, 0) + cw_ref[j:j + 1, cols] * xp
        xc_ref[:, cols] = _silu(acc[CONV_PAD:, :] + cb_ref[:, cols])
        hist_ref[:, cols] = xbc[tm - CONV_PAD:, :]
    for c0 in range(0, SSD_WIDTH, PROJ_COLS):
        zs_ref[:, c0:c0 + PROJ_COLS] = _silu(mm(C_Z + c0, PROJ_COLS))

    for c, r in enumerate(rope_wide(mm(C_Q, ATT_WIDTH), Q_SCALE)):
        q_ref[:, c * LANES:(c + 1) * LANES] = r.astype(BF16)
    for c, r in enumerate(rope_wide(mm(C_IQ, IDX_HEADS * IDX_DIM), 1.0)):
        iq_ref[:, c * LANES:(c + 1) * LANES] = r.astype(BF16)
    kk = rope(mm(C_K, LANES))
    k_ref[...] = kk[:, :HEAD_DIM].astype(BF16)
    ik_ref[...] = kk[:, HEAD_DIM:].astype(BF16)
    misc = mm(C_MISC, LANES) * mscale_ref[...]
    misc_ref[...] = misc
    lane = lax.broadcasted_iota(jnp.int32, misc.shape, 1)
    vext = jnp.where(lane < HEAD_DIM, misc, jnp.where(lane == HEAD_DIM, 1.0, 0.0))
    for c in range(vt_ref.shape[0]):
        vt_ref[c] = vext[c * KC:(c + 1) * KC, :].T.astype(BF16)


def _in_proj(x2, w_perm, cos_t, s1_t, s2_t, mscale, conv_w, conv_b, seq, tm):
    n = x2.shape[0]
    nblk_seq = seq // tm
    row = lambda i: (i, 0)
    tab = lambda i: (i % nblk_seq, 0)
    const = lambda i: (0, 0)
    outs = [
        ((n, ATT_WIDTH), BF16, pl.BlockSpec((tm, ATT_WIDTH), row)),
        ((n, IDX_HEADS * IDX_DIM), BF16, pl.BlockSpec((tm, IDX_HEADS * IDX_DIM), row)),
        ((n, HEAD_DIM), BF16, pl.BlockSpec((tm, HEAD_DIM), row)),
        ((n, IDX_DIM), BF16, pl.BlockSpec((tm, IDX_DIM), row)),
        ((n // KC, LANES, KC), BF16, pl.BlockSpec((tm // KC, LANES, KC), lambda i: (i, 0, 0))),
        ((n, LANES), F32, pl.BlockSpec((tm, LANES), row)),
        ((n, SSD_WIDTH), F32, pl.BlockSpec((tm, SSD_WIDTH), row)),
        ((n, CONV_CH), F32, pl.BlockSpec((tm, CONV_CH), row)),
    ]
    return pl.pallas_call(
        functools.partial(_in_proj_kernel, nblk_seq=nblk_seq),
        grid=(n // tm,),
        in_specs=[
            pl.BlockSpec((tm, D_MODEL), row),
            pl.BlockSpec((D_MODEL, IN_COLS), const),
            pl.BlockSpec((tm, LANES), tab),
            pl.BlockSpec((tm, LANES), tab),
            pl.BlockSpec((tm, LANES), tab),
            pl.BlockSpec((1, LANES), const),
            pl.BlockSpec((CONV_WIDTH, CONV_CH), const),
            pl.BlockSpec((1, CONV_CH), const),
        ],
        out_specs=[spec for _, _, spec in outs],
        out_shape=[jax.ShapeDtypeStruct(shape, dt) for shape, dt, _ in outs],
        scratch_shapes=[pltpu.VMEM((CONV_PAD, CONV_CH), F32)],
        compiler_params=pltpu.CompilerParams(
            dimension_semantics=("arbitrary",), vmem_limit_bytes=VMEM_LIMIT),
        name="in_proj",
    )(x2, w_perm, cos_t, s1_t, s2_t, mscale, conv_w, conv_b)


def _dsa_kernel(q_ref, iq_ref, misc_ref, k_ref, ik_ref, vt_ref, o_ref,
                sc_ref, sc16_ref, lg_ref, acc_ref, *, topk):
    qi = pl.program_id(1)
    nj = qi + 1
    neg_inf = -jnp.inf
    kf = float(topk)
    key_i = lax.broadcasted_iota(jnp.int32, (KC, TQ), 0)
    qry_i = lax.broadcasted_iota(jnp.int32, (KC, TQ), 1)
    causal = key_i <= qry_i

    def key_rows(j):
        return pl.ds(pl.multiple_of(j * KC, KC), KC)

    iw_t = misc_ref[...].T[MISC_IW:MISC_IW + SUBLANES, :]

    def chunk_scores(j):
        ikj = ik_ref[key_rows(j), :]
        sc = None
        for h in range(IDX_HEADS):
            d = _nt_dot(ikj, iq_ref[:, h * IDX_DIM:(h + 1) * IDX_DIM])
            term = iw_t[h:h + 1, :] * jnp.maximum(d, 0.0)
            sc = term if sc is None else sc + term
        sc_ref[j] = sc
        sc16_ref[j] = sc.astype(BF16)

    def scores_body(p, carry):
        chunk_scores(2 * p)
        chunk_scores(jnp.minimum(2 * p + 1, qi))
        return carry

    lax.fori_loop(0, (nj + 1) // 2, scores_body, 0)
    sc_diag = jnp.where(causal, sc_ref[qi], neg_inf)
    sc_ref[qi] = sc_diag
    sc16_ref[qi] = sc_diag.astype(BF16)

    def key_to_float(key):
        mag = jnp.where(key < 0, -key, key)
        sub = jnp.logical_and(mag > 0, mag < MIN_NORMAL_KEY)
        mag = jnp.where(sub, jnp.where(key < 0, 0, MIN_NORMAL_KEY), mag)
        f = lax.bitcast_convert_type(mag, F32)
        return jnp.where(key < 0, -f, f)

    one16 = jnp.ones((), BF16)
    zero16 = jnp.zeros((), BF16)

    def count16_ge(cand16):
        def body(j, acc):
            ind = jnp.where(sc16_ref[j] >= cand16, one16, zero16)
            return acc + _fold_rows(ind, jnp.add, BF16_ROWS)

        acc = lax.fori_loop(0, nj, body, jnp.zeros((BF16_ROWS, TQ), BF16))
        return jnp.sum(acc.astype(F32), axis=0, keepdims=True)

    def count_ge(cand):
        def body(j, acc):
            ind = jnp.where(sc_ref[j] >= cand, 1.0, 0.0)
            return acc + _fold_rows(ind, jnp.add)

        acc = lax.fori_loop(0, nj, body, jnp.zeros((SUBLANES, TQ), F32))
        return jnp.sum(acc, axis=0, keepdims=True)

    def coarse_body(b, m):
        trial = m + lax.shift_left(jnp.int32(1), jnp.int32(15) - b)
        cand16 = key_to_float(trial * KEY_STEP16).astype(BF16)
        return jnp.where(count16_ge(cand16) >= kf, trial, m)

    m16 = lax.fori_loop(0, 16, coarse_body, jnp.full((1, TQ), -(INF_KEY // KEY_STEP16), jnp.int32))
    key_base = jnp.maximum(m16 * KEY_STEP16 - (KEY_STEP16 // 2 + 1), -INF_KEY)

    def fine_body(b, off):
        trial = off + lax.shift_left(jnp.int32(1), jnp.int32(16) - b)
        ok = count_ge(key_to_float(key_base + trial)) >= kf
        return jnp.where(ok, trial, off)

    off = lax.fori_loop(0, 17, fine_body, jnp.zeros((1, TQ), jnp.int32))
    lo = key_to_float(key_base + off)
    hi = key_to_float(key_base + off + 1)
    need = kf - count_ge(hi)

    lower = (lax.broadcasted_iota(jnp.int32, (KC, KC), 0)
             > lax.broadcasted_iota(jnp.int32, (KC, KC), 1)).astype(BF16)

    acc_ref[...] = jnp.zeros_like(acc_ref)
    no_max = tuple(jnp.minimum(need, neg_inf) for _ in range(ATT_HEADS))

    def logits_stage(j, taken):
        s = sc_ref[j]
        gt = s >= hi
        eq = jnp.logical_and(s >= lo, jnp.logical_not(gt))
        eqf = jnp.where(eq, 1.0, 0.0)
        before = jnp.dot(lower, eqf.astype(BF16), preferred_element_type=F32) + taken
        sel = jnp.logical_or(gt, jnp.logical_and(eq, before < need))
        sel = jnp.logical_and(sel, jnp.logical_or(causal, j < qi))
        bias = jnp.where(sel, 0.0, neg_inf)
        taken = taken + jnp.sum(_fold_rows(eqf, jnp.add), axis=0, keepdims=True)
        kj = k_ref[key_rows(j), :]
        slot = j % 2
        cms = []
        for h in range(ATT_HEADS):
            lg = _nt_dot(kj, q_ref[:, h * HEAD_DIM:(h + 1) * HEAD_DIM]) + bias
            lg_ref[slot, h] = lg
            cms.append(jnp.max(_fold_rows(lg, jnp.maximum), axis=0, keepdims=True))
        return taken, tuple(cms)

    def pv_stage(jp, cms, ms):
        vtj = vt_ref[jp]
        slot = jp % 2
        new_ms = []
        for h in range(ATT_HEADS):
            m_new = jnp.maximum(ms[h], cms[h])
            shift = jnp.where(m_new == neg_inf, 0.0, m_new)
            p = jnp.exp2(lg_ref[slot, h] - shift).astype(BF16)
            alpha = jnp.exp2(ms[h] - shift)
            acc_ref[h] = alpha * acc_ref[h] + jnp.dot(vtj, p, preferred_element_type=F32)
            new_ms.append(m_new)
        return tuple(new_ms)

    def att_body(j, carry):
        taken, cms_prev, ms = carry
        ms = pv_stage(j - 1, cms_prev, ms)
        taken, cms = logits_stage(j, taken)
        return taken, cms, ms

    taken0, cms0 = logits_stage(0, jnp.zeros((1, TQ), F32))
    _, cms_last, ms = lax.fori_loop(1, nj, att_body, (taken0, cms0, no_max))
    pv_stage(qi, cms_last, ms)
    for pair in range(ATT_HEADS // 2):
        halves = []
        for h in (2 * pair, 2 * pair + 1):
            a = acc_ref[h]
            halves.append(a[:HEAD_DIM, :] * (1.0 / a[HEAD_DIM:HEAD_DIM + 1, :]))
        blk = jnp.concatenate(halves, axis=0)
        o_ref[:, pair * LANES:(pair + 1) * LANES] = blk.T.astype(BF16)


def _dsa(q, iq, misc, k, ik, vt, b, s, topk):
    nq = s // TQ
    tile = lambda bi, qi: (bi * nq + qi, 0)
    full = lambda bi, qi: (bi, 0)
    return pl.pallas_call(
        functools.partial(_dsa_kernel, topk=topk),
        grid=(b, nq),
        in_specs=[
            pl.BlockSpec((TQ, ATT_WIDTH), tile),
            pl.BlockSpec((TQ, IDX_HEADS * IDX_DIM), tile),
            pl.BlockSpec((TQ, LANES), tile),
            pl.BlockSpec((s, HEAD_DIM), full),
            pl.BlockSpec((s, IDX_DIM), full),
            pl.BlockSpec((s // KC, LANES, KC), lambda bi, qi: (bi, 0, 0)),
        ],
        out_specs=pl.BlockSpec((TQ, ATT_WIDTH), tile),
        out_shape=jax.ShapeDtypeStruct((b * s, ATT_WIDTH), BF16),
        scratch_shapes=[
            pltpu.VMEM((nq, KC, TQ), F32),
            pltpu.VMEM((nq, KC, TQ), BF16),
            pltpu.VMEM((2, ATT_HEADS, KC, TQ), F32),
            pltpu.VMEM((ATT_HEADS, LANES, TQ), F32),
        ],
        compiler_params=pltpu.CompilerParams(
            dimension_semantics=("parallel", "arbitrary"), vmem_limit_bytes=VMEM_LIMIT),
        name="dsa_attention",
    )(q, iq, misc, k, ik, vt)


SSD_TT = 512


def _ssd_kernel(xc_ref, zs_ref, misc_ref, dtb_ref, arep_ref, dskip_ref, nw_ref,
                expand_ref, triu_ref, o_ref, state_ref):
    t = pl.program_id(1)
    tt = xc_ref.shape[0]

    @pl.when(t == 0)
    def _():
        state_ref[...] = jnp.zeros_like(state_ref)

    tri = (lax.broadcasted_iota(jnp.int32, (CHUNK, CHUNK), 0)
           >= lax.broadcasted_iota(jnp.int32, (CHUNK, CHUNK), 1))
    left_head = lax.broadcasted_iota(jnp.int32, (CHUNK, LANES), 1) < SSD_HEAD_DIM
    left_head_n = lax.broadcasted_iota(jnp.int32, (D_STATE, LANES), 1) < SSD_HEAD_DIM
    gn = SSD_GROUPS * D_STATE
    pairs_per_group = SSD_HEADS // SSD_GROUPS // 2
    expand = expand_ref[...]
    triu = triu_ref[...]
    zpad = jnp.zeros((SUBLANES, CHUNK), F32)

    for c in range(tt // CHUNK):
        rows = slice(c * CHUNK, (c + 1) * CHUNK)
        raw = misc_ref[rows, :].T[MISC_DT:MISC_DT + SSD_HEADS, :] + dtb_ref[...]
        dt_t = jnp.maximum(raw, 0.0) + jnp.log1p(jnp.exp(-jnp.abs(raw)))
        adt = dt_t * arep_ref[...]
        hi = adt.astype(BF16).astype(F32)
        r1 = adt - hi
        mid = r1.astype(BF16).astype(F32)
        pieces = jnp.concatenate([hi, mid, r1 - mid, zpad], axis=0).astype(BF16)
        cs = jnp.dot(pieces, triu, preferred_element_type=F32)
        acum_t = (cs[0:SUBLANES] + cs[SUBLANES:2 * SUBLANES]) + cs[2 * SUBLANES:3 * SUBLANES]
        a_last = acum_t[:, CHUNK - 1:CHUNK]
        ddt_t = jnp.exp(a_last - acum_t) * dt_t
        acum = jnp.concatenate(
            [acum_t, jnp.zeros((LANES - SSD_HEADS, CHUNK), F32)], axis=0).T
        xs = xc_ref[rows, 0:SSD_WIDTH]
        bm = xc_ref[rows, SSD_WIDTH:SSD_WIDTH + gn]
        cm = xc_ref[rows, SSD_WIDTH + gn:SSD_WIDTH + 2 * gn]
        bm_t = bm.T
        xs16 = xs.astype(BF16)
        bm16 = bm.astype(BF16)
        cm16 = cm.astype(BF16)
        ea = jnp.exp(acum)
        ea_hi = ea.astype(BF16)
        ea_lo = (ea - ea_hi.astype(F32)).astype(BF16)
        ea_x = (jnp.dot(ea_hi, expand, preferred_element_type=F32)
                + jnp.dot(ea_lo, expand, preferred_element_type=F32))
        y_pairs = [None] * (SSD_HEADS // 2)
        for g in range(SSD_GROUPS):
            cg = cm16[:, g * D_STATE:(g + 1) * D_STATE]
            bg = bm16[:, g * D_STATE:(g + 1) * D_STATE]
            bg_t = bm_t[g * D_STATE:(g + 1) * D_STATE, :]
            gmat = _nt_dot(cg, bg)
            for pp in range(pairs_per_group):
                pair = g * pairs_per_group + pp
                lanes = slice(pair * LANES, (pair + 1) * LANES)
                xp = xs16[:, lanes]
                y_halves, s_halves = [], []
                for h in (2 * pair, 2 * pair + 1):
                    col = acum[:, h:h + 1]
                    rowv = acum_t[h:h + 1, :]
                    lmat = jnp.exp(jnp.where(tri, col - rowv, -jnp.inf))
                    mmat = (gmat * lmat * dt_t[h:h + 1, :]).astype(BF16)
                    y_halves.append(jnp.dot(mmat, xp, preferred_element_type=F32))
                    bs = (bg_t * ddt_t[h:h + 1, :]).astype(BF16)
                    s_halves.append(jnp.dot(bs, xp, preferred_element_type=F32))
                y_diag = jnp.where(left_head, y_halves[0], y_halves[1])
                new = jnp.where(left_head_n, s_halves[0], s_halves[1])
                prev = state_ref[pair]
                y_off = jnp.dot(cg, prev.astype(BF16), preferred_element_type=F32) * ea_x[:, lanes]
                state_ref[pair] = prev * ea_x[CHUNK - 1:CHUNK, lanes] + new
                y_pairs[pair] = y_diag + y_off
        y = jnp.concatenate(y_pairs, axis=1) + dskip_ref[...] * xs
        y = y * zs_ref[rows, :]
        ms = jnp.mean(y * y, axis=1, keepdims=True)
        o_ref[rows, :] = (y * lax.rsqrt(ms + LN_EPS) * nw_ref[...]).astype(BF16)


def _ssd(xc, zs, misc, dtb_rep, a_rep, dskip_row, nw_row, b, s):
    tt = min(SSD_TT, s)
    nt = s // tt
    tile = lambda bi, ti: (bi * nt + ti, 0)
    const = lambda bi, ti: (0, 0)
    expand = (jnp.arange(LANES)[:, None] == jnp.arange(SSD_WIDTH)[None, :] // SSD_HEAD_DIM).astype(BF16)
    triu = (jnp.arange(CHUNK)[:, None] <= jnp.arange(CHUNK)[None, :]).astype(BF16)
    return pl.pallas_call(
        _ssd_kernel,
        grid=(b, s // tt),
        in_specs=[
            pl.BlockSpec((tt, CONV_CH), tile),
            pl.BlockSpec((tt, SSD_WIDTH), tile),
            pl.BlockSpec((tt, LANES), tile),
            pl.BlockSpec((SSD_HEADS, CHUNK), const),
            pl.BlockSpec((SSD_HEADS, CHUNK), const),
            pl.BlockSpec((1, SSD_WIDTH), const),
            pl.BlockSpec((1, SSD_WIDTH), const),
            pl.BlockSpec((LANES, SSD_WIDTH), const),
            pl.BlockSpec((CHUNK, CHUNK), const),
        ],
        out_specs=pl.BlockSpec((tt, SSD_WIDTH), tile),
        out_shape=jax.ShapeDtypeStruct((b * s, SSD_WIDTH), BF16),
        scratch_shapes=[
            pltpu.VMEM((SSD_HEADS // 2, D_STATE, LANES), F32),
        ],
        compiler_params=pltpu.CompilerParams(
            dimension_semantics=("parallel", "arbitrary"), vmem_limit_bytes=VMEM_LIMIT),
        name="ssd_mixer",
    )(xc, zs, misc, dtb_rep, a_rep, dskip_row, nw_row, expand, triu)


def _layer_norm(y, g, b):
    mu = jnp.mean(y, axis=1, keepdims=True)
    yc = y - mu
    var = jnp.mean(yc * yc, axis=1, keepdims=True)
    return yc * lax.rsqrt(var + LN_EPS) * g + b


OUT_TM = 1024
OUT_SUB = 256


def _out_proj_kernel(att_ref, ssd_ref, x_ref, wa_ref, ws_ref, g_ref, b_ref, h_ref):
    for r0 in range(0, x_ref.shape[0], OUT_SUB):
        rows = slice(r0, r0 + OUT_SUB)
        mixed = jnp.dot(att_ref[rows, :], wa_ref[...], preferred_element_type=F32)
        mixed = mixed + jnp.dot(ssd_ref[rows, :], ws_ref[...], preferred_element_type=F32)
        h_ref[rows, :] = _layer_norm(ALPHA * x_ref[rows, :] + mixed, g_ref[...], b_ref[...])


def _out_proj(att2, ssd2, x2, w_att, w_ssd, g_row, b_row, tm):
    n = x2.shape[0]
    row = lambda i: (i, 0)
    const = lambda i: (0, 0)
    return pl.pallas_call(
        _out_proj_kernel,
        grid=(n // tm,),
        in_specs=[
            pl.BlockSpec((tm, ATT_WIDTH), row),
            pl.BlockSpec((tm, SSD_WIDTH), row),
            pl.BlockSpec((tm, D_MODEL), row),
            pl.BlockSpec((ATT_WIDTH, D_MODEL), const),
            pl.BlockSpec((SSD_WIDTH, D_MODEL), const),
            pl.BlockSpec((1, D_MODEL), const),
            pl.BlockSpec((1, D_MODEL), const),
        ],
        out_specs=pl.BlockSpec((tm, D_MODEL), row),
        out_shape=jax.ShapeDtypeStruct((n, D_MODEL), F32),
        compiler_params=pltpu.CompilerParams(
            dimension_semantics=("parallel",), vmem_limit_bytes=VMEM_LIMIT),
        name="out_proj_ln",
    )(att2, ssd2, x2, w_att, w_ssd, g_row, b_row)


ROUTE_E0 = N_GROUPS_MOE
MOE_TM = 512
MOE_CAP = 160


def _first_max(vals):
    best = vals[0]
    for v in vals[1:]:
        best = jnp.maximum(best, v)
    idx = jnp.full(best.shape, len(vals) - 1, jnp.int32)
    for i in range(len(vals) - 2, -1, -1):
        idx = jnp.where(vals[i] == best, i, idx)
    return best, idx


def _moe_kernel(h_ref, wrh_ref, wrl_ref, br_ref, wg_ref, wu_ref, wd_ref, g_ref, b_ref, upper_ref,
                o_ref, hb_ref, hp_ref, gp_ref, yp_ref):
    tm = h_ref.shape[0]
    h = h_ref[...]
    h_hi = h.astype(BF16)
    hb_ref[...] = h_hi
    h_lo = (h - h_hi.astype(F32)).astype(BF16)
    wrh = wrh_ref[...]
    both = jnp.dot(h_hi, jnp.concatenate([wrh, wrl_ref[...]], axis=1), preferred_element_type=F32)
    logits = (both[:, :LANES] + jnp.dot(h_lo, wrh, preferred_element_type=F32)
              + both[:, LANES:]) + br_ref[...]
    lt = logits.T
    row = lambda r: lt[r:r + 1, :]
    gl = [row(r) for r in range(N_GROUPS_MOE)]
    gmax, gidx = _first_max(gl)
    denom = jnp.exp(gl[0] - gmax)
    for v in gl[1:]:
        denom = denom + jnp.exp(v - gmax)
    gprob = 1.0 / denom
    el = []
    for k in range(EXPERTS_PER_GROUP):
        v = row(ROUTE_E0 + (N_GROUPS_MOE - 1) * EXPERTS_PER_GROUP + k)
        for gg in range(N_GROUPS_MOE - 2, -1, -1):
            v = jnp.where(gidx == gg, row(ROUTE_E0 + gg * EXPERTS_PER_GROUP + k), v)
        el.append(v)
    l1, i1 = _first_max(el)
    l2, i2 = _first_max([jnp.where(i1 == k, -jnp.inf, el[k]) for k in range(EXPERTS_PER_GROUP)])
    e2 = jnp.exp(l2 - l1)
    w1 = gprob / (1.0 + e2)
    w2 = gprob * e2 / (1.0 + e2)
    gate4 = [jnp.where(i1 == k, w1, jnp.where(i2 == k, w2, 0.0)) for k in range(EXPERTS_PER_GROUP)]

    member = [jnp.where(gidx == gg, 1.0, 0.0) for gg in range(N_GROUPS_MOE)]
    member_blk = jnp.concatenate(
        member + [jnp.zeros((2 * SUBLANES - N_GROUPS_MOE, tm), F32)], axis=0).astype(BF16)
    earlier = jnp.dot(member_blk, upper_ref[...], preferred_element_type=F32)
    counts = [jnp.sum(m, axis=1, keepdims=True) for m in member]
    most = counts[0]
    for c in counts[1:]:
        most = jnp.maximum(most, c)
    fits = jnp.max(most) <= float(MOE_CAP)
    slot = member[0] * earlier[0:1, :]
    for gg in range(1, N_GROUPS_MOE):
        slot = slot + member[gg] * (earlier[gg:gg + 1, :] + float(gg * MOE_CAP))
    tok = jnp.concatenate(
        gate4 + [slot, gidx.astype(F32), jnp.zeros((LANES - EXPERTS_PER_GROUP - 2, tm), F32)],
        axis=0).T
    slot_lane, grp_lane = EXPERTS_PER_GROUP, EXPERTS_PER_GROUP + 1
    rows_p = N_GROUPS_MOE * MOE_CAP

    @pl.when(fits)
    def _():
        hb = hb_ref[...]
        place = jnp.where(lax.broadcasted_iota(jnp.int32, (rows_p, tm), 0) == slot.astype(jnp.int32),
                          1.0, 0.0).astype(BF16)
        hp_ref[...] = jnp.dot(place, hb, preferred_element_type=F32).astype(BF16)
        t_hi = tok.astype(BF16)
        t_lo = (tok - t_hi.astype(F32)).astype(BF16)
        both = jnp.dot(place, jnp.concatenate([t_hi, t_lo], axis=1), preferred_element_type=F32)
        gp_ref[...] = both[:, :LANES] + both[:, LANES:]
        for gg in range(N_GROUPS_MOE):
            rows = slice(gg * MOE_CAP, (gg + 1) * MOE_CAP)
            xg = hp_ref[rows, :]
            acc = None
            for k in range(EXPERTS_PER_GROUP):
                e = gg * EXPERTS_PER_GROUP + k
                a = jnp.dot(xg, wg_ref[e], preferred_element_type=F32)
                u = jnp.dot(xg, wu_ref[e], preferred_element_type=F32)
                hid = (_silu(a) * u * gp_ref[rows, k:k + 1]).astype(BF16)
                part = jnp.dot(hid, wd_ref[e], preferred_element_type=F32)
                acc = part if acc is None else acc + part
            yp_ref[rows, :] = acc.astype(BF16)
        back = jnp.where(lax.broadcasted_iota(jnp.int32, (tm, rows_p), 1)
                         == tok[:, slot_lane:slot_lane + 1].astype(jnp.int32), 1.0, 0.0).astype(BF16)
        for r0 in range(0, tm, OUT_SUB):
            rr = slice(r0, r0 + OUT_SUB)
            y = jnp.dot(back[rr, :], yp_ref[...], preferred_element_type=F32)
            o_ref[rr, :] = _layer_norm(ALPHA * h_ref[rr, :] + y, g_ref[...], b_ref[...])

    @pl.when(jnp.logical_not(fits))
    def _():
        hb = hb_ref[...]
        acc = None
        for gg in range(N_GROUPS_MOE):
            in_grp = tok[:, grp_lane:grp_lane + 1] == float(gg)
            for k in range(EXPERTS_PER_GROUP):
                e = gg * EXPERTS_PER_GROUP + k
                a = jnp.dot(hb, wg_ref[e], preferred_element_type=F32)
                u = jnp.dot(hb, wu_ref[e], preferred_element_type=F32)
                gate = jnp.where(in_grp, tok[:, k:k + 1], 0.0)
                hid = (_silu(a) * u * gate).astype(BF16)
                part = jnp.dot(hid, wd_ref[e], preferred_element_type=F32)
                acc = part if acc is None else acc + part
        o_ref[...] = _layer_norm(ALPHA * h_ref[...] + acc, g_ref[...], b_ref[...])


def _moe(h2, wr_hi, wr_lo, br, wg, wu, wd, g_row, b_row):
    n = h2.shape[0]
    tm = min(MOE_TM, n)
    rows_p = N_GROUPS_MOE * MOE_CAP
    row = lambda i: (i, 0)
    const = lambda i: (0, 0)
    whole = lambda i: (0, 0, 0)
    once = pl.Buffered(1)
    upper = (jnp.arange(tm)[:, None] < jnp.arange(tm)[None, :]).astype(BF16)
    return pl.pallas_call(
        _moe_kernel,
        grid=(n // tm,),
        in_specs=[
            pl.BlockSpec((tm, D_MODEL), row),
            pl.BlockSpec((D_MODEL, LANES), const),
            pl.BlockSpec((D_MODEL, LANES), const),
            pl.BlockSpec((1, LANES), const),
            pl.BlockSpec((N_EXPERTS, D_MODEL, EXPERT_FF), whole, pipeline_mode=once),
            pl.BlockSpec((N_EXPERTS, D_MODEL, EXPERT_FF), whole, pipeline_mode=once),
            pl.BlockSpec((N_EXPERTS, EXPERT_FF, D_MODEL), whole, pipeline_mode=once),
            pl.BlockSpec((1, D_MODEL), const),
            pl.BlockSpec((1, D_MODEL), const),
            pl.BlockSpec((tm, tm), const, pipeline_mode=once),
        ],
        out_specs=pl.BlockSpec((tm, D_MODEL), row),
        out_shape=jax.ShapeDtypeStruct((n, D_MODEL), F32),
        scratch_shapes=[
            pltpu.VMEM((tm, D_MODEL), BF16),
            pltpu.VMEM((rows_p, D_MODEL), BF16),
            pltpu.VMEM((rows_p, LANES), F32),
            pltpu.VMEM((rows_p, D_MODEL), BF16),
        ],
        compiler_params=pltpu.CompilerParams(
            dimension_semantics=("parallel",), vmem_limit_bytes=VMEM_LIMIT),
        name="hier_moe_ln",
    )(h2, wr_hi, wr_lo, br, wg, wu, wd, g_row, b_row, upper)


def _rope_tables(seq):
    inv = ROPE_THETA ** (-jnp.arange(0, HEAD_DIM, 2, dtype=F32) / HEAD_DIM)
    ang = jnp.arange(seq, dtype=F32)[:, None] * inv[None, :]
    cos, sin = jnp.cos(ang), jnp.sin(ang)
    zero = jnp.zeros_like(sin)
    cos_t = jnp.tile(cos, (1, LANES // (HEAD_DIM // 2)))
    s1_t = jnp.tile(jnp.concatenate([-sin, zero], 1), (1, LANES // HEAD_DIM))
    s2_t = jnp.tile(jnp.concatenate([zero, sin], 1), (1, LANES // HEAD_DIM))
    return cos_t, s1_t, s2_t


def _permute_w_in(w):
    sizes = (ATT_WIDTH, HEAD_DIM, HEAD_DIM, IDX_HEADS * IDX_DIM, IDX_DIM, IDX_HEADS,
             SSD_WIDTH, CONV_CH, SSD_HEADS)
    pts = np.cumsum((0,) + sizes)
    q, k, v, iq, ik, iw, z, xbc, dt = [w[:, pts[i]:pts[i + 1]] for i in range(len(sizes))]
    d = w.shape[0]
    pad = lambda n: jnp.zeros((d, n), w.dtype)
    misc = jnp.concatenate([v, iw, pad(MISC_DT - MISC_IW - IDX_HEADS), dt,
                            pad(LANES - MISC_DT - SSD_HEADS)], 1)
    return jnp.concatenate([q, iq, k, ik, misc, z, xbc], 1).astype(BF16)


def _head_rep(vals):
    return jnp.broadcast_to(vals.astype(F32)[:, None], (vals.shape[0], CHUNK))


def kernel(x, w_in, conv_w, conv_b, dt_bias, a_log, d_skip, ssd_norm_w, w_out, ln1_g, ln1_b,
           w_route_group, b_route_group, w_route_expert, b_route_expert, w_gate, w_up,
           w_down, ln2_g, ln2_b):
    bsz, seq, d = x.shape
    n = bsz * seq
    topk = min(TOPK_MAX, seq // 4)
    tm = 512
    assert d == D_MODEL and TQ == KC and seq % TQ == 0 and seq % tm == 0 and topk <= KC
    assert seq // BF16_ROWS <= 256
    cos_t, s1_t, s2_t = _rope_tables(seq)
    mscale = jnp.ones((1, LANES), F32).at[0, MISC_IW:MISC_IW + IDX_HEADS].set(INDEXER_SCALE)
    for l in range(DEPTH):
        x2 = x.reshape(n, d)
        q, iq, k, ik, vt, misc, zs, xc = _in_proj(
            x2, _permute_w_in(w_in[l]), cos_t, s1_t, s2_t, mscale,
            conv_w[l].astype(F32), conv_b[l].astype(F32)[None, :], seq, tm)
        att = _dsa(q, iq, misc, k, ik, vt, bsz, seq, topk)
        ssd = _ssd(
            xc, zs, misc,
            _head_rep(dt_bias[l]), _head_rep(-jnp.exp(a_log[l].astype(F32))),
            jnp.repeat(d_skip[l].astype(F32), SSD_HEAD_DIM)[None, :], ssd_norm_w[l][None, :],
            bsz, seq)
        w_o = w_out[l].astype(BF16)
        h2 = _out_proj(att, ssd, x2, w_o[:ATT_WIDTH], w_o[ATT_WIDTH:],
                       ln1_g[l][None, :], ln1_b[l][None, :], min(OUT_TM, n))
        pad = jnp.zeros((d, LANES - ROUTE_E0 - N_EXPERTS), F32)
        wr = jnp.concatenate([w_route_group[l].astype(F32), w_route_expert[l].astype(F32), pad], 1)
        wr_hi = wr.astype(BF16)
        wr_lo = (wr - wr_hi.astype(F32)).astype(BF16)
        br = jnp.concatenate([b_route_group[l].astype(F32), b_route_expert[l].astype(F32),
                              pad[0]])[None, :]
        x = _moe(h2, wr_hi, wr_lo, br, w_gate[l].astype(BF16), w_up[l].astype(BF16),
                 w_down[l].astype(BF16), ln2_g[l][None, :], ln2_b[l][None, :]).reshape(bsz, seq, d)
    return x
```

```python
import functools

import jax
import jax.numpy as jnp
import numpy as np
from jax import lax
from jax.experimental import pallas as pl
from jax.experimental.pallas import tpu as pltpu

F32 = jnp.float32
BF16 = jnp.bfloat16

D_MODEL = 1024
HEAD_DIM = 64
ATT_WIDTH = 512
ATT_HEADS = 8
IDX_HEADS = 4
IDX_DIM = 64
TOPK_MAX = 256
ROPE_THETA = 10000.0
INDEXER_SCALE = (IDX_HEADS ** -0.5) * (IDX_DIM ** -0.5)
SSD_WIDTH = 512
SSD_HEADS = 8
SSD_HEAD_DIM = 64
SSD_GROUPS = 2
D_STATE = 64
CONV_WIDTH = 4
CONV_CH = SSD_WIDTH + 2 * SSD_GROUPS * D_STATE
CHUNK = 128
N_GROUPS_MOE = 4
EXPERTS_PER_GROUP = 4
N_EXPERTS = 16
EXPERT_FF = 256
DEPTH = 1
ALPHA = (2 * DEPTH) ** 0.25
LN_EPS = 1e-5

LANES = 128
SUBLANES = 8
BF16_ROWS = 16
INF_KEY = 0x7F800000
MIN_NORMAL_KEY = 0x00800000
KEY_STEP16 = 1 << 16
VMEM_LIMIT = 56 * 1024 * 1024

C_Q = 0
C_IQ = 512
C_K = 768
C_IK = 832
C_MISC = 896
MISC_IW = 64
MISC_DT = 72
C_Z = 1024
C_XBC = 1536
IN_COLS = 2304

TQ = 256
KC = 256

Q_SCALE = HEAD_DIM ** -0.5 * float(np.log2(np.e))

NT_DIMS = (((1,), (1,)), ((), ()))


def _nt_dot(a, b):
    return lax.dot_general(a, b, NT_DIMS, preferred_element_type=F32)


def _fold_rows(x, op, rows=SUBLANES):
    slabs = [x[r * rows:(r + 1) * rows, :] for r in range(x.shape[0] // rows)]
    while len(slabs) > 1:
        nxt = [op(slabs[i], slabs[i + 1]) for i in range(0, len(slabs) - 1, 2)]
        if len(slabs) % 2:
            nxt.append(slabs[-1])
        slabs = nxt
    return slabs[0]


CONV_PAD = 8
PROJ_COLS = 256


def _silu(x):
    return x * (1.0 / (1.0 + jnp.exp(-x)))


def _in_proj_kernel(x_ref, w_ref, cos_ref, s1_ref, s2_ref, mscale_ref, cw_ref, cb_ref,
                    q_ref, iq_ref, k_ref, ik_ref, vt_ref, misc_ref, zs_ref, xc_ref, hist_ref,
                    *, nblk_seq):
    @pl.when(pl.program_id(0) == 0)
    def _():
        hist_ref[...] = jnp.zeros_like(hist_ref)

    xb = x_ref[...].astype(BF16)
    cos = cos_ref[...]
    s1 = s1_ref[...]
    s2 = s2_ref[...]

    def mm(c0, width):
        return jnp.dot(xb, w_ref[:, c0:c0 + width], preferred_element_type=F32)

    def rope(y):
        fwd = pltpu.roll(y, LANES - HEAD_DIM // 2, 1)
        bwd = pltpu.roll(y, HEAD_DIM // 2, 1)
        return y * cos + fwd * s1 + bwd * s2

    def rope_wide(y, scale):
        parts = []
        for c in range(y.shape[1] // LANES):
            r = rope(y[:, c * LANES:(c + 1) * LANES])
            parts.append(r * scale if scale != 1.0 else r)
        return parts

    tm = x_ref.shape[0]
    seq_start = pl.program_id(0) % nblk_seq == 0
    for c0 in range(0, CONV_CH, PROJ_COLS):
        cols = slice(c0, c0 + PROJ_COLS)
        xbc = mm(C_XBC + c0, PROJ_COLS)
        hist = jnp.where(seq_start, 0.0, hist_ref[:, cols])
        xp = jnp.concatenate([hist, xbc], axis=0)
        acc = cw_ref[0:1, cols] * xp
        for j in range(1, CONV_WIDTH):
            acc = pltpu.roll(acc, 1, 0) + cw_ref[j:j + 1, cols] * xp
        xc_ref[:, cols] = _silu(acc[CONV_PAD:, :] + cb_ref[:, cols])
        hist_ref[:, cols] = xbc[tm - CONV_PAD:, :]
    for c0 in range(0, SSD_WIDTH, PROJ_COLS):
        zs_ref[:, c0:c0 + PROJ_COLS] = _silu(mm(C_Z + c0, PROJ_COLS))

    for c, r in enumerate(rope_wide(mm(C_Q, ATT_WIDTH), Q_SCALE)):
        q_ref[:, c * LANES:(c + 1) * LANES] = r.astype(BF16)
    for c, r in enumerate(rope_wide(mm(C_IQ, IDX_HEADS * IDX_DIM), 1.0)):
        iq_ref[:, c * LANES:(c + 1) * LANES] = r.astype(BF16)
    kk = rope(mm(C_K, LANES))
    k_ref[...] = kk[:, :HEAD_DIM].astype(BF16)
    ik_ref[...] = kk[:, HEAD_DIM:].astype(BF16)
    misc = mm(C_MISC, LANES) * mscale_ref[...]
    misc_ref[...] = misc
    lane = lax.broadcasted_iota(jnp.int32, misc.shape, 1)
    vext = jnp.where(lane < HEAD_DIM, misc, jnp.where(lane == HEAD_DIM, 1.0, 0.0))
    for c in range(vt_ref.shape[0]):
        vt_ref[c] = vext[c * KC:(c + 1) * KC, :].T.astype(BF16)


def _in_proj(x2, w_perm, cos_t, s1_t, s2_t, mscale, conv_w, conv_b, seq, tm):
    n = x2.shape[0]
    nblk_seq = seq // tm
    row = lambda i: (i, 0)
    tab = lambda i: (i % nblk_seq, 0)
    const = lambda i: (0, 0)
    outs = [
        ((n, ATT_WIDTH), BF16, pl.BlockSpec((tm, ATT_WIDTH), row)),
        ((n, IDX_HEADS * IDX_DIM), BF16, pl.BlockSpec((tm, IDX_HEADS * IDX_DIM), row)),
        ((n, HEAD_DIM), BF16, pl.BlockSpec((tm, HEAD_DIM), row)),
        ((n, IDX_DIM), BF16, pl.BlockSpec((tm, IDX_DIM), row)),
        ((n // KC, LANES, KC), BF16, pl.BlockSpec((tm // KC, LANES, KC), lambda i: (i, 0, 0))),
        ((n, LANES), F32, pl.BlockSpec((tm, LANES), row)),
        ((n, SSD_WIDTH), F32, pl.BlockSpec((tm, SSD_WIDTH), row)),
        ((n, CONV_CH), F32, pl.BlockSpec((tm, CONV_CH), row)),
    ]
    return pl.pallas_call(
        functools.partial(_in_proj_kernel, nblk_seq=nblk_seq),
        grid=(n // tm,),
        in_specs=[
            pl.BlockSpec((tm, D_MODEL), row),
            pl.BlockSpec((D_MODEL, IN_COLS), const),
            pl.BlockSpec((tm, LANES), tab),
            pl.BlockSpec((tm, LANES), tab),
            pl.BlockSpec((tm, LANES), tab),
            pl.BlockSpec((1, LANES), const),
            pl.BlockSpec((CONV_WIDTH, CONV_CH), const),
            pl.BlockSpec((1, CONV_CH), const),
        ],
        out_specs=[spec for _, _, spec in outs],
        out_shape=[jax.ShapeDtypeStruct(shape, dt) for shape, dt, _ in outs],
        scratch_shapes=[pltpu.VMEM((CONV_PAD, CONV_CH), F32)],
        compiler_params=pltpu.CompilerParams(
            dimension_semantics=("arbitrary",), vmem_limit_bytes=VMEM_LIMIT),
        name="in_proj",
    )(x2, w_perm, cos_t, s1_t, s2_t, mscale, conv_w, conv_b)


def _dsa_kernel(q_ref, iq_ref, misc_ref, k_ref, ik_ref, vt_ref, o_ref,
                sc_ref, sc16_ref, lg_ref, acc_ref, *, topk):
    qi = pl.program_id(1)
    nj = qi + 1
    neg_inf = -jnp.inf
    kf = float(topk)
    key_i = lax.broadcasted_iota(jnp.int32, (KC, TQ), 0)
    qry_i = lax.broadcasted_iota(jnp.int32, (KC, TQ), 1)
    causal = key_i <= qry_i

    def key_rows(j):
        return pl.ds(pl.multiple_of(j * KC, KC), KC)

    iw_t = misc_ref[...].T[MISC_IW:MISC_IW + SUBLANES, :]

    def chunk_scores(j):
        ikj = ik_ref[key_rows(j), :]
        sc = None
        for h in range(IDX_HEADS):
            d = _nt_dot(ikj, iq_ref[:, h * IDX_DIM:(h + 1) * IDX_DIM])
            term = iw_t[h:h + 1, :] * jnp.maximum(d, 0.0)
            sc = term if sc is None else sc + term
        sc_ref[j] = sc
        sc16_ref[j] = sc.astype(BF16)

    def scores_body(p, carry):
        chunk_scores(2 * p)
        chunk_scores(jnp.minimum(2 * p + 1, qi))
        return carry

    lax.fori_loop(0, (nj + 1) // 2, scores_body, 0)
    sc_diag = jnp.where(causal, sc_ref[qi], neg_inf)
    sc_ref[qi] = sc_diag
    sc16_ref[qi] = sc_diag.astype(BF16)

    def key_to_float(key):
        mag = jnp.where(key < 0, -key, key)
        sub = jnp.logical_and(mag > 0, mag < MIN_NORMAL_KEY)
        mag = jnp.where(sub, jnp.where(key < 0, 0, MIN_NORMAL_KEY), mag)
        f = lax.bitcast_convert_type(mag, F32)
        return jnp.where(key < 0, -f, f)

    one16 = jnp.ones((), BF16)
    zero16 = jnp.zeros((), BF16)

    def count16_ge(cand16):
        def body(j, acc):
            ind = jnp.where(sc16_ref[j] >= cand16, one16, zero16)
            return acc + _fold_rows(ind, jnp.add, BF16_ROWS)

        acc = lax.fori_loop(0, nj, body, jnp.zeros((BF16_ROWS, TQ), BF16))
        return jnp.sum(acc.astype(F32), axis=0, keepdims=True)

    def count_ge(cand):
        def body(j, acc):
            ind = jnp.where(sc_ref[j] >= cand, 1.0, 0.0)
            return acc + _fold_rows(ind, jnp.add)

        acc = lax.fori_loop(0, nj, body, jnp.zeros((SUBLANES, TQ), F32))
        return jnp.sum(acc, axis=0, keepdims=True)

    def coarse_body(b, m):
        trial = m + lax.shift_left(jnp.int32(1), jnp.int32(15) - b)
        cand16 = key_to_float(trial * KEY_STEP16).astype(BF16)
        return jnp.where(count16_ge(cand16) >= kf, trial, m)

    m16 = lax.fori_loop(0, 16, coarse_body, jnp.full((1, TQ), -(INF_KEY // KEY_STEP16), jnp.int32))
    key_base = jnp.maximum(m16 * KEY_STEP16 - (KEY_STEP16 // 2 + 1), -INF_KEY)

    def fine_body(b, off):
        trial = off + lax.shift_left(jnp.int32(1), jnp.int32(16) - b)
        ok = count_ge(key_to_float(key_base + trial)) >= kf
        return jnp.where(ok, trial, off)

    off = lax.fori_loop(0, 17, fine_body, jnp.zeros((1, TQ), jnp.int32))
    lo = key_to_float(key_base + off)
    hi = key_to_float(key_base + off + 1)
    need = kf - count_ge(hi)

    lower = (lax.broadcasted_iota(jnp.int32, (KC, KC), 0)
             > lax.broadcasted_iota(jnp.int32, (KC, KC), 1)).astype(BF16)

    acc_ref[...] = jnp.zeros_like(acc_ref)
    no_max = tuple(jnp.minimum(need, neg_inf) for _ in range(ATT_HEADS))

    def logits_stage(j, taken):
        s = sc_ref[j]
        eqf = jnp.where(s >= hi, 0.0, jnp.where(s >= lo, 1.0, 0.0))
        before = jnp.dot(lower, eqf.astype(BF16), preferred_element_type=F32) + taken
        tie_bias = jnp.where(before < need, 0.0, neg_inf)
        bias = jnp.where(s >= hi, 0.0, jnp.where(s >= lo, tie_bias, neg_inf))
        bias = jnp.where(j < qi, bias, jnp.where(causal, bias, neg_inf))
        taken = taken + jnp.sum(_fold_rows(eqf, jnp.add), axis=0, keepdims=True)
        kj = k_ref[key_rows(j), :]
        slot = j % 2
        cms = []
        for h in range(ATT_HEADS):
            lg = _nt_dot(kj, q_ref[:, h * HEAD_DIM:(h + 1) * HEAD_DIM]) + bias
            lg_ref[slot, h] = lg
            cms.append(jnp.max(_fold_rows(lg, jnp.maximum), axis=0, keepdims=True))
        return taken, tuple(cms)

    def pv_stage(jp, cms, ms):
        vtj = vt_ref[jp]
        slot = jp % 2
        new_ms = []
        for h in range(ATT_HEADS):
            m_new = jnp.maximum(ms[h], cms[h])
            shift = jnp.where(m_new == neg_inf, 0.0, m_new)
            p = jnp.exp2(lg_ref[slot, h] - shift).astype(BF16)
            alpha = jnp.exp2(ms[h] - shift)
            acc_ref[h] = alpha * acc_ref[h] + jnp.dot(vtj, p, preferred_element_type=F32)
            new_ms.append(m_new)
        return tuple(new_ms)

    def att_body(j, carry):
        taken, cms_prev, ms = carry
        ms = pv_stage(j - 1, cms_prev, ms)
        taken, cms = logits_stage(j, taken)
        return taken, cms, ms

    taken0, cms0 = logits_stage(0, jnp.zeros((1, TQ), F32))
    _, cms_last, ms = lax.fori_loop(1, nj, att_body, (taken0, cms0, no_max))
    pv_stage(qi, cms_last, ms)
    for pair in range(ATT_HEADS // 2):
        halves = []
        for h in (2 * pair, 2 * pair + 1):
            a = acc_ref[h]
            halves.append(a[:HEAD_DIM, :] * (1.0 / a[HEAD_DIM:HEAD_DIM + 1, :]))
        blk = jnp.concatenate(halves, axis=0)
        o_ref[:, pair * LANES:(pair + 1) * LANES] = blk.T.astype(BF16)


def _dsa(q, iq, misc, k, ik, vt, b, s, topk):
    nq = s // TQ
    tile = lambda bi, qi: (bi * nq + qi, 0)
    full = lambda bi, qi: (bi, 0)
    return pl.pallas_call(
        functools.partial(_dsa_kernel, topk=topk),
        grid=(b, nq),
        in_specs=[
            pl.BlockSpec((TQ, ATT_WIDTH), tile),
            pl.BlockSpec((TQ, IDX_HEADS * IDX_DIM), tile),
            pl.BlockSpec((TQ, LANES), tile),
            pl.BlockSpec((s, HEAD_DIM), full),
            pl.BlockSpec((s, IDX_DIM), full),
            pl.BlockSpec((s // KC, LANES, KC), lambda bi, qi: (bi, 0, 0)),
        ],
        out_specs=pl.BlockSpec((TQ, ATT_WIDTH), tile),
        out_shape=jax.ShapeDtypeStruct((b * s, ATT_WIDTH), BF16),
        scratch_shapes=[
            pltpu.VMEM((nq, KC, TQ), F32),
            pltpu.VMEM((nq, KC, TQ), BF16),
            pltpu.VMEM((2, ATT_HEADS, KC, TQ), F32),
            pltpu.VMEM((ATT_HEADS, LANES, TQ), F32),
        ],
        compiler_params=pltpu.CompilerParams(
            dimension_semantics=("parallel", "arbitrary"), vmem_limit_bytes=VMEM_LIMIT),
        name="dsa_attention",
    )(q, iq, misc, k, ik, vt)


SSD_TT = 512


def _ssd_kernel(xc_ref, zs_ref, misc_ref, dtb_ref, arep_ref, dskip_ref, nw_ref,
                expand_ref, triu_ref, o_ref, state_ref):
    t = pl.program_id(1)
    tt = xc_ref.shape[0]

    @pl.when(t == 0)
    def _():
        state_ref[...] = jnp.zeros_like(state_ref)

    tri = (lax.broadcasted_iota(jnp.int32, (CHUNK, CHUNK), 0)
           >= lax.broadcasted_iota(jnp.int32, (CHUNK, CHUNK), 1))
    left_head = lax.broadcasted_iota(jnp.int32, (CHUNK, LANES), 1) < SSD_HEAD_DIM
    left_head_n = lax.broadcasted_iota(jnp.int32, (D_STATE, LANES), 1) < SSD_HEAD_DIM
    gn = SSD_GROUPS * D_STATE
    pairs_per_group = SSD_HEADS // SSD_GROUPS // 2
    expand = expand_ref[...]
    triu = triu_ref[...]
    zpad = jnp.zeros((SUBLANES, CHUNK), F32)

    for c in range(tt // CHUNK):
        rows = slice(c * CHUNK, (c + 1) * CHUNK)
        raw = misc_ref[rows, :].T[MISC_DT:MISC_DT + SSD_HEADS, :] + dtb_ref[...]
        dt_t = jnp.maximum(raw, 0.0) + jnp.log1p(jnp.exp(-jnp.abs(raw)))
        adt = dt_t * arep_ref[...]
        hi = adt.astype(BF16).astype(F32)
        r1 = adt - hi
        mid = r1.astype(BF16).astype(F32)
        pieces = jnp.concatenate([hi, mid, r1 - mid, zpad], axis=0).astype(BF16)
        cs = jnp.dot(pieces, triu, preferred_element_type=F32)
        acum_t = (cs[0:SUBLANES] + cs[SUBLANES:2 * SUBLANES]) + cs[2 * SUBLANES:3 * SUBLANES]
        a_last = acum_t[:, CHUNK - 1:CHUNK]
        ddt_t = jnp.exp(a_last - acum_t) * dt_t
        acum = jnp.concatenate(
            [acum_t, jnp.zeros((LANES - SSD_HEADS, CHUNK), F32)], axis=0).T
        xs = xc_ref[rows, 0:SSD_WIDTH]
        bm = xc_ref[rows, SSD_WIDTH:SSD_WIDTH + gn]
        cm = xc_ref[rows, SSD_WIDTH + gn:SSD_WIDTH + 2 * gn]
        bm_t = bm.T
        xs16 = xs.astype(BF16)
        bm16 = bm.astype(BF16)
        cm16 = cm.astype(BF16)
        ea = jnp.exp(acum)
        ea_hi = ea.astype(BF16)
        ea_lo = (ea - ea_hi.astype(F32)).astype(BF16)
        ea_x = (jnp.dot(ea_hi, expand, preferred_element_type=F32)
                + jnp.dot(ea_lo, expand, preferred_element_type=F32))
        y_pairs = [None] * (SSD_HEADS // 2)
        for g in range(SSD_GROUPS):
            cg = cm16[:, g * D_STATE:(g + 1) * D_STATE]
            bg = bm16[:, g * D_STATE:(g + 1) * D_STATE]
            bg_t = bm_t[g * D_STATE:(g + 1) * D_STATE, :]
            gmat = _nt_dot(cg, bg)
            for pp in range(pairs_per_group):
                pair = g * pairs_per_group + pp
                lanes = slice(pair * LANES, (pair + 1) * LANES)
                xp = xs16[:, lanes]
                y_halves, s_halves = [], []
                for h in (2 * pair, 2 * pair + 1):
                    col = acum[:, h:h + 1]
                    rowv = acum_t[h:h + 1, :]
                    lmat = jnp.exp(jnp.where(tri, col - rowv, -jnp.inf))
                    mmat = (gmat * lmat * dt_t[h:h + 1, :]).astype(BF16)
                    y_halves.append(jnp.dot(mmat, xp, preferred_element_type=F32))
                    bs = (bg_t * ddt_t[h:h + 1, :]).astype(BF16)
                    s_halves.append(jnp.dot(bs, xp, preferred_element_type=F32))
                y_diag = jnp.where(left_head, y_halves[0], y_halves[1])
                new = jnp.where(left_head_n, s_halves[0], s_halves[1])
                prev = state_ref[pair]
                y_off = jnp.dot(cg, prev.astype(BF16), preferred_element_type=F32) * ea_x[:, lanes]
                state_ref[pair] = prev * ea_x[CHUNK - 1:CHUNK, lanes] + new
                y_pairs[pair] = y_diag + y_off
        y = jnp.concatenate(y_pairs, axis=1) + dskip_ref[...] * xs
        y = y * zs_ref[rows, :]
        ms = jnp.mean(y * y, axis=1, keepdims=True)
        o_ref[rows, :] = (y * lax.rsqrt(ms + LN_EPS) * nw_ref[...]).astype(BF16)


def _ssd(xc, zs, misc, dtb_rep, a_rep, dskip_row, nw_row, b, s):
    tt = min(SSD_TT, s)
    nt = s // tt
    tile = lambda bi, ti: (bi * nt + ti, 0)
    const = lambda bi, ti: (0, 0)
    expand = (jnp.arange(LANES)[:, None] == jnp.arange(SSD_WIDTH)[None, :] // SSD_HEAD_DIM).astype(BF16)
    triu = (jnp.arange(CHUNK)[:, None] <= jnp.arange(CHUNK)[None, :]).astype(BF16)
    return pl.pallas_call(
        _ssd_kernel,
        grid=(b, s // tt),
        in_specs=[
            pl.BlockSpec((tt, CONV_CH), tile),
            pl.BlockSpec((tt, SSD_WIDTH), tile),
            pl.BlockSpec((tt, LANES), tile),
            pl.BlockSpec((SSD_HEADS, CHUNK), const),
            pl.BlockSpec((SSD_HEADS, CHUNK), const),
            pl.BlockSpec((1, SSD_WIDTH), const),
            pl.BlockSpec((1, SSD_WIDTH), const),
            pl.BlockSpec((LANES, SSD_WIDTH), const),
            pl.BlockSpec((CHUNK, CHUNK), const),
        ],
        out_specs=pl.BlockSpec((tt, SSD_WIDTH), tile),
        out_shape=jax.ShapeDtypeStruct((b * s, SSD_WIDTH), BF16),
        scratch_shapes=[
            pltpu.VMEM((SSD_HEADS // 2, D_STATE, LANES), F32),
        ],
        compiler_params=pltpu.CompilerParams(
            dimension_semantics=("parallel", "arbitrary"), vmem_limit_bytes=VMEM_LIMIT),
        name="ssd_mixer",
    )(xc, zs, misc, dtb_rep, a_rep, dskip_row, nw_row, expand, triu)


def _layer_norm(y, g, b):
    mu = jnp.mean(y, axis=1, keepdims=True)
    yc = y - mu
    var = jnp.mean(yc * yc, axis=1, keepdims=True)
    return yc * lax.rsqrt(var + LN_EPS) * g + b


OUT_TM = 1024
OUT_SUB = 256


def _out_proj_kernel(att_ref, ssd_ref, x_ref, wa_ref, ws_ref, g_ref, b_ref, h_ref):
    for r0 in range(0, x_ref.shape[0], OUT_SUB):
        rows = slice(r0, r0 + OUT_SUB)
        mixed = jnp.dot(att_ref[rows, :], wa_ref[...], preferred_element_type=F32)
        mixed = mixed + jnp.dot(ssd_ref[rows, :], ws_ref[...], preferred_element_type=F32)
        h_ref[rows, :] = _layer_norm(ALPHA * x_ref[rows, :] + mixed, g_ref[...], b_ref[...])


def _out_proj(att2, ssd2, x2, w_att, w_ssd, g_row, b_row, tm):
    n = x2.shape[0]
    row = lambda i: (i, 0)
    const = lambda i: (0, 0)
    return pl.pallas_call(
        _out_proj_kernel,
        grid=(n // tm,),
        in_specs=[
            pl.BlockSpec((tm, ATT_WIDTH), row),
            pl.BlockSpec((tm, SSD_WIDTH), row),
            pl.BlockSpec((tm, D_MODEL), row),
            pl.BlockSpec((ATT_WIDTH, D_MODEL), const),
            pl.BlockSpec((SSD_WIDTH, D_MODEL), const),
            pl.BlockSpec((1, D_MODEL), const),
            pl.BlockSpec((1, D_MODEL), const),
        ],
        out_specs=pl.BlockSpec((tm, D_MODEL), row),
        out_shape=jax.ShapeDtypeStruct((n, D_MODEL), F32),
        compiler_params=pltpu.CompilerParams(
            dimension_semantics=("parallel",), vmem_limit_bytes=VMEM_LIMIT),
        name="out_proj_ln",
    )(att2, ssd2, x2, w_att, w_ssd, g_row, b_row)


ROUTE_E0 = N_GROUPS_MOE
MOE_TM = 512
MOE_CAP = 160


def _first_max(vals):
    best = vals[0]
    for v in vals[1:]:
        best = jnp.maximum(best, v)
    idx = jnp.full(best.shape, len(vals) - 1, jnp.int32)
    for i in range(len(vals) - 2, -1, -1):
        idx = jnp.where(vals[i] == best, i, idx)
    return best, idx


def _moe_kernel(h_ref, wrh_ref, wrl_ref, br_ref, wg_ref, wu_ref, wd_ref, g_ref, b_ref, upper_ref,
                o_ref, hb_ref, hp_ref, gp_ref, yp_ref):
    tm = h_ref.shape[0]
    h = h_ref[...]
    h_hi = h.astype(BF16)
    hb_ref[...] = h_hi
    h_lo = (h - h_hi.astype(F32)).astype(BF16)
    wrh = wrh_ref[...]
    both = jnp.dot(h_hi, jnp.concatenate([wrh, wrl_ref[...]], axis=1), preferred_element_type=F32)
    logits = (both[:, :LANES] + jnp.dot(h_lo, wrh, preferred_element_type=F32)
              + both[:, LANES:]) + br_ref[...]
    lt = logits.T
    row = lambda r: lt[r:r + 1, :]
    gl = [row(r) for r in range(N_GROUPS_MOE)]
    gmax, gidx = _first_max(gl)
    denom = jnp.exp(gl[0] - gmax)
    for v in gl[1:]:
        denom = denom + jnp.exp(v - gmax)
    gprob = 1.0 / denom
    el = []
    for k in range(EXPERTS_PER_GROUP):
        v = row(ROUTE_E0 + (N_GROUPS_MOE - 1) * EXPERTS_PER_GROUP + k)
        for gg in range(N_GROUPS_MOE - 2, -1, -1):
            v = jnp.where(gidx == gg, row(ROUTE_E0 + gg * EXPERTS_PER_GROUP + k), v)
        el.append(v)
    l1, i1 = _first_max(el)
    l2, i2 = _first_max([jnp.where(i1 == k, -jnp.inf, el[k]) for k in range(EXPERTS_PER_GROUP)])
    e2 = jnp.exp(l2 - l1)
    w1 = gprob / (1.0 + e2)
    w2 = gprob * e2 / (1.0 + e2)
    gate4 = [jnp.where(i1 == k, w1, jnp.where(i2 == k, w2, 0.0)) for k in range(EXPERTS_PER_GROUP)]

    member = [jnp.where(gidx == gg, 1.0, 0.0) for gg in range(N_GROUPS_MOE)]
    member_blk = jnp.concatenate(
        member + [jnp.zeros((2 * SUBLANES - N_GROUPS_MOE, tm), F32)], axis=0).astype(BF16)
    earlier = jnp.dot(member_blk, upper_ref[...], preferred_element_type=F32)
    counts = [jnp.sum(m, axis=1, keepdims=True) for m in member]
    most = counts[0]
    for c in counts[1:]:
        most = jnp.maximum(most, c)
    fits = jnp.max(most) <= float(MOE_CAP)
    slot = member[0] * earlier[0:1, :]
    for gg in range(1, N_GROUPS_MOE):
        slot = slot + member[gg] * (earlier[gg:gg + 1, :] + float(gg * MOE_CAP))
    tok = jnp.concatenate(
        gate4 + [slot, gidx.astype(F32), jnp.zeros((LANES - EXPERTS_PER_GROUP - 2, tm), F32)],
        axis=0).T
    slot_lane, grp_lane = EXPERTS_PER_GROUP, EXPERTS_PER_GROUP + 1
    rows_p = N_GROUPS_MOE * MOE_CAP

    @pl.when(fits)
    def _():
        hb = hb_ref[...]
        place = jnp.where(lax.broadcasted_iota(jnp.int32, (rows_p, tm), 0) == slot.astype(jnp.int32),
                          1.0, 0.0).astype(BF16)
        hp_ref[...] = jnp.dot(place, hb, preferred_element_type=F32).astype(BF16)
        t_hi = tok.astype(BF16)
        t_lo = (tok - t_hi.astype(F32)).astype(BF16)
        both = jnp.dot(place, jnp.concatenate([t_hi, t_lo], axis=1), preferred_element_type=F32)
        gp_ref[...] = both[:, :LANES] + both[:, LANES:]
        for gg in range(N_GROUPS_MOE):
            rows = slice(gg * MOE_CAP, (gg + 1) * MOE_CAP)
            xg = hp_ref[rows, :]
            acc = None
            for k in range(EXPERTS_PER_GROUP):
                e = gg * EXPERTS_PER_GROUP + k
                a = jnp.dot(xg, wg_ref[e], preferred_element_type=F32)
                u = jnp.dot(xg, wu_ref[e], preferred_element_type=F32)
                hid = (_silu(a) * u * gp_ref[rows, k:k + 1]).astype(BF16)
                part = jnp.dot(hid, wd_ref[e], preferred_element_type=F32)
                acc = part if acc is None else acc + part
            yp_ref[rows, :] = acc.astype(BF16)
        back = jnp.where(lax.broadcasted_iota(jnp.int32, (tm, rows_p), 1)
                         == tok[:, slot_lane:slot_lane + 1].astype(jnp.int32), 1.0, 0.0).astype(BF16)
        for r0 in range(0, tm, OUT_SUB):
            rr = slice(r0, r0 + OUT_SUB)
            y = jnp.dot(back[rr, :], yp_ref[...], preferred_element_type=F32)
            o_ref[rr, :] = _layer_norm(ALPHA * h_ref[rr, :] + y, g_ref[...], b_ref[...])

    @pl.when(jnp.logical_not(fits))
    def _():
        hb = hb_ref[...]
        acc = None
        for gg in range(N_GROUPS_MOE):
            in_grp = tok[:, grp_lane:grp_lane + 1] == float(gg)
            for k in range(EXPERTS_PER_GROUP):
                e = gg * EXPERTS_PER_GROUP + k
                a = jnp.dot(hb, wg_ref[e], preferred_element_type=F32)
                u = jnp.dot(hb, wu_ref[e], preferred_element_type=F32)
                gate = jnp.where(in_grp, tok[:, k:k + 1], 0.0)
                hid = (_silu(a) * u * gate).astype(BF16)
                part = jnp.dot(hid, wd_ref[e], preferred_element_type=F32)
                acc = part if acc is None else acc + part
        o_ref[...] = _layer_norm(ALPHA * h_ref[...] + acc, g_ref[...], b_ref[...])


def _moe(h2, wr_hi, wr_lo, br, wg, wu, wd, g_row, b_row):
    n = h2.shape[0]
    tm = min(MOE_TM, n)
    rows_p = N_GROUPS_MOE * MOE_CAP
    row = lambda i: (i, 0)
    const = lambda i: (0, 0)
    whole = lambda i: (0, 0, 0)
    once = pl.Buffered(1)
    upper = (jnp.arange(tm)[:, None] < jnp.arange(tm)[None, :]).astype(BF16)
    return pl.pallas_call(
        _moe_kernel,
        grid=(n // tm,),
        in_specs=[
            pl.BlockSpec((tm, D_MODEL), row),
            pl.BlockSpec((D_MODEL, LANES), const),
            pl.BlockSpec((D_MODEL, LANES), const),
            pl.BlockSpec((1, LANES), const),
            pl.BlockSpec((N_EXPERTS, D_MODEL, EXPERT_FF), whole, pipeline_mode=once),
            pl.BlockSpec((N_EXPERTS, D_MODEL, EXPERT_FF), whole, pipeline_mode=once),
            pl.BlockSpec((N_EXPERTS, EXPERT_FF, D_MODEL), whole, pipeline_mode=once),
            pl.BlockSpec((1, D_MODEL), const),
            pl.BlockSpec((1, D_MODEL), const),
            pl.BlockSpec((tm, tm), const, pipeline_mode=once),
        ],
        out_specs=pl.BlockSpec((tm, D_MODEL), row),
        out_shape=jax.ShapeDtypeStruct((n, D_MODEL), F32),
        scratch_shapes=[
            pltpu.VMEM((tm, D_MODEL), BF16),
            pltpu.VMEM((rows_p, D_MODEL), BF16),
            pltpu.VMEM((rows_p, LANES), F32),
            pltpu.VMEM((rows_p, D_MODEL), BF16),
        ],
        compiler_params=pltpu.CompilerParams(
            dimension_semantics=("parallel",), vmem_limit_bytes=VMEM_LIMIT),
        name="hier_moe_ln",
    )(h2, wr_hi, wr_lo, br, wg, wu, wd, g_row, b_row, upper)


def _rope_tables(seq):
    inv = ROPE_THETA ** (-jnp.arange(0, HEAD_DIM, 2, dtype=F32) / HEAD_DIM)
    ang = jnp.arange(seq, dtype=F32)[:, None] * inv[None, :]
    cos, sin = jnp.cos(ang), jnp.sin(ang)
    zero = jnp.zeros_like(sin)
    cos_t = jnp.tile(cos, (1, LANES // (HEAD_DIM // 2)))
    s1_t = jnp.tile(jnp.concatenate([-sin, zero], 1), (1, LANES // HEAD_DIM))
    s2_t = jnp.tile(jnp.concatenate([zero, sin], 1), (1, LANES // HEAD_DIM))
    return cos_t, s1_t, s2_t


def _permute_w_in(w):
    sizes = (ATT_WIDTH, HEAD_DIM, HEAD_DIM, IDX_HEADS * IDX_DIM, IDX_DIM, IDX_HEADS,
             SSD_WIDTH, CONV_CH, SSD_HEADS)
    pts = np.cumsum((0,) + sizes)
    q, k, v, iq, ik, iw, z, xbc, dt = [w[:, pts[i]:pts[i + 1]] for i in range(len(sizes))]
    d = w.shape[0]
    pad = lambda n: jnp.zeros((d, n), w.dtype)
    misc = jnp.concatenate([v, iw, pad(MISC_DT - MISC_IW - IDX_HEADS), dt,
                            pad(LANES - MISC_DT - SSD_HEADS)], 1)
    return jnp.concatenate([q, iq, k, ik, misc, z, xbc], 1).astype(BF16)


def _head_rep(vals):
    return jnp.broadcast_to(vals.astype(F32)[:, None], (vals.shape[0], CHUNK))


def kernel(x, w_in, conv_w, conv_b, dt_bias, a_log, d_skip, ssd_norm_w, w_out, ln1_g, ln1_b,
           w_route_group, b_route_group, w_route_expert, b_route_expert, w_gate, w_up,
           w_down, ln2_g, ln2_b):
    bsz, seq, d = x.shape
    n = bsz * seq
    topk = min(TOPK_MAX, seq // 4)
    tm = 512
    assert d == D_MODEL and TQ == KC and seq % TQ == 0 and seq % tm == 0 and topk <= KC
    assert seq // BF16_ROWS <= 256
    cos_t, s1_t, s2_t = _rope_tables(seq)
    mscale = jnp.ones((1, LANES), F32).at[0, MISC_IW:MISC_IW + IDX_HEADS].set(INDEXER_SCALE)
    for l in range(DEPTH):
        x2 = x.reshape(n, d)
        q, iq, k, ik, vt, misc, zs, xc = _in_proj(
            x2, _permute_w_in(w_in[l]), cos_t, s1_t, s2_t, mscale,
            conv_w[l].astype(F32), conv_b[l].astype(F32)[None, :], seq, tm)
        att = _dsa(q, iq, misc, k, ik, vt, bsz, seq, topk)
        ssd = _ssd(
            xc, zs, misc,
            _head_rep(dt_bias[l]), _head_rep(-jnp.exp(a_log[l].astype(F32))),
            jnp.repeat(d_skip[l].astype(F32), SSD_HEAD_DIM)[None, :], ssd_norm_w[l][None, :],
            bsz, seq)
        w_o = w_out[l].astype(BF16)
        h2 = _out_proj(att, ssd, x2, w_o[:ATT_WIDTH], w_o[ATT_WIDTH:],
                       ln1_g[l][None, :], ln1_b[l][None, :], min(OUT_TM, n))
        pad = jnp.zeros((d, LANES - ROUTE_E0 - N_EXPERTS), F32)
        wr = jnp.concatenate([w_route_group[l].astype(F32), w_route_expert[l].astype(F32), pad], 1)
        wr_hi = wr.astype(BF16)
        wr_lo = (wr - wr_hi.astype(F32)).astype(BF16)
        br = jnp.concatenate([b_route_group[l].astype(F32), b_route_expert[l].astype(F32),
                              pad[0]])[None, :]
        x = _moe(h2, wr_hi, wr_lo, br, w_gate[l].astype(BF16), w_up[l].astype(BF16),
                 w_down[l].astype(BF16), ln2_g[l][None, :], ln2_b[l][None, :]).reshape(bsz, seq, d)
    return x
```

```python
import functools

import jax
import jax.numpy as jnp
import numpy as np
from jax import lax
from jax.experimental import pallas as pl
from jax.experimental.pallas import tpu as pltpu

F32 = jnp.float32
BF16 = jnp.bfloat16

D_MODEL = 1024
HEAD_DIM = 64
ATT_WIDTH = 512
ATT_HEADS = 8
IDX_HEADS = 4
IDX_DIM = 64
TOPK_MAX = 256
ROPE_THETA = 10000.0
INDEXER_SCALE = (IDX_HEADS ** -0.5) * (IDX_DIM ** -0.5)
SSD_WIDTH = 512
SSD_HEADS = 8
SSD_HEAD_DIM = 64
SSD_GROUPS = 2
D_STATE = 64
CONV_WIDTH = 4
CONV_CH = SSD_WIDTH + 2 * SSD_GROUPS * D_STATE
CHUNK = 128
N_GROUPS_MOE = 4
EXPERTS_PER_GROUP = 4
N_EXPERTS = 16
EXPERT_FF = 256
DEPTH = 1
ALPHA = (2 * DEPTH) ** 0.25
LN_EPS = 1e-5

LANES = 128
SUBLANES = 8
BF16_ROWS = 16
INF_KEY = 0x7F800000
MIN_NORMAL_KEY = 0x00800000
KEY_STEP16 = 1 << 16
VMEM_LIMIT = 56 * 1024 * 1024

C_Q = 0
C_IQ = 512
C_K = 768
C_IK = 832
C_MISC = 896
MISC_IW = 64
MISC_DT = 72
C_Z = 1024
C_XBC = 1536
IN_COLS = 2304

TQ = 256
KC = 256

Q_SCALE = HEAD_DIM ** -0.5 * float(np.log2(np.e))

NT_DIMS = (((1,), (1,)), ((), ()))


def _nt_dot(a, b):
    return lax.dot_general(a, b, NT_DIMS, preferred_element_type=F32)


def _fold_rows(x, op, rows=SUBLANES):
    slabs = [x[r * rows:(r + 1) * rows, :] for r in range(x.shape[0] // rows)]
    while len(slabs) > 1:
        nxt = [op(slabs[i], slabs[i + 1]) for i in range(0, len(slabs) - 1, 2)]
        if len(slabs) % 2:
            nxt.append(slabs[-1])
        slabs = nxt
    return slabs[0]


CONV_PAD = 8
PROJ_COLS = 256


def _silu(x):
    return x * (1.0 / (1.0 + jnp.exp(-x)))


def _in_proj_kernel(x_ref, w_ref, cos_ref, s1_ref, s2_ref, mscale_ref, cw_ref, cb_ref,
                    q_ref, iq_ref, k_ref, ik_ref, vt_ref, misc_ref, zs_ref, xc_ref, hist_ref,
                    *, nblk_seq):
    @pl.when(pl.program_id(0) == 0)
    def _():
        hist_ref[...] = jnp.zeros_like(hist_ref)

    xb = x_ref[...].astype(BF16)
    cos = cos_ref[...]
    s1 = s1_ref[...]
    s2 = s2_ref[...]

    def mm(c0, width):
        return jnp.dot(xb, w_ref[:, c0:c0 + width], preferred_element_type=F32)

    def rope(y):
        fwd = pltpu.roll(y, LANES - HEAD_DIM // 2, 1)
        bwd = pltpu.roll(y, HEAD_DIM // 2, 1)
        return y * cos + fwd * s1 + bwd * s2

    def rope_wide(y, scale):
        parts = []
        for c in range(y.shape[1] // LANES):
            r = rope(y[:, c * LANES:(c + 1) * LANES])
            parts.append(r * scale if scale != 1.0 else r)
        return parts

    tm = x_ref.shape[0]
    seq_start = pl.program_id(0) % nblk_seq == 0
    for c0 in range(0, CONV_CH, PROJ_COLS):
        cols = slice(c0, c0 + PROJ_COLS)
        xbc = mm(C_XBC + c0, PROJ_COLS)
        hist = jnp.where(seq_start, 0.0, hist_ref[:, cols])
        xp = jnp.concatenate([hist, xbc], axis=0)
        acc = cw_ref[0:1, cols] * xp
        for j in range(1, CONV_WIDTH):
            acc = pltpu.roll(acc, 1, 0) + cw_ref[j:j + 1, cols] * xp
        xc_ref[:, cols] = _silu(acc[CONV_PAD:, :] + cb_ref[:, cols])
        hist_ref[:, cols] = xbc[tm - CONV_PAD:, :]
    for c0 in range(0, SSD_WIDTH, PROJ_COLS):
        zs_ref[:, c0:c0 + PROJ_COLS] = _silu(mm(C_Z + c0, PROJ_COLS))

    for c, r in enumerate(rope_wide(mm(C_Q, ATT_WIDTH), Q_SCALE)):
        q_ref[:, c * LANES:(c + 1) * LANES] = r.astype(BF16)
    for c, r in enumerate(rope_wide(mm(C_IQ, IDX_HEADS * IDX_DIM), 1.0)):
        iq_ref[:, c * LANES:(c + 1) * LANES] = r.astype(BF16)
    kk = rope(mm(C_K, LANES))
    k_ref[...] = kk[:, :HEAD_DIM].astype(BF16)
    ik_ref[...] = kk[:, HEAD_DIM:].astype(BF16)
    misc = mm(C_MISC, LANES) * mscale_ref[...]
    misc_ref[...] = misc
    lane = lax.broadcasted_iota(jnp.int32, misc.shape, 1)
    vext = jnp.where(lane < HEAD_DIM, misc, jnp.where(lane == HEAD_DIM, 1.0, 0.0))
    for c in range(vt_ref.shape[0]):
        vt_ref[c] = vext[c * KC:(c + 1) * KC, :].T.astype(BF16)


def _in_proj(x2, w_perm, cos_t, s1_t, s2_t, mscale, conv_w, conv_b, seq, tm):
    n = x2.shape[0]
    nblk_seq = seq // tm
    row = lambda i: (i, 0)
    tab = lambda i: (i % nblk_seq, 0)
    const = lambda i: (0, 0)
    outs = [
        ((n, ATT_WIDTH), BF16, pl.BlockSpec((tm, ATT_WIDTH), row)),
        ((n, IDX_HEADS * IDX_DIM), BF16, pl.BlockSpec((tm, IDX_HEADS * IDX_DIM), row)),
        ((n, HEAD_DIM), BF16, pl.BlockSpec((tm, HEAD_DIM), row)),
        ((n, IDX_DIM), BF16, pl.BlockSpec((tm, IDX_DIM), row)),
        ((n // KC, LANES, KC), BF16, pl.BlockSpec((tm // KC, LANES, KC), lambda i: (i, 0, 0))),
        ((n, LANES), F32, pl.BlockSpec((tm, LANES), row)),
        ((n, SSD_WIDTH), F32, pl.BlockSpec((tm, SSD_WIDTH), row)),
        ((n, CONV_CH), F32, pl.BlockSpec((tm, CONV_CH), row)),
    ]
    return pl.pallas_call(
        functools.partial(_in_proj_kernel, nblk_seq=nblk_seq),
        grid=(n // tm,),
        in_specs=[
            pl.BlockSpec((tm, D_MODEL), row),
            pl.BlockSpec((D_MODEL, IN_COLS), const),
            pl.BlockSpec((tm, LANES), tab),
            pl.BlockSpec((tm, LANES), tab),
            pl.BlockSpec((tm, LANES), tab),
            pl.BlockSpec((1, LANES), const),
            pl.BlockSpec((CONV_WIDTH, CONV_CH), const),
            pl.BlockSpec((1, CONV_CH), const),
        ],
        out_specs=[spec for _, _, spec in outs],
        out_shape=[jax.ShapeDtypeStruct(shape, dt) for shape, dt, _ in outs],
        scratch_shapes=[pltpu.VMEM((CONV_PAD, CONV_CH), F32)],
        compiler_params=pltpu.CompilerParams(
            dimension_semantics=("arbitrary",), vmem_limit_bytes=VMEM_LIMIT),
        name="in_proj",
    )(x2, w_perm, cos_t, s1_t, s2_t, mscale, conv_w, conv_b)


def _dsa_kernel(q_ref, iq_ref, misc_ref, k_ref, ik_ref, vt_ref, o_ref,
                sc_ref, sc16_ref, lg_ref, acc_ref, *, topk):
    qi = pl.program_id(1)
    nj = qi + 1
    neg_inf = -jnp.inf
    kf = float(topk)
    key_i = lax.broadcasted_iota(jnp.int32, (KC, TQ), 0)
    qry_i = lax.broadcasted_iota(jnp.int32, (KC, TQ), 1)
    causal = key_i <= qry_i

    def key_rows(j):
        return pl.ds(pl.multiple_of(j * KC, KC), KC)

    iw_t = misc_ref[...].T[MISC_IW:MISC_IW + SUBLANES, :]

    def chunk_scores(j):
        ikj = ik_ref[key_rows(j), :]
        sc = None
        for h in range(IDX_HEADS):
            d = _nt_dot(ikj, iq_ref[:, h * IDX_DIM:(h + 1) * IDX_DIM])
            term = iw_t[h:h + 1, :] * jnp.maximum(d, 0.0)
            sc = term if sc is None else sc + term
        sc_ref[j] = sc
        sc16_ref[j] = sc.astype(BF16)

    def scores_body(p, carry):
        chunk_scores(2 * p)
        chunk_scores(jnp.minimum(2 * p + 1, qi))
        return carry

    lax.fori_loop(0, (nj + 1) // 2, scores_body, 0)
    sc_diag = jnp.where(causal, sc_ref[qi], neg_inf)
    sc_ref[qi] = sc_diag
    sc16_ref[qi] = sc_diag.astype(BF16)

    def key_to_float(key):
        mag = jnp.where(key < 0, -key, key)
        sub = jnp.logical_and(mag > 0, mag < MIN_NORMAL_KEY)
        mag = jnp.where(sub, jnp.where(key < 0, 0, MIN_NORMAL_KEY), mag)
        f = lax.bitcast_convert_type(mag, F32)
        return jnp.where(key < 0, -f, f)

    one16 = jnp.ones((), BF16)
    zero16 = jnp.zeros((), BF16)

    nj_search = jnp.where(qi == 0, 0, nj) if topk >= TQ else nj

    def count16_ge(cand16):
        def body(j, acc):
            ind = jnp.where(sc16_ref[j] >= cand16, one16, zero16)
            return acc + _fold_rows(ind, jnp.add, BF16_ROWS)

        acc = lax.fori_loop(0, nj_search, body, jnp.zeros((BF16_ROWS, TQ), BF16))
        return jnp.sum(acc.astype(F32), axis=0, keepdims=True)

    def count_ge(cand, chunks=nj_search):
        def body(j, acc):
            ind = jnp.where(sc_ref[j] >= cand, 1.0, 0.0)
            return acc + _fold_rows(ind, jnp.add)

        acc = lax.fori_loop(0, chunks, body, jnp.zeros((SUBLANES, TQ), F32))
        return jnp.sum(acc, axis=0, keepdims=True)

    def coarse_body(b, m):
        trial = m + lax.shift_left(jnp.int32(1), jnp.int32(15) - b)
        cand16 = key_to_float(trial * KEY_STEP16).astype(BF16)
        return jnp.where(count16_ge(cand16) >= kf, trial, m)

    m16 = lax.fori_loop(0, 16, coarse_body, jnp.full((1, TQ), -(INF_KEY // KEY_STEP16), jnp.int32))
    key_base = jnp.maximum(m16 * KEY_STEP16 - (KEY_STEP16 // 2 + 1), -INF_KEY)

    def fine_body(b, off):
        trial = off + lax.shift_left(jnp.int32(1), jnp.int32(16) - b)
        ok = count_ge(key_to_float(key_base + trial)) >= kf
        return jnp.where(ok, trial, off)

    off = lax.fori_loop(0, 17, fine_body, jnp.zeros((1, TQ), jnp.int32))
    lo = key_to_float(key_base + off)
    hi = key_to_float(key_base + off + 1)
    need = kf - count_ge(hi, nj)

    lower = (lax.broadcasted_iota(jnp.int32, (KC, KC), 0)
             > lax.broadcasted_iota(jnp.int32, (KC, KC), 1)).astype(BF16)

    acc_ref[...] = jnp.zeros_like(acc_ref)
    no_max = tuple(jnp.minimum(need, neg_inf) for _ in range(ATT_HEADS))

    def logits_stage(j, taken):
        s = sc_ref[j]
        eqf = jnp.where(s >= hi, 0.0, jnp.where(s >= lo, 1.0, 0.0))
        before = jnp.dot(lower, eqf.astype(BF16), preferred_element_type=F32) + taken
        tie_bias = jnp.where(before < need, 0.0, neg_inf)
        bias = jnp.where(s >= hi, 0.0, jnp.where(s >= lo, tie_bias, neg_inf))
        bias = jnp.where(j < qi, bias, jnp.where(causal, bias, neg_inf))
        taken = taken + jnp.sum(_fold_rows(eqf, jnp.add), axis=0, keepdims=True)
        kj = k_ref[key_rows(j), :]
        slot = j % 2
        cms = []
        for h in range(ATT_HEADS):
            lg = _nt_dot(kj, q_ref[:, h * HEAD_DIM:(h + 1) * HEAD_DIM]) + bias
            lg_ref[slot, h] = lg
            cms.append(jnp.max(_fold_rows(lg, jnp.maximum), axis=0, keepdims=True))
        return taken, tuple(cms)

    def pv_stage(jp, cms, ms):
        vtj = vt_ref[jp]
        slot = jp % 2
        new_ms = []
        for h in range(ATT_HEADS):
            m_new = jnp.maximum(ms[h], cms[h])
            shift = jnp.where(m_new == neg_inf, 0.0, m_new)
            p = jnp.exp2(lg_ref[slot, h] - shift).astype(BF16)
            alpha = jnp.exp2(ms[h] - shift)
            acc_ref[h] = alpha * acc_ref[h] + jnp.dot(vtj, p, preferred_element_type=F32)
            new_ms.append(m_new)
        return tuple(new_ms)

    def att_body(j, carry):
        taken, cms_prev, ms = carry
        ms = pv_stage(j - 1, cms_prev, ms)
        taken, cms = logits_stage(j, taken)
        return taken, cms, ms

    taken0, cms0 = logits_stage(0, jnp.zeros((1, TQ), F32))
    _, cms_last, ms = lax.fori_loop(1, nj, att_body, (taken0, cms0, no_max))
    pv_stage(qi, cms_last, ms)
    for pair in range(ATT_HEADS // 2):
        halves = []
        for h in (2 * pair, 2 * pair + 1):
            a = acc_ref[h]
            halves.append(a[:HEAD_DIM, :] * (1.0 / a[HEAD_DIM:HEAD_DIM + 1, :]))
        blk = jnp.concatenate(halves, axis=0)
        o_ref[:, pair * LANES:(pair + 1) * LANES] = blk.T.astype(BF16)


def _dsa(q, iq, misc, k, ik, vt, b, s, topk):
    nq = s // TQ
    tile = lambda bi, qi: (bi * nq + qi, 0)
    full = lambda bi, qi: (bi, 0)
    return pl.pallas_call(
        functools.partial(_dsa_kernel, topk=topk),
        grid=(b, nq),
        in_specs=[
            pl.BlockSpec((TQ, ATT_WIDTH), tile),
            pl.BlockSpec((TQ, IDX_HEADS * IDX_DIM), tile),
            pl.BlockSpec((TQ, LANES), tile),
            pl.BlockSpec((s, HEAD_DIM), full),
            pl.BlockSpec((s, IDX_DIM), full),
            pl.BlockSpec((s // KC, LANES, KC), lambda bi, qi: (bi, 0, 0)),
        ],
        out_specs=pl.BlockSpec((TQ, ATT_WIDTH), tile),
        out_shape=jax.ShapeDtypeStruct((b * s, ATT_WIDTH), BF16),
        scratch_shapes=[
            pltpu.VMEM((nq, KC, TQ), F32),
            pltpu.VMEM((nq, KC, TQ), BF16),
            pltpu.VMEM((2, ATT_HEADS, KC, TQ), F32),
            pltpu.VMEM((ATT_HEADS, LANES, TQ), F32),
        ],
        compiler_params=pltpu.CompilerParams(
            dimension_semantics=("parallel", "arbitrary"), vmem_limit_bytes=VMEM_LIMIT),
        name="dsa_attention",
    )(q, iq, misc, k, ik, vt)


SSD_TT = 512


def _ssd_kernel(xc_ref, zs_ref, misc_ref, dtb_ref, arep_ref, dskip_ref, nw_ref,
                expand_ref, triu_ref, o_ref, state_ref):
    t = pl.program_id(1)
    tt = xc_ref.shape[0]

    @pl.when(t == 0)
    def _():
        state_ref[...] = jnp.zeros_like(state_ref)

    tri = (lax.broadcasted_iota(jnp.int32, (CHUNK, CHUNK), 0)
           >= lax.broadcasted_iota(jnp.int32, (CHUNK, CHUNK), 1))
    left_head = lax.broadcasted_iota(jnp.int32, (CHUNK, LANES), 1) < SSD_HEAD_DIM
    left_head_n = lax.broadcasted_iota(jnp.int32, (D_STATE, LANES), 1) < SSD_HEAD_DIM
    gn = SSD_GROUPS * D_STATE
    pairs_per_group = SSD_HEADS // SSD_GROUPS // 2
    expand = expand_ref[...]
    triu = triu_ref[...]
    zpad = jnp.zeros((SUBLANES, CHUNK), F32)

    for c in range(tt // CHUNK):
        rows = slice(c * CHUNK, (c + 1) * CHUNK)
        raw = misc_ref[rows, :].T[MISC_DT:MISC_DT + SSD_HEADS, :] + dtb_ref[...]
        dt_t = jnp.maximum(raw, 0.0) + jnp.log1p(jnp.exp(-jnp.abs(raw)))
        adt = dt_t * arep_ref[...]
        hi = adt.astype(BF16).astype(F32)
        r1 = adt - hi
        mid = r1.astype(BF16).astype(F32)
        pieces = jnp.concatenate([hi, mid, r1 - mid, zpad], axis=0).astype(BF16)
        cs = jnp.dot(pieces, triu, preferred_element_type=F32)
        acum_t = (cs[0:SUBLANES] + cs[SUBLANES:2 * SUBLANES]) + cs[2 * SUBLANES:3 * SUBLANES]
        a_last = acum_t[:, CHUNK - 1:CHUNK]
        ddt_t = jnp.exp(a_last - acum_t) * dt_t
        acum = jnp.concatenate(
            [acum_t, jnp.zeros((LANES - SSD_HEADS, CHUNK), F32)], axis=0).T
        xs = xc_ref[rows, 0:SSD_WIDTH]
        bm = xc_ref[rows, SSD_WIDTH:SSD_WIDTH + gn]
        cm = xc_ref[rows, SSD_WIDTH + gn:SSD_WIDTH + 2 * gn]
        bm_t = bm.T
        xs16 = xs.astype(BF16)
        bm16 = bm.astype(BF16)
        cm16 = cm.astype(BF16)
        ea = jnp.exp(acum)
        ea_hi = ea.astype(BF16)
        ea_lo = (ea - ea_hi.astype(F32)).astype(BF16)
        ea_x = (jnp.dot(ea_hi, expand, preferred_element_type=F32)
                + jnp.dot(ea_lo, expand, preferred_element_type=F32))
        y_pairs = [None] * (SSD_HEADS // 2)
        for g in range(SSD_GROUPS):
            cg = cm16[:, g * D_STATE:(g + 1) * D_STATE]
            bg = bm16[:, g * D_STATE:(g + 1) * D_STATE]
            bg_t = bm_t[g * D_STATE:(g + 1) * D_STATE, :]
            gmat = _nt_dot(cg, bg)
            for pp in range(pairs_per_group):
                pair = g * pairs_per_group + pp
                lanes = slice(pair * LANES, (pair + 1) * LANES)
                xp = xs16[:, lanes]
                y_halves, s_halves = [], []
                for h in (2 * pair, 2 * pair + 1):
                    col = acum[:, h:h + 1]
                    rowv = acum_t[h:h + 1, :]
                    lmat = jnp.exp(jnp.where(tri, col - rowv, -jnp.inf))
                    mmat = (gmat * lmat * dt_t[h:h + 1, :]).astype(BF16)
                    y_halves.append(jnp.dot(mmat, xp, preferred_element_type=F32))
                    bs = (bg_t * ddt_t[h:h + 1, :]).astype(BF16)
                    s_halves.append(jnp.dot(bs, xp, preferred_element_type=F32))
                y_diag = jnp.where(left_head, y_halves[0], y_halves[1])
                new = jnp.where(left_head_n, s_halves[0], s_halves[1])
                prev = state_ref[pair]
                y_off = jnp.dot(cg, prev.astype(BF16), preferred_element_type=F32) * ea_x[:, lanes]
                state_ref[pair] = prev * ea_x[CHUNK - 1:CHUNK, lanes] + new
                y_pairs[pair] = y_diag + y_off
        y = jnp.concatenate(y_pairs, axis=1) + dskip_ref[...] * xs
        y = y * zs_ref[rows, :]
        ms = jnp.mean(y * y, axis=1, keepdims=True)
        o_ref[rows, :] = (y * lax.rsqrt(ms + LN_EPS) * nw_ref[...]).astype(BF16)


def _ssd(xc, zs, misc, dtb_rep, a_rep, dskip_row, nw_row, b, s):
    tt = min(SSD_TT, s)
    nt = s // tt
    tile = lambda bi, ti: (bi * nt + ti, 0)
    const = lambda bi, ti: (0, 0)
    expand = (jnp.arange(LANES)[:, None] == jnp.arange(SSD_WIDTH)[None, :] // SSD_HEAD_DIM).astype(BF16)
    triu = (jnp.arange(CHUNK)[:, None] <= jnp.arange(CHUNK)[None, :]).astype(BF16)
    return pl.pallas_call(
        _ssd_kernel,
        grid=(b, s // tt),
        in_specs=[
            pl.BlockSpec((tt, CONV_CH), tile),
            pl.BlockSpec((tt, SSD_WIDTH), tile),
            pl.BlockSpec((tt, LANES), tile),
            pl.BlockSpec((SSD_HEADS, CHUNK), const),
            pl.BlockSpec((SSD_HEADS, CHUNK), const),
            pl.BlockSpec((1, SSD_WIDTH), const),
            pl.BlockSpec((1, SSD_WIDTH), const),
            pl.BlockSpec((LANES, SSD_WIDTH), const),
            pl.BlockSpec((CHUNK, CHUNK), const),
        ],
        out_specs=pl.BlockSpec((tt, SSD_WIDTH), tile),
        out_shape=jax.ShapeDtypeStruct((b * s, SSD_WIDTH), BF16),
        scratch_shapes=[
            pltpu.VMEM((SSD_HEADS // 2, D_STATE, LANES), F32),
        ],
        compiler_params=pltpu.CompilerParams(
            dimension_semantics=("parallel", "arbitrary"), vmem_limit_bytes=VMEM_LIMIT),
        name="ssd_mixer",
    )(xc, zs, misc, dtb_rep, a_rep, dskip_row, nw_row, expand, triu)


def _layer_norm(y, g, b):
    mu = jnp.mean(y, axis=1, keepdims=True)
    yc = y - mu
    var = jnp.mean(yc * yc, axis=1, keepdims=True)
    return yc * lax.rsqrt(var + LN_EPS) * g + b


OUT_TM = 1024
OUT_SUB = 256


def _out_proj_kernel(att_ref, ssd_ref, x_ref, wa_ref, ws_ref, g_ref, b_ref, h_ref):
    for r0 in range(0, x_ref.shape[0], OUT_SUB):
        rows = slice(r0, r0 + OUT_SUB)
        mixed = jnp.dot(att_ref[rows, :], wa_ref[...], preferred_element_type=F32)
        mixed = mixed + jnp.dot(ssd_ref[rows, :], ws_ref[...], preferred_element_type=F32)
        h_ref[rows, :] = _layer_norm(ALPHA * x_ref[rows, :] + mixed, g_ref[...], b_ref[...])


def _out_proj(att2, ssd2, x2, w_att, w_ssd, g_row, b_row, tm):
    n = x2.shape[0]
    row = lambda i: (i, 0)
    const = lambda i: (0, 0)
    return pl.pallas_call(
        _out_proj_kernel,
        grid=(n // tm,),
        in_specs=[
            pl.BlockSpec((tm, ATT_WIDTH), row),
            pl.BlockSpec((tm, SSD_WIDTH), row),
            pl.BlockSpec((tm, D_MODEL), row),
            pl.BlockSpec((ATT_WIDTH, D_MODEL), const),
            pl.BlockSpec((SSD_WIDTH, D_MODEL), const),
            pl.BlockSpec((1, D_MODEL), const),
            pl.BlockSpec((1, D_MODEL), const),
        ],
        out_specs=pl.BlockSpec((tm, D_MODEL), row),
        out_shape=jax.ShapeDtypeStruct((n, D_MODEL), F32),
        compiler_params=pltpu.CompilerParams(
            dimension_semantics=("parallel",), vmem_limit_bytes=VMEM_LIMIT),
        name="out_proj_ln",
    )(att2, ssd2, x2, w_att, w_ssd, g_row, b_row)


ROUTE_E0 = N_GROUPS_MOE
MOE_TM = 512
MOE_CAP = 160


def _first_max(vals):
    best = vals[0]
    for v in vals[1:]:
        best = jnp.maximum(best, v)
    idx = jnp.full(best.shape, len(vals) - 1, jnp.int32)
    for i in range(len(vals) - 2, -1, -1):
        idx = jnp.where(vals[i] == best, i, idx)
    return best, idx


def _moe_kernel(h_ref, wrh_ref, wrl_ref, br_ref, wg_ref, wu_ref, wd_ref, g_ref, b_ref, upper_ref,
                o_ref, hb_ref, hp_ref, gp_ref, yp_ref):
    tm = h_ref.shape[0]
    h = h_ref[...]
    h_hi = h.astype(BF16)
    hb_ref[...] = h_hi
    h_lo = (h - h_hi.astype(F32)).astype(BF16)
    wrh = wrh_ref[...]
    both = jnp.dot(h_hi, jnp.concatenate([wrh, wrl_ref[...]], axis=1), preferred_element_type=F32)
    logits = (both[:, :LANES] + jnp.dot(h_lo, wrh, preferred_element_type=F32)
              + both[:, LANES:]) + br_ref[...]
    lt = logits.T
    row = lambda r: lt[r:r + 1, :]
    gl = [row(r) for r in range(N_GROUPS_MOE)]
    gmax, gidx = _first_max(gl)
    denom = jnp.exp(gl[0] - gmax)
    for v in gl[1:]:
        denom = denom + jnp.exp(v - gmax)
    gprob = 1.0 / denom
    el = []
    for k in range(EXPERTS_PER_GROUP):
        v = row(ROUTE_E0 + (N_GROUPS_MOE - 1) * EXPERTS_PER_GROUP + k)
        for gg in range(N_GROUPS_MOE - 2, -1, -1):
            v = jnp.where(gidx == gg, row(ROUTE_E0 + gg * EXPERTS_PER_GROUP + k), v)
        el.append(v)
    l1, i1 = _first_max(el)
    l2, i2 = _first_max([jnp.where(i1 == k, -jnp.inf, el[k]) for k in range(EXPERTS_PER_GROUP)])
    e2 = jnp.exp(l2 - l1)
    w1 = gprob / (1.0 + e2)
    w2 = gprob * e2 / (1.0 + e2)
    gate4 = [jnp.where(i1 == k, w1, jnp.where(i2 == k, w2, 0.0)) for k in range(EXPERTS_PER_GROUP)]

    member = [jnp.where(gidx == gg, 1.0, 0.0) for gg in range(N_GROUPS_MOE)]
    member_blk = jnp.concatenate(
        member + [jnp.zeros((2 * SUBLANES - N_GROUPS_MOE, tm), F32)], axis=0).astype(BF16)
    earlier = jnp.dot(member_blk, upper_ref[...], preferred_element_type=F32)
    counts = [jnp.sum(m, axis=1, keepdims=True) for m in member]
    most = counts[0]
    for c in counts[1:]:
        most = jnp.maximum(most, c)
    fits = jnp.max(most) <= float(MOE_CAP)
    slot = member[0] * earlier[0:1, :]
    for gg in range(1, N_GROUPS_MOE):
        slot = slot + member[gg] * (earlier[gg:gg + 1, :] + float(gg * MOE_CAP))
    tok = jnp.concatenate(
        gate4 + [slot, gidx.astype(F32), jnp.zeros((LANES - EXPERTS_PER_GROUP - 2, tm), F32)],
        axis=0).T
    slot_lane, grp_lane = EXPERTS_PER_GROUP, EXPERTS_PER_GROUP + 1
    rows_p = N_GROUPS_MOE * MOE_CAP

    @pl.when(fits)
    def _():
        hb = hb_ref[...]
        place = jnp.where(lax.broadcasted_iota(jnp.int32, (rows_p, tm), 0) == slot.astype(jnp.int32),
                          1.0, 0.0).astype(BF16)
        hp_ref[...] = jnp.dot(place, hb, preferred_element_type=F32).astype(BF16)
        t_hi = tok.astype(BF16)
        t_lo = (tok - t_hi.astype(F32)).astype(BF16)
        both = jnp.dot(place, jnp.concatenate([t_hi, t_lo], axis=1), preferred_element_type=F32)
        gp_ref[...] = both[:, :LANES] + both[:, LANES:]
        for gg in range(N_GROUPS_MOE):
            rows = slice(gg * MOE_CAP, (gg + 1) * MOE_CAP)
            xg = hp_ref[rows, :]
            acc = None
            for k in range(EXPERTS_PER_GROUP):
                e = gg * EXPERTS_PER_GROUP + k
                a = jnp.dot(xg, wg_ref[e], preferred_element_type=F32)
                u = jnp.dot(xg, wu_ref[e], preferred_element_type=F32)
                hid = (_silu(a) * u * gp_ref[rows, k:k + 1]).astype(BF16)
                part = jnp.dot(hid, wd_ref[e], preferred_element_type=F32)
                acc = part if acc is None else acc + part
            yp_ref[rows, :] = acc.astype(BF16)
        back = jnp.where(lax.broadcasted_iota(jnp.int32, (tm, rows_p), 1)
                         == tok[:, slot_lane:slot_lane + 1].astype(jnp.int32), 1.0, 0.0).astype(BF16)
        for r0 in range(0, tm, OUT_SUB):
            rr = slice(r0, r0 + OUT_SUB)
            y = jnp.dot(back[rr, :], yp_ref[...], preferred_element_type=F32)
            o_ref[rr, :] = _layer_norm(ALPHA * h_ref[rr, :] + y, g_ref[...], b_ref[...])

    @pl.when(jnp.logical_not(fits))
    def _():
        hb = hb_ref[...]
        acc = None
        for gg in range(N_GROUPS_MOE):
            in_grp = tok[:, grp_lane:grp_lane + 1] == float(gg)
            for k in range(EXPERTS_PER_GROUP):
                e = gg * EXPERTS_PER_GROUP + k
                a = jnp.dot(hb, wg_ref[e], preferred_element_type=F32)
                u = jnp.dot(hb, wu_ref[e], preferred_element_type=F32)
                gate = jnp.where(in_grp, tok[:, k:k + 1], 0.0)
                hid = (_silu(a) * u * gate).astype(BF16)
                part = jnp.dot(hid, wd_ref[e], preferred_element_type=F32)
                acc = part if acc is None else acc + part
        o_ref[...] = _layer_norm(ALPHA * h_ref[...] + acc, g_ref[...], b_ref[...])


def _moe(h2, wr_hi, wr_lo, br, wg, wu, wd, g_row, b_row):
    n = h2.shape[0]
    tm = min(MOE_TM, n)
    rows_p = N_GROUPS_MOE * MOE_CAP
    row = lambda i: (i, 0)
    const = lambda i: (0, 0)
    whole = lambda i: (0, 0, 0)
    once = pl.Buffered(1)
    upper = (jnp.arange(tm)[:, None] < jnp.arange(tm)[None, :]).astype(BF16)
    return pl.pallas_call(
        _moe_kernel,
        grid=(n // tm,),
        in_specs=[
            pl.BlockSpec((tm, D_MODEL), row),
            pl.BlockSpec((D_MODEL, LANES), const),
            pl.BlockSpec((D_MODEL, LANES), const),
            pl.BlockSpec((1, LANES), const),
            pl.BlockSpec((N_EXPERTS, D_MODEL, EXPERT_FF), whole, pipeline_mode=once),
            pl.BlockSpec((N_EXPERTS, D_MODEL, EXPERT_FF), whole, pipeline_mode=once),
            pl.BlockSpec((N_EXPERTS, EXPERT_FF, D_MODEL), whole, pipeline_mode=once),
            pl.BlockSpec((1, D_MODEL), const),
            pl.BlockSpec((1, D_MODEL), const),
            pl.BlockSpec((tm, tm), const, pipeline_mode=once),
        ],
        out_specs=pl.BlockSpec((tm, D_MODEL), row),
        out_shape=jax.ShapeDtypeStruct((n, D_MODEL), F32),
        scratch_shapes=[
            pltpu.VMEM((tm, D_MODEL), BF16),
            pltpu.VMEM((rows_p, D_MODEL), BF16),
            pltpu.VMEM((rows_p, LANES), F32),
            pltpu.VMEM((rows_p, D_MODEL), BF16),
        ],
        compiler_params=pltpu.CompilerParams(
            dimension_semantics=("parallel",), vmem_limit_bytes=VMEM_LIMIT),
        name="hier_moe_ln",
    )(h2, wr_hi, wr_lo, br, wg, wu, wd, g_row, b_row, upper)


def _rope_tables(seq):
    inv = ROPE_THETA ** (-jnp.arange(0, HEAD_DIM, 2, dtype=F32) / HEAD_DIM)
    ang = jnp.arange(seq, dtype=F32)[:, None] * inv[None, :]
    cos, sin = jnp.cos(ang), jnp.sin(ang)
    zero = jnp.zeros_like(sin)
    cos_t = jnp.tile(cos, (1, LANES // (HEAD_DIM // 2)))
    s1_t = jnp.tile(jnp.concatenate([-sin, zero], 1), (1, LANES // HEAD_DIM))
    s2_t = jnp.tile(jnp.concatenate([zero, sin], 1), (1, LANES // HEAD_DIM))
    return cos_t, s1_t, s2_t


def _permute_w_in(w):
    sizes = (ATT_WIDTH, HEAD_DIM, HEAD_DIM, IDX_HEADS * IDX_DIM, IDX_DIM, IDX_HEADS,
             SSD_WIDTH, CONV_CH, SSD_HEADS)
    pts = np.cumsum((0,) + sizes)
    q, k, v, iq, ik, iw, z, xbc, dt = [w[:, pts[i]:pts[i + 1]] for i in range(len(sizes))]
    d = w.shape[0]
    pad = lambda n: jnp.zeros((d, n), w.dtype)
    misc = jnp.concatenate([v, iw, pad(MISC_DT - MISC_IW - IDX_HEADS), dt,
                            pad(LANES - MISC_DT - SSD_HEADS)], 1)
    return jnp.concatenate([q, iq, k, ik, misc, z, xbc], 1).astype(BF16)


def _head_rep(vals):
    return jnp.broadcast_to(vals.astype(F32)[:, None], (vals.shape[0], CHUNK))


def kernel(x, w_in, conv_w, conv_b, dt_bias, a_log, d_skip, ssd_norm_w, w_out, ln1_g, ln1_b,
           w_route_group, b_route_group, w_route_expert, b_route_expert, w_gate, w_up,
           w_down, ln2_g, ln2_b):
    bsz, seq, d = x.shape
    n = bsz * seq
    topk = min(TOPK_MAX, seq // 4)
    tm = 512
    assert d == D_MODEL and TQ == KC and seq % TQ == 0 and seq % tm == 0 and topk <= KC
    assert seq // BF16_ROWS <= 256
    cos_t, s1_t, s2_t = _rope_tables(seq)
    mscale = jnp.ones((1, LANES), F32).at[0, MISC_IW:MISC_IW + IDX_HEADS].set(INDEXER_SCALE)
    for l in range(DEPTH):
        x2 = x.reshape(n, d)
        q, iq, k, ik, vt, misc, zs, xc = _in_proj(
            x2, _permute_w_in(w_in[l]), cos_t, s1_t, s2_t, mscale,
            conv_w[l].astype(F32), conv_b[l].astype(F32)[None, :], seq, tm)
        att = _dsa(q, iq, misc, k, ik, vt, bsz, seq, topk)
        ssd = _ssd(
            xc, zs, misc,
            _head_rep(dt_bias[l]), _head_rep(-jnp.exp(a_log[l].astype(F32))),
            jnp.repeat(d_skip[l].astype(F32), SSD_HEAD_DIM)[None, :], ssd_norm_w[l][None, :],
            bsz, seq)
        w_o = w_out[l].astype(BF16)
        h2 = _out_proj(att, ssd, x2, w_o[:ATT_WIDTH], w_o[ATT_WIDTH:],
                       ln1_g[l][None, :], ln1_b[l][None, :], min(OUT_TM, n))
        pad = jnp.zeros((d, LANES - ROUTE_E0 - N_EXPERTS), F32)
        wr = jnp.concatenate([w_route_group[l].astype(F32), w_route_expert[l].astype(F32), pad], 1)
        wr_hi = wr.astype(BF16)
        wr_lo = (wr - wr_hi.astype(F32)).astype(BF16)
        br = jnp.concatenate([b_route_group[l].astype(F32), b_route_expert[l].astype(F32),
                              pad[0]])[None, :]
        x = _moe(h2, wr_hi, wr_lo, br, w_gate[l].astype(BF16), w_up[l].astype(BF16),
                 w_down[l].astype(BF16), ln2_g[l][None, :], ln2_b[l][None, :]).reshape(bsz, seq, d)
    return x
```

```python
import functools

import jax
import jax.numpy as jnp
import numpy as np
from jax import lax
from jax.experimental import pallas as pl
from jax.experimental.pallas import tpu as pltpu

F32 = jnp.float32
BF16 = jnp.bfloat16

D_MODEL = 1024
HEAD_DIM = 64
ATT_WIDTH = 512
ATT_HEADS = 8
IDX_HEADS = 4
IDX_DIM = 64
TOPK_MAX = 256
ROPE_THETA = 10000.0
INDEXER_SCALE = (IDX_HEADS ** -0.5) * (IDX_DIM ** -0.5)
SSD_WIDTH = 512
SSD_HEADS = 8
SSD_HEAD_DIM = 64
SSD_GROUPS = 2
D_STATE = 64
CONV_WIDTH = 4
CONV_CH = SSD_WIDTH + 2 * SSD_GROUPS * D_STATE
CHUNK = 128
N_GROUPS_MOE = 4
EXPERTS_PER_GROUP = 4
N_EXPERTS = 16
EXPERT_FF = 256
DEPTH = 1
ALPHA = (2 * DEPTH) ** 0.25
LN_EPS = 1e-5

LANES = 128
SUBLANES = 8
BF16_ROWS = 16
INF_KEY = 0x7F800000
MIN_NORMAL_KEY = 0x00800000
KEY_STEP16 = 1 << 16
VMEM_LIMIT = 56 * 1024 * 1024

C_Q = 0
C_IQ = 512
C_K = 768
C_IK = 832
C_MISC = 896
MISC_IW = 64
MISC_DT = 72
C_Z = 1024
C_XBC = 1536
IN_COLS = 2304

TQ = 256
KC = 256

Q_SCALE = HEAD_DIM ** -0.5 * float(np.log2(np.e))

NT_DIMS = (((1,), (1,)), ((), ()))


def _nt_dot(a, b):
    return lax.dot_general(a, b, NT_DIMS, preferred_element_type=F32)


def _fold_rows(x, op, rows=SUBLANES):
    slabs = [x[r * rows:(r + 1) * rows, :] for r in range(x.shape[0] // rows)]
    while len(slabs) > 1:
        nxt = [op(slabs[i], slabs[i + 1]) for i in range(0, len(slabs) - 1, 2)]
        if len(slabs) % 2:
            nxt.append(slabs[-1])
        slabs = nxt
    return slabs[0]


CONV_PAD = 8
PROJ_COLS = 256


def _silu(x):
    return x * (1.0 / (1.0 + jnp.exp(-x)))


def _in_proj_kernel(x_ref, w_ref, cos_ref, s1_ref, s2_ref, mscale_ref, cw_ref, cb_ref,
                    q_ref, iq_ref, k_ref, ik_ref, vt_ref, misc_ref, zs_ref, xc_ref, hist_ref,
                    *, nblk_seq):
    @pl.when(pl.program_id(0) == 0)
    def _():
        hist_ref[...] = jnp.zeros_like(hist_ref)

    xb = x_ref[...].astype(BF16)
    cos = cos_ref[...]
    s1 = s1_ref[...]
    s2 = s2_ref[...]

    def mm(c0, width):
        return jnp.dot(xb, w_ref[:, c0:c0 + width], preferred_element_type=F32)

    def rope(y):
        fwd = pltpu.roll(y, LANES - HEAD_DIM // 2, 1)
        bwd = pltpu.roll(y, HEAD_DIM // 2, 1)
        return y * cos + fwd * s1 + bwd * s2

    def rope_wide(y, scale):
        parts = []
        for c in range(y.shape[1] // LANES):
            r = rope(y[:, c * LANES:(c + 1) * LANES])
            parts.append(r * scale if scale != 1.0 else r)
        return parts

    tm = x_ref.shape[0]
    seq_start = pl.program_id(0) % nblk_seq == 0
    for c0 in range(0, CONV_CH, PROJ_COLS):
        cols = slice(c0, c0 + PROJ_COLS)
        xbc = mm(C_XBC + c0, PROJ_COLS)
        hist = jnp.where(seq_start, 0.0, hist_ref[:, cols])
        xp = jnp.concatenate([hist, xbc], axis=0)
        acc = cw_ref[0:1, cols] * xp
        for j in range(1, CONV_WIDTH):
            acc = pltpu.roll(acc, 1, 0) + cw_ref[j:j + 1, cols] * xp
        xc_ref[:, cols] = _silu(acc[CONV_PAD:, :] + cb_ref[:, cols])
        hist_ref[:, cols] = xbc[tm - CONV_PAD:, :]
    for c0 in range(0, SSD_WIDTH, PROJ_COLS):
        zs_ref[:, c0:c0 + PROJ_COLS] = _silu(mm(C_Z + c0, PROJ_COLS))

    for c, r in enumerate(rope_wide(mm(C_Q, ATT_WIDTH), Q_SCALE)):
        q_ref[:, c * LANES:(c + 1) * LANES] = r.astype(BF16)
    for c, r in enumerate(rope_wide(mm(C_IQ, IDX_HEADS * IDX_DIM), 1.0)):
        iq_ref[:, c * LANES:(c + 1) * LANES] = r.astype(BF16)
    kk = rope(mm(C_K, LANES))
    k_ref[...] = kk[:, :HEAD_DIM].astype(BF16)
    ik_ref[...] = kk[:, HEAD_DIM:].astype(BF16)
    misc = mm(C_MISC, LANES) * mscale_ref[...]
    misc_ref[...] = misc
    lane = lax.broadcasted_iota(jnp.int32, misc.shape, 1)
    vext = jnp.where(lane < HEAD_DIM, misc, jnp.where(lane == HEAD_DIM, 1.0, 0.0))
    for c in range(vt_ref.shape[0]):
        vt_ref[c] = vext[c * KC:(c + 1) * KC, :].T.astype(BF16)


def _in_proj(x2, w_perm, cos_t, s1_t, s2_t, mscale, conv_w, conv_b, seq, tm):
    n = x2.shape[0]
    nblk_seq = seq // tm
    row = lambda i: (i, 0)
    tab = lambda i: (i % nblk_seq, 0)
    const = lambda i: (0, 0)
    outs = [
        ((n, ATT_WIDTH), BF16, pl.BlockSpec((tm, ATT_WIDTH), row)),
        ((n, IDX_HEADS * IDX_DIM), BF16, pl.BlockSpec((tm, IDX_HEADS * IDX_DIM), row)),
        ((n, HEAD_DIM), BF16, pl.BlockSpec((tm, HEAD_DIM), row)),
        ((n, IDX_DIM), BF16, pl.BlockSpec((tm, IDX_DIM), row)),
        ((n // KC, LANES, KC), BF16, pl.BlockSpec((tm // KC, LANES, KC), lambda i: (i, 0, 0))),
        ((n, LANES), F32, pl.BlockSpec((tm, LANES), row)),
        ((n, SSD_WIDTH), F32, pl.BlockSpec((tm, SSD_WIDTH), row)),
        ((n, CONV_CH), F32, pl.BlockSpec((tm, CONV_CH), row)),
    ]
    return pl.pallas_call(
        functools.partial(_in_proj_kernel, nblk_seq=nblk_seq),
        grid=(n // tm,),
        in_specs=[
            pl.BlockSpec((tm, D_MODEL), row),
            pl.BlockSpec((D_MODEL, IN_COLS), const),
            pl.BlockSpec((tm, LANES), tab),
            pl.BlockSpec((tm, LANES), tab),
            pl.BlockSpec((tm, LANES), tab),
            pl.BlockSpec((1, LANES), const),
            pl.BlockSpec((CONV_WIDTH, CONV_CH), const),
            pl.BlockSpec((1, CONV_CH), const),
        ],
        out_specs=[spec for _, _, spec in outs],
        out_shape=[jax.ShapeDtypeStruct(shape, dt) for shape, dt, _ in outs],
        scratch_shapes=[pltpu.VMEM((CONV_PAD, CONV_CH), F32)],
        compiler_params=pltpu.CompilerParams(
            dimension_semantics=("arbitrary",), vmem_limit_bytes=VMEM_LIMIT),
        name="in_proj",
    )(x2, w_perm, cos_t, s1_t, s2_t, mscale, conv_w, conv_b)


def _dsa_kernel(q_ref, iq_ref, misc_ref, k_ref, ik_ref, vt_ref, o_ref,
                sc_ref, sc16_ref, lg_ref, acc_ref, *, topk):
    qi = pl.program_id(1)
    nj = qi + 1
    neg_inf = -jnp.inf
    kf = float(topk)
    key_i = lax.broadcasted_iota(jnp.int32, (KC, TQ), 0)
    qry_i = lax.broadcasted_iota(jnp.int32, (KC, TQ), 1)
    causal = key_i <= qry_i

    def key_rows(j):
        return pl.ds(pl.multiple_of(j * KC, KC), KC)

    iw_t = misc_ref[...].T[MISC_IW:MISC_IW + SUBLANES, :]

    def chunk_scores(j):
        ikj = ik_ref[key_rows(j), :]
        sc = None
        for h in range(IDX_HEADS):
            d = _nt_dot(ikj, iq_ref[:, h * IDX_DIM:(h + 1) * IDX_DIM])
            term = iw_t[h:h + 1, :] * jnp.maximum(d, 0.0)
            sc = term if sc is None else sc + term
        sc_ref[j] = sc
        sc16_ref[j] = sc.astype(BF16)

    def scores_body(p, carry):
        chunk_scores(2 * p)
        chunk_scores(jnp.minimum(2 * p + 1, qi))
        return carry

    lax.fori_loop(0, (nj + 1) // 2, scores_body, 0)
    sc_diag = jnp.where(causal, sc_ref[qi], neg_inf)
    sc_ref[qi] = sc_diag
    sc16_ref[qi] = sc_diag.astype(BF16)

    def key_to_float(key):
        mag = jnp.where(key < 0, -key, key)
        sub = jnp.logical_and(mag > 0, mag < MIN_NORMAL_KEY)
        mag = jnp.where(sub, jnp.where(key < 0, 0, MIN_NORMAL_KEY), mag)
        f = lax.bitcast_convert_type(mag, F32)
        return jnp.where(key < 0, -f, f)

    one16 = jnp.ones((), BF16)
    zero16 = jnp.zeros((), BF16)

    nj_search = jnp.where(qi == 0, 0, nj) if topk >= TQ else nj

    def count16_ge(cand16):
        def body(j, acc):
            ind = jnp.where(sc16_ref[j] >= cand16, one16, zero16)
            return acc + _fold_rows(ind, jnp.add, BF16_ROWS)

        acc = lax.fori_loop(0, nj_search, body, jnp.zeros((BF16_ROWS, TQ), BF16))
        return jnp.sum(acc.astype(F32), axis=0, keepdims=True)

    def count_ge(cand, chunks=nj_search):
        def body(j, acc):
            ind = jnp.where(sc_ref[j] >= cand, 1.0, 0.0)
            return acc + _fold_rows(ind, jnp.add)

        acc = lax.fori_loop(0, chunks, body, jnp.zeros((SUBLANES, TQ), F32))
        return jnp.sum(acc, axis=0, keepdims=True)

    def coarse_body(b, m):
        trial = m + lax.shift_left(jnp.int32(1), jnp.int32(15) - b)
        cand16 = key_to_float(trial * KEY_STEP16).astype(BF16)
        return jnp.where(count16_ge(cand16) >= kf, trial, m)

    m16 = lax.fori_loop(0, 16, coarse_body, jnp.full((1, TQ), -(INF_KEY // KEY_STEP16), jnp.int32))
    key_base = jnp.maximum(m16 * KEY_STEP16 - (KEY_STEP16 // 2 + 1), -INF_KEY)

    def fine_body(b, off):
        trial = off + lax.shift_left(jnp.int32(1), jnp.int32(16) - b)
        ok = count_ge(key_to_float(key_base + trial)) >= kf
        return jnp.where(ok, trial, off)

    off = lax.fori_loop(0, 17, fine_body, jnp.zeros((1, TQ), jnp.int32))
    lo = key_to_float(key_base + off)
    hi = key_to_float(key_base + off + 1)
    need = kf - count_ge(hi, nj)

    lower = (lax.broadcasted_iota(jnp.int32, (KC, KC), 0)
             > lax.broadcasted_iota(jnp.int32, (KC, KC), 1)).astype(BF16)

    acc_ref[...] = jnp.zeros_like(acc_ref)
    no_max = tuple(jnp.minimum(need, neg_inf) for _ in range(ATT_HEADS))

    def logits_stage(j, taken):
        s = sc_ref[j]
        eqf = jnp.where(s >= hi, 0.0, jnp.where(s >= lo, 1.0, 0.0))
        before = jnp.dot(lower, eqf.astype(BF16), preferred_element_type=F32) + taken
        tie_bias = jnp.where(before < need, 0.0, neg_inf)
        bias = jnp.where(s >= hi, 0.0, jnp.where(s >= lo, tie_bias, neg_inf))
        bias = jnp.where(j < qi, bias, jnp.where(causal, bias, neg_inf))
        taken = taken + jnp.sum(_fold_rows(eqf, jnp.add), axis=0, keepdims=True)
        kj = k_ref[key_rows(j), :]
        slot = j % 2
        cms = []
        for h in range(ATT_HEADS):
            lg = _nt_dot(kj, q_ref[:, h * HEAD_DIM:(h + 1) * HEAD_DIM]) + bias
            lg_ref[slot, h] = lg
            cms.append(jnp.max(_fold_rows(lg, jnp.maximum), axis=0, keepdims=True))
        return taken, tuple(cms)

    def pv_stage(jp, cms, ms):
        vtj = vt_ref[jp]
        slot = jp % 2
        new_ms = []
        for h in range(ATT_HEADS):
            m_new = jnp.maximum(ms[h], cms[h])
            shift = jnp.where(m_new == neg_inf, 0.0, m_new)
            p = jnp.exp2(lg_ref[slot, h] - shift).astype(BF16)
            alpha = jnp.exp2(ms[h] - shift)
            acc_ref[h] = alpha * acc_ref[h] + jnp.dot(vtj, p, preferred_element_type=F32)
            new_ms.append(m_new)
        return tuple(new_ms)

    def att_body(j, carry):
        taken, cms_prev, ms = carry
        ms = pv_stage(j - 1, cms_prev, ms)
        taken, cms = logits_stage(j, taken)
        return taken, cms, ms

    taken0, cms0 = logits_stage(0, jnp.zeros((1, TQ), F32))
    _, cms_last, ms = lax.fori_loop(1, nj, att_body, (taken0, cms0, no_max))
    pv_stage(qi, cms_last, ms)
    for pair in range(ATT_HEADS // 2):
        halves = []
        for h in (2 * pair, 2 * pair + 1):
            a = acc_ref[h]
            halves.append(a[:HEAD_DIM, :] * (1.0 / a[HEAD_DIM:HEAD_DIM + 1, :]))
        blk = jnp.concatenate(halves, axis=0)
        o_ref[:, pair * LANES:(pair + 1) * LANES] = blk.T.astype(BF16)


def _dsa(q, iq, misc, k, ik, vt, b, s, topk):
    nq = s // TQ
    tile = lambda bi, qi: (bi * nq + qi, 0)
    full = lambda bi, qi: (bi, 0)
    return pl.pallas_call(
        functools.partial(_dsa_kernel, topk=topk),
        grid=(b, nq),
        in_specs=[
            pl.BlockSpec((TQ, ATT_WIDTH), tile),
            pl.BlockSpec((TQ, IDX_HEADS * IDX_DIM), tile),
            pl.BlockSpec((TQ, LANES), tile),
            pl.BlockSpec((s, HEAD_DIM), full),
            pl.BlockSpec((s, IDX_DIM), full),
            pl.BlockSpec((s // KC, LANES, KC), lambda bi, qi: (bi, 0, 0)),
        ],
        out_specs=pl.BlockSpec((TQ, ATT_WIDTH), tile),
        out_shape=jax.ShapeDtypeStruct((b * s, ATT_WIDTH), BF16),
        scratch_shapes=[
            pltpu.VMEM((nq, KC, TQ), F32),
            pltpu.VMEM((nq, KC, TQ), BF16),
            pltpu.VMEM((2, ATT_HEADS, KC, TQ), F32),
            pltpu.VMEM((ATT_HEADS, LANES, TQ), F32),
        ],
        compiler_params=pltpu.CompilerParams(
            dimension_semantics=("parallel", "arbitrary"), vmem_limit_bytes=VMEM_LIMIT),
        name="dsa_attention",
    )(q, iq, misc, k, ik, vt)


SSD_TT = 2048


def _ssd_kernel(xc_ref, zs_ref, misc_ref, dtb_ref, arep_ref, dskip_ref, nw_ref,
                expand_ref, triu_ref, o_ref, state_ref):
    t = pl.program_id(1)
    tt = xc_ref.shape[0]

    @pl.when(t == 0)
    def _():
        state_ref[...] = jnp.zeros_like(state_ref)

    tri = (lax.broadcasted_iota(jnp.int32, (CHUNK, CHUNK), 0)
           >= lax.broadcasted_iota(jnp.int32, (CHUNK, CHUNK), 1))
    left_head = lax.broadcasted_iota(jnp.int32, (CHUNK, LANES), 1) < SSD_HEAD_DIM
    left_head_n = lax.broadcasted_iota(jnp.int32, (D_STATE, LANES), 1) < SSD_HEAD_DIM
    gn = SSD_GROUPS * D_STATE
    pairs_per_group = SSD_HEADS // SSD_GROUPS // 2
    expand = expand_ref[...]
    triu = triu_ref[...]
    zpad = jnp.zeros((SUBLANES, CHUNK), F32)

    for c in range(tt // CHUNK):
        rows = slice(c * CHUNK, (c + 1) * CHUNK)
        raw = misc_ref[rows, :].T[MISC_DT:MISC_DT + SSD_HEADS, :] + dtb_ref[...]
        dt_t = jnp.maximum(raw, 0.0) + jnp.log1p(jnp.exp(-jnp.abs(raw)))
        adt = dt_t * arep_ref[...]
        hi = adt.astype(BF16).astype(F32)
        r1 = adt - hi
        mid = r1.astype(BF16).astype(F32)
        pieces = jnp.concatenate([hi, mid, r1 - mid, zpad], axis=0).astype(BF16)
        cs = jnp.dot(pieces, triu, preferred_element_type=F32)
        acum_t = (cs[0:SUBLANES] + cs[SUBLANES:2 * SUBLANES]) + cs[2 * SUBLANES:3 * SUBLANES]
        a_last = acum_t[:, CHUNK - 1:CHUNK]
        ddt_t = jnp.exp(a_last - acum_t) * dt_t
        acum = jnp.concatenate(
            [acum_t, jnp.zeros((LANES - SSD_HEADS, CHUNK), F32)], axis=0).T
        xs = xc_ref[rows, 0:SSD_WIDTH]
        bm = xc_ref[rows, SSD_WIDTH:SSD_WIDTH + gn]
        cm = xc_ref[rows, SSD_WIDTH + gn:SSD_WIDTH + 2 * gn]
        bm_t = bm.T
        xs16 = xs.astype(BF16)
        bm16 = bm.astype(BF16)
        cm16 = cm.astype(BF16)
        ea = jnp.exp(acum)
        ea_hi = ea.astype(BF16)
        ea_lo = (ea - ea_hi.astype(F32)).astype(BF16)
        ea_x = (jnp.dot(ea_hi, expand, preferred_element_type=F32)
                + jnp.dot(ea_lo, expand, preferred_element_type=F32))
        y_pairs = [None] * (SSD_HEADS // 2)
        for g in range(SSD_GROUPS):
            cg = cm16[:, g * D_STATE:(g + 1) * D_STATE]
            bg = bm16[:, g * D_STATE:(g + 1) * D_STATE]
            bg_t = bm_t[g * D_STATE:(g + 1) * D_STATE, :]
            gmat = _nt_dot(cg, bg)
            for pp in range(pairs_per_group):
                pair = g * pairs_per_group + pp
                lanes = slice(pair * LANES, (pair + 1) * LANES)
                xp = xs16[:, lanes]
                y_halves, s_halves = [], []
                for h in (2 * pair, 2 * pair + 1):
                    col = acum[:, h:h + 1]
                    rowv = acum_t[h:h + 1, :]
                    lmat = jnp.exp(jnp.where(tri, col - rowv, -jnp.inf))
                    mmat = (gmat * lmat * dt_t[h:h + 1, :]).astype(BF16)
                    y_halves.append(jnp.dot(mmat, xp, preferred_element_type=F32))
                    bs = (bg_t * ddt_t[h:h + 1, :]).astype(BF16)
                    s_halves.append(jnp.dot(bs, xp, preferred_element_type=F32))
                y_diag = jnp.where(left_head, y_halves[0], y_halves[1])
                new = jnp.where(left_head_n, s_halves[0], s_halves[1])
                prev = state_ref[pair]
                y_off = jnp.dot(cg, prev.astype(BF16), preferred_element_type=F32) * ea_x[:, lanes]
                state_ref[pair] = prev * ea_x[CHUNK - 1:CHUNK, lanes] + new
                y_pairs[pair] = y_diag + y_off
        y = jnp.concatenate(y_pairs, axis=1) + dskip_ref[...] * xs
        y = y * zs_ref[rows, :]
        ms = jnp.mean(y * y, axis=1, keepdims=True)
        o_ref[rows, :] = (y * lax.rsqrt(ms + LN_EPS) * nw_ref[...]).astype(BF16)


def _ssd(xc, zs, misc, dtb_rep, a_rep, dskip_row, nw_row, b, s):
    tt = min(SSD_TT, s)
    nt = s // tt
    tile = lambda bi, ti: (bi * nt + ti, 0)
    const = lambda bi, ti: (0, 0)
    expand = (jnp.arange(LANES)[:, None] == jnp.arange(SSD_WIDTH)[None, :] // SSD_HEAD_DIM).astype(BF16)
    triu = (jnp.arange(CHUNK)[:, None] <= jnp.arange(CHUNK)[None, :]).astype(BF16)
    return pl.pallas_call(
        _ssd_kernel,
        grid=(b, s // tt),
        in_specs=[
            pl.BlockSpec((tt, CONV_CH), tile),
            pl.BlockSpec((tt, SSD_WIDTH), tile),
            pl.BlockSpec((tt, LANES), tile),
            pl.BlockSpec((SSD_HEADS, CHUNK), const),
            pl.BlockSpec((SSD_HEADS, CHUNK), const),
            pl.BlockSpec((1, SSD_WIDTH), const),
            pl.BlockSpec((1, SSD_WIDTH), const),
            pl.BlockSpec((LANES, SSD_WIDTH), const),
            pl.BlockSpec((CHUNK, CHUNK), const),
        ],
        out_specs=pl.BlockSpec((tt, SSD_WIDTH), tile),
        out_shape=jax.ShapeDtypeStruct((b * s, SSD_WIDTH), BF16),
        scratch_shapes=[
            pltpu.VMEM((SSD_HEADS // 2, D_STATE, LANES), F32),
        ],
        compiler_params=pltpu.CompilerParams(
            dimension_semantics=("parallel", "arbitrary"), vmem_limit_bytes=VMEM_LIMIT),
        name="ssd_mixer",
    )(xc, zs, misc, dtb_rep, a_rep, dskip_row, nw_row, expand, triu)


def _layer_norm(y, g, b):
    mu = jnp.mean(y, axis=1, keepdims=True)
    yc = y - mu
    var = jnp.mean(yc * yc, axis=1, keepdims=True)
    return yc * lax.rsqrt(var + LN_EPS) * g + b


OUT_TM = 1024
OUT_SUB = 256


def _out_proj_kernel(att_ref, ssd_ref, x_ref, wa_ref, ws_ref, g_ref, b_ref, h_ref):
    for r0 in range(0, x_ref.shape[0], OUT_SUB):
        rows = slice(r0, r0 + OUT_SUB)
        mixed = jnp.dot(att_ref[rows, :], wa_ref[...], preferred_element_type=F32)
        mixed = mixed + jnp.dot(ssd_ref[rows, :], ws_ref[...], preferred_element_type=F32)
        h_ref[rows, :] = _layer_norm(ALPHA * x_ref[rows, :] + mixed, g_ref[...], b_ref[...])


def _out_proj(att2, ssd2, x2, w_att, w_ssd, g_row, b_row, tm):
    n = x2.shape[0]
    row = lambda i: (i, 0)
    const = lambda i: (0, 0)
    return pl.pallas_call(
        _out_proj_kernel,
        grid=(n // tm,),
        in_specs=[
            pl.BlockSpec((tm, ATT_WIDTH), row),
            pl.BlockSpec((tm, SSD_WIDTH), row),
            pl.BlockSpec((tm, D_MODEL), row),
            pl.BlockSpec((ATT_WIDTH, D_MODEL), const),
            pl.BlockSpec((SSD_WIDTH, D_MODEL), const),
            pl.BlockSpec((1, D_MODEL), const),
            pl.BlockSpec((1, D_MODEL), const),
        ],
        out_specs=pl.BlockSpec((tm, D_MODEL), row),
        out_shape=jax.ShapeDtypeStruct((n, D_MODEL), F32),
        compiler_params=pltpu.CompilerParams(
            dimension_semantics=("parallel",), vmem_limit_bytes=VMEM_LIMIT),
        name="out_proj_ln",
    )(att2, ssd2, x2, w_att, w_ssd, g_row, b_row)


ROUTE_E0 = N_GROUPS_MOE
MOE_TM = 512
MOE_CAP = 160


def _first_max(vals):
    best = vals[0]
    for v in vals[1:]:
        best = jnp.maximum(best, v)
    idx = jnp.full(best.shape, len(vals) - 1, jnp.int32)
    for i in range(len(vals) - 2, -1, -1):
        idx = jnp.where(vals[i] == best, i, idx)
    return best, idx


def _moe_kernel(h_ref, wrh_ref, wrl_ref, br_ref, wg_ref, wu_ref, wd_ref, g_ref, b_ref, upper_ref,
                o_ref, hb_ref, hp_ref, gp_ref, yp_ref):
    tm = h_ref.shape[0]
    h = h_ref[...]
    h_hi = h.astype(BF16)
    hb_ref[...] = h_hi
    h_lo = (h - h_hi.astype(F32)).astype(BF16)
    wrh = wrh_ref[...]
    both = jnp.dot(h_hi, jnp.concatenate([wrh, wrl_ref[...]], axis=1), preferred_element_type=F32)
    logits = (both[:, :LANES] + jnp.dot(h_lo, wrh, preferred_element_type=F32)
              + both[:, LANES:]) + br_ref[...]
    lt = logits.T
    row = lambda r: lt[r:r + 1, :]
    gl = [row(r) for r in range(N_GROUPS_MOE)]
    gmax, gidx = _first_max(gl)
    denom = jnp.exp(gl[0] - gmax)
    for v in gl[1:]:
        denom = denom + jnp.exp(v - gmax)
    gprob = 1.0 / denom
    el = []
    for k in range(EXPERTS_PER_GROUP):
        v = row(ROUTE_E0 + (N_GROUPS_MOE - 1) * EXPERTS_PER_GROUP + k)
        for gg in range(N_GROUPS_MOE - 2, -1, -1):
            v = jnp.where(gidx == gg, row(ROUTE_E0 + gg * EXPERTS_PER_GROUP + k), v)
        el.append(v)
    l1, i1 = _first_max(el)
    l2, i2 = _first_max([jnp.where(i1 == k, -jnp.inf, el[k]) for k in range(EXPERTS_PER_GROUP)])
    e2 = jnp.exp(l2 - l1)
    w1 = gprob / (1.0 + e2)
    w2 = gprob * e2 / (1.0 + e2)
    gate4 = [jnp.where(i1 == k, w1, jnp.where(i2 == k, w2, 0.0)) for k in range(EXPERTS_PER_GROUP)]

    member = [jnp.where(gidx == gg, 1.0, 0.0) for gg in range(N_GROUPS_MOE)]
    member_blk = jnp.concatenate(
        member + [jnp.zeros((2 * SUBLANES - N_GROUPS_MOE, tm), F32)], axis=0).astype(BF16)
    earlier = jnp.dot(member_blk, upper_ref[...], preferred_element_type=F32)
    counts = [jnp.sum(m, axis=1, keepdims=True) for m in member]
    most = counts[0]
    for c in counts[1:]:
        most = jnp.maximum(most, c)
    fits = jnp.max(most) <= float(MOE_CAP)
    slot = member[0] * earlier[0:1, :]
    for gg in range(1, N_GROUPS_MOE):
        slot = slot + member[gg] * (earlier[gg:gg + 1, :] + float(gg * MOE_CAP))
    tok = jnp.concatenate(
        gate4 + [slot, gidx.astype(F32), jnp.zeros((LANES - EXPERTS_PER_GROUP - 2, tm), F32)],
        axis=0).T
    slot_lane, grp_lane = EXPERTS_PER_GROUP, EXPERTS_PER_GROUP + 1
    rows_p = N_GROUPS_MOE * MOE_CAP

    @pl.when(fits)
    def _():
        hb = hb_ref[...]
        place = jnp.where(lax.broadcasted_iota(jnp.int32, (rows_p, tm), 0) == slot.astype(jnp.int32),
                          1.0, 0.0).astype(BF16)
        hp_ref[...] = jnp.dot(place, hb, preferred_element_type=F32).astype(BF16)
        t_hi = tok.astype(BF16)
        t_lo = (tok - t_hi.astype(F32)).astype(BF16)
        both = jnp.dot(place, jnp.concatenate([t_hi, t_lo], axis=1), preferred_element_type=F32)
        gp_ref[...] = both[:, :LANES] + both[:, LANES:]
        for gg in range(N_GROUPS_MOE):
            rows = slice(gg * MOE_CAP, (gg + 1) * MOE_CAP)
            xg = hp_ref[rows, :]
            acc = None
            for k in range(EXPERTS_PER_GROUP):
                e = gg * EXPERTS_PER_GROUP + k
                a = jnp.dot(xg, wg_ref[e], preferred_element_type=F32)
                u = jnp.dot(xg, wu_ref[e], preferred_element_type=F32)
                hid = (_silu(a) * u * gp_ref[rows, k:k + 1]).astype(BF16)
                part = jnp.dot(hid, wd_ref[e], preferred_element_type=F32)
                acc = part if acc is None else acc + part
            yp_ref[rows, :] = acc.astype(BF16)
        back = jnp.where(lax.broadcasted_iota(jnp.int32, (tm, rows_p), 1)
                         == tok[:, slot_lane:slot_lane + 1].astype(jnp.int32), 1.0, 0.0).astype(BF16)
        for r0 in range(0, tm, OUT_SUB):
            rr = slice(r0, r0 + OUT_SUB)
            y = jnp.dot(back[rr, :], yp_ref[...], preferred_element_type=F32)
            o_ref[rr, :] = _layer_norm(ALPHA * h_ref[rr, :] + y, g_ref[...], b_ref[...])

    @pl.when(jnp.logical_not(fits))
    def _():
        hb = hb_ref[...]
        acc = None
        for gg in range(N_GROUPS_MOE):
            in_grp = tok[:, grp_lane:grp_lane + 1] == float(gg)
            for k in range(EXPERTS_PER_GROUP):
                e = gg * EXPERTS_PER_GROUP + k
                a = jnp.dot(hb, wg_ref[e], preferred_element_type=F32)
                u = jnp.dot(hb, wu_ref[e], preferred_element_type=F32)
                gate = jnp.where(in_grp, tok[:, k:k + 1], 0.0)
                hid = (_silu(a) * u * gate).astype(BF16)
                part = jnp.dot(hid, wd_ref[e], preferred_element_type=F32)
                acc = part if acc is None else acc + part
        o_ref[...] = _layer_norm(ALPHA * h_ref[...] + acc, g_ref[...], b_ref[...])


def _moe(h2, wr_hi, wr_lo, br, wg, wu, wd, g_row, b_row):
    n = h2.shape[0]
    tm = min(MOE_TM, n)
    rows_p = N_GROUPS_MOE * MOE_CAP
    row = lambda i: (i, 0)
    const = lambda i: (0, 0)
    whole = lambda i: (0, 0, 0)
    once = pl.Buffered(1)
    upper = (jnp.arange(tm)[:, None] < jnp.arange(tm)[None, :]).astype(BF16)
    return pl.pallas_call(
        _moe_kernel,
        grid=(n // tm,),
        in_specs=[
            pl.BlockSpec((tm, D_MODEL), row),
            pl.BlockSpec((D_MODEL, LANES), const),
            pl.BlockSpec((D_MODEL, LANES), const),
            pl.BlockSpec((1, LANES), const),
            pl.BlockSpec((N_EXPERTS, D_MODEL, EXPERT_FF), whole, pipeline_mode=once),
            pl.BlockSpec((N_EXPERTS, D_MODEL, EXPERT_FF), whole, pipeline_mode=once),
            pl.BlockSpec((N_EXPERTS, EXPERT_FF, D_MODEL), whole, pipeline_mode=once),
            pl.BlockSpec((1, D_MODEL), const),
            pl.BlockSpec((1, D_MODEL), const),
            pl.BlockSpec((tm, tm), const, pipeline_mode=once),
        ],
        out_specs=pl.BlockSpec((tm, D_MODEL), row),
        out_shape=jax.ShapeDtypeStruct((n, D_MODEL), F32),
        scratch_shapes=[
            pltpu.VMEM((tm, D_MODEL), BF16),
            pltpu.VMEM((rows_p, D_MODEL), BF16),
            pltpu.VMEM((rows_p, LANES), F32),
            pltpu.VMEM((rows_p, D_MODEL), BF16),
        ],
        compiler_params=pltpu.CompilerParams(
            dimension_semantics=("parallel",), vmem_limit_bytes=VMEM_LIMIT),
        name="hier_moe_ln",
    )(h2, wr_hi, wr_lo, br, wg, wu, wd, g_row, b_row, upper)


def _rope_tables(seq):
    inv = ROPE_THETA ** (-jnp.arange(0, HEAD_DIM, 2, dtype=F32) / HEAD_DIM)
    ang = jnp.arange(seq, dtype=F32)[:, None] * inv[None, :]
    cos, sin = jnp.cos(ang), jnp.sin(ang)
    zero = jnp.zeros_like(sin)
    cos_t = jnp.tile(cos, (1, LANES // (HEAD_DIM // 2)))
    s1_t = jnp.tile(jnp.concatenate([-sin, zero], 1), (1, LANES // HEAD_DIM))
    s2_t = jnp.tile(jnp.concatenate([zero, sin], 1), (1, LANES // HEAD_DIM))
    return cos_t, s1_t, s2_t


def _permute_w_in(w):
    sizes = (ATT_WIDTH, HEAD_DIM, HEAD_DIM, IDX_HEADS * IDX_DIM, IDX_DIM, IDX_HEADS,
             SSD_WIDTH, CONV_CH, SSD_HEADS)
    pts = np.cumsum((0,) + sizes)
    q, k, v, iq, ik, iw, z, xbc, dt = [w[:, pts[i]:pts[i + 1]] for i in range(len(sizes))]
    d = w.shape[0]
    pad = lambda n: jnp.zeros((d, n), w.dtype)
    misc = jnp.concatenate([v, iw, pad(MISC_DT - MISC_IW - IDX_HEADS), dt,
                            pad(LANES - MISC_DT - SSD_HEADS)], 1)
    return jnp.concatenate([q, iq, k, ik, misc, z, xbc], 1).astype(BF16)


def _head_rep(vals):
    return jnp.broadcast_to(vals.astype(F32)[:, None], (vals.shape[0], CHUNK))


def kernel(x, w_in, conv_w, conv_b, dt_bias, a_log, d_skip, ssd_norm_w, w_out, ln1_g, ln1_b,
           w_route_group, b_route_group, w_route_expert, b_route_expert, w_gate, w_up,
           w_down, ln2_g, ln2_b):
    bsz, seq, d = x.shape
    n = bsz * seq
    topk = min(TOPK_MAX, seq // 4)
    tm = 512
    assert d == D_MODEL and TQ == KC and seq % TQ == 0 and seq % tm == 0 and topk <= KC
    assert seq // BF16_ROWS <= 256
    cos_t, s1_t, s2_t = _rope_tables(seq)
    mscale = jnp.ones((1, LANES), F32).at[0, MISC_IW:MISC_IW + IDX_HEADS].set(INDEXER_SCALE)
    for l in range(DEPTH):
        x2 = x.reshape(n, d)
        q, iq, k, ik, vt, misc, zs, xc = _in_proj(
            x2, _permute_w_in(w_in[l]), cos_t, s1_t, s2_t, mscale,
            conv_w[l].astype(F32), conv_b[l].astype(F32)[None, :], seq, tm)
        att = _dsa(q, iq, misc, k, ik, vt, bsz, seq, topk)
        ssd = _ssd(
            xc, zs, misc,
            _head_rep(dt_bias[l]), _head_rep(-jnp.exp(a_log[l].astype(F32))),
            jnp.repeat(d_skip[l].astype(F32), SSD_HEAD_DIM)[None, :], ssd_norm_w[l][None, :],
            bsz, seq)
        w_o = w_out[l].astype(BF16)
        h2 = _out_proj(att, ssd, x2, w_o[:ATT_WIDTH], w_o[ATT_WIDTH:],
                       ln1_g[l][None, :], ln1_b[l][None, :], min(OUT_TM, n))
        pad = jnp.zeros((d, LANES - ROUTE_E0 - N_EXPERTS), F32)
        wr = jnp.concatenate([w_route_group[l].astype(F32), w_route_expert[l].astype(F32), pad], 1)
        wr_hi = wr.astype(BF16)
        wr_lo = (wr - wr_hi.astype(F32)).astype(BF16)
        br = jnp.concatenate([b_route_group[l].astype(F32), b_route_expert[l].astype(F32),
                              pad[0]])[None, :]
        x = _moe(h2, wr_hi, wr_lo, br, w_gate[l].astype(BF16), w_up[l].astype(BF16),
                 w_down[l].astype(BF16), ln2_g[l][None, :], ln2_b[l][None, :]).reshape(bsz, seq, d)
    return x
```

```python
import functools

import jax
import jax.numpy as jnp
import numpy as np
from jax import lax
from jax.experimental import pallas as pl
from jax.experimental.pallas import tpu as pltpu

F32 = jnp.float32
BF16 = jnp.bfloat16

D_MODEL = 1024
HEAD_DIM = 64
ATT_WIDTH = 512
ATT_HEADS = 8
IDX_HEADS = 4
IDX_DIM = 64
TOPK_MAX = 256
ROPE_THETA = 10000.0
INDEXER_SCALE = (IDX_HEADS ** -0.5) * (IDX_DIM ** -0.5)
SSD_WIDTH = 512
SSD_HEADS = 8
SSD_HEAD_DIM = 64
SSD_GROUPS = 2
D_STATE = 64
CONV_WIDTH = 4
CONV_CH = SSD_WIDTH + 2 * SSD_GROUPS * D_STATE
CHUNK = 128
N_GROUPS_MOE = 4
EXPERTS_PER_GROUP = 4
N_EXPERTS = 16
EXPERT_FF = 256
DEPTH = 1
ALPHA = (2 * DEPTH) ** 0.25
LN_EPS = 1e-5

LANES = 128
SUBLANES = 8
BF16_ROWS = 16
INF_KEY = 0x7F800000
MIN_NORMAL_KEY = 0x00800000
FINE_PASSES = 17
FINE_FIXED_PASSES = 9
KEY_STEP16 = 1 << 16
VMEM_LIMIT = 56 * 1024 * 1024

C_Q = 0
C_IQ = 512
C_K = 768
C_IK = 832
C_MISC = 896
MISC_IW = 64
MISC_DT = 72
C_Z = 1024
C_XBC = 1536
IN_COLS = 2304

TQ = 256
KC = 256

Q_SCALE = HEAD_DIM ** -0.5 * float(np.log2(np.e))

NT_DIMS = (((1,), (1,)), ((), ()))


def _nt_dot(a, b):
    return lax.dot_general(a, b, NT_DIMS, preferred_element_type=F32)


def _fold_rows(x, op, rows=SUBLANES):
    slabs = [x[r * rows:(r + 1) * rows, :] for r in range(x.shape[0] // rows)]
    while len(slabs) > 1:
        nxt = [op(slabs[i], slabs[i + 1]) for i in range(0, len(slabs) - 1, 2)]
        if len(slabs) % 2:
            nxt.append(slabs[-1])
        slabs = nxt
    return slabs[0]


CONV_PAD = 8
PROJ_COLS = 256


def _silu(x):
    return x * (1.0 / (1.0 + jnp.exp(-x)))


def _in_proj_kernel(x_ref, w_ref, cos_ref, s1_ref, s2_ref, mscale_ref, cw_ref, cb_ref,
                    q_ref, iq_ref, k_ref, ik_ref, vt_ref, misc_ref, zs_ref, xc_ref, hist_ref,
                    *, nblk_seq):
    @pl.when(pl.program_id(0) == 0)
    def _():
        hist_ref[...] = jnp.zeros_like(hist_ref)

    xb = x_ref[...].astype(BF16)
    cos = cos_ref[...]
    s1 = s1_ref[...]
    s2 = s2_ref[...]

    def mm(c0, width):
        return jnp.dot(xb, w_ref[:, c0:c0 + width], preferred_element_type=F32)

    def rope(y):
        fwd = pltpu.roll(y, LANES - HEAD_DIM // 2, 1)
        bwd = pltpu.roll(y, HEAD_DIM // 2, 1)
        return y * cos + fwd * s1 + bwd * s2

    def rope_wide(y, scale):
        parts = []
        for c in range(y.shape[1] // LANES):
            r = rope(y[:, c * LANES:(c + 1) * LANES])
            parts.append(r * scale if scale != 1.0 else r)
        return parts

    tm = x_ref.shape[0]
    seq_start = pl.program_id(0) % nblk_seq == 0
    for c0 in range(0, CONV_CH, PROJ_COLS):
        cols = slice(c0, c0 + PROJ_COLS)
        xbc = mm(C_XBC + c0, PROJ_COLS)
        hist = jnp.where(seq_start, 0.0, hist_ref[:, cols])
        xp = jnp.concatenate([hist, xbc], axis=0)
        acc = cw_ref[0:1, cols] * xp
        for j in range(1, CONV_WIDTH):
            acc = pltpu.roll(acc, 1, 0) + cw_ref[j:j + 1, cols] * xp
        xc_ref[:, cols] = _silu(acc[CONV_PAD:, :] + cb_ref[:, cols])
        hist_ref[:, cols] = xbc[tm - CONV_PAD:, :]
    for c0 in range(0, SSD_WIDTH, PROJ_COLS):
        zs_ref[:, c0:c0 + PROJ_COLS] = _silu(mm(C_Z + c0, PROJ_COLS))

    for c, r in enumerate(rope_wide(mm(C_Q, ATT_WIDTH), Q_SCALE)):
        q_ref[:, c * LANES:(c + 1) * LANES] = r.astype(BF16)
    for c, r in enumerate(rope_wide(mm(C_IQ, IDX_HEADS * IDX_DIM), 1.0)):
        iq_ref[:, c * LANES:(c + 1) * LANES] = r.astype(BF16)
    kk = rope(mm(C_K, LANES))
    k_ref[...] = kk[:, :HEAD_DIM].astype(BF16)
    ik_ref[...] = kk[:, HEAD_DIM:].astype(BF16)
    misc = mm(C_MISC, LANES) * mscale_ref[...]
    misc_ref[...] = misc
    lane = lax.broadcasted_iota(jnp.int32, misc.shape, 1)
    vext = jnp.where(lane < HEAD_DIM, misc, jnp.where(lane == HEAD_DIM, 1.0, 0.0))
    for c in range(vt_ref.shape[0]):
        vt_ref[c] = vext[c * KC:(c + 1) * KC, :].T.astype(BF16)


def _in_proj(x2, w_perm, cos_t, s1_t, s2_t, mscale, conv_w, conv_b, seq, tm):
    n = x2.shape[0]
    nblk_seq = seq // tm
    row = lambda i: (i, 0)
    tab = lambda i: (i % nblk_seq, 0)
    const = lambda i: (0, 0)
    outs = [
        ((n, ATT_WIDTH), BF16, pl.BlockSpec((tm, ATT_WIDTH), row)),
        ((n, IDX_HEADS * IDX_DIM), BF16, pl.BlockSpec((tm, IDX_HEADS * IDX_DIM), row)),
        ((n, HEAD_DIM), BF16, pl.BlockSpec((tm, HEAD_DIM), row)),
        ((n, IDX_DIM), BF16, pl.BlockSpec((tm, IDX_DIM), row)),
        ((n // KC, LANES, KC), BF16, pl.BlockSpec((tm // KC, LANES, KC), lambda i: (i, 0, 0))),
        ((n, LANES), F32, pl.BlockSpec((tm, LANES), row)),
        ((n, SSD_WIDTH), F32, pl.BlockSpec((tm, SSD_WIDTH), row)),
        ((n, CONV_CH), F32, pl.BlockSpec((tm, CONV_CH), row)),
    ]
    return pl.pallas_call(
        functools.partial(_in_proj_kernel, nblk_seq=nblk_seq),
        grid=(n // tm,),
        in_specs=[
            pl.BlockSpec((tm, D_MODEL), row),
            pl.BlockSpec((D_MODEL, IN_COLS), const),
            pl.BlockSpec((tm, LANES), tab),
            pl.BlockSpec((tm, LANES), tab),
            pl.BlockSpec((tm, LANES), tab),
            pl.BlockSpec((1, LANES), const),
            pl.BlockSpec((CONV_WIDTH, CONV_CH), const),
            pl.BlockSpec((1, CONV_CH), const),
        ],
        out_specs=[spec for _, _, spec in outs],
        out_shape=[jax.ShapeDtypeStruct(shape, dt) for shape, dt, _ in outs],
        scratch_shapes=[pltpu.VMEM((CONV_PAD, CONV_CH), F32)],
        compiler_params=pltpu.CompilerParams(
            dimension_semantics=("arbitrary",), vmem_limit_bytes=VMEM_LIMIT),
        name="in_proj",
    )(x2, w_perm, cos_t, s1_t, s2_t, mscale, conv_w, conv_b)


def _dsa_kernel(q_ref, iq_ref, misc_ref, k_ref, ik_ref, vt_ref, o_ref,
                sc_ref, sc16_ref, lg_ref, acc_ref, *, topk):
    qi = pl.program_id(1)
    nj = qi + 1
    neg_inf = -jnp.inf
    kf = float(topk)
    key_i = lax.broadcasted_iota(jnp.int32, (KC, TQ), 0)
    qry_i = lax.broadcasted_iota(jnp.int32, (KC, TQ), 1)
    causal = key_i <= qry_i

    def key_rows(j):
        return pl.ds(pl.multiple_of(j * KC, KC), KC)

    iw_t = misc_ref[...].T[MISC_IW:MISC_IW + SUBLANES, :]

    def chunk_scores(j):
        ikj = ik_ref[key_rows(j), :]
        sc = None
        for h in range(IDX_HEADS):
            d = _nt_dot(ikj, iq_ref[:, h * IDX_DIM:(h + 1) * IDX_DIM])
            term = iw_t[h:h + 1, :] * jnp.maximum(d, 0.0)
            sc = term if sc is None else sc + term
        sc_ref[j] = sc
        sc16_ref[j] = sc.astype(BF16)

    def scores_body(p, carry):
        chunk_scores(2 * p)
        chunk_scores(jnp.minimum(2 * p + 1, qi))
        return carry

    lax.fori_loop(0, (nj + 1) // 2, scores_body, 0)
    sc_diag = jnp.where(causal, sc_ref[qi], neg_inf)
    sc_ref[qi] = sc_diag
    sc16_ref[qi] = sc_diag.astype(BF16)

    def key_to_float(key):
        mag = jnp.where(key < 0, -key, key)
        sub = jnp.logical_and(mag > 0, mag < MIN_NORMAL_KEY)
        mag = jnp.where(sub, jnp.where(key < 0, 0, MIN_NORMAL_KEY), mag)
        f = lax.bitcast_convert_type(mag, F32)
        return jnp.where(key < 0, -f, f)

    one16 = jnp.ones((), BF16)
    zero16 = jnp.zeros((), BF16)

    nj_search = jnp.where(qi == 0, 0, nj) if topk >= TQ else nj

    def count16_ge(cand16):
        def body(j, acc):
            ind = jnp.where(sc16_ref[j] >= cand16, one16, zero16)
            return acc + _fold_rows(ind, jnp.add, BF16_ROWS)

        acc = lax.fori_loop(0, nj_search, body, jnp.zeros((BF16_ROWS, TQ), BF16))
        return jnp.sum(acc.astype(F32), axis=0, keepdims=True)

    def count_ge(cand, chunks=nj_search):
        def body(j, acc):
            ind = jnp.where(sc_ref[j] >= cand, 1.0, 0.0)
            return acc + _fold_rows(ind, jnp.add)

        acc = lax.fori_loop(0, chunks, body, jnp.zeros((SUBLANES, TQ), F32))
        return jnp.sum(acc, axis=0, keepdims=True)

    def coarse_body(b, m):
        trial = m + lax.shift_left(jnp.int32(1), jnp.int32(15) - b)
        cand16 = key_to_float(trial * KEY_STEP16).astype(BF16)
        return jnp.where(count16_ge(cand16) >= kf, trial, m)

    m16 = lax.fori_loop(0, 16, coarse_body, jnp.full((1, TQ), -(INF_KEY // KEY_STEP16), jnp.int32))
    key_base = jnp.maximum(m16 * KEY_STEP16 - (KEY_STEP16 // 2 + 1), -INF_KEY)

    min_normal = lax.bitcast_convert_type(jnp.int32(MIN_NORMAL_KEY), F32)

    def fine_step(b, state):
        off, cur_count, reject = state
        trial = off + lax.shift_left(jnp.int32(1), jnp.int32(16) - b)
        cand = key_to_float(key_base + trial)
        cnt = count_ge(cand)
        ok = cnt >= kf
        return (jnp.where(ok, trial, off), jnp.where(ok, cnt, cur_count),
                jnp.where(ok, reject, jnp.minimum(reject, cand)))

    def zero_tie(off, reject):
        return jnp.logical_and(key_to_float(key_base + off) == 0.0, reject <= min_normal)

    def unsettled(state):
        off, cur_count, reject = state
        done = jnp.logical_or(cur_count == kf, zero_tie(off, reject))
        return jnp.max(jnp.where(done, 0.0, 1.0)) > 0.0

    unknown = jnp.maximum(key_base.astype(F32), jnp.inf)
    state = lax.fori_loop(0, FINE_FIXED_PASSES, fine_step,
                          (jnp.zeros((1, TQ), jnp.int32), unknown, unknown))

    def more_cond(carry):
        b, go, _ = carry
        return jnp.logical_and(b < FINE_PASSES, go)

    def more_body(carry):
        b, _, state = carry
        state = fine_step(b + 1, fine_step(b, state))
        return b + 2, unsettled(state), state

    _, _, state = lax.while_loop(more_cond, more_body,
                                 (jnp.int32(FINE_FIXED_PASSES), unsettled(state), state))
    off, _, reject = state
    off = jnp.where(zero_tie(off, reject), -key_base, off)
    lo = key_to_float(key_base + off)
    hi = key_to_float(key_base + off + 1)
    need = kf - count_ge(hi, nj)

    lower = (lax.broadcasted_iota(jnp.int32, (KC, KC), 0)
             > lax.broadcasted_iota(jnp.int32, (KC, KC), 1)).astype(BF16)

    acc_ref[...] = jnp.zeros_like(acc_ref)
    no_max = tuple(jnp.minimum(need, neg_inf) for _ in range(ATT_HEADS))

    def logits_stage(j, taken):
        s = sc_ref[j]
        eqf = jnp.where(s >= hi, 0.0, jnp.where(s >= lo, 1.0, 0.0))
        before = jnp.dot(lower, eqf.astype(BF16), preferred_element_type=F32) + taken
        tie_bias = jnp.where(before < need, 0.0, neg_inf)
        bias = jnp.where(s >= hi, 0.0, jnp.where(s >= lo, tie_bias, neg_inf))
        bias = jnp.where(j < qi, bias, jnp.where(causal, bias, neg_inf))
        taken = taken + jnp.sum(_fold_rows(eqf, jnp.add), axis=0, keepdims=True)
        kj = k_ref[key_rows(j), :]
        slot = j % 2
        cms = []
        for h in range(ATT_HEADS):
            lg = _nt_dot(kj, q_ref[:, h * HEAD_DIM:(h + 1) * HEAD_DIM]) + bias
            lg_ref[slot, h] = lg
            cms.append(jnp.max(_fold_rows(lg, jnp.maximum), axis=0, keepdims=True))
        return taken, tuple(cms)

    def pv_stage(jp, cms, ms):
        vtj = vt_ref[jp]
        slot = jp % 2
        new_ms = []
        for h in range(ATT_HEADS):
            m_new = jnp.maximum(ms[h], cms[h])
            shift = jnp.where(m_new == neg_inf, 0.0, m_new)
            p = jnp.exp2(lg_ref[slot, h] - shift).astype(BF16)
            alpha = jnp.exp2(ms[h] - shift)
            acc_ref[h] = alpha * acc_ref[h] + jnp.dot(vtj, p, preferred_element_type=F32)
            new_ms.append(m_new)
        return tuple(new_ms)

    def att_body(j, carry):
        taken, cms_prev, ms = carry
        ms = pv_stage(j - 1, cms_prev, ms)
        taken, cms = logits_stage(j, taken)
        return taken, cms, ms

    taken0, cms0 = logits_stage(0, jnp.zeros((1, TQ), F32))
    _, cms_last, ms = lax.fori_loop(1, nj, att_body, (taken0, cms0, no_max))
    pv_stage(qi, cms_last, ms)
    for pair in range(ATT_HEADS // 2):
        halves = []
        for h in (2 * pair, 2 * pair + 1):
            a = acc_ref[h]
            halves.append(a[:HEAD_DIM, :] * (1.0 / a[HEAD_DIM:HEAD_DIM + 1, :]))
        blk = jnp.concatenate(halves, axis=0)
        o_ref[:, pair * LANES:(pair + 1) * LANES] = blk.T.astype(BF16)


def _dsa(q, iq, misc, k, ik, vt, b, s, topk):
    nq = s // TQ
    tile = lambda bi, qi: (bi * nq + qi, 0)
    full = lambda bi, qi: (bi, 0)
    return pl.pallas_call(
        functools.partial(_dsa_kernel, topk=topk),
        grid=(b, nq),
        in_specs=[
            pl.BlockSpec((TQ, ATT_WIDTH), tile),
            pl.BlockSpec((TQ, IDX_HEADS * IDX_DIM), tile),
            pl.BlockSpec((TQ, LANES), tile),
            pl.BlockSpec((s, HEAD_DIM), full),
            pl.BlockSpec((s, IDX_DIM), full),
            pl.BlockSpec((s // KC, LANES, KC), lambda bi, qi: (bi, 0, 0)),
        ],
        out_specs=pl.BlockSpec((TQ, ATT_WIDTH), tile),
        out_shape=jax.ShapeDtypeStruct((b * s, ATT_WIDTH), BF16),
        scratch_shapes=[
            pltpu.VMEM((nq, KC, TQ), F32),
            pltpu.VMEM((nq, KC, TQ), BF16),
            pltpu.VMEM((2, ATT_HEADS, KC, TQ), F32),
            pltpu.VMEM((ATT_HEADS, LANES, TQ), F32),
        ],
        compiler_params=pltpu.CompilerParams(
            dimension_semantics=("parallel", "arbitrary"), vmem_limit_bytes=VMEM_LIMIT),
        name="dsa_attention",
    )(q, iq, misc, k, ik, vt)


SSD_TT = 2048


def _ssd_kernel(xc_ref, zs_ref, misc_ref, dtb_ref, arep_ref, dskip_ref, nw_ref,
                expand_ref, triu_ref, o_ref, state_ref):
    t = pl.program_id(1)
    tt = xc_ref.shape[0]

    @pl.when(t == 0)
    def _():
        state_ref[...] = jnp.zeros_like(state_ref)

    tri = (lax.broadcasted_iota(jnp.int32, (CHUNK, CHUNK), 0)
           >= lax.broadcasted_iota(jnp.int32, (CHUNK, CHUNK), 1))
    left_head = lax.broadcasted_iota(jnp.int32, (CHUNK, LANES), 1) < SSD_HEAD_DIM
    left_head_n = lax.broadcasted_iota(jnp.int32, (D_STATE, LANES), 1) < SSD_HEAD_DIM
    gn = SSD_GROUPS * D_STATE
    pairs_per_group = SSD_HEADS // SSD_GROUPS // 2
    expand = expand_ref[...]
    triu = triu_ref[...]
    zpad = jnp.zeros((SUBLANES, CHUNK), F32)

    for c in range(tt // CHUNK):
        rows = slice(c * CHUNK, (c + 1) * CHUNK)
        raw = misc_ref[rows, :].T[MISC_DT:MISC_DT + SSD_HEADS, :] + dtb_ref[...]
        dt_t = jnp.maximum(raw, 0.0) + jnp.log1p(jnp.exp(-jnp.abs(raw)))
        adt = dt_t * arep_ref[...]
        hi = adt.astype(BF16).astype(F32)
        r1 = adt - hi
        mid = r1.astype(BF16).astype(F32)
        pieces = jnp.concatenate([hi, mid, r1 - mid, zpad], axis=0).astype(BF16)
        cs = jnp.dot(pieces, triu, preferred_element_type=F32)
        acum_t = (cs[0:SUBLANES] + cs[SUBLANES:2 * SUBLANES]) + cs[2 * SUBLANES:3 * SUBLANES]
        a_last = acum_t[:, CHUNK - 1:CHUNK]
        ddt_t = jnp.exp(a_last - acum_t) * dt_t
        acum = jnp.concatenate(
            [acum_t, jnp.zeros((LANES - SSD_HEADS, CHUNK), F32)], axis=0).T
        xs = xc_ref[rows, 0:SSD_WIDTH]
        bm = xc_ref[rows, SSD_WIDTH:SSD_WIDTH + gn]
        cm = xc_ref[rows, SSD_WIDTH + gn:SSD_WIDTH + 2 * gn]
        bm_t = bm.T
        xs16 = xs.astype(BF16)
        bm16 = bm.astype(BF16)
        cm16 = cm.astype(BF16)
        ea = jnp.exp(acum)
        ea_hi = ea.astype(BF16)
        ea_lo = (ea - ea_hi.astype(F32)).astype(BF16)
        ea_x = (jnp.dot(ea_hi, expand, preferred_element_type=F32)
                + jnp.dot(ea_lo, expand, preferred_element_type=F32))
        y_pairs = [None] * (SSD_HEADS // 2)
        for g in range(SSD_GROUPS):
            cg = cm16[:, g * D_STATE:(g + 1) * D_STATE]
            bg = bm16[:, g * D_STATE:(g + 1) * D_STATE]
            bg_t = bm_t[g * D_STATE:(g + 1) * D_STATE, :]
            gmat = _nt_dot(cg, bg)
            for pp in range(pairs_per_group):
                pair = g * pairs_per_group + pp
                lanes = slice(pair * LANES, (pair + 1) * LANES)
                xp = xs16[:, lanes]
                y_halves, s_halves = [], []
                for h in (2 * pair, 2 * pair + 1):
                    col = acum[:, h:h + 1]
                    rowv = acum_t[h:h + 1, :]
                    lmat = jnp.exp(jnp.where(tri, col - rowv, -jnp.inf))
                    mmat = (gmat * lmat * dt_t[h:h + 1, :]).astype(BF16)
                    y_halves.append(jnp.dot(mmat, xp, preferred_element_type=F32))
                    bs = (bg_t * ddt_t[h:h + 1, :]).astype(BF16)
                    s_halves.append(jnp.dot(bs, xp, preferred_element_type=F32))
                y_diag = jnp.where(left_head, y_halves[0], y_halves[1])
                new = jnp.where(left_head_n, s_halves[0], s_halves[1])
                prev = state_ref[pair]
                y_off = jnp.dot(cg, prev.astype(BF16), preferred_element_type=F32) * ea_x[:, lanes]
                state_ref[pair] = prev * ea_x[CHUNK - 1:CHUNK, lanes] + new
                y_pairs[pair] = y_diag + y_off
        y = jnp.concatenate(y_pairs, axis=1) + dskip_ref[...] * xs
        y = y * zs_ref[rows, :]
        ms = jnp.mean(y * y, axis=1, keepdims=True)
        o_ref[rows, :] = (y * lax.rsqrt(ms + LN_EPS) * nw_ref[...]).astype(BF16)


def _ssd(xc, zs, misc, dtb_rep, a_rep, dskip_row, nw_row, b, s):
    tt = min(SSD_TT, s)
    nt = s // tt
    tile = lambda bi, ti: (bi * nt + ti, 0)
    const = lambda bi, ti: (0, 0)
    expand = (jnp.arange(LANES)[:, None] == jnp.arange(SSD_WIDTH)[None, :] // SSD_HEAD_DIM).astype(BF16)
    triu = (jnp.arange(CHUNK)[:, None] <= jnp.arange(CHUNK)[None, :]).astype(BF16)
    return pl.pallas_call(
        _ssd_kernel,
        grid=(b, s // tt),
        in_specs=[
            pl.BlockSpec((tt, CONV_CH), tile),
            pl.BlockSpec((tt, SSD_WIDTH), tile),
            pl.BlockSpec((tt, LANES), tile),
            pl.BlockSpec((SSD_HEADS, CHUNK), const),
            pl.BlockSpec((SSD_HEADS, CHUNK), const),
            pl.BlockSpec((1, SSD_WIDTH), const),
            pl.BlockSpec((1, SSD_WIDTH), const),
            pl.BlockSpec((LANES, SSD_WIDTH), const),
            pl.BlockSpec((CHUNK, CHUNK), const),
        ],
        out_specs=pl.BlockSpec((tt, SSD_WIDTH), tile),
        out_shape=jax.ShapeDtypeStruct((b * s, SSD_WIDTH), BF16),
        scratch_shapes=[
            pltpu.VMEM((SSD_HEADS // 2, D_STATE, LANES), F32),
        ],
        compiler_params=pltpu.CompilerParams(
            dimension_semantics=("parallel", "arbitrary"), vmem_limit_bytes=VMEM_LIMIT),
        name="ssd_mixer",
    )(xc, zs, misc, dtb_rep, a_rep, dskip_row, nw_row, expand, triu)


def _layer_norm(y, g, b):
    mu = jnp.mean(y, axis=1, keepdims=True)
    yc = y - mu
    var = jnp.mean(yc * yc, axis=1, keepdims=True)
    return yc * lax.rsqrt(var + LN_EPS) * g + b


OUT_TM = 1024
OUT_SUB = 256


def _out_proj_kernel(att_ref, ssd_ref, x_ref, wa_ref, ws_ref, g_ref, b_ref, h_ref):
    for r0 in range(0, x_ref.shape[0], OUT_SUB):
        rows = slice(r0, r0 + OUT_SUB)
        mixed = jnp.dot(att_ref[rows, :], wa_ref[...], preferred_element_type=F32)
        mixed = mixed + jnp.dot(ssd_ref[rows, :], ws_ref[...], preferred_element_type=F32)
        h_ref[rows, :] = _layer_norm(ALPHA * x_ref[rows, :] + mixed, g_ref[...], b_ref[...])


def _out_proj(att2, ssd2, x2, w_att, w_ssd, g_row, b_row, tm):
    n = x2.shape[0]
    row = lambda i: (i, 0)
    const = lambda i: (0, 0)
    return pl.pallas_call(
        _out_proj_kernel,
        grid=(n // tm,),
        in_specs=[
            pl.BlockSpec((tm, ATT_WIDTH), row),
            pl.BlockSpec((tm, SSD_WIDTH), row),
            pl.BlockSpec((tm, D_MODEL), row),
            pl.BlockSpec((ATT_WIDTH, D_MODEL), const),
            pl.BlockSpec((SSD_WIDTH, D_MODEL), const),
            pl.BlockSpec((1, D_MODEL), const),
            pl.BlockSpec((1, D_MODEL), const),
        ],
        out_specs=pl.BlockSpec((tm, D_MODEL), row),
        out_shape=jax.ShapeDtypeStruct((n, D_MODEL), F32),
        compiler_params=pltpu.CompilerParams(
            dimension_semantics=("parallel",), vmem_limit_bytes=VMEM_LIMIT),
        name="out_proj_ln",
    )(att2, ssd2, x2, w_att, w_ssd, g_row, b_row)


ROUTE_E0 = N_GROUPS_MOE
MOE_TM = 512
MOE_CAP = 160


def _first_max(vals):
    best = vals[0]
    for v in vals[1:]:
        best = jnp.maximum(best, v)
    idx = jnp.full(best.shape, len(vals) - 1, jnp.int32)
    for i in range(len(vals) - 2, -1, -1):
        idx = jnp.where(vals[i] == best, i, idx)
    return best, idx


def _moe_kernel(h_ref, wrh_ref, wrl_ref, br_ref, wg_ref, wu_ref, wd_ref, g_ref, b_ref, upper_ref,
                o_ref, hb_ref, hp_ref, gp_ref, yp_ref):
    tm = h_ref.shape[0]
    h = h_ref[...]
    h_hi = h.astype(BF16)
    hb_ref[...] = h_hi
    h_lo = (h - h_hi.astype(F32)).astype(BF16)
    wrh = wrh_ref[...]
    both = jnp.dot(h_hi, jnp.concatenate([wrh, wrl_ref[...]], axis=1), preferred_element_type=F32)
    logits = (both[:, :LANES] + jnp.dot(h_lo, wrh, preferred_element_type=F32)
              + both[:, LANES:]) + br_ref[...]
    lt = logits.T
    row = lambda r: lt[r:r + 1, :]
    gl = [row(r) for r in range(N_GROUPS_MOE)]
    gmax, gidx = _first_max(gl)
    denom = jnp.exp(gl[0] - gmax)
    for v in gl[1:]:
        denom = denom + jnp.exp(v - gmax)
    gprob = 1.0 / denom
    el = []
    for k in range(EXPERTS_PER_GROUP):
        v = row(ROUTE_E0 + (N_GROUPS_MOE - 1) * EXPERTS_PER_GROUP + k)
        for gg in range(N_GROUPS_MOE - 2, -1, -1):
            v = jnp.where(gidx == gg, row(ROUTE_E0 + gg * EXPERTS_PER_GROUP + k), v)
        el.append(v)
    l1, i1 = _first_max(el)
    l2, i2 = _first_max([jnp.where(i1 == k, -jnp.inf, el[k]) for k in range(EXPERTS_PER_GROUP)])
    e2 = jnp.exp(l2 - l1)
    w1 = gprob / (1.0 + e2)
    w2 = gprob * e2 / (1.0 + e2)
    gate4 = [jnp.where(i1 == k, w1, jnp.where(i2 == k, w2, 0.0)) for k in range(EXPERTS_PER_GROUP)]

    member = [jnp.where(gidx == gg, 1.0, 0.0) for gg in range(N_GROUPS_MOE)]
    member_blk = jnp.concatenate(
        member + [jnp.zeros((2 * SUBLANES - N_GROUPS_MOE, tm), F32)], axis=0).astype(BF16)
    earlier = jnp.dot(member_blk, upper_ref[...], preferred_element_type=F32)
    counts = [jnp.sum(m, axis=1, keepdims=True) for m in member]
    most = counts[0]
    for c in counts[1:]:
        most = jnp.maximum(most, c)
    fits = jnp.max(most) <= float(MOE_CAP)
    slot = member[0] * earlier[0:1, :]
    for gg in range(1, N_GROUPS_MOE):
        slot = slot + member[gg] * (earlier[gg:gg + 1, :] + float(gg * MOE_CAP))
    tok = jnp.concatenate(
        gate4 + [slot, gidx.astype(F32), jnp.zeros((LANES - EXPERTS_PER_GROUP - 2, tm), F32)],
        axis=0).T
    slot_lane, grp_lane = EXPERTS_PER_GROUP, EXPERTS_PER_GROUP + 1
    rows_p = N_GROUPS_MOE * MOE_CAP

    @pl.when(fits)
    def _():
        hb = hb_ref[...]
        place = jnp.where(lax.broadcasted_iota(jnp.int32, (rows_p, tm), 0) == slot.astype(jnp.int32),
                          1.0, 0.0).astype(BF16)
        hp_ref[...] = jnp.dot(place, hb, preferred_element_type=F32).astype(BF16)
        t_hi = tok.astype(BF16)
        t_lo = (tok - t_hi.astype(F32)).astype(BF16)
        both = jnp.dot(place, jnp.concatenate([t_hi, t_lo], axis=1), preferred_element_type=F32)
        gp_ref[...] = both[:, :LANES] + both[:, LANES:]
        for gg in range(N_GROUPS_MOE):
            rows = slice(gg * MOE_CAP, (gg + 1) * MOE_CAP)
            xg = hp_ref[rows, :]
            acc = None
            for k in range(EXPERTS_PER_GROUP):
                e = gg * EXPERTS_PER_GROUP + k
                a = jnp.dot(xg, wg_ref[e], preferred_element_type=F32)
                u = jnp.dot(xg, wu_ref[e], preferred_element_type=F32)
                hid = (_silu(a) * u * gp_ref[rows, k:k + 1]).astype(BF16)
                part = jnp.dot(hid, wd_ref[e], preferred_element_type=F32)
                acc = part if acc is None else acc + part
            yp_ref[rows, :] = acc.astype(BF16)
        back = jnp.where(lax.broadcasted_iota(jnp.int32, (tm, rows_p), 1)
                         == tok[:, slot_lane:slot_lane + 1].astype(jnp.int32), 1.0, 0.0).astype(BF16)
        for r0 in range(0, tm, OUT_SUB):
            rr = slice(r0, r0 + OUT_SUB)
            y = jnp.dot(back[rr, :], yp_ref[...], preferred_element_type=F32)
            o_ref[rr, :] = _layer_norm(ALPHA * h_ref[rr, :] + y, g_ref[...], b_ref[...])

    @pl.when(jnp.logical_not(fits))
    def _():
        hb = hb_ref[...]
        acc = None
        for gg in range(N_GROUPS_MOE):
            in_grp = tok[:, grp_lane:grp_lane + 1] == float(gg)
            for k in range(EXPERTS_PER_GROUP):
                e = gg * EXPERTS_PER_GROUP + k
                a = jnp.dot(hb, wg_ref[e], preferred_element_type=F32)
                u = jnp.dot(hb, wu_ref[e], preferred_element_type=F32)
                gate = jnp.where(in_grp, tok[:, k:k + 1], 0.0)
                hid = (_silu(a) * u * gate).astype(BF16)
                part = jnp.dot(hid, wd_ref[e], preferred_element_type=F32)
                acc = part if acc is None else acc + part
        o_ref[...] = _layer_norm(ALPHA * h_ref[...] + acc, g_ref[...], b_ref[...])


def _moe(h2, wr_hi, wr_lo, br, wg, wu, wd, g_row, b_row):
    n = h2.shape[0]
    tm = min(MOE_TM, n)
    rows_p = N_GROUPS_MOE * MOE_CAP
    row = lambda i: (i, 0)
    const = lambda i: (0, 0)
    whole = lambda i: (0, 0, 0)
    once = pl.Buffered(1)
    upper = (jnp.arange(tm)[:, None] < jnp.arange(tm)[None, :]).astype(BF16)
    return pl.pallas_call(
        _moe_kernel,
        grid=(n // tm,),
        in_specs=[
            pl.BlockSpec((tm, D_MODEL), row),
            pl.BlockSpec((D_MODEL, LANES), const),
            pl.BlockSpec((D_MODEL, LANES), const),
            pl.BlockSpec((1, LANES), const),
            pl.BlockSpec((N_EXPERTS, D_MODEL, EXPERT_FF), whole, pipeline_mode=once),
            pl.BlockSpec((N_EXPERTS, D_MODEL, EXPERT_FF), whole, pipeline_mode=once),
            pl.BlockSpec((N_EXPERTS, EXPERT_FF, D_MODEL), whole, pipeline_mode=once),
            pl.BlockSpec((1, D_MODEL), const),
            pl.BlockSpec((1, D_MODEL), const),
            pl.BlockSpec((tm, tm), const, pipeline_mode=once),
        ],
        out_specs=pl.BlockSpec((tm, D_MODEL), row),
        out_shape=jax.ShapeDtypeStruct((n, D_MODEL), F32),
        scratch_shapes=[
            pltpu.VMEM((tm, D_MODEL), BF16),
            pltpu.VMEM((rows_p, D_MODEL), BF16),
            pltpu.VMEM((rows_p, LANES), F32),
            pltpu.VMEM((rows_p, D_MODEL), BF16),
        ],
        compiler_params=pltpu.CompilerParams(
            dimension_semantics=("parallel",), vmem_limit_bytes=VMEM_LIMIT),
        name="hier_moe_ln",
    )(h2, wr_hi, wr_lo, br, wg, wu, wd, g_row, b_row, upper)


def _rope_tables(seq):
    inv = ROPE_THETA ** (-jnp.arange(0, HEAD_DIM, 2, dtype=F32) / HEAD_DIM)
    ang = jnp.arange(seq, dtype=F32)[:, None] * inv[None, :]
    cos, sin = jnp.cos(ang), jnp.sin(ang)
    zero = jnp.zeros_like(sin)
    cos_t = jnp.tile(cos, (1, LANES // (HEAD_DIM // 2)))
    s1_t = jnp.tile(jnp.concatenate([-sin, zero], 1), (1, LANES // HEAD_DIM))
    s2_t = jnp.tile(jnp.concatenate([zero, sin], 1), (1, LANES // HEAD_DIM))
    return cos_t, s1_t, s2_t


def _permute_w_in(w):
    sizes = (ATT_WIDTH, HEAD_DIM, HEAD_DIM, IDX_HEADS * IDX_DIM, IDX_DIM, IDX_HEADS,
             SSD_WIDTH, CONV_CH, SSD_HEADS)
    pts = np.cumsum((0,) + sizes)
    q, k, v, iq, ik, iw, z, xbc, dt = [w[:, pts[i]:pts[i + 1]] for i in range(len(sizes))]
    d = w.shape[0]
    pad = lambda n: jnp.zeros((d, n), w.dtype)
    misc = jnp.concatenate([v, iw, pad(MISC_DT - MISC_IW - IDX_HEADS), dt,
                            pad(LANES - MISC_DT - SSD_HEADS)], 1)
    return jnp.concatenate([q, iq, k, ik, misc, z, xbc], 1).astype(BF16)


def _head_rep(vals):
    return jnp.broadcast_to(vals.astype(F32)[:, None], (vals.shape[0], CHUNK))


def kernel(x, w_in, conv_w, conv_b, dt_bias, a_log, d_skip, ssd_norm_w, w_out, ln1_g, ln1_b,
           w_route_group, b_route_group, w_route_expert, b_route_expert, w_gate, w_up,
           w_down, ln2_g, ln2_b):
    bsz, seq, d = x.shape
    n = bsz * seq
    topk = min(TOPK_MAX, seq // 4)
    tm = 512
    assert d == D_MODEL and TQ == KC and seq % TQ == 0 and seq % tm == 0 and topk <= KC
    assert seq // BF16_ROWS <= 256
    assert (FINE_PASSES - FINE_FIXED_PASSES) % 2 == 0
    cos_t, s1_t, s2_t = _rope_tables(seq)
    mscale = jnp.ones((1, LANES), F32).at[0, MISC_IW:MISC_IW + IDX_HEADS].set(INDEXER_SCALE)
    for l in range(DEPTH):
        x2 = x.reshape(n, d)
        q, iq, k, ik, vt, misc, zs, xc = _in_proj(
            x2, _permute_w_in(w_in[l]), cos_t, s1_t, s2_t, mscale,
            conv_w[l].astype(F32), conv_b[l].astype(F32)[None, :], seq, tm)
        att = _dsa(q, iq, misc, k, ik, vt, bsz, seq, topk)
        ssd = _ssd(
            xc, zs, misc,
            _head_rep(dt_bias[l]), _head_rep(-jnp.exp(a_log[l].astype(F32))),
            jnp.repeat(d_skip[l].astype(F32), SSD_HEAD_DIM)[None, :], ssd_norm_w[l][None, :],
            bsz, seq)
        w_o = w_out[l].astype(BF16)
        h2 = _out_proj(att, ssd, x2, w_o[:ATT_WIDTH], w_o[ATT_WIDTH:],
                       ln1_g[l][None, :], ln1_b[l][None, :], min(OUT_TM, n))
        pad = jnp.zeros((d, LANES - ROUTE_E0 - N_EXPERTS), F32)
        wr = jnp.concatenate([w_route_group[l].astype(F32), w_route_expert[l].astype(F32), pad], 1)
        wr_hi = wr.astype(BF16)
        wr_lo = (wr - wr_hi.astype(F32)).astype(BF16)
        br = jnp.concatenate([b_route_group[l].astype(F32), b_route_expert[l].astype(F32),
                              pad[0]])[None, :]
        x = _moe(h2, wr_hi, wr_lo, br, w_gate[l].astype(BF16), w_up[l].astype(BF16),
                 w_down[l].astype(BF16), ln2_g[l][None, :], ln2_b[l][None, :]).reshape(bsz, seq, d)
    return x
```

```python
import functools

import jax
import jax.numpy as jnp
import numpy as np
from jax import lax
from jax.experimental import pallas as pl
from jax.experimental.pallas import tpu as pltpu

F32 = jnp.float32
BF16 = jnp.bfloat16

D_MODEL = 1024
HEAD_DIM = 64
ATT_WIDTH = 512
ATT_HEADS = 8
IDX_HEADS = 4
IDX_DIM = 64
TOPK_MAX = 256
ROPE_THETA = 10000.0
INDEXER_SCALE = (IDX_HEADS ** -0.5) * (IDX_DIM ** -0.5)
SSD_WIDTH = 512
SSD_HEADS = 8
SSD_HEAD_DIM = 64
SSD_GROUPS = 2
D_STATE = 64
CONV_WIDTH = 4
CONV_CH = SSD_WIDTH + 2 * SSD_GROUPS * D_STATE
CHUNK = 128
N_GROUPS_MOE = 4
EXPERTS_PER_GROUP = 4
N_EXPERTS = 16
EXPERT_FF = 256
DEPTH = 1
ALPHA = (2 * DEPTH) ** 0.25
LN_EPS = 1e-5

LANES = 128
SUBLANES = 8
BF16_ROWS = 16
INF_KEY = 0x7F800000
MIN_NORMAL_KEY = 0x00800000
FINE_PASSES = 17
FINE_FIXED_PASSES = 11
KEY_STEP16 = 1 << 16
VMEM_LIMIT = 56 * 1024 * 1024

C_Q = 0
C_IQ = 512
C_K = 768
C_IK = 832
C_MISC = 896
MISC_IW = 64
MISC_DT = 72
C_Z = 1024
C_XBC = 1536
IN_COLS = 2304

TQ = 256
KC = 256

Q_SCALE = HEAD_DIM ** -0.5 * float(np.log2(np.e))

NT_DIMS = (((1,), (1,)), ((), ()))


def _nt_dot(a, b):
    return lax.dot_general(a, b, NT_DIMS, preferred_element_type=F32)


def _fold_rows(x, op, rows=SUBLANES):
    slabs = [x[r * rows:(r + 1) * rows, :] for r in range(x.shape[0] // rows)]
    while len(slabs) > 1:
        nxt = [op(slabs[i], slabs[i + 1]) for i in range(0, len(slabs) - 1, 2)]
        if len(slabs) % 2:
            nxt.append(slabs[-1])
        slabs = nxt
    return slabs[0]


CONV_PAD = 8
PROJ_COLS = 256


def _silu(x):
    return x * (1.0 / (1.0 + jnp.exp(-x)))


def _in_proj_kernel(x_ref, w_ref, cos_ref, s1_ref, s2_ref, mscale_ref, cw_ref, cb_ref,
                    q_ref, iq_ref, k_ref, ik_ref, vt_ref, misc_ref, zs_ref, xc_ref, hist_ref,
                    *, nblk_seq):
    @pl.when(pl.program_id(0) == 0)
    def _():
        hist_ref[...] = jnp.zeros_like(hist_ref)

    xb = x_ref[...].astype(BF16)
    cos = cos_ref[...]
    s1 = s1_ref[...]
    s2 = s2_ref[...]

    def mm(c0, width):
        return jnp.dot(xb, w_ref[:, c0:c0 + width], preferred_element_type=F32)

    def rope(y):
        fwd = pltpu.roll(y, LANES - HEAD_DIM // 2, 1)
        bwd = pltpu.roll(y, HEAD_DIM // 2, 1)
        return y * cos + fwd * s1 + bwd * s2

    def rope_wide(y, scale):
        parts = []
        for c in range(y.shape[1] // LANES):
            r = rope(y[:, c * LANES:(c + 1) * LANES])
            parts.append(r * scale if scale != 1.0 else r)
        return parts

    tm = x_ref.shape[0]
    seq_start = pl.program_id(0) % nblk_seq == 0
    for c0 in range(0, CONV_CH, PROJ_COLS):
        cols = slice(c0, c0 + PROJ_COLS)
        xbc = mm(C_XBC + c0, PROJ_COLS)
        hist = jnp.where(seq_start, 0.0, hist_ref[:, cols])
        xp = jnp.concatenate([hist, xbc], axis=0)
        acc = cw_ref[0:1, cols] * xp
        for j in range(1, CONV_WIDTH):
            acc = pltpu.roll(acc, 1, 0) + cw_ref[j:j + 1, cols] * xp
        xc_ref[:, cols] = _silu(acc[CONV_PAD:, :] + cb_ref[:, cols])
        hist_ref[:, cols] = xbc[tm - CONV_PAD:, :]
    for c0 in range(0, SSD_WIDTH, PROJ_COLS):
        zs_ref[:, c0:c0 + PROJ_COLS] = _silu(mm(C_Z + c0, PROJ_COLS))

    for c, r in enumerate(rope_wide(mm(C_Q, ATT_WIDTH), Q_SCALE)):
        q_ref[:, c * LANES:(c + 1) * LANES] = r.astype(BF16)
    for c, r in enumerate(rope_wide(mm(C_IQ, IDX_HEADS * IDX_DIM), 1.0)):
        iq_ref[:, c * LANES:(c + 1) * LANES] = r.astype(BF16)
    kk = rope(mm(C_K, LANES))
    k_ref[...] = kk[:, :HEAD_DIM].astype(BF16)
    ik_ref[...] = kk[:, HEAD_DIM:].astype(BF16)
    misc = mm(C_MISC, LANES) * mscale_ref[...]
    misc_ref[...] = misc
    lane = lax.broadcasted_iota(jnp.int32, misc.shape, 1)
    vext = jnp.where(lane < HEAD_DIM, misc, jnp.where(lane == HEAD_DIM, 1.0, 0.0))
    for c in range(vt_ref.shape[0]):
        vt_ref[c] = vext[c * KC:(c + 1) * KC, :].T.astype(BF16)


def _in_proj(x2, w_perm, cos_t, s1_t, s2_t, mscale, conv_w, conv_b, seq, tm):
    n = x2.shape[0]
    nblk_seq = seq // tm
    row = lambda i: (i, 0)
    tab = lambda i: (i % nblk_seq, 0)
    const = lambda i: (0, 0)
    outs = [
        ((n, ATT_WIDTH), BF16, pl.BlockSpec((tm, ATT_WIDTH), row)),
        ((n, IDX_HEADS * IDX_DIM), BF16, pl.BlockSpec((tm, IDX_HEADS * IDX_DIM), row)),
        ((n, HEAD_DIM), BF16, pl.BlockSpec((tm, HEAD_DIM), row)),
        ((n, IDX_DIM), BF16, pl.BlockSpec((tm, IDX_DIM), row)),
        ((n // KC, LANES, KC), BF16, pl.BlockSpec((tm // KC, LANES, KC), lambda i: (i, 0, 0))),
        ((n, LANES), F32, pl.BlockSpec((tm, LANES), row)),
        ((n, SSD_WIDTH), F32, pl.BlockSpec((tm, SSD_WIDTH), row)),
        ((n, CONV_CH), F32, pl.BlockSpec((tm, CONV_CH), row)),
    ]
    return pl.pallas_call(
        functools.partial(_in_proj_kernel, nblk_seq=nblk_seq),
        grid=(n // tm,),
        in_specs=[
            pl.BlockSpec((tm, D_MODEL), row),
            pl.BlockSpec((D_MODEL, IN_COLS), const),
            pl.BlockSpec((tm, LANES), tab),
            pl.BlockSpec((tm, LANES), tab),
            pl.BlockSpec((tm, LANES), tab),
            pl.BlockSpec((1, LANES), const),
            pl.BlockSpec((CONV_WIDTH, CONV_CH), const),
            pl.BlockSpec((1, CONV_CH), const),
        ],
        out_specs=[spec for _, _, spec in outs],
        out_shape=[jax.ShapeDtypeStruct(shape, dt) for shape, dt, _ in outs],
        scratch_shapes=[pltpu.VMEM((CONV_PAD, CONV_CH), F32)],
        compiler_params=pltpu.CompilerParams(
            dimension_semantics=("arbitrary",), vmem_limit_bytes=VMEM_LIMIT),
        name="in_proj",
    )(x2, w_perm, cos_t, s1_t, s2_t, mscale, conv_w, conv_b)


def _dsa_kernel(q_ref, iq_ref, misc_ref, k_ref, ik_ref, vt_ref, o_ref,
                sc_ref, sc16_ref, lg_ref, acc_ref, *, topk):
    qi = pl.program_id(1)
    nj = qi + 1
    neg_inf = -jnp.inf
    kf = float(topk)
    key_i = lax.broadcasted_iota(jnp.int32, (KC, TQ), 0)
    qry_i = lax.broadcasted_iota(jnp.int32, (KC, TQ), 1)
    causal = key_i <= qry_i

    def key_rows(j):
        return pl.ds(pl.multiple_of(j * KC, KC), KC)

    iw_t = misc_ref[...].T[MISC_IW:MISC_IW + SUBLANES, :]

    def chunk_scores(j):
        ikj = ik_ref[key_rows(j), :]
        sc = None
        for h in range(IDX_HEADS):
            d = _nt_dot(ikj, iq_ref[:, h * IDX_DIM:(h + 1) * IDX_DIM])
            term = iw_t[h:h + 1, :] * jnp.maximum(d, 0.0)
            sc = term if sc is None else sc + term
        sc_ref[j] = sc
        sc16_ref[j] = sc.astype(BF16)

    def scores_body(p, carry):
        chunk_scores(2 * p)
        chunk_scores(jnp.minimum(2 * p + 1, qi))
        return carry

    lax.fori_loop(0, (nj + 1) // 2, scores_body, 0)
    sc_diag = jnp.where(causal, sc_ref[qi], neg_inf)
    sc_ref[qi] = sc_diag
    sc16_ref[qi] = sc_diag.astype(BF16)

    def key_to_float(key):
        mag = jnp.where(key < 0, -key, key)
        sub = jnp.logical_and(mag > 0, mag < MIN_NORMAL_KEY)
        mag = jnp.where(sub, jnp.where(key < 0, 0, MIN_NORMAL_KEY), mag)
        f = lax.bitcast_convert_type(mag, F32)
        return jnp.where(key < 0, -f, f)

    one16 = jnp.ones((), BF16)
    zero16 = jnp.zeros((), BF16)

    nj_search = jnp.where(qi == 0, 0, nj) if topk >= TQ else nj

    def count16_ge(cand16):
        def body(j, acc):
            ind = jnp.where(sc16_ref[j] >= cand16, one16, zero16)
            return acc + _fold_rows(ind, jnp.add, BF16_ROWS)

        acc = lax.fori_loop(0, nj_search, body, jnp.zeros((BF16_ROWS, TQ), BF16))
        return jnp.sum(acc.astype(F32), axis=0, keepdims=True)

    def count_ge(cand, chunks=nj_search):
        def body(j, acc):
            ind = jnp.where(sc_ref[j] >= cand, 1.0, 0.0)
            return acc + _fold_rows(ind, jnp.add)

        acc = lax.fori_loop(0, chunks, body, jnp.zeros((SUBLANES, TQ), F32))
        return jnp.sum(acc, axis=0, keepdims=True)

    def coarse_body(b, m):
        trial = m + lax.shift_left(jnp.int32(1), jnp.int32(15) - b)
        cand16 = key_to_float(trial * KEY_STEP16).astype(BF16)
        return jnp.where(count16_ge(cand16) >= kf, trial, m)

    m16 = lax.fori_loop(0, 16, coarse_body, jnp.full((1, TQ), -(INF_KEY // KEY_STEP16), jnp.int32))
    key_base = jnp.maximum(m16 * KEY_STEP16 - (KEY_STEP16 // 2 + 1), -INF_KEY)

    min_normal = lax.bitcast_convert_type(jnp.int32(MIN_NORMAL_KEY), F32)

    def fine_step(b, state):
        off, cur_count, reject = state
        trial = off + lax.shift_left(jnp.int32(1), jnp.int32(16) - b)
        cand = key_to_float(key_base + trial)
        cnt = count_ge(cand)
        ok = cnt >= kf
        return (jnp.where(ok, trial, off), jnp.where(ok, cnt, cur_count),
                jnp.where(ok, reject, jnp.minimum(reject, cand)))

    def zero_tie(off, reject):
        return jnp.logical_and(key_to_float(key_base + off) == 0.0, reject <= min_normal)

    def unsettled(state):
        off, cur_count, reject = state
        done = jnp.logical_or(cur_count == kf, zero_tie(off, reject))
        return jnp.max(jnp.where(done, 0.0, 1.0)) > 0.0

    unknown = jnp.maximum(key_base.astype(F32), jnp.inf)
    state = lax.fori_loop(0, FINE_FIXED_PASSES, fine_step,
                          (jnp.zeros((1, TQ), jnp.int32), unknown, unknown))

    def more_cond(carry):
        b, go, _ = carry
        return jnp.logical_and(b < FINE_PASSES, go)

    def more_body(carry):
        b, _, state = carry
        state = fine_step(b + 1, fine_step(b, state))
        return b + 2, unsettled(state), state

    _, _, state = lax.while_loop(more_cond, more_body,
                                 (jnp.int32(FINE_FIXED_PASSES), unsettled(state), state))
    off, _, reject = state
    off = jnp.where(zero_tie(off, reject), -key_base, off)
    lo = key_to_float(key_base + off)
    hi = key_to_float(key_base + off + 1)
    need = kf - count_ge(hi, nj)

    lower = (lax.broadcasted_iota(jnp.int32, (KC, KC), 0)
             > lax.broadcasted_iota(jnp.int32, (KC, KC), 1)).astype(BF16)

    acc_ref[...] = jnp.zeros_like(acc_ref)
    no_max = tuple(jnp.minimum(need, neg_inf) for _ in range(ATT_HEADS))

    def logits_stage(j, taken):
        s = sc_ref[j]
        eqf = jnp.where(s >= hi, 0.0, jnp.where(s >= lo, 1.0, 0.0))
        before = jnp.dot(lower, eqf.astype(BF16), preferred_element_type=F32) + taken
        tie_bias = jnp.where(before < need, 0.0, neg_inf)
        bias = jnp.where(s >= hi, 0.0, jnp.where(s >= lo, tie_bias, neg_inf))
        bias = jnp.where(j < qi, bias, jnp.where(causal, bias, neg_inf))
        taken = taken + jnp.sum(_fold_rows(eqf, jnp.add), axis=0, keepdims=True)
        kj = k_ref[key_rows(j), :]
        slot = j % 2
        cms = []
        for h in range(ATT_HEADS):
            lg = _nt_dot(kj, q_ref[:, h * HEAD_DIM:(h + 1) * HEAD_DIM]) + bias
            lg_ref[slot, h] = lg
            cms.append(jnp.max(_fold_rows(lg, jnp.maximum), axis=0, keepdims=True))
        return taken, tuple(cms)

    def pv_stage(jp, cms, ms):
        vtj = vt_ref[jp]
        slot = jp % 2
        new_ms = []
        for h in range(ATT_HEADS):
            m_new = jnp.maximum(ms[h], cms[h])
            shift = jnp.where(m_new == neg_inf, 0.0, m_new)
            p = jnp.exp2(lg_ref[slot, h] - shift).astype(BF16)
            alpha = jnp.exp2(ms[h] - shift)
            acc_ref[h] = alpha * acc_ref[h] + jnp.dot(vtj, p, preferred_element_type=F32)
            new_ms.append(m_new)
        return tuple(new_ms)

    def att_body(j, carry):
        taken, cms_prev, ms = carry
        ms = pv_stage(j - 1, cms_prev, ms)
        taken, cms = logits_stage(j, taken)
        return taken, cms, ms

    taken0, cms0 = logits_stage(0, jnp.zeros((1, TQ), F32))
    _, cms_last, ms = lax.fori_loop(1, nj, att_body, (taken0, cms0, no_max))
    pv_stage(qi, cms_last, ms)
    for pair in range(ATT_HEADS // 2):
        halves = []
        for h in (2 * pair, 2 * pair + 1):
            a = acc_ref[h]
            halves.append(a[:HEAD_DIM, :] * (1.0 / a[HEAD_DIM:HEAD_DIM + 1, :]))
        blk = jnp.concatenate(halves, axis=0)
        o_ref[:, pair * LANES:(pair + 1) * LANES] = blk.T.astype(BF16)


def _dsa(q, iq, misc, k, ik, vt, b, s, topk):
    nq = s // TQ
    tile = lambda bi, qi: (bi * nq + qi, 0)
    full = lambda bi, qi: (bi, 0)
    return pl.pallas_call(
        functools.partial(_dsa_kernel, topk=topk),
        grid=(b, nq),
        in_specs=[
            pl.BlockSpec((TQ, ATT_WIDTH), tile),
            pl.BlockSpec((TQ, IDX_HEADS * IDX_DIM), tile),
            pl.BlockSpec((TQ, LANES), tile),
            pl.BlockSpec((s, HEAD_DIM), full),
            pl.BlockSpec((s, IDX_DIM), full),
            pl.BlockSpec((s // KC, LANES, KC), lambda bi, qi: (bi, 0, 0)),
        ],
        out_specs=pl.BlockSpec((TQ, ATT_WIDTH), tile),
        out_shape=jax.ShapeDtypeStruct((b * s, ATT_WIDTH), BF16),
        scratch_shapes=[
            pltpu.VMEM((nq, KC, TQ), F32),
            pltpu.VMEM((nq, KC, TQ), BF16),
            pltpu.VMEM((2, ATT_HEADS, KC, TQ), F32),
            pltpu.VMEM((ATT_HEADS, LANES, TQ), F32),
        ],
        compiler_params=pltpu.CompilerParams(
            dimension_semantics=("parallel", "arbitrary"), vmem_limit_bytes=VMEM_LIMIT),
        name="dsa_attention",
    )(q, iq, misc, k, ik, vt)


SSD_TT = 2048


def _ssd_kernel(xc_ref, zs_ref, misc_ref, dtb_ref, arep_ref, dskip_ref, nw_ref,
                expand_ref, triu_ref, o_ref, state_ref):
    t = pl.program_id(1)
    tt = xc_ref.shape[0]

    @pl.when(t == 0)
    def _():
        state_ref[...] = jnp.zeros_like(state_ref)

    tri = (lax.broadcasted_iota(jnp.int32, (CHUNK, CHUNK), 0)
           >= lax.broadcasted_iota(jnp.int32, (CHUNK, CHUNK), 1))
    left_head = lax.broadcasted_iota(jnp.int32, (CHUNK, LANES), 1) < SSD_HEAD_DIM
    left_head_n = lax.broadcasted_iota(jnp.int32, (D_STATE, LANES), 1) < SSD_HEAD_DIM
    gn = SSD_GROUPS * D_STATE
    pairs_per_group = SSD_HEADS // SSD_GROUPS // 2
    expand = expand_ref[...]
    triu = triu_ref[...]
    zpad = jnp.zeros((SUBLANES, CHUNK), F32)

    for c in range(tt // CHUNK):
        rows = slice(c * CHUNK, (c + 1) * CHUNK)
        raw = misc_ref[rows, :].T[MISC_DT:MISC_DT + SSD_HEADS, :] + dtb_ref[...]
        dt_t = jnp.maximum(raw, 0.0) + jnp.log1p(jnp.exp(-jnp.abs(raw)))
        adt = dt_t * arep_ref[...]
        hi = adt.astype(BF16).astype(F32)
        r1 = adt - hi
        mid = r1.astype(BF16).astype(F32)
        pieces = jnp.concatenate([hi, mid, r1 - mid, zpad], axis=0).astype(BF16)
        cs = jnp.dot(pieces, triu, preferred_element_type=F32)
        acum_t = (cs[0:SUBLANES] + cs[SUBLANES:2 * SUBLANES]) + cs[2 * SUBLANES:3 * SUBLANES]
        a_last = acum_t[:, CHUNK - 1:CHUNK]
        ddt_t = jnp.exp(a_last - acum_t) * dt_t
        acum = jnp.concatenate(
            [acum_t, jnp.zeros((LANES - SSD_HEADS, CHUNK), F32)], axis=0).T
        xs = xc_ref[rows, 0:SSD_WIDTH]
        bm = xc_ref[rows, SSD_WIDTH:SSD_WIDTH + gn]
        cm = xc_ref[rows, SSD_WIDTH + gn:SSD_WIDTH + 2 * gn]
        bm_t = bm.T
        xs16 = xs.astype(BF16)
        bm16 = bm.astype(BF16)
        cm16 = cm.astype(BF16)
        ea = jnp.exp(acum)
        ea_hi = ea.astype(BF16)
        ea_lo = (ea - ea_hi.astype(F32)).astype(BF16)
        ea_x = (jnp.dot(ea_hi, expand, preferred_element_type=F32)
                + jnp.dot(ea_lo, expand, preferred_element_type=F32))
        y_pairs = [None] * (SSD_HEADS // 2)
        for g in range(SSD_GROUPS):
            cg = cm16[:, g * D_STATE:(g + 1) * D_STATE]
            bg = bm16[:, g * D_STATE:(g + 1) * D_STATE]
            bg_t = bm_t[g * D_STATE:(g + 1) * D_STATE, :]
            gmat = _nt_dot(cg, bg)
            for pp in range(pairs_per_group):
                pair = g * pairs_per_group + pp
                lanes = slice(pair * LANES, (pair + 1) * LANES)
                xp = xs16[:, lanes]
                y_halves, s_halves = [], []
                for h in (2 * pair, 2 * pair + 1):
                    col = acum[:, h:h + 1]
                    rowv = acum_t[h:h + 1, :]
                    lmat = jnp.exp(jnp.where(tri, col - rowv, -jnp.inf))
                    mmat = (gmat * lmat * dt_t[h:h + 1, :]).astype(BF16)
                    y_halves.append(jnp.dot(mmat, xp, preferred_element_type=F32))
                    bs = (bg_t * ddt_t[h:h + 1, :]).astype(BF16)
                    s_halves.append(jnp.dot(bs, xp, preferred_element_type=F32))
                y_diag = jnp.where(left_head, y_halves[0], y_halves[1])
                new = jnp.where(left_head_n, s_halves[0], s_halves[1])
                prev = state_ref[pair]
                y_off = jnp.dot(cg, prev.astype(BF16), preferred_element_type=F32) * ea_x[:, lanes]
                state_ref[pair] = prev * ea_x[CHUNK - 1:CHUNK, lanes] + new
                y_pairs[pair] = y_diag + y_off
        y = jnp.concatenate(y_pairs, axis=1) + dskip_ref[...] * xs
        y = y * zs_ref[rows, :]
        ms = jnp.mean(y * y, axis=1, keepdims=True)
        o_ref[rows, :] = (y * lax.rsqrt(ms + LN_EPS) * nw_ref[...]).astype(BF16)


def _ssd(xc, zs, misc, dtb_rep, a_rep, dskip_row, nw_row, b, s):
    tt = min(SSD_TT, s)
    nt = s // tt
    tile = lambda bi, ti: (bi * nt + ti, 0)
    const = lambda bi, ti: (0, 0)
    expand = (jnp.arange(LANES)[:, None] == jnp.arange(SSD_WIDTH)[None, :] // SSD_HEAD_DIM).astype(BF16)
    triu = (jnp.arange(CHUNK)[:, None] <= jnp.arange(CHUNK)[None, :]).astype(BF16)
    return pl.pallas_call(
        _ssd_kernel,
        grid=(b, s // tt),
        in_specs=[
            pl.BlockSpec((tt, CONV_CH), tile),
            pl.BlockSpec((tt, SSD_WIDTH), tile),
            pl.BlockSpec((tt, LANES), tile),
            pl.BlockSpec((SSD_HEADS, CHUNK), const),
            pl.BlockSpec((SSD_HEADS, CHUNK), const),
            pl.BlockSpec((1, SSD_WIDTH), const),
            pl.BlockSpec((1, SSD_WIDTH), const),
            pl.BlockSpec((LANES, SSD_WIDTH), const),
            pl.BlockSpec((CHUNK, CHUNK), const),
        ],
        out_specs=pl.BlockSpec((tt, SSD_WIDTH), tile),
        out_shape=jax.ShapeDtypeStruct((b * s, SSD_WIDTH), BF16),
        scratch_shapes=[
            pltpu.VMEM((SSD_HEADS // 2, D_STATE, LANES), F32),
        ],
        compiler_params=pltpu.CompilerParams(
            dimension_semantics=("parallel", "arbitrary"), vmem_limit_bytes=VMEM_LIMIT),
        name="ssd_mixer",
    )(xc, zs, misc, dtb_rep, a_rep, dskip_row, nw_row, expand, triu)


def _layer_norm(y, g, b):
    mu = jnp.mean(y, axis=1, keepdims=True)
    yc = y - mu
    var = jnp.mean(yc * yc, axis=1, keepdims=True)
    return yc * lax.rsqrt(var + LN_EPS) * g + b


OUT_TM = 1024
OUT_SUB = 256


def _out_proj_kernel(att_ref, ssd_ref, x_ref, wa_ref, ws_ref, g_ref, b_ref, h_ref):
    for r0 in range(0, x_ref.shape[0], OUT_SUB):
        rows = slice(r0, r0 + OUT_SUB)
        mixed = jnp.dot(att_ref[rows, :], wa_ref[...], preferred_element_type=F32)
        mixed = mixed + jnp.dot(ssd_ref[rows, :], ws_ref[...], preferred_element_type=F32)
        h_ref[rows, :] = _layer_norm(ALPHA * x_ref[rows, :] + mixed, g_ref[...], b_ref[...])


def _out_proj(att2, ssd2, x2, w_att, w_ssd, g_row, b_row, tm):
    n = x2.shape[0]
    row = lambda i: (i, 0)
    const = lambda i: (0, 0)
    return pl.pallas_call(
        _out_proj_kernel,
        grid=(n // tm,),
        in_specs=[
            pl.BlockSpec((tm, ATT_WIDTH), row),
            pl.BlockSpec((tm, SSD_WIDTH), row),
            pl.BlockSpec((tm, D_MODEL), row),
            pl.BlockSpec((ATT_WIDTH, D_MODEL), const),
            pl.BlockSpec((SSD_WIDTH, D_MODEL), const),
            pl.BlockSpec((1, D_MODEL), const),
            pl.BlockSpec((1, D_MODEL), const),
        ],
        out_specs=pl.BlockSpec((tm, D_MODEL), row),
        out_shape=jax.ShapeDtypeStruct((n, D_MODEL), F32),
        compiler_params=pltpu.CompilerParams(
            dimension_semantics=("parallel",), vmem_limit_bytes=VMEM_LIMIT),
        name="out_proj_ln",
    )(att2, ssd2, x2, w_att, w_ssd, g_row, b_row)


ROUTE_E0 = N_GROUPS_MOE
MOE_TM = 512
MOE_CAP = 160


def _first_max(vals):
    best = vals[0]
    for v in vals[1:]:
        best = jnp.maximum(best, v)
    idx = jnp.full(best.shape, len(vals) - 1, jnp.int32)
    for i in range(len(vals) - 2, -1, -1):
        idx = jnp.where(vals[i] == best, i, idx)
    return best, idx


def _moe_kernel(h_ref, wrh_ref, wrl_ref, br_ref, wg_ref, wu_ref, wd_ref, g_ref, b_ref, upper_ref,
                o_ref, hb_ref, hp_ref, gp_ref, yp_ref):
    tm = h_ref.shape[0]
    h = h_ref[...]
    h_hi = h.astype(BF16)
    hb_ref[...] = h_hi
    h_lo = (h - h_hi.astype(F32)).astype(BF16)
    wrh = wrh_ref[...]
    both = jnp.dot(h_hi, jnp.concatenate([wrh, wrl_ref[...]], axis=1), preferred_element_type=F32)
    logits = (both[:, :LANES] + jnp.dot(h_lo, wrh, preferred_element_type=F32)
              + both[:, LANES:]) + br_ref[...]
    lt = logits.T
    row = lambda r: lt[r:r + 1, :]
    gl = [row(r) for r in range(N_GROUPS_MOE)]
    gmax, gidx = _first_max(gl)
    denom = jnp.exp(gl[0] - gmax)
    for v in gl[1:]:
        denom = denom + jnp.exp(v - gmax)
    gprob = 1.0 / denom
    el = []
    for k in range(EXPERTS_PER_GROUP):
        v = row(ROUTE_E0 + (N_GROUPS_MOE - 1) * EXPERTS_PER_GROUP + k)
        for gg in range(N_GROUPS_MOE - 2, -1, -1):
            v = jnp.where(gidx == gg, row(ROUTE_E0 + gg * EXPERTS_PER_GROUP + k), v)
        el.append(v)
    l1, i1 = _first_max(el)
    l2, i2 = _first_max([jnp.where(i1 == k, -jnp.inf, el[k]) for k in range(EXPERTS_PER_GROUP)])
    e2 = jnp.exp(l2 - l1)
    w1 = gprob / (1.0 + e2)
    w2 = gprob * e2 / (1.0 + e2)
    gate4 = [jnp.where(i1 == k, w1, jnp.where(i2 == k, w2, 0.0)) for k in range(EXPERTS_PER_GROUP)]

    member = [jnp.where(gidx == gg, 1.0, 0.0) for gg in range(N_GROUPS_MOE)]
    member_blk = jnp.concatenate(
        member + [jnp.zeros((2 * SUBLANES - N_GROUPS_MOE, tm), F32)], axis=0).astype(BF16)
    earlier = jnp.dot(member_blk, upper_ref[...], preferred_element_type=F32)
    counts = [jnp.sum(m, axis=1, keepdims=True) for m in member]
    most = counts[0]
    for c in counts[1:]:
        most = jnp.maximum(most, c)
    fits = jnp.max(most) <= float(MOE_CAP)
    slot = member[0] * earlier[0:1, :]
    for gg in range(1, N_GROUPS_MOE):
        slot = slot + member[gg] * (earlier[gg:gg + 1, :] + float(gg * MOE_CAP))
    tok = jnp.concatenate(
        gate4 + [slot, gidx.astype(F32), jnp.zeros((LANES - EXPERTS_PER_GROUP - 2, tm), F32)],
        axis=0).T
    slot_lane, grp_lane = EXPERTS_PER_GROUP, EXPERTS_PER_GROUP + 1
    rows_p = N_GROUPS_MOE * MOE_CAP

    @pl.when(fits)
    def _():
        hb = hb_ref[...]
        place = jnp.where(lax.broadcasted_iota(jnp.int32, (rows_p, tm), 0) == slot.astype(jnp.int32),
                          1.0, 0.0).astype(BF16)
        hp_ref[...] = jnp.dot(place, hb, preferred_element_type=F32).astype(BF16)
        t_hi = tok.astype(BF16)
        t_lo = (tok - t_hi.astype(F32)).astype(BF16)
        both = jnp.dot(place, jnp.concatenate([t_hi, t_lo], axis=1), preferred_element_type=F32)
        gp_ref[...] = both[:, :LANES] + both[:, LANES:]
        for gg in range(N_GROUPS_MOE):
            rows = slice(gg * MOE_CAP, (gg + 1) * MOE_CAP)
            xg = hp_ref[rows, :]
            acc = None
            for k in range(EXPERTS_PER_GROUP):
                e = gg * EXPERTS_PER_GROUP + k
                a = jnp.dot(xg, wg_ref[e], preferred_element_type=F32)
                u = jnp.dot(xg, wu_ref[e], preferred_element_type=F32)
                hid = (_silu(a) * u * gp_ref[rows, k:k + 1]).astype(BF16)
                part = jnp.dot(hid, wd_ref[e], preferred_element_type=F32)
                acc = part if acc is None else acc + part
            yp_ref[rows, :] = acc.astype(BF16)
        back = jnp.where(lax.broadcasted_iota(jnp.int32, (tm, rows_p), 1)
                         == tok[:, slot_lane:slot_lane + 1].astype(jnp.int32), 1.0, 0.0).astype(BF16)
        for r0 in range(0, tm, OUT_SUB):
            rr = slice(r0, r0 + OUT_SUB)
            y = jnp.dot(back[rr, :], yp_ref[...], preferred_element_type=F32)
            o_ref[rr, :] = _layer_norm(ALPHA * h_ref[rr, :] + y, g_ref[...], b_ref[...])

    @pl.when(jnp.logical_not(fits))
    def _():
        hb = hb_ref[...]
        acc = None
        for gg in range(N_GROUPS_MOE):
            in_grp = tok[:, grp_lane:grp_lane + 1] == float(gg)
            for k in range(EXPERTS_PER_GROUP):
                e = gg * EXPERTS_PER_GROUP + k
                a = jnp.dot(hb, wg_ref[e], preferred_element_type=F32)
                u = jnp.dot(hb, wu_ref[e], preferred_element_type=F32)
                gate = jnp.where(in_grp, tok[:, k:k + 1], 0.0)
                hid = (_silu(a) * u * gate).astype(BF16)
                part = jnp.dot(hid, wd_ref[e], preferred_element_type=F32)
                acc = part if acc is None else acc + part
        o_ref[...] = _layer_norm(ALPHA * h_ref[...] + acc, g_ref[...], b_ref[...])


def _moe(h2, wr_hi, wr_lo, br, wg, wu, wd, g_row, b_row):
    n = h2.shape[0]
    tm = min(MOE_TM, n)
    rows_p = N_GROUPS_MOE * MOE_CAP
    row = lambda i: (i, 0)
    const = lambda i: (0, 0)
    whole = lambda i: (0, 0, 0)
    once = pl.Buffered(1)
    upper = (jnp.arange(tm)[:, None] < jnp.arange(tm)[None, :]).astype(BF16)
    return pl.pallas_call(
        _moe_kernel,
        grid=(n // tm,),
        in_specs=[
            pl.BlockSpec((tm, D_MODEL), row),
            pl.BlockSpec((D_MODEL, LANES), const),
            pl.BlockSpec((D_MODEL, LANES), const),
            pl.BlockSpec((1, LANES), const),
            pl.BlockSpec((N_EXPERTS, D_MODEL, EXPERT_FF), whole, pipeline_mode=once),
            pl.BlockSpec((N_EXPERTS, D_MODEL, EXPERT_FF), whole, pipeline_mode=once),
            pl.BlockSpec((N_EXPERTS, EXPERT_FF, D_MODEL), whole, pipeline_mode=once),
            pl.BlockSpec((1, D_MODEL), const),
            pl.BlockSpec((1, D_MODEL), const),
            pl.BlockSpec((tm, tm), const, pipeline_mode=once),
        ],
        out_specs=pl.BlockSpec((tm, D_MODEL), row),
        out_shape=jax.ShapeDtypeStruct((n, D_MODEL), F32),
        scratch_shapes=[
            pltpu.VMEM((tm, D_MODEL), BF16),
            pltpu.VMEM((rows_p, D_MODEL), BF16),
            pltpu.VMEM((rows_p, LANES), F32),
            pltpu.VMEM((rows_p, D_MODEL), BF16),
        ],
        compiler_params=pltpu.CompilerParams(
            dimension_semantics=("parallel",), vmem_limit_bytes=VMEM_LIMIT),
        name="hier_moe_ln",
    )(h2, wr_hi, wr_lo, br, wg, wu, wd, g_row, b_row, upper)


def _rope_tables(seq):
    inv = ROPE_THETA ** (-jnp.arange(0, HEAD_DIM, 2, dtype=F32) / HEAD_DIM)
    ang = jnp.arange(seq, dtype=F32)[:, None] * inv[None, :]
    cos, sin = jnp.cos(ang), jnp.sin(ang)
    zero = jnp.zeros_like(sin)
    cos_t = jnp.tile(cos, (1, LANES // (HEAD_DIM // 2)))
    s1_t = jnp.tile(jnp.concatenate([-sin, zero], 1), (1, LANES // HEAD_DIM))
    s2_t = jnp.tile(jnp.concatenate([zero, sin], 1), (1, LANES // HEAD_DIM))
    return cos_t, s1_t, s2_t


def _permute_w_in(w):
    sizes = (ATT_WIDTH, HEAD_DIM, HEAD_DIM, IDX_HEADS * IDX_DIM, IDX_DIM, IDX_HEADS,
             SSD_WIDTH, CONV_CH, SSD_HEADS)
    pts = np.cumsum((0,) + sizes)
    q, k, v, iq, ik, iw, z, xbc, dt = [w[:, pts[i]:pts[i + 1]] for i in range(len(sizes))]
    d = w.shape[0]
    pad = lambda n: jnp.zeros((d, n), w.dtype)
    misc = jnp.concatenate([v, iw, pad(MISC_DT - MISC_IW - IDX_HEADS), dt,
                            pad(LANES - MISC_DT - SSD_HEADS)], 1)
    return jnp.concatenate([q, iq, k, ik, misc, z, xbc], 1).astype(BF16)


def _head_rep(vals):
    return jnp.broadcast_to(vals.astype(F32)[:, None], (vals.shape[0], CHUNK))


def kernel(x, w_in, conv_w, conv_b, dt_bias, a_log, d_skip, ssd_norm_w, w_out, ln1_g, ln1_b,
           w_route_group, b_route_group, w_route_expert, b_route_expert, w_gate, w_up,
           w_down, ln2_g, ln2_b):
    bsz, seq, d = x.shape
    n = bsz * seq
    topk = min(TOPK_MAX, seq // 4)
    tm = 512
    assert d == D_MODEL and TQ == KC and seq % TQ == 0 and seq % tm == 0 and topk <= KC
    assert seq // BF16_ROWS <= 256
    assert (FINE_PASSES - FINE_FIXED_PASSES) % 2 == 0
    cos_t, s1_t, s2_t = _rope_tables(seq)
    mscale = jnp.ones((1, LANES), F32).at[0, MISC_IW:MISC_IW + IDX_HEADS].set(INDEXER_SCALE)
    for l in range(DEPTH):
        x2 = x.reshape(n, d)
        q, iq, k, ik, vt, misc, zs, xc = _in_proj(
            x2, _permute_w_in(w_in[l]), cos_t, s1_t, s2_t, mscale,
            conv_w[l].astype(F32), conv_b[l].astype(F32)[None, :], seq, tm)
        att = _dsa(q, iq, misc, k, ik, vt, bsz, seq, topk)
        ssd = _ssd(
            xc, zs, misc,
            _head_rep(dt_bias[l]), _head_rep(-jnp.exp(a_log[l].astype(F32))),
            jnp.repeat(d_skip[l].astype(F32), SSD_HEAD_DIM)[None, :], ssd_norm_w[l][None, :],
            bsz, seq)
        w_o = w_out[l].astype(BF16)
        h2 = _out_proj(att, ssd, x2, w_o[:ATT_WIDTH], w_o[ATT_WIDTH:],
                       ln1_g[l][None, :], ln1_b[l][None, :], min(OUT_TM, n))
        pad = jnp.zeros((d, LANES - ROUTE_E0 - N_EXPERTS), F32)
        wr = jnp.concatenate([w_route_group[l].astype(F32), w_route_expert[l].astype(F32), pad], 1)
        wr_hi = wr.astype(BF16)
        wr_lo = (wr - wr_hi.astype(F32)).astype(BF16)
        br = jnp.concatenate([b_route_group[l].astype(F32), b_route_expert[l].astype(F32),
                              pad[0]])[None, :]
        x = _moe(h2, wr_hi, wr_lo, br, w_gate[l].astype(BF16), w_up[l].astype(BF16),
                 w_down[l].astype(BF16), ln2_g[l][None, :], ln2_b[l][None, :]).reshape(bsz, seq, d)
    return x
```

```python
import functools

import jax
import jax.numpy as jnp
import numpy as np
from jax import lax
from jax.experimental import pallas as pl
from jax.experimental.pallas import tpu as pltpu

F32 = jnp.float32
BF16 = jnp.bfloat16

D_MODEL = 1024
HEAD_DIM = 64
ATT_WIDTH = 512
ATT_HEADS = 8
IDX_HEADS = 4
IDX_DIM = 64
TOPK_MAX = 256
ROPE_THETA = 10000.0
INDEXER_SCALE = (IDX_HEADS ** -0.5) * (IDX_DIM ** -0.5)
SSD_WIDTH = 512
SSD_HEADS = 8
SSD_HEAD_DIM = 64
SSD_GROUPS = 2
D_STATE = 64
CONV_WIDTH = 4
CONV_CH = SSD_WIDTH + 2 * SSD_GROUPS * D_STATE
CHUNK = 128
N_GROUPS_MOE = 4
EXPERTS_PER_GROUP = 4
N_EXPERTS = 16
EXPERT_FF = 256
DEPTH = 1
ALPHA = (2 * DEPTH) ** 0.25
LN_EPS = 1e-5

LANES = 128
SUBLANES = 8
BF16_ROWS = 16
INF_KEY = 0x7F800000
MIN_NORMAL_KEY = 0x00800000
FINE_PASSES = 17
FINE_FIXED_PASSES = 9
KEY_STEP16 = 1 << 16
VMEM_LIMIT = 56 * 1024 * 1024

C_Q = 0
C_IQ = 512
C_K = 768
C_IK = 832
C_MISC = 896
MISC_IW = 64
MISC_DT = 72
C_Z = 1024
C_XBC = 1536
IN_COLS = 2304

TQ = 256
KC = 256

Q_SCALE = HEAD_DIM ** -0.5 * float(np.log2(np.e))

NT_DIMS = (((1,), (1,)), ((), ()))


def _nt_dot(a, b):
    return lax.dot_general(a, b, NT_DIMS, preferred_element_type=F32)


def _fold_rows(x, op, rows=SUBLANES):
    slabs = [x[r * rows:(r + 1) * rows, :] for r in range(x.shape[0] // rows)]
    while len(slabs) > 1:
        nxt = [op(slabs[i], slabs[i + 1]) for i in range(0, len(slabs) - 1, 2)]
        if len(slabs) % 2:
            nxt.append(slabs[-1])
        slabs = nxt
    return slabs[0]


CONV_PAD = 8
PROJ_COLS = 256


def _silu(x):
    return x * (1.0 / (1.0 + jnp.exp(-x)))


def _in_proj_kernel(x_ref, w_ref, cos_ref, s1_ref, s2_ref, mscale_ref, cw_ref, cb_ref,
                    q_ref, iq_ref, k_ref, ik_ref, vt_ref, misc_ref, zs_ref, xc_ref, hist_ref,
                    *, nblk_seq):
    @pl.when(pl.program_id(0) == 0)
    def _():
        hist_ref[...] = jnp.zeros_like(hist_ref)

    xb = x_ref[...].astype(BF16)
    cos = cos_ref[...]
    s1 = s1_ref[...]
    s2 = s2_ref[...]

    def mm(c0, width):
        return jnp.dot(xb, w_ref[:, c0:c0 + width], preferred_element_type=F32)

    def rope(y):
        fwd = pltpu.roll(y, LANES - HEAD_DIM // 2, 1)
        bwd = pltpu.roll(y, HEAD_DIM // 2, 1)
        return y * cos + fwd * s1 + bwd * s2

    def rope_wide(y, scale):
        parts = []
        for c in range(y.shape[1] // LANES):
            r = rope(y[:, c * LANES:(c + 1) * LANES])
            parts.append(r * scale if scale != 1.0 else r)
        return parts

    tm = x_ref.shape[0]
    seq_start = pl.program_id(0) % nblk_seq == 0
    for c0 in range(0, CONV_CH, PROJ_COLS):
        cols = slice(c0, c0 + PROJ_COLS)
        xbc = mm(C_XBC + c0, PROJ_COLS)
        hist = jnp.where(seq_start, 0.0, hist_ref[:, cols])
        xp = jnp.concatenate([hist, xbc], axis=0)
        acc = cw_ref[0:1, cols] * xp
        for j in range(1, CONV_WIDTH):
            acc = pltpu.roll(acc, 1, 0) + cw_ref[j:j + 1, cols] * xp
        xc_ref[:, cols] = _silu(acc[CONV_PAD:, :] + cb_ref[:, cols])
        hist_ref[:, cols] = xbc[tm - CONV_PAD:, :]
    for c0 in range(0, SSD_WIDTH, PROJ_COLS):
        zs_ref[:, c0:c0 + PROJ_COLS] = _silu(mm(C_Z + c0, PROJ_COLS))

    for c, r in enumerate(rope_wide(mm(C_Q, ATT_WIDTH), Q_SCALE)):
        q_ref[:, c * LANES:(c + 1) * LANES] = r.astype(BF16)
    for c, r in enumerate(rope_wide(mm(C_IQ, IDX_HEADS * IDX_DIM), 1.0)):
        iq_ref[:, c * LANES:(c + 1) * LANES] = r.astype(BF16)
    kk = rope(mm(C_K, LANES))
    k_ref[...] = kk[:, :HEAD_DIM].astype(BF16)
    ik_ref[...] = kk[:, HEAD_DIM:].astype(BF16)
    misc = mm(C_MISC, LANES) * mscale_ref[...]
    misc_ref[...] = misc
    lane = lax.broadcasted_iota(jnp.int32, misc.shape, 1)
    vext = jnp.where(lane < HEAD_DIM, misc, jnp.where(lane == HEAD_DIM, 1.0, 0.0))
    for c in range(vt_ref.shape[0]):
        vt_ref[c] = vext[c * KC:(c + 1) * KC, :].T.astype(BF16)


def _in_proj(x2, w_perm, cos_t, s1_t, s2_t, mscale, conv_w, conv_b, seq, tm):
    n = x2.shape[0]
    nblk_seq = seq // tm
    row = lambda i: (i, 0)
    tab = lambda i: (i % nblk_seq, 0)
    const = lambda i: (0, 0)
    outs = [
        ((n, ATT_WIDTH), BF16, pl.BlockSpec((tm, ATT_WIDTH), row)),
        ((n, IDX_HEADS * IDX_DIM), BF16, pl.BlockSpec((tm, IDX_HEADS * IDX_DIM), row)),
        ((n, HEAD_DIM), BF16, pl.BlockSpec((tm, HEAD_DIM), row)),
        ((n, IDX_DIM), BF16, pl.BlockSpec((tm, IDX_DIM), row)),
        ((n // KC, LANES, KC), BF16, pl.BlockSpec((tm // KC, LANES, KC), lambda i: (i, 0, 0))),
        ((n, LANES), F32, pl.BlockSpec((tm, LANES), row)),
        ((n, SSD_WIDTH), F32, pl.BlockSpec((tm, SSD_WIDTH), row)),
        ((n, CONV_CH), F32, pl.BlockSpec((tm, CONV_CH), row)),
    ]
    return pl.pallas_call(
        functools.partial(_in_proj_kernel, nblk_seq=nblk_seq),
        grid=(n // tm,),
        in_specs=[
            pl.BlockSpec((tm, D_MODEL), row),
            pl.BlockSpec((D_MODEL, IN_COLS), const),
            pl.BlockSpec((tm, LANES), tab),
            pl.BlockSpec((tm, LANES), tab),
            pl.BlockSpec((tm, LANES), tab),
            pl.BlockSpec((1, LANES), const),
            pl.BlockSpec((CONV_WIDTH, CONV_CH), const),
            pl.BlockSpec((1, CONV_CH), const),
        ],
        out_specs=[spec for _, _, spec in outs],
        out_shape=[jax.ShapeDtypeStruct(shape, dt) for shape, dt, _ in outs],
        scratch_shapes=[pltpu.VMEM((CONV_PAD, CONV_CH), F32)],
        compiler_params=pltpu.CompilerParams(
            dimension_semantics=("arbitrary",), vmem_limit_bytes=VMEM_LIMIT),
        name="in_proj",
    )(x2, w_perm, cos_t, s1_t, s2_t, mscale, conv_w, conv_b)


def _dsa_kernel(q_ref, iq_ref, misc_ref, k_ref, ik_ref, vt_ref, o_ref,
                sc_ref, sc16_ref, lg_ref, acc_ref, *, topk):
    qi = pl.program_id(1)
    nj = qi + 1
    neg_inf = -jnp.inf
    kf = float(topk)
    key_i = lax.broadcasted_iota(jnp.int32, (KC, TQ), 0)
    qry_i = lax.broadcasted_iota(jnp.int32, (KC, TQ), 1)
    causal = key_i <= qry_i

    def key_rows(j):
        return pl.ds(pl.multiple_of(j * KC, KC), KC)

    iw_t = misc_ref[...].T[MISC_IW:MISC_IW + SUBLANES, :]

    def chunk_scores(j):
        ikj = ik_ref[key_rows(j), :]
        sc = None
        for h in range(IDX_HEADS):
            d = _nt_dot(ikj, iq_ref[:, h * IDX_DIM:(h + 1) * IDX_DIM])
            term = iw_t[h:h + 1, :] * jnp.maximum(d, 0.0)
            sc = term if sc is None else sc + term
        sc_ref[j] = sc
        sc16_ref[j] = sc.astype(BF16)

    def scores_body(p, carry):
        chunk_scores(2 * p)
        chunk_scores(jnp.minimum(2 * p + 1, qi))
        return carry

    lax.fori_loop(0, (nj + 1) // 2, scores_body, 0)
    sc_diag = jnp.where(causal, sc_ref[qi], neg_inf)
    sc_ref[qi] = sc_diag
    sc16_ref[qi] = sc_diag.astype(BF16)

    def key_to_float(key):
        mag = jnp.where(key < 0, -key, key)
        sub = jnp.logical_and(mag > 0, mag < MIN_NORMAL_KEY)
        mag = jnp.where(sub, jnp.where(key < 0, 0, MIN_NORMAL_KEY), mag)
        f = lax.bitcast_convert_type(mag, F32)
        return jnp.where(key < 0, -f, f)

    one16 = jnp.ones((), BF16)
    zero16 = jnp.zeros((), BF16)

    nj_search = jnp.where(qi == 0, 0, nj) if topk >= TQ else nj

    def count16_ge(cand16):
        def body(j, acc):
            ind = jnp.where(sc16_ref[j] >= cand16, one16, zero16)
            return acc + _fold_rows(ind, jnp.add, BF16_ROWS)

        acc = lax.fori_loop(0, nj_search, body, jnp.zeros((BF16_ROWS, TQ), BF16))
        return jnp.sum(acc.astype(F32), axis=0, keepdims=True)

    def count_ge(cand, chunks=nj_search):
        def body(j, acc):
            ind = jnp.where(sc_ref[j] >= cand, 1.0, 0.0)
            return acc + _fold_rows(ind, jnp.add)

        acc = lax.fori_loop(0, chunks, body, jnp.zeros((SUBLANES, TQ), F32))
        return jnp.sum(acc, axis=0, keepdims=True)

    def coarse_body(b, m):
        trial = m + lax.shift_left(jnp.int32(1), jnp.int32(15) - b)
        cand16 = key_to_float(trial * KEY_STEP16).astype(BF16)
        return jnp.where(count16_ge(cand16) >= kf, trial, m)

    m16 = lax.fori_loop(0, 16, coarse_body, jnp.full((1, TQ), -(INF_KEY // KEY_STEP16), jnp.int32))
    key_base = jnp.maximum(m16 * KEY_STEP16 - (KEY_STEP16 // 2 + 1), -INF_KEY)

    min_normal = lax.bitcast_convert_type(jnp.int32(MIN_NORMAL_KEY), F32)

    def fine_step(b, state):
        off, cur_count, reject = state
        trial = off + lax.shift_left(jnp.int32(1), jnp.int32(16) - b)
        cand = key_to_float(key_base + trial)
        cnt = count_ge(cand)
        ok = cnt >= kf
        return (jnp.where(ok, trial, off), jnp.where(ok, cnt, cur_count),
                jnp.where(ok, reject, jnp.minimum(reject, cand)))

    def zero_tie(off, reject):
        return jnp.logical_and(key_to_float(key_base + off) == 0.0, reject <= min_normal)

    def unsettled(state):
        off, cur_count, reject = state
        done = jnp.logical_or(cur_count == kf, zero_tie(off, reject))
        return jnp.max(jnp.where(done, 0.0, 1.0)) > 0.0

    unknown = jnp.maximum(key_base.astype(F32), jnp.inf)
    state = lax.fori_loop(0, FINE_FIXED_PASSES, fine_step,
                          (jnp.zeros((1, TQ), jnp.int32), unknown, unknown))

    def more_cond(carry):
        b, go, _ = carry
        return jnp.logical_and(b < FINE_PASSES, go)

    def more_body(carry):
        b, _, state = carry
        state = fine_step(b + 1, fine_step(b, state))
        return b + 2, unsettled(state), state

    _, _, state = lax.while_loop(more_cond, more_body,
                                 (jnp.int32(FINE_FIXED_PASSES), unsettled(state), state))
    off, _, reject = state
    off = jnp.where(zero_tie(off, reject), -key_base, off)
    lo = key_to_float(key_base + off)
    hi = key_to_float(key_base + off + 1)
    need = kf - count_ge(hi, nj)

    lower = (lax.broadcasted_iota(jnp.int32, (KC, KC), 0)
             > lax.broadcasted_iota(jnp.int32, (KC, KC), 1)).astype(BF16)

    acc_ref[...] = jnp.zeros_like(acc_ref)
    no_max = tuple(jnp.minimum(need, neg_inf) for _ in range(ATT_HEADS))

    def logits_stage(j, taken):
        s = sc_ref[j]
        eqf = jnp.where(s >= hi, 0.0, jnp.where(s >= lo, 1.0, 0.0))
        before = jnp.dot(lower, eqf.astype(BF16), preferred_element_type=F32) + taken
        tie_bias = jnp.where(before < need, 0.0, neg_inf)
        bias = jnp.where(s >= hi, 0.0, jnp.where(s >= lo, tie_bias, neg_inf))
        bias = jnp.where(j < qi, bias, jnp.where(causal, bias, neg_inf))
        taken = taken + jnp.sum(_fold_rows(eqf, jnp.add), axis=0, keepdims=True)
        kj = k_ref[key_rows(j), :]
        slot = j % 2
        cms = []
        for h in range(ATT_HEADS):
            lg = _nt_dot(kj, q_ref[:, h * HEAD_DIM:(h + 1) * HEAD_DIM]) + bias
            lg_ref[slot, h] = lg
            cms.append(jnp.max(_fold_rows(lg, jnp.maximum), axis=0, keepdims=True))
        return taken, tuple(cms)

    def pv_stage(jp, cms, ms):
        vtj = vt_ref[jp]
        slot = jp % 2
        new_ms = []
        for h in range(ATT_HEADS):
            m_new = jnp.maximum(ms[h], cms[h])
            shift = jnp.where(m_new == neg_inf, 0.0, m_new)
            p = jnp.exp2(lg_ref[slot, h] - shift).astype(BF16)
            alpha = jnp.exp2(ms[h] - shift)
            acc_ref[h] = alpha * acc_ref[h] + jnp.dot(vtj, p, preferred_element_type=F32)
            new_ms.append(m_new)
        return tuple(new_ms)

    def att_body(j, carry):
        taken, cms_prev, ms = carry
        ms = pv_stage(j - 1, cms_prev, ms)
        taken, cms = logits_stage(j, taken)
        return taken, cms, ms

    taken0, cms0 = logits_stage(0, jnp.zeros((1, TQ), F32))
    _, cms_last, ms = lax.fori_loop(1, nj, att_body, (taken0, cms0, no_max))
    pv_stage(qi, cms_last, ms)
    for pair in range(ATT_HEADS // 2):
        halves = []
        for h in (2 * pair, 2 * pair + 1):
            a = acc_ref[h]
            halves.append(a[:HEAD_DIM, :] * (1.0 / a[HEAD_DIM:HEAD_DIM + 1, :]))
        blk = jnp.concatenate(halves, axis=0)
        o_ref[:, pair * LANES:(pair + 1) * LANES] = blk.T.astype(BF16)


def _dsa(q, iq, misc, k, ik, vt, b, s, topk):
    nq = s // TQ
    tile = lambda bi, qi: (bi * nq + qi, 0)
    full = lambda bi, qi: (bi, 0)
    return pl.pallas_call(
        functools.partial(_dsa_kernel, topk=topk),
        grid=(b, nq),
        in_specs=[
            pl.BlockSpec((TQ, ATT_WIDTH), tile),
            pl.BlockSpec((TQ, IDX_HEADS * IDX_DIM), tile),
            pl.BlockSpec((TQ, LANES), tile),
            pl.BlockSpec((s, HEAD_DIM), full),
            pl.BlockSpec((s, IDX_DIM), full),
            pl.BlockSpec((s // KC, LANES, KC), lambda bi, qi: (bi, 0, 0)),
        ],
        out_specs=pl.BlockSpec((TQ, ATT_WIDTH), tile),
        out_shape=jax.ShapeDtypeStruct((b * s, ATT_WIDTH), BF16),
        scratch_shapes=[
            pltpu.VMEM((nq, KC, TQ), F32),
            pltpu.VMEM((nq, KC, TQ), BF16),
            pltpu.VMEM((2, ATT_HEADS, KC, TQ), F32),
            pltpu.VMEM((ATT_HEADS, LANES, TQ), F32),
        ],
        compiler_params=pltpu.CompilerParams(
            dimension_semantics=("parallel", "arbitrary"), vmem_limit_bytes=VMEM_LIMIT),
        name="dsa_attention",
    )(q, iq, misc, k, ik, vt)


SSD_TT = 2048


def _ssd_kernel(xc_ref, zs_ref, misc_ref, dtb_ref, arep_ref, dskip_ref, nw_ref,
                expand_ref, triu_ref, o_ref, state_ref):
    t = pl.program_id(1)
    tt = xc_ref.shape[0]

    @pl.when(t == 0)
    def _():
        state_ref[...] = jnp.zeros_like(state_ref)

    tri = (lax.broadcasted_iota(jnp.int32, (CHUNK, CHUNK), 0)
           >= lax.broadcasted_iota(jnp.int32, (CHUNK, CHUNK), 1))
    left_head = lax.broadcasted_iota(jnp.int32, (CHUNK, LANES), 1) < SSD_HEAD_DIM
    left_head_n = lax.broadcasted_iota(jnp.int32, (D_STATE, LANES), 1) < SSD_HEAD_DIM
    gn = SSD_GROUPS * D_STATE
    pairs_per_group = SSD_HEADS // SSD_GROUPS // 2
    expand = expand_ref[...]
    triu = triu_ref[...]
    zpad = jnp.zeros((SUBLANES, CHUNK), F32)

    for c in range(tt // CHUNK):
        rows = slice(c * CHUNK, (c + 1) * CHUNK)
        raw = misc_ref[rows, :].T[MISC_DT:MISC_DT + SSD_HEADS, :] + dtb_ref[...]
        dt_t = jnp.maximum(raw, 0.0) + jnp.log1p(jnp.exp(-jnp.abs(raw)))
        adt = dt_t * arep_ref[...]
        hi = adt.astype(BF16).astype(F32)
        r1 = adt - hi
        mid = r1.astype(BF16).astype(F32)
        pieces = jnp.concatenate([hi, mid, r1 - mid, zpad], axis=0).astype(BF16)
        cs = jnp.dot(pieces, triu, preferred_element_type=F32)
        acum_t = (cs[0:SUBLANES] + cs[SUBLANES:2 * SUBLANES]) + cs[2 * SUBLANES:3 * SUBLANES]
        a_last = acum_t[:, CHUNK - 1:CHUNK]
        ddt_t = jnp.exp(a_last - acum_t) * dt_t
        acum = jnp.concatenate(
            [acum_t, jnp.zeros((LANES - SSD_HEADS, CHUNK), F32)], axis=0).T
        xs = xc_ref[rows, 0:SSD_WIDTH]
        bm = xc_ref[rows, SSD_WIDTH:SSD_WIDTH + gn]
        cm = xc_ref[rows, SSD_WIDTH + gn:SSD_WIDTH + 2 * gn]
        bm_t = bm.T
        xs16 = xs.astype(BF16)
        bm16 = bm.astype(BF16)
        cm16 = cm.astype(BF16)
        ea = jnp.exp(acum)
        ea_hi = ea.astype(BF16)
        ea_lo = (ea - ea_hi.astype(F32)).astype(BF16)
        ea_x = (jnp.dot(ea_hi, expand, preferred_element_type=F32)
                + jnp.dot(ea_lo, expand, preferred_element_type=F32))
        y_pairs = [None] * (SSD_HEADS // 2)
        for g in range(SSD_GROUPS):
            cg = cm16[:, g * D_STATE:(g + 1) * D_STATE]
            bg = bm16[:, g * D_STATE:(g + 1) * D_STATE]
            bg_t = bm_t[g * D_STATE:(g + 1) * D_STATE, :]
            gmat = _nt_dot(cg, bg)
            for pp in range(pairs_per_group):
                pair = g * pairs_per_group + pp
                lanes = slice(pair * LANES, (pair + 1) * LANES)
                xp = xs16[:, lanes]
                y_halves, s_halves = [], []
                for h in (2 * pair, 2 * pair + 1):
                    col = acum[:, h:h + 1]
                    rowv = acum_t[h:h + 1, :]
                    lmat = jnp.exp(jnp.where(tri, col - rowv, -jnp.inf))
                    mmat = (gmat * lmat * dt_t[h:h + 1, :]).astype(BF16)
                    y_halves.append(jnp.dot(mmat, xp, preferred_element_type=F32))
                    bs = (bg_t * ddt_t[h:h + 1, :]).astype(BF16)
                    s_halves.append(jnp.dot(bs, xp, preferred_element_type=F32))
                y_diag = jnp.where(left_head, y_halves[0], y_halves[1])
                new = jnp.where(left_head_n, s_halves[0], s_halves[1])
                prev = state_ref[pair]
                y_off = jnp.dot(cg, prev.astype(BF16), preferred_element_type=F32) * ea_x[:, lanes]
                state_ref[pair] = prev * ea_x[CHUNK - 1:CHUNK, lanes] + new
                y_pairs[pair] = y_diag + y_off
        y = jnp.concatenate(y_pairs, axis=1) + dskip_ref[...] * xs
        y = y * zs_ref[rows, :]
        ms = jnp.mean(y * y, axis=1, keepdims=True)
        o_ref[rows, :] = (y * lax.rsqrt(ms + LN_EPS) * nw_ref[...]).astype(BF16)


def _ssd(xc, zs, misc, dtb_rep, a_rep, dskip_row, nw_row, b, s):
    tt = min(SSD_TT, s)
    nt = s // tt
    tile = lambda bi, ti: (bi * nt + ti, 0)
    const = lambda bi, ti: (0, 0)
    expand = (jnp.arange(LANES)[:, None] == jnp.arange(SSD_WIDTH)[None, :] // SSD_HEAD_DIM).astype(BF16)
    triu = (jnp.arange(CHUNK)[:, None] <= jnp.arange(CHUNK)[None, :]).astype(BF16)
    return pl.pallas_call(
        _ssd_kernel,
        grid=(b, s // tt),
        in_specs=[
            pl.BlockSpec((tt, CONV_CH), tile),
            pl.BlockSpec((tt, SSD_WIDTH), tile),
            pl.BlockSpec((tt, LANES), tile),
            pl.BlockSpec((SSD_HEADS, CHUNK), const),
            pl.BlockSpec((SSD_HEADS, CHUNK), const),
            pl.BlockSpec((1, SSD_WIDTH), const),
            pl.BlockSpec((1, SSD_WIDTH), const),
            pl.BlockSpec((LANES, SSD_WIDTH), const),
            pl.BlockSpec((CHUNK, CHUNK), const),
        ],
        out_specs=pl.BlockSpec((tt, SSD_WIDTH), tile),
        out_shape=jax.ShapeDtypeStruct((b * s, SSD_WIDTH), BF16),
        scratch_shapes=[
            pltpu.VMEM((SSD_HEADS // 2, D_STATE, LANES), F32),
        ],
        compiler_params=pltpu.CompilerParams(
            dimension_semantics=("parallel", "arbitrary"), vmem_limit_bytes=VMEM_LIMIT),
        name="ssd_mixer",
    )(xc, zs, misc, dtb_rep, a_rep, dskip_row, nw_row, expand, triu)


def _layer_norm(y, g, b):
    mu = jnp.mean(y, axis=1, keepdims=True)
    yc = y - mu
    var = jnp.mean(yc * yc, axis=1, keepdims=True)
    return yc * lax.rsqrt(var + LN_EPS) * g + b


OUT_TM = 1024
OUT_SUB = 256


def _out_proj_kernel(att_ref, ssd_ref, x_ref, wa_ref, ws_ref, g_ref, b_ref, h_ref):
    for r0 in range(0, x_ref.shape[0], OUT_SUB):
        rows = slice(r0, r0 + OUT_SUB)
        mixed = jnp.dot(att_ref[rows, :], wa_ref[...], preferred_element_type=F32)
        mixed = mixed + jnp.dot(ssd_ref[rows, :], ws_ref[...], preferred_element_type=F32)
        h_ref[rows, :] = _layer_norm(ALPHA * x_ref[rows, :] + mixed, g_ref[...], b_ref[...])


def _out_proj(att2, ssd2, x2, w_att, w_ssd, g_row, b_row, tm):
    n = x2.shape[0]
    row = lambda i: (i, 0)
    const = lambda i: (0, 0)
    return pl.pallas_call(
        _out_proj_kernel,
        grid=(n // tm,),
        in_specs=[
            pl.BlockSpec((tm, ATT_WIDTH), row),
            pl.BlockSpec((tm, SSD_WIDTH), row),
            pl.BlockSpec((tm, D_MODEL), row),
            pl.BlockSpec((ATT_WIDTH, D_MODEL), const),
            pl.BlockSpec((SSD_WIDTH, D_MODEL), const),
            pl.BlockSpec((1, D_MODEL), const),
            pl.BlockSpec((1, D_MODEL), const),
        ],
        out_specs=pl.BlockSpec((tm, D_MODEL), row),
        out_shape=jax.ShapeDtypeStruct((n, D_MODEL), F32),
        compiler_params=pltpu.CompilerParams(
            dimension_semantics=("parallel",), vmem_limit_bytes=VMEM_LIMIT),
        name="out_proj_ln",
    )(att2, ssd2, x2, w_att, w_ssd, g_row, b_row)


ROUTE_E0 = N_GROUPS_MOE
MOE_TM = 512
MOE_CAP = 160


def _first_max(vals):
    best = vals[0]
    for v in vals[1:]:
        best = jnp.maximum(best, v)
    idx = jnp.full(best.shape, len(vals) - 1, jnp.int32)
    for i in range(len(vals) - 2, -1, -1):
        idx = jnp.where(vals[i] == best, i, idx)
    return best, idx


def _moe_kernel(h_ref, wrh_ref, wrl_ref, br_ref, wg_ref, wu_ref, wd_ref, g_ref, b_ref, upper_ref,
                o_ref, hb_ref, hp_ref, gp_ref, yp_ref):
    tm = h_ref.shape[0]
    h = h_ref[...]
    h_hi = h.astype(BF16)
    hb_ref[...] = h_hi
    h_lo = (h - h_hi.astype(F32)).astype(BF16)
    wrh = wrh_ref[...]
    both = jnp.dot(h_hi, jnp.concatenate([wrh, wrl_ref[...]], axis=1), preferred_element_type=F32)
    logits = (both[:, :LANES] + jnp.dot(h_lo, wrh, preferred_element_type=F32)
              + both[:, LANES:]) + br_ref[...]
    lt = logits.T
    row = lambda r: lt[r:r + 1, :]
    gl = [row(r) for r in range(N_GROUPS_MOE)]
    gmax, gidx = _first_max(gl)
    denom = jnp.exp(gl[0] - gmax)
    for v in gl[1:]:
        denom = denom + jnp.exp(v - gmax)
    gprob = 1.0 / denom
    el = []
    for k in range(EXPERTS_PER_GROUP):
        v = row(ROUTE_E0 + (N_GROUPS_MOE - 1) * EXPERTS_PER_GROUP + k)
        for gg in range(N_GROUPS_MOE - 2, -1, -1):
            v = jnp.where(gidx == gg, row(ROUTE_E0 + gg * EXPERTS_PER_GROUP + k), v)
        el.append(v)
    l1, i1 = _first_max(el)
    l2, i2 = _first_max([jnp.where(i1 == k, -jnp.inf, el[k]) for k in range(EXPERTS_PER_GROUP)])
    e2 = jnp.exp(l2 - l1)
    w1 = gprob / (1.0 + e2)
    w2 = gprob * e2 / (1.0 + e2)
    gate4 = [jnp.where(i1 == k, w1, jnp.where(i2 == k, w2, 0.0)) for k in range(EXPERTS_PER_GROUP)]

    member = [jnp.where(gidx == gg, 1.0, 0.0) for gg in range(N_GROUPS_MOE)]
    member_blk = jnp.concatenate(
        member + [jnp.zeros((2 * SUBLANES - N_GROUPS_MOE, tm), F32)], axis=0).astype(BF16)
    earlier = jnp.dot(member_blk, upper_ref[...], preferred_element_type=F32)
    counts = [jnp.sum(m, axis=1, keepdims=True) for m in member]
    most = counts[0]
    for c in counts[1:]:
        most = jnp.maximum(most, c)
    fits = jnp.max(most) <= float(MOE_CAP)
    slot = member[0] * earlier[0:1, :]
    for gg in range(1, N_GROUPS_MOE):
        slot = slot + member[gg] * (earlier[gg:gg + 1, :] + float(gg * MOE_CAP))
    tok = jnp.concatenate(
        gate4 + [slot, gidx.astype(F32), jnp.zeros((LANES - EXPERTS_PER_GROUP - 2, tm), F32)],
        axis=0).T
    slot_lane, grp_lane = EXPERTS_PER_GROUP, EXPERTS_PER_GROUP + 1
    rows_p = N_GROUPS_MOE * MOE_CAP

    @pl.when(fits)
    def _():
        hb = hb_ref[...]
        place = jnp.where(lax.broadcasted_iota(jnp.int32, (rows_p, tm), 0) == slot.astype(jnp.int32),
                          1.0, 0.0).astype(BF16)
        hp_ref[...] = jnp.dot(place, hb, preferred_element_type=F32).astype(BF16)
        t_hi = tok.astype(BF16)
        t_lo = (tok - t_hi.astype(F32)).astype(BF16)
        both = jnp.dot(place, jnp.concatenate([t_hi, t_lo], axis=1), preferred_element_type=F32)
        gp_ref[...] = both[:, :LANES] + both[:, LANES:]
        for gg in range(N_GROUPS_MOE):
            rows = slice(gg * MOE_CAP, (gg + 1) * MOE_CAP)
            xg = hp_ref[rows, :]
            acc = None
            for k in range(EXPERTS_PER_GROUP):
                e = gg * EXPERTS_PER_GROUP + k
                a = jnp.dot(xg, wg_ref[e], preferred_element_type=F32)
                u = jnp.dot(xg, wu_ref[e], preferred_element_type=F32)
                hid = (_silu(a) * u * gp_ref[rows, k:k + 1]).astype(BF16)
                part = jnp.dot(hid, wd_ref[e], preferred_element_type=F32)
                acc = part if acc is None else acc + part
            yp_ref[rows, :] = acc.astype(BF16)
        back = jnp.where(lax.broadcasted_iota(jnp.int32, (tm, rows_p), 1)
                         == tok[:, slot_lane:slot_lane + 1].astype(jnp.int32), 1.0, 0.0).astype(BF16)
        for r0 in range(0, tm, OUT_SUB):
            rr = slice(r0, r0 + OUT_SUB)
            y = jnp.dot(back[rr, :], yp_ref[...], preferred_element_type=F32)
            o_ref[rr, :] = _layer_norm(ALPHA * h_ref[rr, :] + y, g_ref[...], b_ref[...])

    @pl.when(jnp.logical_not(fits))
    def _():
        hb = hb_ref[...]
        acc = None
        for gg in range(N_GROUPS_MOE):
            in_grp = tok[:, grp_lane:grp_lane + 1] == float(gg)
            for k in range(EXPERTS_PER_GROUP):
                e = gg * EXPERTS_PER_GROUP + k
                a = jnp.dot(hb, wg_ref[e], preferred_element_type=F32)
                u = jnp.dot(hb, wu_ref[e], preferred_element_type=F32)
                gate = jnp.where(in_grp, tok[:, k:k + 1], 0.0)
                hid = (_silu(a) * u * gate).astype(BF16)
                part = jnp.dot(hid, wd_ref[e], preferred_element_type=F32)
                acc = part if acc is None else acc + part
        o_ref[...] = _layer_norm(ALPHA * h_ref[...] + acc, g_ref[...], b_ref[...])


def _moe(h2, wr_hi, wr_lo, br, wg, wu, wd, g_row, b_row):
    n = h2.shape[0]
    tm = min(MOE_TM, n)
    rows_p = N_GROUPS_MOE * MOE_CAP
    row = lambda i: (i, 0)
    const = lambda i: (0, 0)
    whole = lambda i: (0, 0, 0)
    once = pl.Buffered(1)
    upper = (jnp.arange(tm)[:, None] < jnp.arange(tm)[None, :]).astype(BF16)
    return pl.pallas_call(
        _moe_kernel,
        grid=(n // tm,),
        in_specs=[
            pl.BlockSpec((tm, D_MODEL), row),
            pl.BlockSpec((D_MODEL, LANES), const),
            pl.BlockSpec((D_MODEL, LANES), const),
            pl.BlockSpec((1, LANES), const),
            pl.BlockSpec((N_EXPERTS, D_MODEL, EXPERT_FF), whole, pipeline_mode=once),
            pl.BlockSpec((N_EXPERTS, D_MODEL, EXPERT_FF), whole, pipeline_mode=once),
            pl.BlockSpec((N_EXPERTS, EXPERT_FF, D_MODEL), whole, pipeline_mode=once),
            pl.BlockSpec((1, D_MODEL), const),
            pl.BlockSpec((1, D_MODEL), const),
            pl.BlockSpec((tm, tm), const, pipeline_mode=once),
        ],
        out_specs=pl.BlockSpec((tm, D_MODEL), row),
        out_shape=jax.ShapeDtypeStruct((n, D_MODEL), F32),
        scratch_shapes=[
            pltpu.VMEM((tm, D_MODEL), BF16),
            pltpu.VMEM((rows_p, D_MODEL), BF16),
            pltpu.VMEM((rows_p, LANES), F32),
            pltpu.VMEM((rows_p, D_MODEL), BF16),
        ],
        compiler_params=pltpu.CompilerParams(
            dimension_semantics=("parallel",), vmem_limit_bytes=VMEM_LIMIT),
        name="hier_moe_ln",
    )(h2, wr_hi, wr_lo, br, wg, wu, wd, g_row, b_row, upper)


def _rope_tables(seq):
    inv = ROPE_THETA ** (-jnp.arange(0, HEAD_DIM, 2, dtype=F32) / HEAD_DIM)
    ang = jnp.arange(seq, dtype=F32)[:, None] * inv[None, :]
    cos, sin = jnp.cos(ang), jnp.sin(ang)
    zero = jnp.zeros_like(sin)
    cos_t = jnp.tile(cos, (1, LANES // (HEAD_DIM // 2)))
    s1_t = jnp.tile(jnp.concatenate([-sin, zero], 1), (1, LANES // HEAD_DIM))
    s2_t = jnp.tile(jnp.concatenate([zero, sin], 1), (1, LANES // HEAD_DIM))
    return cos_t, s1_t, s2_t


def _permute_w_in(w):
    sizes = (ATT_WIDTH, HEAD_DIM, HEAD_DIM, IDX_HEADS * IDX_DIM, IDX_DIM, IDX_HEADS,
             SSD_WIDTH, CONV_CH, SSD_HEADS)
    pts = np.cumsum((0,) + sizes)
    q, k, v, iq, ik, iw, z, xbc, dt = [w[:, pts[i]:pts[i + 1]] for i in range(len(sizes))]
    d = w.shape[0]
    pad = lambda n: jnp.zeros((d, n), w.dtype)
    misc = jnp.concatenate([v, iw, pad(MISC_DT - MISC_IW - IDX_HEADS), dt,
                            pad(LANES - MISC_DT - SSD_HEADS)], 1)
    return jnp.concatenate([q, iq, k, ik, misc, z, xbc], 1).astype(BF16)


def _head_rep(vals):
    return jnp.broadcast_to(vals.astype(F32)[:, None], (vals.shape[0], CHUNK))


def kernel(x, w_in, conv_w, conv_b, dt_bias, a_log, d_skip, ssd_norm_w, w_out, ln1_g, ln1_b,
           w_route_group, b_route_group, w_route_expert, b_route_expert, w_gate, w_up,
           w_down, ln2_g, ln2_b):
    bsz, seq, d = x.shape
    n = bsz * seq
    topk = min(TOPK_MAX, seq // 4)
    tm = 512
    assert d == D_MODEL and TQ == KC and seq % TQ == 0 and seq % tm == 0 and topk <= KC
    assert seq // BF16_ROWS <= 256
    assert (FINE_PASSES - FINE_FIXED_PASSES) % 2 == 0
    cos_t, s1_t, s2_t = _rope_tables(seq)
    mscale = jnp.ones((1, LANES), F32).at[0, MISC_IW:MISC_IW + IDX_HEADS].set(INDEXER_SCALE)
    for l in range(DEPTH):
        x2 = x.reshape(n, d)
        q, iq, k, ik, vt, misc, zs, xc = _in_proj(
            x2, _permute_w_in(w_in[l]), cos_t, s1_t, s2_t, mscale,
            conv_w[l].astype(F32), conv_b[l].astype(F32)[None, :], seq, tm)
        att = _dsa(q, iq, misc, k, ik, vt, bsz, seq, topk)
        ssd = _ssd(
            xc, zs, misc,
            _head_rep(dt_bias[l]), _head_rep(-jnp.exp(a_log[l].astype(F32))),
            jnp.repeat(d_skip[l].astype(F32), SSD_HEAD_DIM)[None, :], ssd_norm_w[l][None, :],
            bsz, seq)
        w_o = w_out[l].astype(BF16)
        h2 = _out_proj(att, ssd, x2, w_o[:ATT_WIDTH], w_o[ATT_WIDTH:],
                       ln1_g[l][None, :], ln1_b[l][None, :], min(OUT_TM, n))
        pad = jnp.zeros((d, LANES - ROUTE_E0 - N_EXPERTS), F32)
        wr = jnp.concatenate([w_route_group[l].astype(F32), w_route_expert[l].astype(F32), pad], 1)
        wr_hi = wr.astype(BF16)
        wr_lo = (wr - wr_hi.astype(F32)).astype(BF16)
        br = jnp.concatenate([b_route_group[l].astype(F32), b_route_expert[l].astype(F32),
                              pad[0]])[None, :]
        x = _moe(h2, wr_hi, wr_lo, br, w_gate[l].astype(BF16), w_up[l].astype(BF16),
                 w_down[l].astype(BF16), ln2_g[l][None, :], ln2_b[l][None, :]).reshape(bsz, seq, d)
    return x
```
